```python
import math
import jax, jax.numpy as jnp
from jax import lax
import numpy as np

D_MODEL = 1024
BATCH = 8
SEQ = 2048
DEPTH = 4

HEAD_DIM = 64
N_HEADS = D_MODEL // HEAD_DIM
FOX_HEADS = N_HEADS // 2
NSA_HEADS = N_HEADS - FOX_HEADS
NSA_KV_GROUPS = 2
NSA_GQA = NSA_HEADS // NSA_KV_GROUPS
SB_HEADS = N_HEADS
CMP_BLOCK = 32
CMP_STRIDE = 16
SLC_BLOCK = 64
SLC_TOPK = 8
WINDOW = 256
Q_BLOCK = 128
N_BUCKETS = 32
MAX_DISTANCE = 128
XA_HEADS = 4
XA_HEAD_DIM = 128
XA_WIDTH = XA_HEADS * XA_HEAD_DIM
MEM_LEN = 256
D_FF = 4 * D_MODEL
EPS = 1e-6
NEG = -1e30
FORCED_SCORE = 1e4
FORGET_BIAS_INIT = 4.0

FOX_W = FOX_HEADS * HEAD_DIM
NSA_W = NSA_HEADS * HEAD_DIM
KV_W = NSA_KV_GROUPS * HEAD_DIM
EVEN_SPLITS = [FOX_W, FOX_W, FOX_W, FOX_HEADS, NSA_W, KV_W, KV_W, KV_W, KV_W, KV_W, KV_W, 3 * NSA_HEADS]
EVEN_PROJ = sum(EVEN_SPLITS)
EVEN_OFFSETS = [int(o) for o in np.cumsum(EVEN_SPLITS)[:-1]]
ODD_PROJ = 3 * SB_HEADS * HEAD_DIM

kernel_name = 'hybrid_fox_nsa_stickbreaking_trunk'


def rmsnorm(x, g):
    xf = x.astype(jnp.float32)
    y = xf * lax.rsqrt(jnp.mean(xf * xf, axis=-1, keepdims=True) + EPS)
    return (y * g.astype(jnp.float32)).astype(x.dtype)


def t5_bucket(dist):
    dist = jnp.maximum(dist, 0)
    max_exact = N_BUCKETS // 2
    d_f = jnp.maximum(dist, 1).astype(jnp.float32)
    large = max_exact + (jnp.log(d_f / max_exact) / math.log(MAX_DISTANCE / max_exact)
                         * (N_BUCKETS - max_exact)).astype(jnp.int32)
    large = jnp.minimum(large, N_BUCKETS - 1)
    return jnp.where(dist < max_exact, dist, large)


def _heads(a, n):
    return a.reshape(a.shape[0], a.shape[1], n, -1)


def fox_attention(q, k, v, logf):
    B, T, H, dh = q.shape
    scale = dh ** -0.5
    F = jnp.cumsum(logf.astype(jnp.float32), axis=1).transpose(0, 2, 1)
    outs = []
    for q0 in range(0, T, Q_BLOCK):
        q1 = q0 + Q_BLOCK
        s = jnp.einsum('bqhd,bkhd->bhqk', q[:, q0:q1], k[:, :q1]).astype(jnp.float32) * scale
        s = s + F[:, :, q0:q1, None] - F[:, :, None, :q1]
        causal = np.arange(q0, q1)[:, None] >= np.arange(q1)[None, :]
        p = jax.nn.softmax(jnp.where(causal, s, NEG), axis=-1).astype(v.dtype)
        outs.append(jnp.einsum('bhqk,bkhd->bqhd', p, v[:, :q1]))
    return jnp.concatenate(outs, axis=1)


def stick_breaking_attention(q, k, v):
    B, T, H, dh = q.shape
    scale = dh ** -0.5
    outs = []
    for q0 in range(0, T, Q_BLOCK):
        q1 = q0 + Q_BLOCK
        z = jnp.einsum('bqhd,bkhd->bhqk', q[:, q0:q1], k[:, :q1]).astype(jnp.float32) * scale
        strict = np.arange(q0, q1)[:, None] > np.arange(q1)[None, :]
        log_1mb = jnp.where(strict, jax.nn.log_sigmoid(-z), 0.0)
        later = lax.cumsum(log_1mb, axis=3, reverse=True) - log_1mb
        a = jnp.where(strict, jnp.exp(jax.nn.log_sigmoid(z) + later), 0.0).astype(v.dtype)
        outs.append(jnp.einsum('bhqk,bkhd->bqhd', a, v[:, :q1]))
    return jnp.concatenate(outs, axis=1)


def _compress(x, pe, w1, w2):
    T = x.shape[1]
    n_cmp = (T - CMP_BLOCK) // CMP_STRIDE + 1
    idx = np.arange(n_cmp)[:, None] * CMP_STRIDE + np.arange(CMP_BLOCK)[None, :]
    xb = x[:, idx] + pe[None, None, :, None, :]
    h = jax.nn.gelu(jnp.einsum('bclgd,lde->bcge', xb, w1))
    return jnp.einsum('bcge,ef->bgcf', h, w2)


def nsa_attention(q, k_cmp, v_cmp, k_slc, v_slc, k_win, v_win, gate_logits,
                  pe_k, w1_k, w2_k, pe_v, w1_v, w2_v, rel_bias):
    B, T, H, dh = q.shape
    G, R = NSA_KV_GROUPS, NSA_GQA
    scale = dh ** -0.5
    qg = q.reshape(B, T, G, R, dh).transpose(0, 2, 3, 1, 4)
    kc = _compress(k_cmp, pe_k, w1_k, w2_k)
    vc = _compress(v_cmp, pe_v, w1_v, w2_v)
    n_cmp = kc.shape[2]
    cmp_start = np.arange(n_cmp) * CMP_STRIDE
    cmp_end = cmp_start + CMP_BLOCK - 1
    n_slc = T // SLC_BLOCK
    n_top = min(SLC_TOPK, n_slc)
    slc_start = np.arange(n_slc) * SLC_BLOCK
    slc_end = slc_start + SLC_BLOCK - 1
    overlap = jnp.asarray(((cmp_start[:, None] <= slc_end[None, :]) &
                           (cmp_end[:, None] >= slc_start[None, :])).astype(np.float32))
    ks = k_slc.reshape(B, n_slc, SLC_BLOCK, G, dh).transpose(0, 3, 1, 2, 4)
    vs = v_slc.reshape(B, n_slc, SLC_BLOCK, G, dh).transpose(0, 3, 1, 2, 4)
    kw = jnp.pad(k_win.transpose(0, 2, 1, 3), ((0, 0), (0, 0), (WINDOW, 0), (0, 0)))
    vw = jnp.pad(v_win.transpose(0, 2, 1, 3), ((0, 0), (0, 0), (WINDOW, 0), (0, 0)))
    gates = jax.nn.sigmoid(gate_logits.astype(jnp.float32)).reshape(B, T, G, R, 3).transpose(0, 2, 3, 1, 4)
    tab = rel_bias.astype(jnp.float32).T.reshape(G, R, N_BUCKETS)
    wdist = np.arange(Q_BLOCK)[:, None] + WINDOW - np.arange(WINDOW + Q_BLOCK)[None, :]
    w_in_band = (wdist >= 0) & (wdist < WINDOW)
    wdist_j = jnp.asarray(wdist, dtype=jnp.int32)
    w_bias = tab[:, :, t5_bucket(wdist_j)]
    cmp_end_j = jnp.asarray(cmp_end, dtype=jnp.int32)
    blk = jnp.arange(n_slc, dtype=jnp.int32)
    b_ix = jnp.arange(B)[:, None, None, None]
    g_ix = jnp.arange(G)[None, :, None, None]
    g5 = jnp.arange(G)[None, :, None, None, None]
    r5 = jnp.arange(R)[None, None, :, None, None]

    def attend(q0):
        t = q0 + jnp.arange(Q_BLOCK, dtype=jnp.int32)
        qb = lax.dynamic_slice_in_dim(qg, q0, Q_BLOCK, axis=3)
        cdist = t[:, None] - cmp_end_j[None, :]
        cvalid = cdist >= 0
        sc = jnp.einsum('bgrqd,bgcd->bgrqc', qb, kc).astype(jnp.float32) * scale + tab[:, :, t5_bucket(cdist)]
        pc = jnp.where(cvalid, jax.nn.softmax(jnp.where(cvalid, sc, NEG), axis=-1), 0.0)
        o_cmp = jnp.einsum('bgrqc,bgcd->bgrqd', pc.astype(vc.dtype), vc)
        score = jnp.einsum('bgrqc,cn->bgqn', pc, overlap)
        cur = t // SLC_BLOCK
        forced = (blk[None, :] == 0) | (blk[None, :] == cur[:, None]) | (blk[None, :] == cur[:, None] - 1)
        future = blk[None, :] > cur[:, None]
        score = jnp.where(future, -1.0, jnp.where(forced, FORCED_SCORE, score))
        _, sel = lax.top_k(score, n_top)
        kg = ks[b_ix, g_ix, sel].reshape(B, G, Q_BLOCK, n_top * SLC_BLOCK, dh)
        vg = vs[b_ix, g_ix, sel].reshape(B, G, Q_BLOCK, n_top * SLC_BLOCK, dh)
        pos = (sel[..., None] * SLC_BLOCK + jnp.arange(SLC_BLOCK, dtype=jnp.int32)).reshape(B, G, Q_BLOCK, -1)
        sdist = t[None, None, :, None] - pos
        svalid = (sdist >= 0)[:, :, None]
        ss = (jnp.einsum('bgrqd,bgqkd->bgrqk', qb, kg).astype(jnp.float32) * scale
              + tab[g5, r5, t5_bucket(sdist)[:, :, None]])
        p_s = jax.nn.softmax(jnp.where(svalid, ss, NEG), axis=-1)
        o_slc = jnp.einsum('bgrqk,bgqkd->bgrqd', p_s.astype(vg.dtype), vg)
        kb = lax.dynamic_slice_in_dim(kw, q0, WINDOW + Q_BLOCK, axis=2)
        vb = lax.dynamic_slice_in_dim(vw, q0, WINDOW + Q_BLOCK, axis=2)
        wvalid = w_in_band & ((t[:, None] - wdist_j) >= 0)
        sw = jnp.einsum('bgrqd,bgkd->bgrqk', qb, kb).astype(jnp.float32) * scale + w_bias
        p_w = jax.nn.softmax(jnp.where(wvalid, sw, NEG), axis=-1)
        o_win = jnp.einsum('bgrqk,bgkd->bgrqd', p_w.astype(vb.dtype), vb)
        g = lax.dynamic_slice_in_dim(gates, q0, Q_BLOCK, axis=3)
        return (g[..., 0:1] * o_cmp + g[..., 1:2] * o_slc + g[..., 2:3] * o_win).astype(q.dtype)

    outs = lax.map(attend, jnp.arange(0, T, Q_BLOCK, dtype=jnp.int32))
    return outs.transpose(1, 0, 4, 2, 3, 5).reshape(B, T, H, dh)


def memory_cross_attention(h, mem_n, wq, wkv, wo):
    B, T, _ = h.shape
    q = _heads(h @ wq, XA_HEADS)
    k, v = jnp.split(mem_n @ wkv, 2, axis=-1)
    k, v = _heads(k, XA_HEADS), _heads(v, XA_HEADS)
    s = jnp.einsum('bthd,bmhd->bhtm', q, k).astype(jnp.float32) * (XA_HEAD_DIM ** -0.5)
    p = jax.nn.softmax(s, axis=-1).astype(v.dtype)
    o = jnp.einsum('bhtm,bmhd->bthd', p, v).reshape(B, T, XA_WIDTH)
    return o @ wo


def setup_inputs(seed: int = 0) -> dict:
    key = jax.random.key(seed)
    keys = iter(jax.random.split(key, 32))

    def nrm(shape, scale):
        return jax.random.normal(next(keys), shape, jnp.float32) * scale

    def gain(shape):
        return 1.0 + nrm(shape, 0.05)

    ne = (DEPTH + 1) // 2
    no = DEPTH // 2
    d = D_MODEL
    return {
        'x': nrm((BATCH, SEQ, d), 1.0),
        'mem': nrm((BATCH, MEM_LEN, d), 1.0),
        'rel_bias': nrm((N_BUCKETS, NSA_HEADS), 0.2),
        'mem_norm_g': gain((d,)),
        'norm_mix_g': gain((DEPTH, d)),
        'norm_xattn_g': gain((DEPTH, d)),
        'norm_mlp_g': gain((DEPTH, d)),
        'final_norm_g': gain((d,)),
        'w_in_even': nrm((ne, d, EVEN_PROJ), d ** -0.5),
        'b_forget': FORGET_BIAS_INIT + nrm((ne, FOX_HEADS), 0.5),
        'cmp_pe_k': nrm((ne, CMP_BLOCK, HEAD_DIM), 0.02),
        'cmp_w1_k': nrm((ne, CMP_BLOCK, HEAD_DIM, HEAD_DIM), (CMP_BLOCK * HEAD_DIM) ** -0.5),
        'cmp_w2_k': nrm((ne, HEAD_DIM, HEAD_DIM), HEAD_DIM ** -0.5),
        'cmp_pe_v': nrm((ne, CMP_BLOCK, HEAD_DIM), 0.02),
        'cmp_w1_v': nrm((ne, CMP_BLOCK, HEAD_DIM, HEAD_DIM), (CMP_BLOCK * HEAD_DIM) ** -0.5),
        'cmp_w2_v': nrm((ne, HEAD_DIM, HEAD_DIM), HEAD_DIM ** -0.5),
        'w_out_even': nrm((ne, FOX_W + NSA_W, d), (FOX_W + NSA_W) ** -0.5),
        'w_in_odd': nrm((no, d, ODD_PROJ), d ** -0.5),
        'w_out_odd': nrm((no, SB_HEADS * HEAD_DIM, d), (SB_HEADS * HEAD_DIM) ** -0.5),
        'xa_wq': nrm((DEPTH, d, XA_WIDTH), d ** -0.5),
        'xa_wkv': nrm((DEPTH, d, 2 * XA_WIDTH), d ** -0.5),
        'xa_wo': nrm((DEPTH, XA_WIDTH, d), XA_WIDTH ** -0.5),
        'mlp_w1': nrm((DEPTH, d, D_FF), d ** -0.5),
        'mlp_w2': nrm((DEPTH, D_FF, d), D_FF ** -0.5),
    }


def reference(x, mem, rel_bias, mem_norm_g, norm_mix_g, norm_xattn_g, norm_mlp_g, final_norm_g,
              w_in_even, b_forget, cmp_pe_k, cmp_w1_k, cmp_w2_k, cmp_pe_v, cmp_w1_v, cmp_w2_v,
              w_out_even, w_in_odd, w_out_odd, xa_wq, xa_wkv, xa_wo, mlp_w1, mlp_w2):
    B, T, _ = x.shape
    mem_n = rmsnorm(mem, mem_norm_g)
    for layer in range(DEPTH):
        h = rmsnorm(x, norm_mix_g[layer])
        if layer % 2 == 0:
            e = layer // 2
            (fq, fk, fv, ff, nq, kc, vc, ksl, vsl, kwn, vwn, ng) = jnp.split(h @ w_in_even[e], EVEN_OFFSETS, axis=-1)
            logf = jax.nn.log_sigmoid(ff.astype(jnp.float32) + b_forget[e].astype(jnp.float32))
            fox = fox_attention(_heads(fq, FOX_HEADS), _heads(fk, FOX_HEADS), _heads(fv, FOX_HEADS), logf)
            nsa = nsa_attention(_heads(nq, NSA_HEADS),
                                _heads(kc, NSA_KV_GROUPS), _heads(vc, NSA_KV_GROUPS),
                                _heads(ksl, NSA_KV_GROUPS), _heads(vsl, NSA_KV_GROUPS),
                                _heads(kwn, NSA_KV_GROUPS), _heads(vwn, NSA_KV_GROUPS), ng,
                                cmp_pe_k[e], cmp_w1_k[e], cmp_w2_k[e],
                                cmp_pe_v[e], cmp_w1_v[e], cmp_w2_v[e], rel_bias)
            mixed = jnp.concatenate([fox.reshape(B, T, FOX_W), nsa.reshape(B, T, NSA_W)], axis=-1) @ w_out_even[e]
        else:
            o = layer // 2
            sq, sk, sv = jnp.split(h @ w_in_odd[o], 3, axis=-1)
            sb = stick_breaking_attention(_heads(sq, SB_HEADS), _heads(sk, SB_HEADS), _heads(sv, SB_HEADS))
            mixed = sb.reshape(B, T, SB_HEADS * HEAD_DIM) @ w_out_odd[o]
        x = x + mixed
        x = x + memory_cross_attention(rmsnorm(x, norm_xattn_g[layer]), mem_n,
                                       xa_wq[layer], xa_wkv[layer], xa_wo[layer])
        hm = rmsnorm(x, norm_mlp_g[layer])
        x = x + jnp.square(jax.nn.relu(hm @ mlp_w1[layer])) @ mlp_w2[layer]
    return rmsnorm(x, final_norm_g)
```

```python
import functools
import math

import numpy as np
import jax
import jax.numpy as jnp
from jax import lax
from jax.experimental import pallas as pl
from jax.experimental.pallas import tpu as pltpu

F32 = jnp.float32
BF16 = jnp.bfloat16

LANES = 128
HEAD_DIM = 64
FOX_HEADS = 8
NSA_HEADS = 8
NSA_GROUPS = 2
NSA_GQA = NSA_HEADS // NSA_GROUPS
SB_HEADS = 16
CMP_BLOCK = 32
CMP_STRIDE = 16
SLC_BLOCK = 64
SLC_TOPK = 8
WINDOW = 256
N_BUCKETS = 32
MAX_DISTANCE = 128
XA_HEADS = 4
XA_HEAD_DIM = 128
EPS = 1e-6
NEG = -1e30
FORCED_SCORE = 1e4

NSA_TILE = 128
ATT_TILE = 256
ROW_TILE = 512
MLP_ROW_TILE = 1024
MLP_FF_TILE = 512


def _dot(a, b):
    return jnp.dot(a, b, preferred_element_type=F32)


def _dot_nt(a, b):
    return lax.dot_general(a, b, (((1,), (1,)), ((), ())), preferred_element_type=F32)


def _split3(x):
    hi = x.astype(BF16)
    r1 = x - hi.astype(F32)
    mid = r1.astype(BF16)
    lo = (r1 - mid.astype(F32)).astype(BF16)
    return hi, mid, lo


def _dot_exact01(x, m01):
    hi, mid, lo = _split3(x)
    return _dot(hi, m01) + _dot(mid, m01) + _dot(lo, m01)


def _rmsnorm(x, g):
    ms = jnp.mean(x * x, axis=-1, keepdims=True)
    return x * lax.rsqrt(ms + EPS) * g


def _sigmoid(x):
    return 1.0 / (1.0 + jnp.exp(-x))


def _softplus(x):
    return jnp.maximum(x, 0.0) + jnp.log1p(jnp.exp(-jnp.abs(x)))


def _lane_lo(shape):
    return lax.broadcasted_iota(jnp.int32, shape, len(shape) - 1) < HEAD_DIM


def _split_pair(q, scale):
    lo = _lane_lo(q.shape)
    qs = q * jnp.asarray(scale, q.dtype)
    zero = jnp.zeros_like(qs)
    return jnp.where(lo, qs, zero), jnp.where(lo, zero, qs)


def _t5_bucket_np(dist):
    dist = np.maximum(dist, 0)
    max_exact = N_BUCKETS // 2
    d_f = np.maximum(dist, 1).astype(np.float64)
    large = max_exact + (np.log(d_f / max_exact) / math.log(MAX_DISTANCE / max_exact)
                         * (N_BUCKETS - max_exact)).astype(np.int32)
    large = np.minimum(large, N_BUCKETS - 1)
    return np.where(dist < max_exact, dist, large).astype(np.int32)


def _params(*sem):
    return pltpu.CompilerParams(dimension_semantics=sem)


def _norm_matmul_kernel(x_ref, g_ref, w_ref, o_ref):
    xn = _rmsnorm(x_ref[...], g_ref[...]).astype(BF16)
    o_ref[...] = _dot(xn, w_ref[...]).astype(o_ref.dtype)


def norm_matmul(x, g, w, out_dtype=BF16, tm=ROW_TILE):
    n, d = x.shape
    p = w.shape[1]
    tm = min(tm, n)
    return pl.pallas_call(
        _norm_matmul_kernel,
        grid=(n // tm,),
        in_specs=[pl.BlockSpec((tm, d), lambda i: (i, 0)),
                  pl.BlockSpec((1, d), lambda i: (0, 0)),
                  pl.BlockSpec((d, p), lambda i: (0, 0))],
        out_specs=pl.BlockSpec((tm, p), lambda i: (i, 0)),
        out_shape=jax.ShapeDtypeStruct((n, p), out_dtype),
        compiler_params=_params("parallel"),
        name="norm_matmul",
    )(x, g, w)


def _even_proj_kernel(x_ref, g_ref, w_ref, wkc_ref, wvc_ref, wng_ref, wff_ref, bf_ref, u_ref,
                      main_ref, kc_ref, vc_ref, gate_ref, f_ref, carry_ref, *, tiles_per_seq, tm):
    i = pl.program_id(0)
    xn = _rmsnorm(x_ref[...], g_ref[...]).astype(BF16)
    main_ref[...] = _dot(xn, w_ref[...]).astype(BF16)
    kc_ref[...] = _dot(xn, wkc_ref[...]).astype(BF16)
    vc_ref[...] = _dot(xn, wvc_ref[...]).astype(BF16)
    gate_ref[...] = _sigmoid(_dot(xn, wng_ref[...]))
    ff = _dot_nt(wff_ref[...], xn) + bf_ref[...]
    logf = jnp.minimum(ff, 0.0) - jnp.log1p(jnp.exp(-jnp.abs(ff)))

    @pl.when(i % tiles_per_seq == 0)
    def _():
        carry_ref[...] = jnp.zeros_like(carry_ref)

    c = carry_ref[...]
    u = u_ref[...]
    for ch in range(tm // LANES):
        cs = _dot_exact01(logf[:, ch * LANES:(ch + 1) * LANES], u) + c
        f_ref[:, ch * LANES:(ch + 1) * LANES] = cs
        c = jnp.broadcast_to(cs[:, LANES - 1:LANES], c.shape)
    carry_ref[...] = c


def even_proj(x, g, w_main, w_kc, w_vc, w_ng, w_ff_t, b_f, tri_u, seq_len, tm=ROW_TILE):
    n, d = x.shape
    tm = min(tm, seq_len)
    pm = w_main.shape[1]
    pg = w_ng.shape[1]
    nh = w_ff_t.shape[0]
    full = lambda a: pl.BlockSpec(a.shape, lambda i: (0,) * a.ndim)
    return pl.pallas_call(
        functools.partial(_even_proj_kernel, tiles_per_seq=seq_len // tm, tm=tm),
        grid=(n // tm,),
        in_specs=[pl.BlockSpec((tm, d), lambda i: (i, 0)), full(g), full(w_main), full(w_kc),
                  full(w_vc), full(w_ng), full(w_ff_t), full(b_f), full(tri_u)],
        out_specs=[pl.BlockSpec((tm, pm), lambda i: (i, 0)),
                   pl.BlockSpec((tm, LANES), lambda i: (i, 0)),
                   pl.BlockSpec((tm, LANES), lambda i: (i, 0)),
                   pl.BlockSpec((tm, pg), lambda i: (i, 0)),
                   pl.BlockSpec((nh, tm), lambda i: (0, i))],
        out_shape=[jax.ShapeDtypeStruct((n, pm), BF16),
                   jax.ShapeDtypeStruct((n, LANES), BF16),
                   jax.ShapeDtypeStruct((n, LANES), BF16),
                   jax.ShapeDtypeStruct((n, pg), F32),
                   jax.ShapeDtypeStruct((nh, n), F32)],
        scratch_shapes=[pltpu.VMEM((nh, LANES), F32)],
        compiler_params=_params("arbitrary"),
        name="even_proj",
    )(x, g, w_main, w_kc, w_vc, w_ng, w_ff_t, b_f, tri_u)


def _proj_residual_kernel(*refs, n_in, with_q):
    a_refs = refs[:n_in]
    w_refs = refs[n_in:2 * n_in]
    x_ref = refs[2 * n_in]
    rest = refs[2 * n_in + 1:]
    acc = x_ref[...]
    for a_ref, w_ref in zip(a_refs, w_refs):
        acc = acc + _dot(a_ref[...], w_ref[...])
    if with_q:
        g_ref, wq_ref, o_ref, q_ref = rest
        o_ref[...] = acc
        q_ref[...] = _dot(_rmsnorm(acc, g_ref[...]).astype(BF16), wq_ref[...]).astype(BF16)
    else:
        (o_ref,) = rest
        o_ref[...] = acc


def proj_residual_q(a_list, w_list, x, g2, wq, tm=ROW_TILE):
    n, d = x.shape
    tm = min(tm, n)
    n_in = len(a_list)
    full = lambda a: pl.BlockSpec(a.shape, lambda i: (0,) * a.ndim)
    row = lambda a: pl.BlockSpec((tm, a.shape[1]), lambda i: (i, 0))
    return pl.pallas_call(
        functools.partial(_proj_residual_kernel, n_in=n_in, with_q=True),
        grid=(n // tm,),
        in_specs=[row(a) for a in a_list] + [full(w) for w in w_list] + [row(x), full(g2), full(wq)],
        out_specs=[row(x), pl.BlockSpec((tm, wq.shape[1]), lambda i: (i, 0))],
        out_shape=[jax.ShapeDtypeStruct((n, d), F32),
                   jax.ShapeDtypeStruct((n, wq.shape[1]), BF16)],
        compiler_params=_params("parallel"),
        name="proj_residual_q",
    )(*a_list, *w_list, x, g2, wq)


def _bucket_lookup_kernel(tab_ref, idx_ref, o_ref):
    h = pl.program_id(0)
    idx = idx_ref[...]
    out = jnp.full(idx.shape, NEG, F32)
    for b in range(N_BUCKETS):
        out = jnp.where(idx == b, tab_ref[b, h], out)
    o_ref[0] = out


def bucket_lookup(tab, idx, rows=256):
    r = idx.shape[0]
    rows = min(rows, r)
    h = tab.shape[1]
    return pl.pallas_call(
        _bucket_lookup_kernel,
        grid=(h, r // rows),
        in_specs=[pl.BlockSpec(memory_space=pltpu.SMEM),
                  pl.BlockSpec((rows, LANES), lambda hh, j: (j, 0))],
        out_specs=pl.BlockSpec((1, rows, LANES), lambda hh, j: (hh, j, 0)),
        out_shape=jax.ShapeDtypeStruct((h, r, LANES), F32),
        compiler_params=_params("parallel", "parallel"),
        name="bucket_lookup",
    )(tab, idx)


def _fox_kernel(q_ref, k_ref, v_ref, f_ref, o_ref, m_ref, l_ref, acc_ref, *, tile):
    i = pl.program_id(2)
    qa, qb = _split_pair(q_ref[...], HEAD_DIM ** -0.5)
    heads = (qa, qb)
    m_ref[...] = jnp.full(m_ref.shape, NEG, F32)
    l_ref[...] = jnp.zeros(l_ref.shape, F32)
    acc_ref[...] = jnp.zeros(acc_ref.shape, F32)

    def step(j, masked):
        ks = k_ref[pl.ds(j * tile, tile), :]
        vs = v_ref[pl.ds(j * tile, tile), :]
        fr = f_ref[0, j]
        if masked:
            rr = lax.broadcasted_iota(jnp.int32, (tile, tile), 0)
            cc = lax.broadcasted_iota(jnp.int32, (tile, tile), 1)
            causal = cc <= rr
        for a in range(2):
            s = _dot_nt(heads[a], ks) - fr[a:a + 1, :]
            if masked:
                s = jnp.where(causal, s, NEG)
            m_prev = m_ref[a]
            m_new = jnp.maximum(m_prev, jnp.max(s, axis=-1, keepdims=True))
            alpha = jnp.exp(m_prev - m_new)
            p = jnp.exp(s - m_new)
            if masked:
                p = jnp.where(causal, p, 0.0)
            l_ref[a] = alpha * l_ref[a] + jnp.sum(p, axis=-1, keepdims=True)
            acc_ref[a] = alpha * acc_ref[a] + _dot(p.astype(BF16), vs)
            m_ref[a] = m_new

    def body(j, carry):
        step(j, False)
        return carry

    lax.fori_loop(0, i, body, 0)
    step(i, True)
    oa = acc_ref[0] / l_ref[0]
    ob = acc_ref[1] / l_ref[1]
    o_ref[...] = jnp.where(_lane_lo(oa.shape), oa, ob).astype(o_ref.dtype)


def fox_attention(proj, f_pairs, batch, seq_len, q_col, k_col, v_col, n_pairs, tile=ATT_TILE):
    tile = min(tile, seq_len)
    nq = seq_len // tile
    n = batch * seq_len
    return pl.pallas_call(
        functools.partial(_fox_kernel, tile=tile),
        grid=(batch, n_pairs, nq),
        in_specs=[pl.BlockSpec((tile, LANES), lambda b, p, i: (b * nq + i, q_col + p)),
                  pl.BlockSpec((seq_len, LANES), lambda b, p, i: (b, k_col + p)),
                  pl.BlockSpec((seq_len, LANES), lambda b, p, i: (b, v_col + p)),
                  pl.BlockSpec((1, nq, 2, tile), lambda b, p, i: (p, b, 0, 0))],
        out_specs=pl.BlockSpec((tile, LANES), lambda b, p, i: (b * nq + i, p)),
        out_shape=jax.ShapeDtypeStruct((n, n_pairs * LANES), BF16),
        scratch_shapes=[pltpu.VMEM((2, tile, 1), F32), pltpu.VMEM((2, tile, 1), F32),
                        pltpu.VMEM((2, tile, LANES), F32)],
        compiler_params=_params("parallel", "parallel", "arbitrary"),
        name="fox_attention",
    )(proj, proj, proj, f_pairs)


def _sb_kernel(q_ref, k_ref, v_ref, tri_ref, o_ref, r_ref, acc_ref, *, tile):
    i = pl.program_id(2)
    qa, qb = _split_pair(q_ref[...], HEAD_DIM ** -0.5)
    heads = (qa, qb)
    r_ref[...] = jnp.zeros(r_ref.shape, F32)
    acc_ref[...] = jnp.zeros(acc_ref.shape, F32)
    tri = tri_ref[...]

    def step(j, masked):
        ks = k_ref[pl.ds(j * tile, tile), :]
        vs = v_ref[pl.ds(j * tile, tile), :]
        if masked:
            rr = lax.broadcasted_iota(jnp.int32, (tile, tile), 0)
            cc = lax.broadcasted_iota(jnp.int32, (tile, tile), 1)
            strict = cc < rr
        for a in range(2):
            z = _dot_nt(heads[a], ks)
            sp = _softplus(z)
            lg = -sp
            if masked:
                lg = jnp.where(strict, lg, 0.0)
            hi, mid, lo = _split3(lg)
            later = _dot(hi, tri) + _dot(mid, tri) + _dot(lo, tri)
            wgt = jnp.exp((z - sp) + later + r_ref[a])
            if masked:
                wgt = jnp.where(strict, wgt, 0.0)
            acc_ref[a] = acc_ref[a] + _dot(wgt.astype(BF16), vs)
            r_ref[a] = r_ref[a] + jnp.sum(lg, axis=-1, keepdims=True)

    step(i, True)

    def body(jj, carry):
        step(i - 1 - jj, False)
        return carry

    lax.fori_loop(0, i, body, 0)
    o_ref[...] = jnp.where(_lane_lo((tile, LANES)), acc_ref[0], acc_ref[1]).astype(o_ref.dtype)


def sb_attention(proj, batch, seq_len, q_col, k_col, v_col, n_pairs, tri, tile=ATT_TILE):
    tile = min(tile, seq_len)
    nq = seq_len // tile
    n = batch * seq_len
    return pl.pallas_call(
        functools.partial(_sb_kernel, tile=tile),
        grid=(batch, n_pairs, nq),
        in_specs=[pl.BlockSpec((tile, LANES), lambda b, p, i: (b * nq + i, q_col + p)),
                  pl.BlockSpec((seq_len, LANES), lambda b, p, i: (b, k_col + p)),
                  pl.BlockSpec((seq_len, LANES), lambda b, p, i: (b, v_col + p)),
                  pl.BlockSpec((tile, tile), lambda b, p, i: (0, 0))],
        out_specs=pl.BlockSpec((tile, LANES), lambda b, p, i: (b * nq + i, p)),
        out_shape=jax.ShapeDtypeStruct((n, n_pairs * LANES), BF16),
        scratch_shapes=[pltpu.VMEM((2, tile, 1), F32), pltpu.VMEM((2, tile, LANES), F32)],
        compiler_params=_params("parallel", "parallel", "arbitrary"),
        name="sb_attention",
    )(proj, proj, proj, tri)


def _gelu_tanh(x):
    return 0.5 * x * (1.0 + jnp.tanh(math.sqrt(2.0 / math.pi) * (x + 0.044715 * (x * x * x))))


def _compress_kernel(xk_ref, xv_ref, pek_ref, pev_ref, w1k_ref, w1v_ref, w2k_ref, w2v_ref,
                     ok_ref, ov_ref, *, n_cmp):
    def one(x_ref, pe_ref, w1_ref, w2_ref, o_ref):
        x = x_ref[...].astype(F32)
        xa = (x + pe_ref[0:1, :]).astype(BF16)
        xb = (x + pe_ref[1:2, :]).astype(BF16)
        ha = _dot(xa, w1_ref[0])
        hb = _dot(xb, w1_ref[1])
        rows = ha.shape[0]
        hb = pltpu.roll(hb, rows - 1, 0)
        h = _gelu_tanh(ha + hb)
        out = _dot(h.astype(BF16), w2_ref[...])
        ridx = lax.broadcasted_iota(jnp.int32, out.shape, 0)
        o_ref[0] = jnp.where(ridx < n_cmp, out, 0.0).astype(o_ref.dtype)

    one(xk_ref, pek_ref, w1k_ref, w2k_ref, ok_ref)
    one(xv_ref, pev_ref, w1v_ref, w2v_ref, ov_ref)


def compress(xk, xv, pek, pev, w1k, w1v, w2k, w2v, batch, n_cmp):
    chunks = xk.shape[0] // batch
    full = lambda a: pl.BlockSpec(a.shape, lambda b: (0,) * a.ndim)
    row = pl.BlockSpec((chunks, xk.shape[1]), lambda b: (b, 0))
    out = pl.BlockSpec((1, chunks, LANES), lambda b: (b, 0, 0))
    return pl.pallas_call(
        functools.partial(_compress_kernel, n_cmp=n_cmp),
        grid=(batch,),
        in_specs=[row, row, full(pek), full(pev), full(w1k), full(w1v), full(w2k), full(w2v)],
        out_specs=[out, out],
        out_shape=[jax.ShapeDtypeStruct((batch, chunks, LANES), BF16)] * 2,
        compiler_params=_params("parallel"),
        name="nsa_compress",
    )(xk, xv, pek, pev, w1k, w1v, w2k, w2v)


def _nsa_cmp_kernel(q_ref, kc_ref, vc_ref, cb_ref, ov_ref, ocmp_ref, sel_ref, *, tile, n_blocks):
    i = pl.program_id(1)
    kc = kc_ref[0]
    vc = vc_ref[0]
    pcsum = [jnp.zeros((tile, LANES), F32), jnp.zeros((tile, LANES), F32)]
    lo = _lane_lo((tile, LANES))
    for r in range(NSA_GQA):
        qa, qb = _split_pair(q_ref[:, r * LANES:(r + 1) * LANES], HEAD_DIM ** -0.5)
        outs = []
        for a, qh in enumerate((qa, qb)):
            bias = cb_ref[2 * r + a]
            valid = bias > 0.5 * NEG
            s = _dot_nt(qh, kc) + bias
            m = jnp.max(s, axis=-1, keepdims=True)
            p = jnp.where(valid, jnp.exp(s - m), 0.0)
            l = jnp.sum(p, axis=-1, keepdims=True)
            pc = p / jnp.where(l > 0.0, l, 1.0)
            pcsum[a] = pcsum[a] + pc
            outs.append(_dot(pc.astype(BF16), vc))
        ocmp_ref[:, r * LANES:(r + 1) * LANES] = jnp.where(lo, outs[0], outs[1])

    score = _dot_exact01(pcsum[0], ov_ref[0]) + _dot_exact01(pcsum[1], ov_ref[1])
    lane = lax.broadcasted_iota(jnp.int32, (tile, LANES), 1)
    row = lax.broadcasted_iota(jnp.int32, (tile, LANES), 0)
    blk = lane % n_blocks
    cur = (i * tile + row) // SLC_BLOCK
    forced = (blk == 0) | (blk == cur) | (blk == cur - 1)
    future = blk > cur
    score = jnp.where(future, -1.0, jnp.where(forced, FORCED_SCORE, score))
    grp0 = lane < n_blocks
    cnt = jnp.zeros((tile, LANES), F32)
    for mth in range(n_blocks):
        other = jnp.where(grp0, score[:, mth:mth + 1], score[:, n_blocks + mth:n_blocks + mth + 1])
        ahead = (other > score) | ((other == score) & (blk > mth))
        cnt = cnt + jnp.where(ahead, 1.0, 0.0)
    n_top = min(SLC_TOPK, n_blocks)
    sel = (cnt < n_top) & (lane < 2 * n_blocks)
    sel_ref[...] = jnp.where(sel, 1.0, 0.0).astype(sel_ref.dtype)


def nsa_cmp_select(main, kc, vc, cmp_bias, overlap2, batch, seq_len, q_col, tile=NSA_TILE):
    nq = seq_len // tile
    n = batch * seq_len
    n_blocks = seq_len // SLC_BLOCK
    qw = NSA_GQA * LANES
    return pl.pallas_call(
        functools.partial(_nsa_cmp_kernel, tile=tile, n_blocks=n_blocks),
        grid=(batch, nq),
        in_specs=[pl.BlockSpec((tile, qw), lambda b, i: (b * nq + i, q_col // NSA_GQA)),
                  pl.BlockSpec((1,) + kc.shape[1:], lambda b, i: (b, 0, 0)),
                  pl.BlockSpec((1,) + vc.shape[1:], lambda b, i: (b, 0, 0)),
                  pl.BlockSpec((NSA_HEADS, tile, LANES), lambda b, i: (0, i, 0)),
                  pl.BlockSpec(overlap2.shape, lambda b, i: (0, 0, 0))],
        out_specs=[pl.BlockSpec((tile, qw), lambda b, i: (b * nq + i, 0)),
                   pl.BlockSpec((tile, LANES), lambda b, i: (b * nq + i, 0))],
        out_shape=[jax.ShapeDtypeStruct((n, qw), F32),
                   jax.ShapeDtypeStruct((n, LANES), BF16)],
        compiler_params=_params("parallel", "parallel"),
        name="nsa_cmp_select",
    )(main, kc, vc, cmp_bias, overlap2)


def _nsa_main_kernel(q_ref, sel_ref, ocmp_ref, gate_ref, ks_ref, vs_ref, kw_ref, vw_ref, tz_ref, ex_ref,
                     o_ref, m_ref, l_ref, acc_ref, *, tile):
    i = pl.program_id(1)
    qa, qb = _split_pair(q_ref[...], HEAD_DIM ** -0.5)
    heads = (qa, qb)
    sel = sel_ref[...]
    rr = lax.broadcasted_iota(jnp.int32, (tile, tile), 0)
    cc = lax.broadcasted_iota(jnp.int32, (tile, tile), 1)
    lo = _lane_lo((tile, LANES))

    m_ref[...] = jnp.full(m_ref.shape, NEG, F32)
    l_ref[...] = jnp.zeros(l_ref.shape, F32)
    acc_ref[...] = jnp.zeros(acc_ref.shape, F32)

    def body(j, carry):
        ks = ks_ref[pl.ds(j * tile, tile), :]
        vs = vs_ref[pl.ds(j * tile, tile), :]
        which = jnp.where(j == i, 0, jnp.where(j == i - 1, 1, 2))
        causal = (j * tile + cc) <= (i * tile + rr)
        for a in range(2):
            chosen = _dot(sel, ex_ref[a, j]) > 0.5
            mask = chosen & causal
            s = jnp.where(mask, _dot_nt(heads[a], ks) + tz_ref[a, which], NEG)
            m_prev = m_ref[a]
            m_new = jnp.maximum(m_prev, jnp.max(s, axis=-1, keepdims=True))
            alpha = jnp.exp(m_prev - m_new)
            p = jnp.where(mask, jnp.exp(s - m_new), 0.0)
            l_ref[a] = alpha * l_ref[a] + jnp.sum(p, axis=-1, keepdims=True)
            acc_ref[a] = alpha * acc_ref[a] + _dot(p.astype(BF16), vs)
            m_ref[a] = m_new
        return carry

    lax.fori_loop(0, i + 1, body, 0)
    o_slc = jnp.where(lo, acc_ref[0] / l_ref[0], acc_ref[1] / l_ref[1])

    j1 = jnp.maximum(i - 1, 0)
    j2 = jnp.maximum(i - 2, 0)
    mask0 = cc <= rr
    mask1 = cc < jnp.where(i >= 1, tile, 0)
    mask2 = (cc > rr) & (cc < jnp.where(i >= 2, tile, 0))
    k0 = kw_ref[pl.ds(i * tile, tile), :]
    k1 = kw_ref[pl.ds(j1 * tile, tile), :]
    k2 = kw_ref[pl.ds(j2 * tile, tile), :]
    v0 = vw_ref[pl.ds(i * tile, tile), :]
    v1 = vw_ref[pl.ds(j1 * tile, tile), :]
    v2 = vw_ref[pl.ds(j2 * tile, tile), :]
    o_win = []
    for a in range(2):
        s0 = jnp.where(mask0, _dot_nt(heads[a], k0) + tz_ref[a, 0], NEG)
        s1 = jnp.where(mask1, _dot_nt(heads[a], k1) + tz_ref[a, 1], NEG)
        s2 = jnp.where(mask2, _dot_nt(heads[a], k2) + tz_ref[a, 2], NEG)
        m = jnp.maximum(jnp.maximum(jnp.max(s0, axis=-1, keepdims=True),
                                    jnp.max(s1, axis=-1, keepdims=True)),
                        jnp.max(s2, axis=-1, keepdims=True))
        p0 = jnp.where(mask0, jnp.exp(s0 - m), 0.0)
        p1 = jnp.where(mask1, jnp.exp(s1 - m), 0.0)
        p2 = jnp.where(mask2, jnp.exp(s2 - m), 0.0)
        l = (jnp.sum(p0, axis=-1, keepdims=True) + jnp.sum(p1, axis=-1, keepdims=True)
             + jnp.sum(p2, axis=-1, keepdims=True))
        o = _dot(p0.astype(BF16), v0) + _dot(p1.astype(BF16), v1) + _dot(p2.astype(BF16), v2)
        o_win.append(o / l)
    o_win = jnp.where(lo, o_win[0], o_win[1])

    g = gate_ref[...]
    gexp = [jnp.where(lo, g[:, 2 * c:2 * c + 1], g[:, 2 * c + 1:2 * c + 2]) for c in range(3)]
    out = gexp[0] * ocmp_ref[...] + gexp[1] * o_slc + gexp[2] * o_win
    o_ref[...] = out.astype(o_ref.dtype)


def nsa_main(main, sel, ocmp, gates, tz, expand2, batch, seq_len, q_col, ks_col, vs_col, kw_col, vw_col,
             tile=NSA_TILE):
    nq = seq_len // tile
    n = batch * seq_len
    kv = lambda col: pl.BlockSpec((seq_len, LANES), lambda b, i, r: (b, col))
    return pl.pallas_call(
        functools.partial(_nsa_main_kernel, tile=tile),
        grid=(batch, nq, NSA_GQA),
        in_specs=[pl.BlockSpec((tile, LANES), lambda b, i, r: (b * nq + i, q_col + r)),
                  pl.BlockSpec((tile, LANES), lambda b, i, r: (b * nq + i, 0)),
                  pl.BlockSpec((tile, LANES), lambda b, i, r: (b * nq + i, r)),
                  pl.BlockSpec((tile, LANES), lambda b, i, r: (b * nq + i, r)),
                  kv(ks_col), kv(vs_col), kv(kw_col), kv(vw_col),
                  pl.BlockSpec((2, 3, tile, LANES), lambda b, i, r: (r, 0, 0, 0)),
                  pl.BlockSpec(expand2.shape, lambda b, i, r: (0, 0, 0, 0))],
        out_specs=pl.BlockSpec((tile, LANES), lambda b, i, r: (b * nq + i, r)),
        out_shape=jax.ShapeDtypeStruct((n, NSA_GQA * LANES), BF16),
        scratch_shapes=[pltpu.VMEM((2, tile, 1), F32), pltpu.VMEM((2, tile, 1), F32),
                        pltpu.VMEM((2, tile, LANES), F32)],
        compiler_params=_params("parallel", "parallel", "arbitrary"),
        name="nsa_main",
    )(main, sel, ocmp, gates, main, main, main, main, tz, expand2)


def _xattn_kernel(q_ref, kv_ref, wo_ref, x_ref, o_ref):
    width = XA_HEADS * XA_HEAD_DIM
    outs = []
    for h in range(XA_HEADS):
        q = q_ref[:, h * XA_HEAD_DIM:(h + 1) * XA_HEAD_DIM]
        k = kv_ref[:, h * XA_HEAD_DIM:(h + 1) * XA_HEAD_DIM]
        v = kv_ref[:, width + h * XA_HEAD_DIM:width + (h + 1) * XA_HEAD_DIM]
        s = _dot_nt(q, k) * (XA_HEAD_DIM ** -0.5)
        m = jnp.max(s, axis=-1, keepdims=True)
        p = jnp.exp(s - m)
        l = jnp.sum(p, axis=-1, keepdims=True)
        outs.append((_dot(p.astype(BF16), v) / l).astype(BF16))
    o = jnp.concatenate(outs, axis=-1)
    o_ref[...] = x_ref[...] + _dot(o, wo_ref[...])


def xattn_residual(q, kv, wo, x, batch, seq_len, mem_len, tm=ROW_TILE):
    n, d = x.shape
    tm = min(tm, seq_len)
    nt = seq_len // tm
    return pl.pallas_call(
        _xattn_kernel,
        grid=(batch, nt),
        in_specs=[pl.BlockSpec((tm, q.shape[1]), lambda b, i: (b * nt + i, 0)),
                  pl.BlockSpec((mem_len, kv.shape[1]), lambda b, i: (b, 0)),
                  pl.BlockSpec(wo.shape, lambda b, i: (0, 0)),
                  pl.BlockSpec((tm, d), lambda b, i: (b * nt + i, 0))],
        out_specs=pl.BlockSpec((tm, d), lambda b, i: (b * nt + i, 0)),
        out_shape=jax.ShapeDtypeStruct((n, d), F32),
        compiler_params=_params("parallel", "parallel"),
        name="xattn_residual",
    )(q, kv, wo, x)


def _mlp_kernel(x_ref, g_ref, w1_ref, w2_ref, gf_ref, o_ref, hn_ref, acc_ref, *, final_norm):
    f = pl.program_id(1)

    @pl.when(f == 0)
    def _():
        hn_ref[...] = _rmsnorm(x_ref[...], g_ref[...]).astype(BF16)
        acc_ref[...] = x_ref[...]

    h = jnp.maximum(_dot(hn_ref[...], w1_ref[...]), 0.0)
    acc_ref[...] += _dot((h * h).astype(BF16), w2_ref[...])

    @pl.when(f == pl.num_programs(1) - 1)
    def _():
        y = acc_ref[...]
        if final_norm:
            y = _rmsnorm(y, gf_ref[...])
        o_ref[...] = y


def mlp_residual(x, g, w1, w2, gf, final_norm, tm=MLP_ROW_TILE, tf=MLP_FF_TILE):
    n, d = x.shape
    ff = w1.shape[1]
    tm = min(tm, n)
    tf = min(tf, ff)
    return pl.pallas_call(
        functools.partial(_mlp_kernel, final_norm=final_norm),
        grid=(n // tm, ff // tf),
        in_specs=[pl.BlockSpec((tm, d), lambda i, f: (i, 0)),
                  pl.BlockSpec((1, d), lambda i, f: (0, 0)),
                  pl.BlockSpec((d, tf), lambda i, f: (0, f)),
                  pl.BlockSpec((tf, d), lambda i, f: (f, 0)),
                  pl.BlockSpec((1, d), lambda i, f: (0, 0))],
        out_specs=pl.BlockSpec((tm, d), lambda i, f: (i, 0)),
        out_shape=jax.ShapeDtypeStruct((n, d), F32),
        scratch_shapes=[pltpu.VMEM((tm, d), BF16), pltpu.VMEM((tm, d), F32)],
        compiler_params=_params("parallel", "arbitrary"),
        name="mlp_residual",
    )(x, g, w1, w2, gf)


def _static_tables(seq_len):
    tile = NSA_TILE
    n_chunks = seq_len // CMP_STRIDE
    n_cmp = (seq_len - CMP_BLOCK) // CMP_STRIDE + 1
    n_slc = seq_len // SLC_BLOCK
    assert n_chunks == LANES and 2 * n_slc <= LANES
    t = np.arange(seq_len)[:, None]
    c = np.arange(n_chunks)[None, :]
    cmp_end = c * CMP_STRIDE + CMP_BLOCK - 1
    cdist = t - cmp_end
    cmp_idx = np.where((cdist >= 0) & (c < n_cmp), _t5_bucket_np(cdist), -1).astype(np.int32)
    tl = np.arange(tile)[:, None]
    sl = np.arange(tile)[None, :]
    tz_idx = np.concatenate([_t5_bucket_np(tl - sl), _t5_bucket_np(tl - sl + tile),
                             _t5_bucket_np(np.full((tile, tile), 2 * tile))], axis=0).astype(np.int32)
    cmp_start = np.arange(n_cmp) * CMP_STRIDE
    cmp_stop = cmp_start + CMP_BLOCK - 1
    slc_start = np.arange(n_slc) * SLC_BLOCK
    slc_stop = slc_start + SLC_BLOCK - 1
    ov = ((cmp_start[:, None] <= slc_stop[None, :]) & (cmp_stop[:, None] >= slc_start[None, :]))
    overlap2 = np.zeros((2, LANES, LANES), np.float32)
    expand2 = np.zeros((2, LANES, seq_len), np.float32)
    key_blk = np.arange(seq_len) // SLC_BLOCK
    for a in range(2):
        overlap2[a, :n_cmp, a * n_slc:(a + 1) * n_slc] = ov
        expand2[a, a * n_slc + key_blk, np.arange(seq_len)] = 1.0
    expand2 = expand2.reshape(2, LANES, seq_len // tile, tile).transpose(0, 2, 1, 3)
    return cmp_idx, tz_idx, overlap2, expand2, n_cmp


def _tri_incl(nn):
    j = np.arange(nn)[:, None]
    s = np.arange(nn)[None, :]
    return (j <= s).astype(np.float32)


def _tri_later(nn):
    j = np.arange(nn)[:, None]
    s = np.arange(nn)[None, :]
    return (j > s).astype(np.float32)


_NSA_HEAD_PERM = [a * NSA_GQA + r for r in range(NSA_GQA) for a in range(NSA_GROUPS)]


def _perm_head_cols(w, perm):
    d = w.shape[0]
    return w.reshape(d, len(perm), HEAD_DIM)[:, perm, :].reshape(d, len(perm) * HEAD_DIM)


def _block_diag2(m):
    z = jnp.zeros_like(m)
    return jnp.concatenate([jnp.concatenate([m, z], axis=1), jnp.concatenate([z, m], axis=1)], axis=0)


def _compress_weights(pe, w1, w2):
    half = CMP_BLOCK // 2
    pe2 = jnp.concatenate([pe, pe], axis=1).reshape(2, half * LANES)
    w1e = jax.vmap(_block_diag2)(w1).reshape(2, half * LANES, LANES)
    return pe2.astype(F32), w1e.astype(BF16), _block_diag2(w2).astype(BF16)


def kernel(x, mem, rel_bias, mem_norm_g, norm_mix_g, norm_xattn_g, norm_mlp_g, final_norm_g, w_in_even, b_forget, cmp_pe_k, cmp_w1_k, cmp_w2_k, cmp_pe_v, cmp_w1_v, cmp_w2_v, w_out_even, w_in_odd, w_out_odd, xa_wq, xa_wkv, xa_wo, mlp_w1, mlp_w2):
    batch, seq_len, d = x.shape
    mem_len = mem.shape[1]
    depth = norm_mix_g.shape[0]
    n = batch * seq_len
    fox_w = FOX_HEADS * HEAD_DIM
    nsa_w = NSA_HEADS * HEAD_DIM
    kv_w = NSA_GROUPS * HEAD_DIM
    splits = np.cumsum([fox_w, fox_w, fox_w, FOX_HEADS, nsa_w, kv_w, kv_w, kv_w, kv_w, kv_w, kv_w])

    cmp_idx, tz_idx, overlap2, expand2, n_cmp = _static_tables(seq_len)
    tab = rel_bias.astype(F32)[:, _NSA_HEAD_PERM]
    cmp_bias = bucket_lookup(tab, jnp.asarray(cmp_idx))
    tz = bucket_lookup(tab, jnp.asarray(tz_idx), rows=NSA_TILE)
    tz = tz.reshape(NSA_HEADS, 3, NSA_TILE, LANES)
    overlap2 = jnp.asarray(overlap2, BF16)
    expand2 = jnp.asarray(expand2, BF16)
    tri_u = jnp.asarray(_tri_incl(LANES), BF16)
    tri_later = jnp.asarray(_tri_later(min(ATT_TILE, seq_len)), BF16)

    xf = x.reshape(n, d)
    memf = mem.reshape(batch * mem_len, d)
    row = lambda v: v.reshape(1, -1).astype(F32)

    for layer in range(depth):
        g_mix = row(norm_mix_g[layer])
        if layer % 2 == 0:
            e = layer // 2
            (w_fq, w_fk, w_fv, w_ff, w_nq, w_kc, w_vc, w_ks, w_vs, w_kw, w_vw, w_ng) = jnp.split(
                w_in_even[e], splits, axis=1)
            w_main = jnp.concatenate([w_fq, w_fk, w_fv, _perm_head_cols(w_nq, _NSA_HEAD_PERM),
                                      w_ks, w_vs, w_kw, w_vw], axis=1).astype(BF16)
            w_ng4 = w_ng.reshape(d, NSA_GROUPS, NSA_GQA, 3).transpose(0, 2, 3, 1).reshape(d, NSA_GQA, 6)
            w_ng4 = jnp.pad(w_ng4, ((0, 0), (0, 0), (0, LANES - 6))).reshape(d, NSA_GQA * LANES)
            main, kc_in, vc_in, gates, fcum = even_proj(
                xf, g_mix, w_main, w_kc.astype(BF16), w_vc.astype(BF16), w_ng4.astype(BF16),
                w_ff.T.astype(BF16), b_forget[e].reshape(-1, 1).astype(F32), tri_u, seq_len)
            att_tile = min(ATT_TILE, seq_len)
            f_pairs = fcum.reshape(FOX_HEADS // 2, 2, n // att_tile, att_tile).transpose(0, 2, 1, 3)
            nblk = fox_w // LANES
            fox = fox_attention(main, f_pairs, batch, seq_len, 0, nblk, 2 * nblk, FOX_HEADS // 2)

            pek, w1k, w2k = _compress_weights(cmp_pe_k[e], cmp_w1_k[e], cmp_w2_k[e])
            pev, w1v, w2v = _compress_weights(cmp_pe_v[e], cmp_w1_v[e], cmp_w2_v[e])
            chunk_w = CMP_STRIDE * LANES
            kc, vc = compress(kc_in.reshape(n // CMP_STRIDE, chunk_w), vc_in.reshape(n // CMP_STRIDE, chunk_w),
                              pek, pev, w1k, w1v, w2k, w2v, batch, n_cmp)
            q_col = 3 * nblk
            ocmp, sel = nsa_cmp_select(main, kc, vc, cmp_bias, overlap2, batch, seq_len, q_col)
            kv_col = q_col + NSA_GQA
            nsa = nsa_main(main, sel, ocmp, gates, tz, expand2, batch, seq_len, q_col,
                           kv_col, kv_col + 1, kv_col + 2, kv_col + 3)
            w_o = w_out_even[e]
            w_o_nsa = w_o[fox_w:].reshape(NSA_HEADS, HEAD_DIM, d)[jnp.asarray(_NSA_HEAD_PERM)].reshape(nsa_w, d)
            a_list = [fox, nsa]
            w_list = [w_o[:fox_w].astype(BF16), w_o_nsa.astype(BF16)]
        else:
            o = layer // 2
            proj = norm_matmul(xf, g_mix, w_in_odd[o].astype(BF16))
            nblk = SB_HEADS * HEAD_DIM // LANES
            sb = sb_attention(proj, batch, seq_len, 0, nblk, 2 * nblk, SB_HEADS // 2, tri_later)
            a_list = [sb]
            w_list = [w_out_odd[o].astype(BF16)]

        x1, qx = proj_residual_q(a_list, w_list, xf, row(norm_xattn_g[layer]), xa_wq[layer].astype(BF16))
        kv_mem = norm_matmul(memf, row(mem_norm_g), xa_wkv[layer].astype(BF16))
        x2 = xattn_residual(qx, kv_mem, xa_wo[layer].astype(BF16), x1, batch, seq_len, mem_len)
        last = layer == depth - 1
        xf = mlp_residual(x2, row(norm_mlp_g[layer]), mlp_w1[layer].astype(BF16), mlp_w2[layer].astype(BF16),
                          row(final_norm_g), final_norm=last)
    return xf.reshape(batch, seq_len, d)
```

```python
import functools
import math

import numpy as np
import jax
import jax.numpy as jnp
from jax import lax
from jax.experimental import pallas as pl
from jax.experimental.pallas import tpu as pltpu

F32 = jnp.float32
BF16 = jnp.bfloat16

LANES = 128
HEAD_DIM = 64
FOX_HEADS = 8
NSA_HEADS = 8
NSA_GROUPS = 2
NSA_GQA = NSA_HEADS // NSA_GROUPS
SB_HEADS = 16
CMP_BLOCK = 32
CMP_STRIDE = 16
SLC_BLOCK = 64
SLC_TOPK = 8
WINDOW = 256
N_BUCKETS = 32
MAX_DISTANCE = 128
XA_HEADS = 4
XA_HEAD_DIM = 128
EPS = 1e-6
NEG = -1e30
FORCED_SCORE = 1e4
EXP_UNDERFLOW = -104.0

ATT_TILE = 256
ROW_TILE = 512
MLP_ROW_TILE = 1024
MLP_FF_TILE = 512
GATE_ROWS = 8


def _dot(a, b):
    return jnp.dot(a, b, preferred_element_type=F32)


def _dot_nt(a, b):
    return lax.dot_general(a, b, (((1,), (1,)), ((), ())), preferred_element_type=F32)


def _split3(x):
    hi = x.astype(BF16)
    r1 = x - hi.astype(F32)
    mid = r1.astype(BF16)
    lo = (r1 - mid.astype(F32)).astype(BF16)
    return hi, mid, lo


def _dot_exact01(x, m01):
    hi, mid, lo = _split3(x)
    return _dot(hi, m01) + _dot(mid, m01) + _dot(lo, m01)


def _dot_01exact(m01, x):
    hi, mid, lo = _split3(x)
    return _dot(m01, hi) + _dot(m01, mid) + _dot(m01, lo)


def _rmsnorm(x, g):
    ms = jnp.mean(x * x, axis=-1, keepdims=True)
    return x * lax.rsqrt(ms + EPS) * g


def _sigmoid(x):
    return 1.0 / (1.0 + jnp.exp(-x))


def _softplus(x):
    return jnp.maximum(x, 0.0) + jnp.log1p(jnp.exp(-jnp.abs(x)))


def _split_pair(q, scale):
    lo = lax.broadcasted_iota(jnp.int32, q.shape, 1) < HEAD_DIM
    qs = q * jnp.asarray(scale, q.dtype)
    zero = jnp.zeros_like(qs)
    return jnp.where(lo, qs, zero), jnp.where(lo, zero, qs)


def _row_lo(shape):
    return lax.broadcasted_iota(jnp.int32, shape, 0) < HEAD_DIM


def _t5_bucket_np(dist):
    dist = np.maximum(dist, 0)
    max_exact = N_BUCKETS // 2
    d_f = np.maximum(dist, 1).astype(np.float64)
    large = max_exact + (np.log(d_f / max_exact) / math.log(MAX_DISTANCE / max_exact)
                         * (N_BUCKETS - max_exact)).astype(np.int32)
    large = np.minimum(large, N_BUCKETS - 1)
    return np.where(dist < max_exact, dist, large).astype(np.int32)


def _params(*sem):
    return pltpu.CompilerParams(dimension_semantics=sem)


def _full(a):
    return pl.BlockSpec(a.shape, lambda *_: (0,) * a.ndim)


def _norm_proj_kernel(x_ref, g_ref, w_ref, wt_ref, o_ref, ot_ref, *, tile):
    xn = _rmsnorm(x_ref[...], g_ref[...]).astype(BF16)
    o_ref[...] = _dot(xn, w_ref[...]).astype(o_ref.dtype)
    if wt_ref is not None:
        t = _dot_nt(wt_ref[...], xn).astype(ot_ref.dtype)
        for c in range(t.shape[1] // tile):
            ot_ref[c] = t[:, c * tile:(c + 1) * tile]


def norm_proj(x, g, w, w_t=None, tile=ATT_TILE, tm=ROW_TILE):
    n, d = x.shape
    p = w.shape[1]
    tm = min(tm, n)
    in_specs = [pl.BlockSpec((tm, d), lambda i: (i, 0)), _full(g), _full(w)]
    out_specs = [pl.BlockSpec((tm, p), lambda i: (i, 0))]
    out_shape = [jax.ShapeDtypeStruct((n, p), BF16)]
    args = [x, g, w]
    if w_t is None:
        body = lambda x_ref, g_ref, w_ref, o_ref: _norm_proj_kernel(
            x_ref, g_ref, w_ref, None, o_ref, None, tile=tile)
    else:
        pt = w_t.shape[0]
        in_specs.append(_full(w_t))
        out_specs.append(pl.BlockSpec((tm // tile, pt, tile), lambda i: (i, 0, 0)))
        out_shape.append(jax.ShapeDtypeStruct((n // tile, pt, tile), BF16))
        args.append(w_t)
        body = functools.partial(_norm_proj_kernel, tile=tile)
    out = pl.pallas_call(
        body, grid=(n // tm,), in_specs=in_specs, out_specs=out_specs, out_shape=out_shape,
        compiler_params=_params("parallel"), name="norm_proj",
    )(*args)
    return out if w_t is not None else out[0]


def _even_proj_kernel(x_ref, g_ref, w_ref, wt_ref, wkc_ref, wvc_ref, wff_ref, bf_ref, wng_ref, tri_ref,
                      main_ref, vt_ref, kc_ref, vc_ref, f_ref, gate_ref, carry_ref, *, tiles_per_seq, tile):
    i = pl.program_id(0)
    xn = _rmsnorm(x_ref[...], g_ref[...]).astype(BF16)
    main_ref[...] = _dot(xn, w_ref[...]).astype(BF16)
    t = _dot_nt(wt_ref[...], xn).astype(BF16)
    for c in range(t.shape[1] // tile):
        vt_ref[c] = t[:, c * tile:(c + 1) * tile]
    kc_ref[...] = _dot(xn, wkc_ref[...]).astype(BF16)
    vc_ref[...] = _dot(xn, wvc_ref[...]).astype(BF16)
    gate_ref[...] = _sigmoid(_dot_nt(wng_ref[...], xn))
    ff = _dot(xn, wff_ref[...]) + bf_ref[...]
    logf = jnp.minimum(ff, 0.0) - jnp.log1p(jnp.exp(-jnp.abs(ff)))

    @pl.when(i % tiles_per_seq == 0)
    def _():
        carry_ref[...] = jnp.zeros_like(carry_ref)

    cs = _dot_01exact(tri_ref[...], logf) + carry_ref[0:1, :]
    f_ref[...] = cs
    carry_ref[...] = jnp.broadcast_to(cs[cs.shape[0] - 1:, :], carry_ref.shape)


def even_proj(x, g, w_main, w_t, w_kc, w_vc, w_ff, b_f, w_ng_t, tri_l, seq_len, tile=ATT_TILE, tm=ROW_TILE):
    n, d = x.shape
    tm = min(tm, seq_len)
    pm = w_main.shape[1]
    pt = w_t.shape[0]
    gr = w_ng_t.shape[0]
    row = lambda width: pl.BlockSpec((tm, width), lambda i: (i, 0))
    return pl.pallas_call(
        functools.partial(_even_proj_kernel, tiles_per_seq=seq_len // tm, tile=tile),
        grid=(n // tm,),
        in_specs=[row(d), _full(g), _full(w_main), _full(w_t), _full(w_kc), _full(w_vc), _full(w_ff),
                  _full(b_f), _full(w_ng_t), _full(tri_l)],
        out_specs=[row(pm),
                   pl.BlockSpec((tm // tile, pt, tile), lambda i: (i, 0, 0)),
                   row(LANES), row(LANES), row(LANES),
                   pl.BlockSpec((gr, tm), lambda i: (0, i))],
        out_shape=[jax.ShapeDtypeStruct((n, pm), BF16),
                   jax.ShapeDtypeStruct((n // tile, pt, tile), BF16),
                   jax.ShapeDtypeStruct((n, LANES), BF16),
                   jax.ShapeDtypeStruct((n, LANES), BF16),
                   jax.ShapeDtypeStruct((n, LANES), F32),
                   jax.ShapeDtypeStruct((gr, n), F32)],
        scratch_shapes=[pltpu.VMEM((8, LANES), F32)],
        compiler_params=_params("arbitrary"),
        name="even_proj",
    )(x, g, w_main, w_t, w_kc, w_vc, w_ff, b_f, w_ng_t, tri_l)


def _proj_residual_kernel(*refs, n_in):
    a_refs = refs[:n_in]
    w_refs = refs[n_in:2 * n_in]
    x_ref, g_ref, wq_ref, o_ref, q_ref = refs[2 * n_in:]
    acc = x_ref[...]
    for a_ref, w_ref in zip(a_refs, w_refs):
        acc = acc + _dot(a_ref[...], w_ref[...])
    o_ref[...] = acc
    q_ref[...] = _dot(_rmsnorm(acc, g_ref[...]).astype(BF16), wq_ref[...]).astype(BF16)


def proj_residual_q(a_list, w_list, x, g2, wq, tm=ROW_TILE):
    n, d = x.shape
    tm = min(tm, n)
    row = lambda a: pl.BlockSpec((tm, a.shape[1]), lambda i: (i, 0))
    return pl.pallas_call(
        functools.partial(_proj_residual_kernel, n_in=len(a_list)),
        grid=(n // tm,),
        in_specs=[row(a) for a in a_list] + [_full(w) for w in w_list] + [row(x), _full(g2), _full(wq)],
        out_specs=[row(x), pl.BlockSpec((tm, wq.shape[1]), lambda i: (i, 0))],
        out_shape=[jax.ShapeDtypeStruct((n, d), F32),
                   jax.ShapeDtypeStruct((n, wq.shape[1]), BF16)],
        compiler_params=_params("parallel"),
        name="proj_residual_q",
    )(*a_list, *w_list, x, g2, wq)


def _bucket_lookup_kernel(tab_ref, idx_ref, o_ref):
    h = pl.program_id(0)
    idx = idx_ref[...]
    out = jnp.full(idx.shape, NEG, F32)
    for b in range(N_BUCKETS):
        out = jnp.where(idx == b, tab_ref[b, h], out)
    o_ref[0] = out


def bucket_lookup(tab, idx, rows):
    r, c = idx.shape
    h = tab.shape[1]
    return pl.pallas_call(
        _bucket_lookup_kernel,
        grid=(h, r // rows),
        in_specs=[pl.BlockSpec(memory_space=pltpu.SMEM),
                  pl.BlockSpec((rows, c), lambda hh, j: (j, 0))],
        out_specs=pl.BlockSpec((1, rows, c), lambda hh, j: (hh, j, 0)),
        out_shape=jax.ShapeDtypeStruct((h, r, c), F32),
        compiler_params=_params("parallel", "parallel"),
        name="bucket_lookup",
    )(tab, idx)


def _fox_kernel(q_ref, k_ref, vt_ref, f_ref, o_ref, fb_ref, acc_ref, *, tile, n_tiles):
    p = pl.program_id(1)
    i = pl.program_id(2)

    @pl.when(i == 0)
    def _():
        rows = lax.broadcasted_iota(jnp.int32, (LANES, tile), 0)
        for a in range(2):
            pick = jnp.where(rows == 2 * p + a, 1.0, 0.0).astype(BF16)
            for c in range(n_tiles):
                fb_ref[a, c * tile:(c + 1) * tile, :] = _dot_exact01(f_ref[c * tile:(c + 1) * tile, :], pick)

    heads = _split_pair(q_ref[...], HEAD_DIM ** -0.5)
    acc_ref[...] = jnp.zeros(acc_ref.shape, F32)
    key = lax.broadcasted_iota(jnp.int32, (tile, tile), 0)
    qry = lax.broadcasted_iota(jnp.int32, (tile, tile), 1)

    def scores(j):
        ks = k_ref[pl.ds(j * tile, tile), :]
        return tuple(_dot_nt(ks, heads[a]) for a in range(2))

    def consume(j, qk, stats, masked):
        vt = vt_ref[j]
        new, alphas, pvs = [], [], []
        for a in range(2):
            m_prev, l_prev = stats[a]
            s = qk[a] - fb_ref[a, pl.ds(j * tile, tile), :]
            if masked:
                s = jnp.where(key <= qry, s, NEG)
            m_new = jnp.maximum(m_prev, jnp.max(s, axis=0, keepdims=True))
            alpha = jnp.exp(m_prev - m_new)
            pr = jnp.exp(s - m_new)
            new.append((m_new, alpha * l_prev + jnp.sum(pr, axis=0, keepdims=True)))
            alphas.append(alpha)
            pvs.append(_dot(vt, pr.astype(BF16)))
        for a in range(2):
            acc_ref[a] = alphas[a] * acc_ref[a] + pvs[a]
        return tuple(new)

    def body(j, carry):
        qk, stats = carry
        qk_next = scores(j + 1)
        return qk_next, consume(j, qk, stats, False)

    init = tuple((jnp.full((1, tile), NEG, F32), jnp.zeros((1, tile), F32)) for _ in range(2))
    qk, stats = lax.fori_loop(0, i, body, (scores(0), init))
    (_, l0), (_, l1) = consume(i, qk, stats, True)
    o = jnp.where(_row_lo((LANES, tile)), acc_ref[0] / l0, acc_ref[1] / l1)
    o_ref[...] = o.T.astype(o_ref.dtype)


def fox_attention(main, vt, fcum, batch, seq_len, q_col, k_col, v_row, n_pairs, tile=ATT_TILE):
    nq = seq_len // tile
    n = batch * seq_len
    return pl.pallas_call(
        functools.partial(_fox_kernel, tile=tile, n_tiles=nq),
        grid=(batch, n_pairs, nq),
        in_specs=[pl.BlockSpec((tile, LANES), lambda b, p, i: (b * nq + i, q_col + p)),
                  pl.BlockSpec((seq_len, LANES), lambda b, p, i: (b, k_col + p)),
                  pl.BlockSpec((nq, LANES, tile), lambda b, p, i: (b, v_row + p, 0)),
                  pl.BlockSpec((seq_len, LANES), lambda b, p, i: (b, 0))],
        out_specs=pl.BlockSpec((tile, LANES), lambda b, p, i: (b * nq + i, p)),
        out_shape=jax.ShapeDtypeStruct((n, n_pairs * LANES), BF16),
        scratch_shapes=[pltpu.VMEM((2, seq_len, tile), F32), pltpu.VMEM((2, LANES, tile), F32)],
        compiler_params=_params("parallel", "arbitrary", "arbitrary"),
        name="fox_attention",
    )(main, main, vt, fcum)


def _sb_kernel(q_ref, k_ref, vt_ref, tri_ref, o_ref, acc_ref, *, tile):
    i = pl.program_id(2)
    heads = _split_pair(q_ref[...], HEAD_DIM ** -0.5)
    acc_ref[...] = jnp.zeros(acc_ref.shape, F32)
    tri = tri_ref[...]
    key = lax.broadcasted_iota(jnp.int32, (tile, tile), 0)
    qry = lax.broadcasted_iota(jnp.int32, (tile, tile), 1)

    def scores(j):
        ks = k_ref[pl.ds(j * tile, tile), :]
        return tuple(_dot_nt(ks, heads[a]) for a in range(2))

    def consume(j, zz, rs, masked):
        vt = vt_ref[j]
        lgs, base, later, pvs = [], [], [], []
        for a in range(2):
            sp = _softplus(zz[a])
            lg = -sp
            if masked:
                lg = jnp.where(key < qry, lg, 0.0)
            lgs.append(lg)
            base.append(zz[a] - sp)
        for a in range(2):
            hi, mid, _ = _split3(lgs[a])
            later.append(_dot(tri, hi) + _dot(tri, mid))
        for a in range(2):
            wgt = jnp.exp(base[a] + later[a] + rs[a])
            if masked:
                wgt = jnp.where(key < qry, wgt, 0.0)
            pvs.append(_dot(vt, wgt.astype(BF16)))
        for a in range(2):
            acc_ref[a] = acc_ref[a] + pvs[a]
        return tuple(rs[a] + jnp.sum(lgs[a], axis=0, keepdims=True) for a in range(2))

    def live(rs):
        return (jnp.max(jnp.maximum(rs[0], rs[1])) > EXP_UNDERFLOW).astype(jnp.int32)

    zero = jnp.zeros((1, tile), F32)
    zz_diag = scores(i)
    zz_next = scores(jnp.maximum(i - 1, 0))
    rs = consume(i, zz_diag, (zero, zero), True)

    def cond(state):
        j, alive, _, _ = state
        return (j >= 0) & (alive > 0)

    def body(state):
        j, _, rs, zz = state
        zz_next = scores(jnp.maximum(j - 1, 0))
        rs = consume(j, zz, rs, False)
        return j - 1, live(rs), rs, zz_next

    lax.while_loop(cond, body, (i - 1, live(rs), rs, zz_next))
    o = jnp.where(_row_lo((LANES, tile)), acc_ref[0], acc_ref[1])
    o_ref[...] = o.T.astype(o_ref.dtype)


def sb_attention(main, vt, batch, seq_len, q_col, k_col, n_pairs, tri, tile=ATT_TILE):
    nq = seq_len // tile
    n = batch * seq_len
    return pl.pallas_call(
        functools.partial(_sb_kernel, tile=tile),
        grid=(batch, n_pairs, nq),
        in_specs=[pl.BlockSpec((tile, LANES), lambda b, p, i: (b * nq + i, q_col + p)),
                  pl.BlockSpec((seq_len, LANES), lambda b, p, i: (b, k_col + p)),
                  pl.BlockSpec((nq, LANES, tile), lambda b, p, i: (b, p, 0)),
                  _full(tri)],
        out_specs=pl.BlockSpec((tile, LANES), lambda b, p, i: (b * nq + i, p)),
        out_shape=jax.ShapeDtypeStruct((n, n_pairs * LANES), BF16),
        scratch_shapes=[pltpu.VMEM((2, LANES, tile), F32)],
        compiler_params=_params("parallel", "parallel", "arbitrary"),
        name="sb_attention",
    )(main, main, vt, tri)


def _gelu_tanh(x):
    return 0.5 * x * (1.0 + jnp.tanh(math.sqrt(2.0 / math.pi) * (x + 0.044715 * (x * x * x))))


def _compress_kernel(xk_ref, xv_ref, pek_ref, pev_ref, w1k_ref, w1v_ref, w2k_ref, w2v_ref,
                     ok_ref, ov_ref, *, n_cmp):
    def one(x_ref, pe_ref, w1_ref, w2_ref):
        x = x_ref[...].astype(F32)
        xa = (x + pe_ref[0:1, :]).astype(BF16)
        xb = (x + pe_ref[1:2, :]).astype(BF16)
        ha = _dot(xa, w1_ref[0])
        hb = _dot(xb, w1_ref[1])
        rows = ha.shape[0]
        hb = pltpu.roll(hb, rows - 1, 0)
        h = _gelu_tanh(ha + hb)
        out = _dot(h.astype(BF16), w2_ref[...])
        ridx = lax.broadcasted_iota(jnp.int32, out.shape, 0)
        return jnp.where(ridx < n_cmp, out, 0.0)

    ok_ref[0] = one(xk_ref, pek_ref, w1k_ref, w2k_ref).astype(ok_ref.dtype)
    ov_ref[0] = one(xv_ref, pev_ref, w1v_ref, w2v_ref).T.astype(ov_ref.dtype)


def compress(xk, xv, pek, pev, w1k, w1v, w2k, w2v, batch, n_cmp):
    chunks = xk.shape[0] // batch
    row = pl.BlockSpec((chunks, xk.shape[1]), lambda b: (b, 0))
    return pl.pallas_call(
        functools.partial(_compress_kernel, n_cmp=n_cmp),
        grid=(batch,),
        in_specs=[row, row, _full(pek), _full(pev), _full(w1k), _full(w1v), _full(w2k), _full(w2v)],
        out_specs=[pl.BlockSpec((1, chunks, LANES), lambda b: (b, 0, 0)),
                   pl.BlockSpec((1, LANES, chunks), lambda b: (b, 0, 0))],
        out_shape=[jax.ShapeDtypeStruct((batch, chunks, LANES), BF16),
                   jax.ShapeDtypeStruct((batch, LANES, chunks), BF16)],
        compiler_params=_params("parallel"),
        name="nsa_compress",
    )(xk, xv, pek, pev, w1k, w1v, w2k, w2v)


def _nsa_cmp_kernel(q_ref, kc_ref, vct_ref, cb_ref, ov_ref, ocmp_ref, sel_ref, *, tile, n_blocks):
    i = pl.program_id(1)
    kc = kc_ref[0]
    vct = vct_ref[0]
    pcsum = [jnp.zeros((LANES, tile), F32), jnp.zeros((LANES, tile), F32)]
    lo = _row_lo((LANES, tile))
    for r in range(NSA_GQA):
        heads = _split_pair(q_ref[:, r * LANES:(r + 1) * LANES], HEAD_DIM ** -0.5)
        outs = []
        for a in range(2):
            bias = cb_ref[2 * r + a]
            valid = bias > 0.5 * NEG
            s = _dot_nt(kc, heads[a]) + bias
            m = jnp.max(s, axis=0, keepdims=True)
            pr = jnp.where(valid, jnp.exp(s - m), 0.0)
            l = jnp.sum(pr, axis=0, keepdims=True)
            pc = pr / jnp.where(l > 0.0, l, 1.0)
            pcsum[a] = pcsum[a] + pc
            outs.append(_dot(vct, pc.astype(BF16)))
        ocmp_ref[r * LANES:(r + 1) * LANES, :] = jnp.where(lo, outs[0], outs[1])

    score = _dot_01exact(ov_ref[0], pcsum[0]) + _dot_01exact(ov_ref[1], pcsum[1])
    score = score[:2 * n_blocks]
    shape = (2 * n_blocks, tile)
    row = lax.broadcasted_iota(jnp.int32, shape, 0)
    qpos = i * tile + lax.broadcasted_iota(jnp.int32, shape, 1)
    blk = row % n_blocks
    cur = qpos // SLC_BLOCK
    forced = (blk == 0) | (blk == cur) | (blk == cur - 1)
    future = blk > cur
    score = jnp.where(future, -1.0, jnp.where(forced, FORCED_SCORE, score))
    grp0 = row < n_blocks
    cnt = jnp.zeros(shape, F32)
    for mth in range(n_blocks):
        other = jnp.where(grp0, score[mth:mth + 1, :], score[n_blocks + mth:n_blocks + mth + 1, :])
        ahead = (other > score) | ((other == score) & (blk > mth))
        cnt = cnt + jnp.where(ahead, 1.0, 0.0)
    n_top = min(SLC_TOPK, n_blocks)
    sel_ref[...] = jnp.where(cnt < n_top, 1.0, 0.0)


def nsa_cmp_select(main, kc, vct, cmp_bias, overlap2, batch, seq_len, q_col, tile=ATT_TILE):
    nq = seq_len // tile
    n = batch * seq_len
    n_blocks = seq_len // SLC_BLOCK
    qw = NSA_GQA * LANES
    return pl.pallas_call(
        functools.partial(_nsa_cmp_kernel, tile=tile, n_blocks=n_blocks),
        grid=(batch, nq),
        in_specs=[pl.BlockSpec((tile, qw), lambda b, i: (b * nq + i, q_col // NSA_GQA)),
                  pl.BlockSpec((1,) + kc.shape[1:], lambda b, i: (b, 0, 0)),
                  pl.BlockSpec((1,) + vct.shape[1:], lambda b, i: (b, 0, 0)),
                  pl.BlockSpec((NSA_HEADS, cmp_bias.shape[1], tile), lambda b, i: (0, 0, i)),
                  _full(overlap2)],
        out_specs=[pl.BlockSpec((qw, tile), lambda b, i: (0, b * nq + i)),
                   pl.BlockSpec((2 * n_blocks, tile), lambda b, i: (0, b * nq + i))],
        out_shape=[jax.ShapeDtypeStruct((qw, n), F32),
                   jax.ShapeDtypeStruct((2 * n_blocks, n), F32)],
        compiler_params=_params("parallel", "parallel"),
        name="nsa_cmp_select",
    )(main, kc, vct, cmp_bias, overlap2)


def _nsa_main_kernel(far_ref, q_ref, sel_ref, ocmp_ref, gate_ref, ks_ref, vst_ref, kw_ref, vwt_ref, tz_ref,
                     o_ref, acc_ref, *, tile, n_blocks):
    i = pl.program_id(1)
    r = pl.program_id(2)
    heads = _split_pair(q_ref[...], HEAD_DIM ** -0.5)
    key = lax.broadcasted_iota(jnp.int32, (tile, tile), 0)
    qry = lax.broadcasted_iota(jnp.int32, (tile, tile), 1)
    lo = _row_lo((LANES, tile))
    blocks_per_tile = tile // SLC_BLOCK
    j1 = jnp.maximum(i - 1, 0)
    has_prev = jnp.where(i >= 1, tile, 0)

    def chosen(a, j):
        rows = [jnp.broadcast_to(sel_ref[pl.ds(a * n_blocks + j * blocks_per_tile + nb, 1), :],
                                 (SLC_BLOCK, tile)) for nb in range(blocks_per_tile)]
        return jnp.concatenate(rows, axis=0) > 0.5

    def scores(k_ref_, j):
        ks = k_ref_[pl.ds(j * tile, tile), :]
        return tuple(_dot_nt(ks, heads[a]) for a in range(2))

    def consume(j, qk, biases, masks, stats):
        vt = vst_ref[j]
        new, alphas, pvs = [], [], []
        for a in range(2):
            m_prev, l_prev = stats[a]
            s = jnp.where(masks[a], qk[a] + biases[a], NEG)
            m_new = jnp.maximum(m_prev, jnp.max(s, axis=0, keepdims=True))
            alpha = jnp.exp(m_prev - m_new)
            pr = jnp.where(masks[a], jnp.exp(s - m_new), 0.0)
            new.append((m_new, alpha * l_prev + jnp.sum(pr, axis=0, keepdims=True)))
            alphas.append(alpha)
            pvs.append(_dot(vt, pr.astype(BF16)))
        for a in range(2):
            acc_ref[a] = alphas[a] * acc_ref[a] + pvs[a]
        return tuple(new)

    acc_ref[...] = jnp.zeros(acc_ref.shape, F32)

    def far_step(j, carry):
        qk, stats = carry
        qk_next = scores(ks_ref, j + 1)
        far = tuple(far_ref[2 * r + a] for a in range(2))
        return qk_next, consume(j, qk, far, tuple(chosen(a, j) for a in range(2)), stats)

    init = tuple((jnp.full((1, tile), NEG, F32), jnp.zeros((1, tile), F32)) for _ in range(2))
    qk_prev, stats = lax.fori_loop(0, i - 1, far_step, (scores(ks_ref, 0), init))
    qk_diag = scores(ks_ref, i)
    qw_prev = scores(kw_ref, j1)
    qw_diag = scores(kw_ref, i)
    stats = consume(j1, qk_prev, tuple(tz_ref[a, 1] for a in range(2)),
                    tuple(chosen(a, j1) & (key < has_prev) for a in range(2)), stats)
    stats = consume(i, qk_diag, tuple(tz_ref[a, 0] for a in range(2)),
                    tuple(chosen(a, i) & (key <= qry) for a in range(2)), stats)

    mask_prev = (key > qry) & (key < has_prev)
    mask_diag = key <= qry
    o_win, l_win = [], []
    for a in range(2):
        s1 = jnp.where(mask_prev, qw_prev[a] + tz_ref[a, 1], NEG)
        s0 = jnp.where(mask_diag, qw_diag[a] + tz_ref[a, 0], NEG)
        m = jnp.maximum(jnp.max(s0, axis=0, keepdims=True), jnp.max(s1, axis=0, keepdims=True))
        p1 = jnp.where(mask_prev, jnp.exp(s1 - m), 0.0)
        p0 = jnp.exp(s0 - m)
        l_win.append(jnp.sum(p0, axis=0, keepdims=True) + jnp.sum(p1, axis=0, keepdims=True))
        o_win.append(_dot(vwt_ref[j1], p1.astype(BF16)) + _dot(vwt_ref[i], p0.astype(BF16)))
    o_slc = jnp.where(lo, acc_ref[0] / stats[0][1], acc_ref[1] / stats[1][1])
    o_win = jnp.where(lo, o_win[0] / l_win[0], o_win[1] / l_win[1])

    gexp = [jnp.where(lo, gate_ref[2 * c:2 * c + 1, :], gate_ref[2 * c + 1:2 * c + 2, :]) for c in range(3)]
    out = gexp[0] * ocmp_ref[...] + gexp[1] * o_slc + gexp[2] * o_win
    o_ref[...] = out.T.astype(o_ref.dtype)


def nsa_main(far_bias, main, sel, ocmp, gates, vt, tz, batch, seq_len, q_col, ks_col, kw_col, vs_row, vw_row,
             tile=ATT_TILE):
    nq = seq_len // tile
    n = batch * seq_len
    n_blocks = seq_len // SLC_BLOCK
    kv = lambda col: pl.BlockSpec((seq_len, LANES), lambda b, i, r: (b, col))
    vts = lambda rowblk: pl.BlockSpec((nq, LANES, tile), lambda b, i, r: (b, rowblk, 0))
    return pl.pallas_call(
        functools.partial(_nsa_main_kernel, tile=tile, n_blocks=n_blocks),
        grid=(batch, nq, NSA_GQA),
        in_specs=[pl.BlockSpec(memory_space=pltpu.SMEM),
                  pl.BlockSpec((tile, LANES), lambda b, i, r: (b * nq + i, q_col + r)),
                  pl.BlockSpec((2 * n_blocks, tile), lambda b, i, r: (0, b * nq + i)),
                  pl.BlockSpec((LANES, tile), lambda b, i, r: (r, b * nq + i)),
                  pl.BlockSpec((GATE_ROWS, tile), lambda b, i, r: (r, b * nq + i)),
                  kv(ks_col), vts(vs_row), kv(kw_col), vts(vw_row),
                  pl.BlockSpec((2, 2, tile, tile), lambda b, i, r: (r, 0, 0, 0))],
        out_specs=pl.BlockSpec((tile, LANES), lambda b, i, r: (b * nq + i, r)),
        out_shape=jax.ShapeDtypeStruct((n, NSA_GQA * LANES), BF16),
        scratch_shapes=[pltpu.VMEM((2, LANES, tile), F32)],
        compiler_params=_params("parallel", "parallel", "arbitrary"),
        name="nsa_main",
    )(far_bias, main, sel, ocmp, gates, main, vt, main, vt, tz)


def _xattn_kernel(q_ref, kv_ref, wo_ref, x_ref, o_ref):
    width = XA_HEADS * XA_HEAD_DIM
    outs = []
    for h in range(XA_HEADS):
        q = q_ref[:, h * XA_HEAD_DIM:(h + 1) * XA_HEAD_DIM]
        k = kv_ref[:, h * XA_HEAD_DIM:(h + 1) * XA_HEAD_DIM]
        v = kv_ref[:, width + h * XA_HEAD_DIM:width + (h + 1) * XA_HEAD_DIM]
        s = _dot_nt(q, k) * (XA_HEAD_DIM ** -0.5)
        m = jnp.max(s, axis=-1, keepdims=True)
        p = jnp.exp(s - m)
        l = jnp.sum(p, axis=-1, keepdims=True)
        outs.append((_dot(p.astype(BF16), v) / l).astype(BF16))
    o = jnp.concatenate(outs, axis=-1)
    o_ref[...] = x_ref[...] + _dot(o, wo_ref[...])


def xattn_residual(q, kv, wo, x, batch, seq_len, mem_len, tm=ROW_TILE):
    n, d = x.shape
    tm = min(tm, seq_len)
    nt = seq_len // tm
    return pl.pallas_call(
        _xattn_kernel,
        grid=(batch, nt),
        in_specs=[pl.BlockSpec((tm, q.shape[1]), lambda b, i: (b * nt + i, 0)),
                  pl.BlockSpec((mem_len, kv.shape[1]), lambda b, i: (b, 0)),
                  _full(wo),
                  pl.BlockSpec((tm, d), lambda b, i: (b * nt + i, 0))],
        out_specs=pl.BlockSpec((tm, d), lambda b, i: (b * nt + i, 0)),
        out_shape=jax.ShapeDtypeStruct((n, d), F32),
        compiler_params=_params("parallel", "parallel"),
        name="xattn_residual",
    )(q, kv, wo, x)


def _mlp_kernel(x_ref, g_ref, w1_ref, w2_ref, gf_ref, o_ref, hn_ref, acc_ref, *, final_norm):
    f = pl.program_id(1)

    @pl.when(f == 0)
    def _():
        hn_ref[...] = _rmsnorm(x_ref[...], g_ref[...]).astype(BF16)
        acc_ref[...] = x_ref[...]

    h = jnp.maximum(_dot(hn_ref[...], w1_ref[...]), 0.0)
    acc_ref[...] += _dot((h * h).astype(BF16), w2_ref[...])

    @pl.when(f == pl.num_programs(1) - 1)
    def _():
        y = acc_ref[...]
        if final_norm:
            y = _rmsnorm(y, gf_ref[...])
        o_ref[...] = y


def mlp_residual(x, g, w1, w2, gf, final_norm, tm=MLP_ROW_TILE, tf=MLP_FF_TILE):
    n, d = x.shape
    ff = w1.shape[1]
    tm = min(tm, n)
    tf = min(tf, ff)
    return pl.pallas_call(
        functools.partial(_mlp_kernel, final_norm=final_norm),
        grid=(n // tm, ff // tf),
        in_specs=[pl.BlockSpec((tm, d), lambda i, f: (i, 0)),
                  pl.BlockSpec((1, d), lambda i, f: (0, 0)),
                  pl.BlockSpec((d, tf), lambda i, f: (0, f)),
                  pl.BlockSpec((tf, d), lambda i, f: (f, 0)),
                  pl.BlockSpec((1, d), lambda i, f: (0, 0))],
        out_specs=pl.BlockSpec((tm, d), lambda i, f: (i, 0)),
        out_shape=jax.ShapeDtypeStruct((n, d), F32),
        scratch_shapes=[pltpu.VMEM((tm, d), BF16), pltpu.VMEM((tm, d), F32)],
        compiler_params=_params("parallel", "arbitrary"),
        name="mlp_residual",
    )(x, g, w1, w2, gf)


def _static_tables(seq_len):
    tile = ATT_TILE
    n_chunks = seq_len // CMP_STRIDE
    n_cmp = (seq_len - CMP_BLOCK) // CMP_STRIDE + 1
    n_slc = seq_len // SLC_BLOCK
    assert n_chunks == LANES and 2 * n_slc <= LANES and seq_len % tile == 0 and WINDOW == tile
    c = np.arange(n_chunks)[:, None]
    t = np.arange(seq_len)[None, :]
    cdist = t - (c * CMP_STRIDE + CMP_BLOCK - 1)
    cmp_idx = np.where((cdist >= 0) & (c < n_cmp), _t5_bucket_np(cdist), -1).astype(np.int32)
    sl = np.arange(tile)[:, None]
    tl = np.arange(tile)[None, :]
    tz_idx = np.concatenate([_t5_bucket_np(tl - sl), _t5_bucket_np(tl - sl + tile)], axis=0).astype(np.int32)
    far_bucket = int(_t5_bucket_np(np.array([tile + 1]))[0])
    assert far_bucket == int(_t5_bucket_np(np.array([seq_len]))[0])
    cmp_start = np.arange(n_cmp) * CMP_STRIDE
    cmp_stop = cmp_start + CMP_BLOCK - 1
    slc_start = np.arange(n_slc) * SLC_BLOCK
    slc_stop = slc_start + SLC_BLOCK - 1
    ov = ((cmp_start[:, None] <= slc_stop[None, :]) & (cmp_stop[:, None] >= slc_start[None, :]))
    overlap2 = np.zeros((2, LANES, LANES), np.float32)
    for a in range(2):
        overlap2[a, a * n_slc:(a + 1) * n_slc, :n_cmp] = ov.T
    return cmp_idx, tz_idx, far_bucket, overlap2, n_cmp


def _tri_prefix(nn):
    s = np.arange(nn)[:, None]
    j = np.arange(nn)[None, :]
    return (j <= s).astype(np.float32)


def _tri_later(nn):
    s = np.arange(nn)[:, None]
    j = np.arange(nn)[None, :]
    return (j > s).astype(np.float32)


_NSA_HEAD_PERM = [a * NSA_GQA + r for r in range(NSA_GQA) for a in range(NSA_GROUPS)]


def _perm_head_cols(w, perm):
    d = w.shape[0]
    return w.reshape(d, len(perm), HEAD_DIM)[:, perm, :].reshape(d, len(perm) * HEAD_DIM)


def _block_diag2(m):
    z = jnp.zeros_like(m)
    return jnp.concatenate([jnp.concatenate([m, z], axis=1), jnp.concatenate([z, m], axis=1)], axis=0)


def _compress_weights(pe, w1, w2):
    half = CMP_BLOCK // 2
    pe2 = jnp.concatenate([pe, pe], axis=1).reshape(2, half * LANES)
    w1e = jax.vmap(_block_diag2)(w1).reshape(2, half * LANES, LANES)
    return pe2.astype(F32), w1e.astype(BF16), _block_diag2(w2).astype(BF16)


def kernel(x, mem, rel_bias, mem_norm_g, norm_mix_g, norm_xattn_g, norm_mlp_g, final_norm_g, w_in_even, b_forget, cmp_pe_k, cmp_w1_k, cmp_w2_k, cmp_pe_v, cmp_w1_v, cmp_w2_v, w_out_even, w_in_odd, w_out_odd, xa_wq, xa_wkv, xa_wo, mlp_w1, mlp_w2):
    batch, seq_len, d = x.shape
    mem_len = mem.shape[1]
    depth = norm_mix_g.shape[0]
    n = batch * seq_len
    fox_w = FOX_HEADS * HEAD_DIM
    nsa_w = NSA_HEADS * HEAD_DIM
    kv_w = NSA_GROUPS * HEAD_DIM
    splits = np.cumsum([fox_w, fox_w, fox_w, FOX_HEADS, nsa_w, kv_w, kv_w, kv_w, kv_w, kv_w, kv_w])

    cmp_idx, tz_idx, far_bucket, overlap2, n_cmp = _static_tables(seq_len)
    tab = rel_bias.astype(F32)[:, _NSA_HEAD_PERM]
    cmp_bias = bucket_lookup(tab, jnp.asarray(cmp_idx), rows=cmp_idx.shape[0])
    tz = bucket_lookup(tab, jnp.asarray(tz_idx), rows=ATT_TILE)
    tz = tz.reshape(NSA_HEADS, 2, ATT_TILE, ATT_TILE)
    far_bias = tab[far_bucket]
    overlap2 = jnp.asarray(overlap2, BF16)
    tri_prefix = jnp.asarray(_tri_prefix(min(ROW_TILE, seq_len)), BF16)
    tri_later = jnp.asarray(_tri_later(ATT_TILE), BF16)

    xf = x.reshape(n, d)
    memf = mem.reshape(batch * mem_len, d)
    row = lambda v: v.reshape(1, -1).astype(F32)

    for layer in range(depth):
        g_mix = row(norm_mix_g[layer])
        if layer % 2 == 0:
            e = layer // 2
            (w_fq, w_fk, w_fv, w_ff, w_nq, w_kc, w_vc, w_ks, w_vs, w_kw, w_vw, w_ng) = jnp.split(
                w_in_even[e], splits, axis=1)
            w_main = jnp.concatenate([w_fq, w_fk, _perm_head_cols(w_nq, _NSA_HEAD_PERM), w_ks, w_kw],
                                     axis=1).astype(BF16)
            w_t = jnp.concatenate([w_fv, w_vs, w_vw], axis=1).T.astype(BF16)
            w_ng_t = w_ng.reshape(d, NSA_GROUPS, NSA_GQA, 3).transpose(2, 3, 1, 0).reshape(NSA_GQA, 6, d)
            w_ng_t = jnp.pad(w_ng_t, ((0, 0), (0, GATE_ROWS - 6), (0, 0))).reshape(NSA_GQA * GATE_ROWS, d)
            w_ff_p = jnp.pad(w_ff, ((0, 0), (0, LANES - FOX_HEADS)))
            b_f = jnp.pad(b_forget[e].astype(F32), (0, LANES - FOX_HEADS)).reshape(1, LANES)
            main, vt, kc_in, vc_in, fcum, gates = even_proj(
                xf, g_mix, w_main, w_t, w_kc.astype(BF16), w_vc.astype(BF16), w_ff_p.astype(BF16), b_f,
                w_ng_t.astype(BF16), tri_prefix, seq_len)
            nblk = fox_w // LANES
            fox = fox_attention(main, vt, fcum, batch, seq_len, 0, nblk, 0, FOX_HEADS // 2)

            pek, w1k, w2k = _compress_weights(cmp_pe_k[e], cmp_w1_k[e], cmp_w2_k[e])
            pev, w1v, w2v = _compress_weights(cmp_pe_v[e], cmp_w1_v[e], cmp_w2_v[e])
            chunk_w = CMP_STRIDE * LANES
            kc, vct = compress(kc_in.reshape(n // CMP_STRIDE, chunk_w), vc_in.reshape(n // CMP_STRIDE, chunk_w),
                               pek, pev, w1k, w1v, w2k, w2v, batch, n_cmp)
            q_col = 2 * nblk
            ocmp, sel = nsa_cmp_select(main, kc, vct, cmp_bias, overlap2, batch, seq_len, q_col)
            nsa = nsa_main(far_bias, main, sel, ocmp, gates, vt, tz, batch, seq_len, q_col,
                           q_col + NSA_GQA, q_col + NSA_GQA + 1, nblk, nblk + 1)
            w_o = w_out_even[e]
            w_o_nsa = w_o[fox_w:].reshape(NSA_HEADS, HEAD_DIM, d)[jnp.asarray(_NSA_HEAD_PERM)].reshape(nsa_w, d)
            a_list = [fox, nsa]
            w_list = [w_o[:fox_w].astype(BF16), w_o_nsa.astype(BF16)]
        else:
            o = layer // 2
            sb_w = SB_HEADS * HEAD_DIM
            w_in = w_in_odd[o]
            main, vt = norm_proj(xf, g_mix, w_in[:, :2 * sb_w].astype(BF16), w_in[:, 2 * sb_w:].T.astype(BF16))
            nblk = sb_w // LANES
            sb = sb_attention(main, vt, batch, seq_len, 0, nblk, SB_HEADS // 2, tri_later)
            a_list = [sb]
            w_list = [w_out_odd[o].astype(BF16)]

        x1, qx = proj_residual_q(a_list, w_list, xf, row(norm_xattn_g[layer]), xa_wq[layer].astype(BF16))
        kv_mem = norm_proj(memf, row(mem_norm_g), xa_wkv[layer].astype(BF16))
        x2 = xattn_residual(qx, kv_mem, xa_wo[layer].astype(BF16), x1, batch, seq_len, mem_len)
        last = layer == depth - 1
        xf = mlp_residual(x2, row(norm_mlp_g[layer]), mlp_w1[layer].astype(BF16), mlp_w2[layer].astype(BF16),
                          row(final_norm_g), final_norm=last)
    return xf.reshape(batch, seq_len, d)
```

```python
import functools
import math

import numpy as np
import jax
import jax.numpy as jnp
from jax import lax
from jax.experimental import pallas as pl
from jax.experimental.pallas import tpu as pltpu

F32 = jnp.float32
BF16 = jnp.bfloat16

LANES = 128
HEAD_DIM = 64
FOX_HEADS = 8
NSA_HEADS = 8
NSA_GROUPS = 2
NSA_GQA = NSA_HEADS // NSA_GROUPS
SB_HEADS = 16
CMP_BLOCK = 32
CMP_STRIDE = 16
SLC_BLOCK = 64
SLC_TOPK = 8
WINDOW = 256
N_BUCKETS = 32
MAX_DISTANCE = 128
XA_HEADS = 4
XA_HEAD_DIM = 128
EPS = 1e-6
NEG = -1e30
FORCED_SCORE = 1e4
EXP_UNDERFLOW = -104.0

ATT_TILE = 256
ROW_TILE = 512
MLP_ROW_TILE = 1024
MLP_FF_TILE = 512
GATE_ROWS = 8
F_PIECES = 3


def _dot(a, b):
    return jnp.dot(a, b, preferred_element_type=F32)


def _dot_nt(a, b):
    return lax.dot_general(a, b, (((1,), (1,)), ((), ())), preferred_element_type=F32)


def _split3(x):
    hi = x.astype(BF16)
    r1 = x - hi.astype(F32)
    mid = r1.astype(BF16)
    lo = (r1 - mid.astype(F32)).astype(BF16)
    return hi, mid, lo


def _dot_exact01(x, m01):
    hi, mid, lo = _split3(x)
    return _dot(hi, m01) + _dot(mid, m01) + _dot(lo, m01)


def _dot_01exact(m01, x):
    hi, mid, lo = _split3(x)
    return _dot(m01, hi) + _dot(m01, mid) + _dot(m01, lo)


def _rmsnorm(x, g):
    ms = jnp.mean(x * x, axis=-1, keepdims=True)
    return x * lax.rsqrt(ms + EPS) * g


def _sigmoid(x):
    return 1.0 / (1.0 + jnp.exp(-x))


def _softplus(x):
    return jnp.maximum(x, 0.0) + jnp.log1p(jnp.exp(-jnp.abs(x)))


def _split_pair(q, scale):
    lo = lax.broadcasted_iota(jnp.int32, q.shape, 1) < HEAD_DIM
    qs = q * jnp.asarray(scale, q.dtype)
    zero = jnp.zeros_like(qs)
    return jnp.where(lo, qs, zero), jnp.where(lo, zero, qs)


def _row_lo(shape):
    return lax.broadcasted_iota(jnp.int32, shape, 0) < HEAD_DIM


def _t5_bucket_np(dist):
    dist = np.maximum(dist, 0)
    max_exact = N_BUCKETS // 2
    d_f = np.maximum(dist, 1).astype(np.float64)
    large = max_exact + (np.log(d_f / max_exact) / math.log(MAX_DISTANCE / max_exact)
                         * (N_BUCKETS - max_exact)).astype(np.int32)
    large = np.minimum(large, N_BUCKETS - 1)
    return np.where(dist < max_exact, dist, large).astype(np.int32)


def _params(*sem):
    return pltpu.CompilerParams(dimension_semantics=sem)


def _full(a):
    return pl.BlockSpec(a.shape, lambda *_: (0,) * a.ndim)


def _norm_proj_kernel(x_ref, g_ref, w_ref, wt_ref, o_ref, ot_ref, *, tile):
    xn = _rmsnorm(x_ref[...], g_ref[...]).astype(BF16)
    o_ref[...] = _dot(xn, w_ref[...]).astype(o_ref.dtype)
    if wt_ref is not None:
        t = _dot_nt(wt_ref[...], xn).astype(ot_ref.dtype)
        for c in range(t.shape[1] // tile):
            ot_ref[c] = t[:, c * tile:(c + 1) * tile]


def norm_proj(x, g, w, w_t=None, tile=ATT_TILE, tm=ROW_TILE):
    n, d = x.shape
    p = w.shape[1]
    tm = min(tm, n)
    in_specs = [pl.BlockSpec((tm, d), lambda i: (i, 0)), _full(g), _full(w)]
    out_specs = [pl.BlockSpec((tm, p), lambda i: (i, 0))]
    out_shape = [jax.ShapeDtypeStruct((n, p), BF16)]
    args = [x, g, w]
    if w_t is None:
        body = lambda x_ref, g_ref, w_ref, o_ref: _norm_proj_kernel(
            x_ref, g_ref, w_ref, None, o_ref, None, tile=tile)
    else:
        pt = w_t.shape[0]
        in_specs.append(_full(w_t))
        out_specs.append(pl.BlockSpec((tm // tile, pt, tile), lambda i: (i, 0, 0)))
        out_shape.append(jax.ShapeDtypeStruct((n // tile, pt, tile), BF16))
        args.append(w_t)
        body = functools.partial(_norm_proj_kernel, tile=tile)
    out = pl.pallas_call(
        body, grid=(n // tm,), in_specs=in_specs, out_specs=out_specs, out_shape=out_shape,
        compiler_params=_params("parallel"), name="norm_proj",
    )(*args)
    return out if w_t is not None else out[0]


def _even_proj_kernel(x_ref, g_ref, w_ref, wt_ref, wkc_ref, wvc_ref, wff_ref, bf_ref, wng_ref, tri_ref,
                      main_ref, vt_ref, kc_ref, vc_ref, f_ref, gate_ref, carry_ref, *, tiles_per_seq, tile):
    i = pl.program_id(0)
    xn = _rmsnorm(x_ref[...], g_ref[...]).astype(BF16)
    main_ref[...] = _dot(xn, w_ref[...]).astype(BF16)
    t = _dot_nt(wt_ref[...], xn).astype(BF16)
    for c in range(t.shape[1] // tile):
        vt_ref[c] = t[:, c * tile:(c + 1) * tile]
    kc_ref[...] = _dot(xn, wkc_ref[...]).astype(BF16)
    vc_ref[...] = _dot(xn, wvc_ref[...]).astype(BF16)
    gate_ref[...] = _sigmoid(_dot_nt(wng_ref[...], xn))
    ff = _dot(xn, wff_ref[...]) + bf_ref[...]
    logf = jnp.minimum(ff, 0.0) - jnp.log1p(jnp.exp(-jnp.abs(ff)))

    @pl.when(i % tiles_per_seq == 0)
    def _():
        carry_ref[...] = jnp.zeros_like(carry_ref)

    cs = _dot_01exact(tri_ref[...], logf) + carry_ref[0:1, :]
    carry_ref[...] = jnp.broadcast_to(cs[cs.shape[0] - 1:, :], carry_ref.shape)
    hi, mid, lo = _split3(-cs)
    piece = lax.broadcasted_iota(jnp.int32, cs.shape, 1) % F_PIECES
    f_ref[...] = jnp.where(piece == 0, hi, jnp.where(piece == 1, mid, lo))


def even_proj(x, g, w_main, w_t, w_kc, w_vc, w_ff, b_f, w_ng_t, tri_l, seq_len, tile=ATT_TILE, tm=ROW_TILE):
    n, d = x.shape
    tm = min(tm, seq_len)
    pm = w_main.shape[1]
    pt = w_t.shape[0]
    gr = w_ng_t.shape[0]
    row = lambda width: pl.BlockSpec((tm, width), lambda i: (i, 0))
    return pl.pallas_call(
        functools.partial(_even_proj_kernel, tiles_per_seq=seq_len // tm, tile=tile),
        grid=(n // tm,),
        in_specs=[row(d), _full(g), _full(w_main), _full(w_t), _full(w_kc), _full(w_vc), _full(w_ff),
                  _full(b_f), _full(w_ng_t), _full(tri_l)],
        out_specs=[row(pm),
                   pl.BlockSpec((tm // tile, pt, tile), lambda i: (i, 0, 0)),
                   row(LANES), row(LANES), row(LANES),
                   pl.BlockSpec((gr, tm), lambda i: (0, i))],
        out_shape=[jax.ShapeDtypeStruct((n, pm), BF16),
                   jax.ShapeDtypeStruct((n // tile, pt, tile), BF16),
                   jax.ShapeDtypeStruct((n, LANES), BF16),
                   jax.ShapeDtypeStruct((n, LANES), BF16),
                   jax.ShapeDtypeStruct((n, LANES), BF16),
                   jax.ShapeDtypeStruct((gr, n), F32)],
        scratch_shapes=[pltpu.VMEM((8, LANES), F32)],
        compiler_params=_params("arbitrary"),
        name="even_proj",
    )(x, g, w_main, w_t, w_kc, w_vc, w_ff, b_f, w_ng_t, tri_l)


def _proj_residual_kernel(*refs, n_in):
    a_refs = refs[:n_in]
    w_refs = refs[n_in:2 * n_in]
    x_ref, g_ref, wq_ref, o_ref, q_ref = refs[2 * n_in:]
    acc = x_ref[...]
    for a_ref, w_ref in zip(a_refs, w_refs):
        acc = acc + _dot(a_ref[...], w_ref[...])
    o_ref[...] = acc
    q_ref[...] = _dot(_rmsnorm(acc, g_ref[...]).astype(BF16), wq_ref[...]).astype(BF16)


def proj_residual_q(a_list, w_list, x, g2, wq, tm=ROW_TILE):
    n, d = x.shape
    tm = min(tm, n)
    row = lambda a: pl.BlockSpec((tm, a.shape[1]), lambda i: (i, 0))
    return pl.pallas_call(
        functools.partial(_proj_residual_kernel, n_in=len(a_list)),
        grid=(n // tm,),
        in_specs=[row(a) for a in a_list] + [_full(w) for w in w_list] + [row(x), _full(g2), _full(wq)],
        out_specs=[row(x), pl.BlockSpec((tm, wq.shape[1]), lambda i: (i, 0))],
        out_shape=[jax.ShapeDtypeStruct((n, d), F32),
                   jax.ShapeDtypeStruct((n, wq.shape[1]), BF16)],
        compiler_params=_params("parallel"),
        name="proj_residual_q",
    )(*a_list, *w_list, x, g2, wq)


def _bucket_lookup_kernel(tab_ref, idx_ref, o_ref):
    h = pl.program_id(0)
    idx = idx_ref[...]
    out = jnp.full(idx.shape, NEG, F32)
    for b in range(N_BUCKETS):
        out = jnp.where(idx == b, tab_ref[b, h], out)
    o_ref[0] = out


def bucket_lookup(tab, idx, rows):
    r, c = idx.shape
    h = tab.shape[1]
    return pl.pallas_call(
        _bucket_lookup_kernel,
        grid=(h, r // rows),
        in_specs=[pl.BlockSpec(memory_space=pltpu.SMEM),
                  pl.BlockSpec((rows, c), lambda hh, j: (j, 0))],
        out_specs=pl.BlockSpec((1, rows, c), lambda hh, j: (hh, j, 0)),
        out_shape=jax.ShapeDtypeStruct((h, r, c), F32),
        compiler_params=_params("parallel", "parallel"),
        name="bucket_lookup",
    )(tab, idx)


def _fox_kernel(q_ref, k_ref, fs_ref, vt_ref, o_ref, *, tile, n_tiles):
    p = pl.program_id(1)
    lane = lax.broadcasted_iota(jnp.int32, (tile, LANES), 1)
    causal = (lax.broadcasted_iota(jnp.int32, (tile, tile), 0)
              <= lax.broadcasted_iota(jnp.int32, (tile, tile), 1))
    lo = _row_lo((LANES, tile))

    def keys(j):
        sl = slice(j * tile, (j + 1) * tile)
        return jnp.concatenate([k_ref[sl, :], fs_ref[sl, :]], axis=1)

    def queries(i):
        heads = _split_pair(q_ref[i * tile:(i + 1) * tile, :], HEAD_DIM ** -0.5)
        return tuple(jnp.concatenate(
            [heads[a], jnp.where(lane // F_PIECES == 2 * p + a, 1.0, 0.0).astype(BF16)], axis=1)
            for a in range(2))

    tasks = [(i, j) for i in range(n_tiles) for j in range(i + 1)]
    qcache = {}

    def scores(t):
        i, j = tasks[t]
        if i not in qcache:
            qcache[i] = queries(i)
        kj = keys(j)
        return tuple(_dot_nt(kj, qcache[i][a]) for a in range(2))

    qk = scores(0)
    for t, (i, j) in enumerate(tasks):
        qk_next = scores(t + 1) if t + 1 < len(tasks) else None
        if j == 0:
            stats = [(jnp.full((1, tile), NEG, F32), jnp.zeros((1, tile), F32)) for _ in range(2)]
            acc = [jnp.zeros((LANES, tile), F32) for _ in range(2)]
        vt = vt_ref[j]
        alphas, pvs = [], []
        for a in range(2):
            m_prev, l_prev = stats[a]
            s = jnp.where(causal, qk[a], NEG) if j == i else qk[a]
            m_new = jnp.maximum(m_prev, jnp.max(s, axis=0, keepdims=True))
            alpha = jnp.exp(m_prev - m_new)
            pr = jnp.exp(s - m_new)
            stats[a] = (m_new, alpha * l_prev + jnp.sum(pr, axis=0, keepdims=True))
            alphas.append(alpha)
            pvs.append(_dot(vt, pr.astype(BF16)))
        for a in range(2):
            acc[a] = alphas[a] * acc[a] + pvs[a]
        if j == i:
            o = jnp.where(lo, acc[0] / stats[0][1], acc[1] / stats[1][1])
            o_ref[i * tile:(i + 1) * tile, :] = o.T.astype(o_ref.dtype)
        qk = qk_next


def fox_attention(main, fs, vt, batch, seq_len, q_col, k_col, v_row, n_pairs, tile=ATT_TILE):
    nq = seq_len // tile
    n = batch * seq_len
    return pl.pallas_call(
        functools.partial(_fox_kernel, tile=tile, n_tiles=nq),
        grid=(batch, n_pairs),
        in_specs=[pl.BlockSpec((seq_len, LANES), lambda b, p: (b, q_col + p)),
                  pl.BlockSpec((seq_len, LANES), lambda b, p: (b, k_col + p)),
                  pl.BlockSpec((seq_len, LANES), lambda b, p: (b, 0)),
                  pl.BlockSpec((nq, LANES, tile), lambda b, p: (b, v_row + p, 0))],
        out_specs=pl.BlockSpec((seq_len, LANES), lambda b, p: (b, p)),
        out_shape=jax.ShapeDtypeStruct((n, n_pairs * LANES), BF16),
        compiler_params=_params("parallel", "parallel"),
        name="fox_attention",
    )(main, main, fs, vt)


def _sb_kernel(q_ref, k_ref, vt_ref, tri_ref, o_ref, acc_ref, *, tile, n_tiles):
    tri = tri_ref[...]
    strict = (lax.broadcasted_iota(jnp.int32, (tile, tile), 0)
              < lax.broadcasted_iota(jnp.int32, (tile, tile), 1))
    lo = _row_lo((LANES, tile))

    def queries(i):
        return _split_pair(q_ref[i * tile:(i + 1) * tile, :], HEAD_DIM ** -0.5)

    def scores(ks, heads):
        return tuple(_dot_nt(ks, heads[a]) for a in range(2))

    def consume(zz, rs, vt, masked):
        sps, base, later, pvs = [], [], [], []
        for a in range(2):
            z = zz[a]
            sp = jnp.maximum(z, 0.0) + jnp.log(1.0 + jnp.exp(-jnp.abs(z)))
            base.append(z - sp)
            if masked:
                sp = jnp.where(strict, sp, 0.0)
            sps.append(sp)
        for a in range(2):
            hi = sps[a].astype(BF16)
            mid = (sps[a] - hi.astype(F32)).astype(BF16)
            later.append(_dot(tri, hi) + _dot(tri, mid))
        for a in range(2):
            wgt = jnp.exp(base[a] - later[a] + rs[a])
            if masked:
                wgt = jnp.where(strict, wgt, 0.0)
            pvs.append(_dot(vt, wgt.astype(BF16)))
        return tuple(rs[a] - jnp.sum(sps[a], axis=0, keepdims=True) for a in range(2)), pvs

    def live(rs):
        return (jnp.max(jnp.maximum(rs[0], rs[1])) > EXP_UNDERFLOW).astype(jnp.int32)

    tasks = [(i, j) for i in range(n_tiles) for j in ((i, i - 1) if i else (i,))]
    qcache = {}

    def task_scores(t):
        i, j = tasks[t]
        if i not in qcache:
            qcache[i] = queries(i)
        return scores(k_ref[j * tile:(j + 1) * tile, :], qcache[i])

    zero = jnp.zeros((1, tile), F32)
    survival = {}
    zz = task_scores(0)
    for t, (i, j) in enumerate(tasks):
        zz_next = task_scores(t + 1) if t + 1 < len(tasks) else None
        if j == i:
            rs, acc = (zero, zero), None
        rs, pvs = consume(zz, rs, vt_ref[j], j == i)
        acc = pvs if acc is None else [acc[a] + pvs[a] for a in range(2)]
        if j == max(i - 1, 0):
            for a in range(2):
                acc_ref[i, a] = acc[a]
            survival[i] = rs
        zz = zz_next

    for i in range(2, n_tiles):
        heads = qcache[i]

        def cond(state):
            j, alive, _ = state
            return (j >= 0) & (alive > 0)

        def body(state, i=i, heads=heads):
            j, _, rs = state
            rs, pvs = consume(scores(k_ref[pl.ds(j * tile, tile), :], heads), rs, vt_ref[j], False)
            for a in range(2):
                acc_ref[i, a] = acc_ref[i, a] + pvs[a]
            return j - 1, live(rs), rs

        lax.while_loop(cond, body, (i - 2, live(survival[i]), survival[i]))

    for i in range(n_tiles):
        o = jnp.where(lo, acc_ref[i, 0], acc_ref[i, 1])
        o_ref[i * tile:(i + 1) * tile, :] = o.T.astype(o_ref.dtype)


def sb_attention(main, vt, batch, seq_len, q_col, k_col, n_pairs, tri, tile=ATT_TILE):
    nq = seq_len // tile
    n = batch * seq_len
    return pl.pallas_call(
        functools.partial(_sb_kernel, tile=tile, n_tiles=nq),
        grid=(batch, n_pairs),
        in_specs=[pl.BlockSpec((seq_len, LANES), lambda b, p: (b, q_col + p)),
                  pl.BlockSpec((seq_len, LANES), lambda b, p: (b, k_col + p)),
                  pl.BlockSpec((nq, LANES, tile), lambda b, p: (b, p, 0)),
                  _full(tri)],
        out_specs=pl.BlockSpec((seq_len, LANES), lambda b, p: (b, p)),
        out_shape=jax.ShapeDtypeStruct((n, n_pairs * LANES), BF16),
        scratch_shapes=[pltpu.VMEM((nq, 2, LANES, tile), F32)],
        compiler_params=_params("parallel", "parallel"),
        name="sb_attention",
    )(main, main, vt, tri)


def _gelu_tanh(x):
    return 0.5 * x * (1.0 + jnp.tanh(math.sqrt(2.0 / math.pi) * (x + 0.044715 * (x * x * x))))


def _compress_kernel(xk_ref, xv_ref, pek_ref, pev_ref, w1k_ref, w1v_ref, w2k_ref, w2v_ref,
                     ok_ref, ov_ref, *, n_cmp):
    def one(x_ref, pe_ref, w1_ref, w2_ref):
        x = x_ref[...].astype(F32)
        xa = (x + pe_ref[0:1, :]).astype(BF16)
        xb = (x + pe_ref[1:2, :]).astype(BF16)
        ha = _dot(xa, w1_ref[0])
        hb = _dot(xb, w1_ref[1])
        rows = ha.shape[0]
        hb = pltpu.roll(hb, rows - 1, 0)
        h = _gelu_tanh(ha + hb)
        out = _dot(h.astype(BF16), w2_ref[...])
        ridx = lax.broadcasted_iota(jnp.int32, out.shape, 0)
        return jnp.where(ridx < n_cmp, out, 0.0)

    ok_ref[0] = one(xk_ref, pek_ref, w1k_ref, w2k_ref).astype(ok_ref.dtype)
    ov_ref[0] = one(xv_ref, pev_ref, w1v_ref, w2v_ref).T.astype(ov_ref.dtype)


def compress(xk, xv, pek, pev, w1k, w1v, w2k, w2v, batch, n_cmp):
    chunks = xk.shape[0] // batch
    row = pl.BlockSpec((chunks, xk.shape[1]), lambda b: (b, 0))
    return pl.pallas_call(
        functools.partial(_compress_kernel, n_cmp=n_cmp),
        grid=(batch,),
        in_specs=[row, row, _full(pek), _full(pev), _full(w1k), _full(w1v), _full(w2k), _full(w2v)],
        out_specs=[pl.BlockSpec((1, chunks, LANES), lambda b: (b, 0, 0)),
                   pl.BlockSpec((1, LANES, chunks), lambda b: (b, 0, 0))],
        out_shape=[jax.ShapeDtypeStruct((batch, chunks, LANES), BF16),
                   jax.ShapeDtypeStruct((batch, LANES, chunks), BF16)],
        compiler_params=_params("parallel"),
        name="nsa_compress",
    )(xk, xv, pek, pev, w1k, w1v, w2k, w2v)


def _nsa_cmp_kernel(q_ref, kc_ref, vct_ref, cb_ref, ov_ref, ocmp_ref, sel_ref, *, tile, n_blocks):
    i = pl.program_id(1)
    kc = kc_ref[0]
    vct = vct_ref[0]
    pcsum = [jnp.zeros((LANES, tile), F32), jnp.zeros((LANES, tile), F32)]
    lo = _row_lo((LANES, tile))
    for r in range(NSA_GQA):
        heads = _split_pair(q_ref[:, r * LANES:(r + 1) * LANES], HEAD_DIM ** -0.5)
        outs = []
        for a in range(2):
            bias = cb_ref[2 * r + a]
            valid = bias > 0.5 * NEG
            s = _dot_nt(kc, heads[a]) + bias
            m = jnp.max(s, axis=0, keepdims=True)
            pr = jnp.where(valid, jnp.exp(s - m), 0.0)
            l = jnp.sum(pr, axis=0, keepdims=True)
            pc = pr / jnp.where(l > 0.0, l, 1.0)
            pcsum[a] = pcsum[a] + pc
            outs.append(_dot(vct, pc.astype(BF16)))
        ocmp_ref[r * LANES:(r + 1) * LANES, :] = jnp.where(lo, outs[0], outs[1])

    score = _dot_01exact(ov_ref[0], pcsum[0]) + _dot_01exact(ov_ref[1], pcsum[1])
    score = score[:2 * n_blocks]
    shape = (2 * n_blocks, tile)
    row = lax.broadcasted_iota(jnp.int32, shape, 0)
    qpos = i * tile + lax.broadcasted_iota(jnp.int32, shape, 1)
    blk = row % n_blocks
    cur = qpos // SLC_BLOCK
    forced = (blk == 0) | (blk == cur) | (blk == cur - 1)
    future = blk > cur
    score = jnp.where(future, -1.0, jnp.where(forced, FORCED_SCORE, score))
    grp0 = row < n_blocks
    cnt = jnp.zeros(shape, F32)
    for mth in range(n_blocks):
        other = jnp.where(grp0, score[mth:mth + 1, :], score[n_blocks + mth:n_blocks + mth + 1, :])
        ahead = (other > score) | ((other == score) & (blk > mth))
        cnt = cnt + jnp.where(ahead, 1.0, 0.0)
    n_top = min(SLC_TOPK, n_blocks)
    sel_ref[...] = jnp.where(cnt < n_top, 0.0, NEG)


def nsa_cmp_select(main, kc, vct, cmp_bias, overlap2, batch, seq_len, q_col, tile=ATT_TILE):
    nq = seq_len // tile
    n = batch * seq_len
    n_blocks = seq_len // SLC_BLOCK
    qw = NSA_GQA * LANES
    return pl.pallas_call(
        functools.partial(_nsa_cmp_kernel, tile=tile, n_blocks=n_blocks),
        grid=(batch, nq),
        in_specs=[pl.BlockSpec((tile, qw), lambda b, i: (b * nq + i, q_col // NSA_GQA)),
                  pl.BlockSpec((1,) + kc.shape[1:], lambda b, i: (b, 0, 0)),
                  pl.BlockSpec((1,) + vct.shape[1:], lambda b, i: (b, 0, 0)),
                  pl.BlockSpec((NSA_HEADS, cmp_bias.shape[1], tile), lambda b, i: (0, 0, i)),
                  _full(overlap2)],
        out_specs=[pl.BlockSpec((qw, tile), lambda b, i: (0, b * nq + i)),
                   pl.BlockSpec((2 * n_blocks, tile), lambda b, i: (0, b * nq + i))],
        out_shape=[jax.ShapeDtypeStruct((qw, n), F32),
                   jax.ShapeDtypeStruct((2 * n_blocks, n), F32)],
        compiler_params=_params("parallel", "parallel"),
        name="nsa_cmp_select",
    )(main, kc, vct, cmp_bias, overlap2)


def _nsa_main_kernel(far_ref, q_ref, sel_ref, ocmp_ref, gate_ref, ks_ref, vst_ref, kw_ref, vwt_ref, tz_ref,
                     o_ref, *, tile, n_tiles, n_blocks):
    r = pl.program_id(1)
    lo = _row_lo((LANES, tile))
    blocks_per_tile = tile // SLC_BLOCK
    far = tuple(far_ref[2 * r + a] for a in range(2))

    def queries(i):
        return _split_pair(q_ref[i * tile:(i + 1) * tile, :], HEAD_DIM ** -0.5)

    def chosen(a, i, j, offset=None):
        rows = []
        for nb in range(blocks_per_tile):
            blk = a * n_blocks + j * blocks_per_tile + nb
            row = sel_ref[blk:blk + 1, i * tile:(i + 1) * tile]
            if offset is not None:
                row = row + offset
            rows.append(jnp.broadcast_to(row, (SLC_BLOCK, tile)))
        return jnp.concatenate(rows, axis=0)

    tasks = []
    for i in range(n_tiles):
        tasks += [("slc", i, j) for j in range(i + 1)]
        tasks += [("win", i, j) for j in ((i - 1, i) if i else (i,))]
    qcache = {}

    def scores(t):
        kind, i, j = tasks[t]
        if i not in qcache:
            qcache[i] = queries(i)
        k_ref_ = ks_ref if kind == "slc" else kw_ref
        kj = k_ref_[j * tile:(j + 1) * tile, :]
        return tuple(_dot_nt(kj, qcache[i][a]) for a in range(2))

    def fresh():
        return ([(jnp.full((1, tile), NEG, F32), jnp.zeros((1, tile), F32)) for _ in range(2)],
                [jnp.zeros((LANES, tile), F32) for _ in range(2)])

    qk = scores(0)
    for t, (kind, i, j) in enumerate(tasks):
        qk_next = scores(t + 1) if t + 1 < len(tasks) else None
        first = j == 0 if kind == "slc" else j == max(i - 1, 0)
        if first:
            stats, acc = fresh()
        vt = (vst_ref if kind == "slc" else vwt_ref)[j]
        alphas, pvs = [], []
        for a in range(2):
            if j == i:
                s = qk[a] + tz_ref[a, 0]
            elif j == i - 1:
                s = qk[a] + tz_ref[a, 1 if kind == "slc" else 2]
            if kind == "slc":
                s = s + chosen(a, i, j) if j >= i - 1 else qk[a] + chosen(a, i, j, far[a])
            m_prev, l_prev = stats[a]
            m_new = jnp.maximum(m_prev, jnp.max(s, axis=0, keepdims=True))
            alpha = jnp.exp(m_prev - m_new)
            pr = jnp.exp(s - m_new)
            stats[a] = (m_new, alpha * l_prev + jnp.sum(pr, axis=0, keepdims=True))
            alphas.append(alpha)
            pvs.append(_dot(vt, pr.astype(BF16)))
        for a in range(2):
            acc[a] = alphas[a] * acc[a] + pvs[a]
        if j == i:
            branch = jnp.where(lo, acc[0] / stats[0][1], acc[1] / stats[1][1])
            if kind == "slc":
                o_slc = branch
            else:
                cols = slice(i * tile, (i + 1) * tile)
                gexp = [jnp.where(lo, gate_ref[2 * c:2 * c + 1, cols], gate_ref[2 * c + 1:2 * c + 2, cols])
                        for c in range(3)]
                out = gexp[0] * ocmp_ref[:, cols] + gexp[1] * o_slc + gexp[2] * branch
                o_ref[cols, :] = out.T.astype(o_ref.dtype)
        qk = qk_next


def nsa_main(far_bias, main, sel, ocmp, gates, vt, tz, batch, seq_len, q_col, ks_col, kw_col, vs_row, vw_row,
             tile=ATT_TILE):
    nq = seq_len // tile
    n = batch * seq_len
    n_blocks = seq_len // SLC_BLOCK
    kv = lambda col: pl.BlockSpec((seq_len, LANES), lambda b, r: (b, col))
    vts = lambda rowblk: pl.BlockSpec((nq, LANES, tile), lambda b, r: (b, rowblk, 0))
    return pl.pallas_call(
        functools.partial(_nsa_main_kernel, tile=tile, n_tiles=nq, n_blocks=n_blocks),
        grid=(batch, NSA_GQA),
        in_specs=[pl.BlockSpec(memory_space=pltpu.SMEM),
                  pl.BlockSpec((seq_len, LANES), lambda b, r: (b, q_col + r)),
                  pl.BlockSpec((2 * n_blocks, seq_len), lambda b, r: (0, b)),
                  pl.BlockSpec((LANES, seq_len), lambda b, r: (r, b)),
                  pl.BlockSpec((GATE_ROWS, seq_len), lambda b, r: (r, b)),
                  kv(ks_col), vts(vs_row), kv(kw_col), vts(vw_row),
                  pl.BlockSpec((2, 3, tile, tile), lambda b, r: (r, 0, 0, 0))],
        out_specs=pl.BlockSpec((seq_len, LANES), lambda b, r: (b, r)),
        out_shape=jax.ShapeDtypeStruct((n, NSA_GQA * LANES), BF16),
        compiler_params=_params("parallel", "parallel"),
        name="nsa_main",
    )(far_bias, main, sel, ocmp, gates, main, vt, main, vt, tz)


def _xattn_kernel(q_ref, kv_ref, wo_ref, x_ref, o_ref):
    width = XA_HEADS * XA_HEAD_DIM
    outs = []
    for h in range(XA_HEADS):
        q = q_ref[:, h * XA_HEAD_DIM:(h + 1) * XA_HEAD_DIM]
        k = kv_ref[:, h * XA_HEAD_DIM:(h + 1) * XA_HEAD_DIM]
        v = kv_ref[:, width + h * XA_HEAD_DIM:width + (h + 1) * XA_HEAD_DIM]
        s = _dot_nt(q, k) * (XA_HEAD_DIM ** -0.5)
        m = jnp.max(s, axis=-1, keepdims=True)
        p = jnp.exp(s - m)
        l = jnp.sum(p, axis=-1, keepdims=True)
        outs.append((_dot(p.astype(BF16), v) / l).astype(BF16))
    o = jnp.concatenate(outs, axis=-1)
    o_ref[...] = x_ref[...] + _dot(o, wo_ref[...])


def xattn_residual(q, kv, wo, x, batch, seq_len, mem_len, tm=ROW_TILE):
    n, d = x.shape
    tm = min(tm, seq_len)
    nt = seq_len // tm
    return pl.pallas_call(
        _xattn_kernel,
        grid=(batch, nt),
        in_specs=[pl.BlockSpec((tm, q.shape[1]), lambda b, i: (b * nt + i, 0)),
                  pl.BlockSpec((mem_len, kv.shape[1]), lambda b, i: (b, 0)),
                  _full(wo),
                  pl.BlockSpec((tm, d), lambda b, i: (b * nt + i, 0))],
        out_specs=pl.BlockSpec((tm, d), lambda b, i: (b * nt + i, 0)),
        out_shape=jax.ShapeDtypeStruct((n, d), F32),
        compiler_params=_params("parallel", "parallel"),
        name="xattn_residual",
    )(q, kv, wo, x)


def _mlp_kernel(x_ref, g_ref, w1_ref, w2_ref, gf_ref, o_ref, hn_ref, acc_ref, *, final_norm):
    f = pl.program_id(1)

    @pl.when(f == 0)
    def _():
        hn_ref[...] = _rmsnorm(x_ref[...], g_ref[...]).astype(BF16)
        acc_ref[...] = x_ref[...]

    h = jnp.maximum(_dot(hn_ref[...], w1_ref[...]), 0.0)
    acc_ref[...] += _dot((h * h).astype(BF16), w2_ref[...])

    @pl.when(f == pl.num_programs(1) - 1)
    def _():
        y = acc_ref[...]
        if final_norm:
            y = _rmsnorm(y, gf_ref[...])
        o_ref[...] = y


def mlp_residual(x, g, w1, w2, gf, final_norm, tm=MLP_ROW_TILE, tf=MLP_FF_TILE):
    n, d = x.shape
    ff = w1.shape[1]
    tm = min(tm, n)
    tf = min(tf, ff)
    return pl.pallas_call(
        functools.partial(_mlp_kernel, final_norm=final_norm),
        grid=(n // tm, ff // tf),
        in_specs=[pl.BlockSpec((tm, d), lambda i, f: (i, 0)),
                  pl.BlockSpec((1, d), lambda i, f: (0, 0)),
                  pl.BlockSpec((d, tf), lambda i, f: (0, f)),
                  pl.BlockSpec((tf, d), lambda i, f: (f, 0)),
                  pl.BlockSpec((1, d), lambda i, f: (0, 0))],
        out_specs=pl.BlockSpec((tm, d), lambda i, f: (i, 0)),
        out_shape=jax.ShapeDtypeStruct((n, d), F32),
        scratch_shapes=[pltpu.VMEM((tm, d), BF16), pltpu.VMEM((tm, d), F32)],
        compiler_params=_params("parallel", "arbitrary"),
        name="mlp_residual",
    )(x, g, w1, w2, gf)


def _static_tables(seq_len):
    tile = ATT_TILE
    n_chunks = seq_len // CMP_STRIDE
    n_cmp = (seq_len - CMP_BLOCK) // CMP_STRIDE + 1
    n_slc = seq_len // SLC_BLOCK
    assert n_chunks == LANES and 2 * n_slc <= LANES and seq_len % tile == 0 and WINDOW == tile
    c = np.arange(n_chunks)[:, None]
    t = np.arange(seq_len)[None, :]
    cdist = t - (c * CMP_STRIDE + CMP_BLOCK - 1)
    cmp_idx = np.where((cdist >= 0) & (c < n_cmp), _t5_bucket_np(cdist), -1).astype(np.int32)
    sl = np.arange(tile)[:, None]
    tl = np.arange(tile)[None, :]
    prev = _t5_bucket_np(tl - sl + tile)
    tz_idx = np.concatenate([np.where(sl <= tl, _t5_bucket_np(tl - sl), -1), prev,
                             np.where(sl > tl, prev, -1)], axis=0).astype(np.int32)
    far_bucket = int(_t5_bucket_np(np.array([tile + 1]))[0])
    assert far_bucket == int(_t5_bucket_np(np.array([seq_len]))[0])
    cmp_start = np.arange(n_cmp) * CMP_STRIDE
    cmp_stop = cmp_start + CMP_BLOCK - 1
    slc_start = np.arange(n_slc) * SLC_BLOCK
    slc_stop = slc_start + SLC_BLOCK - 1
    ov = ((cmp_start[:, None] <= slc_stop[None, :]) & (cmp_stop[:, None] >= slc_start[None, :]))
    overlap2 = np.zeros((2, LANES, LANES), np.float32)
    for a in range(2):
        overlap2[a, a * n_slc:(a + 1) * n_slc, :n_cmp] = ov.T
    return cmp_idx, tz_idx, far_bucket, overlap2, n_cmp


def _tri_prefix(nn):
    s = np.arange(nn)[:, None]
    j = np.arange(nn)[None, :]
    return (j <= s).astype(np.float32)


def _tri_later(nn):
    s = np.arange(nn)[:, None]
    j = np.arange(nn)[None, :]
    return (j > s).astype(np.float32)


_NSA_HEAD_PERM = [a * NSA_GQA + r for r in range(NSA_GQA) for a in range(NSA_GROUPS)]


def _perm_head_cols(w, perm):
    d = w.shape[0]
    return w.reshape(d, len(perm), HEAD_DIM)[:, perm, :].reshape(d, len(perm) * HEAD_DIM)


def _block_diag2(m):
    z = jnp.zeros_like(m)
    return jnp.concatenate([jnp.concatenate([m, z], axis=1), jnp.concatenate([z, m], axis=1)], axis=0)


def _compress_weights(pe, w1, w2):
    half = CMP_BLOCK // 2
    pe2 = jnp.concatenate([pe, pe], axis=1).reshape(2, half * LANES)
    w1e = jax.vmap(_block_diag2)(w1).reshape(2, half * LANES, LANES)
    return pe2.astype(F32), w1e.astype(BF16), _block_diag2(w2).astype(BF16)


def kernel(x, mem, rel_bias, mem_norm_g, norm_mix_g, norm_xattn_g, norm_mlp_g, final_norm_g, w_in_even, b_forget, cmp_pe_k, cmp_w1_k, cmp_w2_k, cmp_pe_v, cmp_w1_v, cmp_w2_v, w_out_even, w_in_odd, w_out_odd, xa_wq, xa_wkv, xa_wo, mlp_w1, mlp_w2):
    batch, seq_len, d = x.shape
    mem_len = mem.shape[1]
    depth = norm_mix_g.shape[0]
    n = batch * seq_len
    fox_w = FOX_HEADS * HEAD_DIM
    nsa_w = NSA_HEADS * HEAD_DIM
    kv_w = NSA_GROUPS * HEAD_DIM
    splits = np.cumsum([fox_w, fox_w, fox_w, FOX_HEADS, nsa_w, kv_w, kv_w, kv_w, kv_w, kv_w, kv_w])

    cmp_idx, tz_idx, far_bucket, overlap2, n_cmp = _static_tables(seq_len)
    tab = rel_bias.astype(F32)[:, _NSA_HEAD_PERM]
    cmp_bias = bucket_lookup(tab, jnp.asarray(cmp_idx), rows=cmp_idx.shape[0])
    tz = bucket_lookup(tab, jnp.asarray(tz_idx), rows=ATT_TILE)
    tz = tz.reshape(NSA_HEADS, 3, ATT_TILE, ATT_TILE)
    far_bias = tab[far_bucket]
    overlap2 = jnp.asarray(overlap2, BF16)
    tri_prefix = jnp.asarray(_tri_prefix(min(ROW_TILE, seq_len)), BF16)
    tri_later = jnp.asarray(_tri_later(ATT_TILE), BF16)

    xf = x.reshape(n, d)
    memf = mem.reshape(batch * mem_len, d)
    row = lambda v: v.reshape(1, -1).astype(F32)

    for layer in range(depth):
        g_mix = row(norm_mix_g[layer])
        if layer % 2 == 0:
            e = layer // 2
            (w_fq, w_fk, w_fv, w_ff, w_nq, w_kc, w_vc, w_ks, w_vs, w_kw, w_vw, w_ng) = jnp.split(
                w_in_even[e], splits, axis=1)
            w_main = jnp.concatenate([w_fq, w_fk, _perm_head_cols(w_nq, _NSA_HEAD_PERM), w_ks, w_kw],
                                     axis=1).astype(BF16)
            w_t = jnp.concatenate([w_fv, w_vs, w_vw], axis=1).T.astype(BF16)
            w_ng_t = w_ng.reshape(d, NSA_GROUPS, NSA_GQA, 3).transpose(2, 3, 1, 0).reshape(NSA_GQA, 6, d)
            w_ng_t = jnp.pad(w_ng_t, ((0, 0), (0, GATE_ROWS - 6), (0, 0))).reshape(NSA_GQA * GATE_ROWS, d)
            pad_f = LANES - F_PIECES * FOX_HEADS
            w_ff_p = jnp.pad(jnp.repeat(w_ff, F_PIECES, axis=1), ((0, 0), (0, pad_f)))
            b_f = jnp.pad(jnp.repeat(b_forget[e].astype(F32), F_PIECES), (0, pad_f)).reshape(1, LANES)
            main, vt, kc_in, vc_in, fs, gates = even_proj(
                xf, g_mix, w_main, w_t, w_kc.astype(BF16), w_vc.astype(BF16), w_ff_p.astype(BF16), b_f,
                w_ng_t.astype(BF16), tri_prefix, seq_len)
            nblk = fox_w // LANES
            fox = fox_attention(main, fs, vt, batch, seq_len, 0, nblk, 0, FOX_HEADS // 2)

            pek, w1k, w2k = _compress_weights(cmp_pe_k[e], cmp_w1_k[e], cmp_w2_k[e])
            pev, w1v, w2v = _compress_weights(cmp_pe_v[e], cmp_w1_v[e], cmp_w2_v[e])
            chunk_w = CMP_STRIDE * LANES
            kc, vct = compress(kc_in.reshape(n // CMP_STRIDE, chunk_w), vc_in.reshape(n // CMP_STRIDE, chunk_w),
                               pek, pev, w1k, w1v, w2k, w2v, batch, n_cmp)
            q_col = 2 * nblk
            ocmp, sel = nsa_cmp_select(main, kc, vct, cmp_bias, overlap2, batch, seq_len, q_col)
            nsa = nsa_main(far_bias, main, sel, ocmp, gates, vt, tz, batch, seq_len, q_col,
                           q_col + NSA_GQA, q_col + NSA_GQA + 1, nblk, nblk + 1)
            w_o = w_out_even[e]
            w_o_nsa = w_o[fox_w:].reshape(NSA_HEADS, HEAD_DIM, d)[jnp.asarray(_NSA_HEAD_PERM)].reshape(nsa_w, d)
            a_list = [fox, nsa]
            w_list = [w_o[:fox_w].astype(BF16), w_o_nsa.astype(BF16)]
        else:
            o = layer // 2
            sb_w = SB_HEADS * HEAD_DIM
            w_in = w_in_odd[o]
            main, vt = norm_proj(xf, g_mix, w_in[:, :2 * sb_w].astype(BF16), w_in[:, 2 * sb_w:].T.astype(BF16))
            nblk = sb_w // LANES
            sb = sb_attention(main, vt, batch, seq_len, 0, nblk, SB_HEADS // 2, tri_later)
            a_list = [sb]
            w_list = [w_out_odd[o].astype(BF16)]

        x1, qx = proj_residual_q(a_list, w_list, xf, row(norm_xattn_g[layer]), xa_wq[layer].astype(BF16))
        kv_mem = norm_proj(memf, row(mem_norm_g), xa_wkv[layer].astype(BF16))
        x2 = xattn_residual(qx, kv_mem, xa_wo[layer].astype(BF16), x1, batch, seq_len, mem_len)
        last = layer == depth - 1
        xf = mlp_residual(x2, row(norm_mlp_g[layer]), mlp_w1[layer].astype(BF16), mlp_w2[layer].astype(BF16),
                          row(final_norm_g), final_norm=last)
    return xf.reshape(batch, seq_len, d)
```

```python
import functools
import math

import numpy as np
import jax
import jax.numpy as jnp
from jax import lax
from jax.experimental import pallas as pl
from jax.experimental.pallas import tpu as pltpu

F32 = jnp.float32
BF16 = jnp.bfloat16

LANES = 128
HEAD_DIM = 64
FOX_HEADS = 8
NSA_HEADS = 8
NSA_GROUPS = 2
NSA_GQA = NSA_HEADS // NSA_GROUPS
SB_HEADS = 16
CMP_BLOCK = 32
CMP_STRIDE = 16
SLC_BLOCK = 64
SLC_TOPK = 8
WINDOW = 256
N_BUCKETS = 32
MAX_DISTANCE = 128
XA_HEADS = 4
XA_HEAD_DIM = 128
EPS = 1e-6
NEG = -1e30
FORCED_SCORE = 1e4
EXP_UNDERFLOW = -104.0

ATT_TILE = 256
ROW_TILE = 512
MLP_ROW_TILE = 1024
MLP_FF_TILE = 512
GATE_ROWS = 8
F_PIECES = 3


def _dot(a, b):
    return jnp.dot(a, b, preferred_element_type=F32)


def _dot_nt(a, b):
    return lax.dot_general(a, b, (((1,), (1,)), ((), ())), preferred_element_type=F32)


def _split3(x):
    hi = x.astype(BF16)
    r1 = x - hi.astype(F32)
    mid = r1.astype(BF16)
    lo = (r1 - mid.astype(F32)).astype(BF16)
    return hi, mid, lo


def _dot_exact01(x, m01):
    hi, mid, lo = _split3(x)
    return _dot(hi, m01) + _dot(mid, m01) + _dot(lo, m01)


def _dot_01exact(m01, x):
    hi, mid, lo = _split3(x)
    return _dot(m01, hi) + _dot(m01, mid) + _dot(m01, lo)


def _rmsnorm(x, g):
    ms = jnp.mean(x * x, axis=-1, keepdims=True)
    return x * lax.rsqrt(ms + EPS) * g


def _sigmoid(x):
    return 1.0 / (1.0 + jnp.exp(-x))


def _softplus(x):
    return jnp.maximum(x, 0.0) + jnp.log1p(jnp.exp(-jnp.abs(x)))


def _split_pair(q, scale):
    lo = lax.broadcasted_iota(jnp.int32, q.shape, 1) < HEAD_DIM
    qs = q * jnp.asarray(scale, q.dtype)
    zero = jnp.zeros_like(qs)
    return jnp.where(lo, qs, zero), jnp.where(lo, zero, qs)


def _row_lo(shape):
    return lax.broadcasted_iota(jnp.int32, shape, 0) < HEAD_DIM


def _t5_bucket_np(dist):
    dist = np.maximum(dist, 0)
    max_exact = N_BUCKETS // 2
    d_f = np.maximum(dist, 1).astype(np.float64)
    large = max_exact + (np.log(d_f / max_exact) / math.log(MAX_DISTANCE / max_exact)
                         * (N_BUCKETS - max_exact)).astype(np.int32)
    large = np.minimum(large, N_BUCKETS - 1)
    return np.where(dist < max_exact, dist, large).astype(np.int32)


def _params(*sem):
    return pltpu.CompilerParams(dimension_semantics=sem)


def _full(a):
    return pl.BlockSpec(a.shape, lambda *_: (0,) * a.ndim)


def _norm_proj_kernel(x_ref, g_ref, w_ref, wt_ref, o_ref, ot_ref, *, tile):
    xn = _rmsnorm(x_ref[...], g_ref[...]).astype(BF16)
    o_ref[...] = _dot(xn, w_ref[...]).astype(o_ref.dtype)
    if wt_ref is not None:
        t = _dot_nt(wt_ref[...], xn).astype(ot_ref.dtype)
        for c in range(t.shape[1] // tile):
            ot_ref[c] = t[:, c * tile:(c + 1) * tile]


def norm_proj(x, g, w, w_t=None, tile=ATT_TILE, tm=ROW_TILE):
    n, d = x.shape
    p = w.shape[1]
    tm = min(tm, n)
    in_specs = [pl.BlockSpec((tm, d), lambda i: (i, 0)), _full(g), _full(w)]
    out_specs = [pl.BlockSpec((tm, p), lambda i: (i, 0))]
    out_shape = [jax.ShapeDtypeStruct((n, p), BF16)]
    args = [x, g, w]
    if w_t is None:
        body = lambda x_ref, g_ref, w_ref, o_ref: _norm_proj_kernel(
            x_ref, g_ref, w_ref, None, o_ref, None, tile=tile)
    else:
        pt = w_t.shape[0]
        in_specs.append(_full(w_t))
        out_specs.append(pl.BlockSpec((tm // tile, pt, tile), lambda i: (i, 0, 0)))
        out_shape.append(jax.ShapeDtypeStruct((n // tile, pt, tile), BF16))
        args.append(w_t)
        body = functools.partial(_norm_proj_kernel, tile=tile)
    out = pl.pallas_call(
        body, grid=(n // tm,), in_specs=in_specs, out_specs=out_specs, out_shape=out_shape,
        compiler_params=_params("parallel"), name="norm_proj",
    )(*args)
    return out if w_t is not None else out[0]


def _even_proj_kernel(x_ref, g_ref, w_ref, wt_ref, wkc_ref, wvc_ref, wff_ref, bf_ref, wng_ref, tri_ref,
                      main_ref, vt_ref, kc_ref, vc_ref, f_ref, gate_ref, carry_ref, *, tiles_per_seq, tile):
    i = pl.program_id(0)
    xn = _rmsnorm(x_ref[...], g_ref[...]).astype(BF16)
    main_ref[...] = _dot(xn, w_ref[...]).astype(BF16)
    t = _dot_nt(wt_ref[...], xn).astype(BF16)
    for c in range(t.shape[1] // tile):
        vt_ref[c] = t[:, c * tile:(c + 1) * tile]
    kc_ref[...] = _dot(xn, wkc_ref[...]).astype(BF16)
    vc_ref[...] = _dot(xn, wvc_ref[...]).astype(BF16)
    gate_ref[...] = _sigmoid(_dot_nt(wng_ref[...], xn))
    ff = _dot(xn, wff_ref[...]) + bf_ref[...]
    logf = jnp.minimum(ff, 0.0) - jnp.log1p(jnp.exp(-jnp.abs(ff)))

    @pl.when(i % tiles_per_seq == 0)
    def _():
        carry_ref[...] = jnp.zeros_like(carry_ref)

    cs = _dot_01exact(tri_ref[...], logf) + carry_ref[0:1, :]
    carry_ref[...] = jnp.broadcast_to(cs[cs.shape[0] - 1:, :], carry_ref.shape)
    hi, mid, lo = _split3(-cs)
    piece = lax.broadcasted_iota(jnp.int32, cs.shape, 1) % F_PIECES
    f_ref[...] = jnp.where(piece == 0, hi, jnp.where(piece == 1, mid, lo))


def even_proj(x, g, w_main, w_t, w_kc, w_vc, w_ff, b_f, w_ng_t, tri_l, seq_len, tile=ATT_TILE, tm=ROW_TILE):
    n, d = x.shape
    tm = min(tm, seq_len)
    pm = w_main.shape[1]
    pt = w_t.shape[0]
    gr = w_ng_t.shape[0]
    row = lambda width: pl.BlockSpec((tm, width), lambda i: (i, 0))
    return pl.pallas_call(
        functools.partial(_even_proj_kernel, tiles_per_seq=seq_len // tm, tile=tile),
        grid=(n // tm,),
        in_specs=[row(d), _full(g), _full(w_main), _full(w_t), _full(w_kc), _full(w_vc), _full(w_ff),
                  _full(b_f), _full(w_ng_t), _full(tri_l)],
        out_specs=[row(pm),
                   pl.BlockSpec((tm // tile, pt, tile), lambda i: (i, 0, 0)),
                   row(LANES), row(LANES), row(LANES),
                   pl.BlockSpec((gr, tm), lambda i: (0, i))],
        out_shape=[jax.ShapeDtypeStruct((n, pm), BF16),
                   jax.ShapeDtypeStruct((n // tile, pt, tile), BF16),
                   jax.ShapeDtypeStruct((n, LANES), BF16),
                   jax.ShapeDtypeStruct((n, LANES), BF16),
                   jax.ShapeDtypeStruct((n, LANES), BF16),
                   jax.ShapeDtypeStruct((gr, n), F32)],
        scratch_shapes=[pltpu.VMEM((8, LANES), F32)],
        compiler_params=_params("arbitrary"),
        name="even_proj",
    )(x, g, w_main, w_t, w_kc, w_vc, w_ff, b_f, w_ng_t, tri_l)


def _proj_residual_kernel(*refs, n_in):
    a_refs = refs[:n_in]
    w_refs = refs[n_in:2 * n_in]
    x_ref, g_ref, wq_ref, o_ref, q_ref = refs[2 * n_in:]
    acc = x_ref[...]
    for a_ref, w_ref in zip(a_refs, w_refs):
        acc = acc + _dot(a_ref[...], w_ref[...])
    o_ref[...] = acc
    q_ref[...] = _dot(_rmsnorm(acc, g_ref[...]).astype(BF16), wq_ref[...]).astype(BF16)


def proj_residual_q(a_list, w_list, x, g2, wq, tm=ROW_TILE):
    n, d = x.shape
    tm = min(tm, n)
    row = lambda a: pl.BlockSpec((tm, a.shape[1]), lambda i: (i, 0))
    return pl.pallas_call(
        functools.partial(_proj_residual_kernel, n_in=len(a_list)),
        grid=(n // tm,),
        in_specs=[row(a) for a in a_list] + [_full(w) for w in w_list] + [row(x), _full(g2), _full(wq)],
        out_specs=[row(x), pl.BlockSpec((tm, wq.shape[1]), lambda i: (i, 0))],
        out_shape=[jax.ShapeDtypeStruct((n, d), F32),
                   jax.ShapeDtypeStruct((n, wq.shape[1]), BF16)],
        compiler_params=_params("parallel"),
        name="proj_residual_q",
    )(*a_list, *w_list, x, g2, wq)


def _bucket_lookup_kernel(tab_ref, idx_ref, o_ref):
    h = pl.program_id(0)
    idx = idx_ref[...]
    out = jnp.full(idx.shape, NEG, F32)
    for b in range(N_BUCKETS):
        out = jnp.where(idx == b, tab_ref[b, h], out)
    o_ref[0] = out


def bucket_lookup(tab, idx, rows):
    r, c = idx.shape
    h = tab.shape[1]
    return pl.pallas_call(
        _bucket_lookup_kernel,
        grid=(h, r // rows),
        in_specs=[pl.BlockSpec(memory_space=pltpu.SMEM),
                  pl.BlockSpec((rows, c), lambda hh, j: (j, 0))],
        out_specs=pl.BlockSpec((1, rows, c), lambda hh, j: (hh, j, 0)),
        out_shape=jax.ShapeDtypeStruct((h, r, c), F32),
        compiler_params=_params("parallel", "parallel"),
        name="bucket_lookup",
    )(tab, idx)


def _fox_kernel(q_ref, k_ref, fs_ref, vt_ref, o_ref, stage_ref, *, tile, n_tiles):
    p = pl.program_id(1)
    lane = lax.broadcasted_iota(jnp.int32, (tile, LANES), 1)
    causal = (lax.broadcasted_iota(jnp.int32, (tile, tile), 0)
              <= lax.broadcasted_iota(jnp.int32, (tile, tile), 1))
    lo = _row_lo((LANES, tile))

    def keys(j):
        sl = slice(j * tile, (j + 1) * tile)
        return jnp.concatenate([k_ref[sl, :], fs_ref[sl, :]], axis=1)

    def queries(i):
        heads = _split_pair(q_ref[i * tile:(i + 1) * tile, :], HEAD_DIM ** -0.5)
        return tuple(jnp.concatenate(
            [heads[a], jnp.where(lane // F_PIECES == 2 * p + a, 1.0, 0.0).astype(BF16)], axis=1)
            for a in range(2))

    tasks = [(i, j) for i in range(n_tiles) for j in range(i + 1)]
    qcache = {}

    def scores(t):
        i, j = tasks[t]
        if i not in qcache:
            qcache[i] = queries(i)
        kj = keys(j)
        for a in range(2):
            stage_ref[t % 2, a] = _dot_nt(kj, qcache[i][a])

    scores(0)
    for t, (i, j) in enumerate(tasks):
        if t + 1 < len(tasks):
            scores(t + 1)
        qk = [stage_ref[t % 2, a] for a in range(2)]
        if j == 0:
            stats = [(jnp.full((1, tile), NEG, F32), jnp.zeros((1, tile), F32)) for _ in range(2)]
            acc = [jnp.zeros((LANES, tile), F32) for _ in range(2)]
        vt = vt_ref[j]
        alphas, pvs = [], []
        for a in range(2):
            m_prev, l_prev = stats[a]
            s = jnp.where(causal, qk[a], NEG) if j == i else qk[a]
            m_new = jnp.maximum(m_prev, jnp.max(s, axis=0, keepdims=True))
            alpha = jnp.exp(m_prev - m_new)
            pr = jnp.exp(s - m_new)
            stats[a] = (m_new, alpha * l_prev + jnp.sum(pr, axis=0, keepdims=True))
            alphas.append(alpha)
            pvs.append(_dot(vt, pr.astype(BF16)))
        for a in range(2):
            acc[a] = alphas[a] * acc[a] + pvs[a]
        if j == i:
            o = jnp.where(lo, acc[0] / stats[0][1], acc[1] / stats[1][1])
            o_ref[i * tile:(i + 1) * tile, :] = o.T.astype(o_ref.dtype)


def fox_attention(main, fs, vt, batch, seq_len, q_col, k_col, v_row, n_pairs, tile=ATT_TILE):
    nq = seq_len // tile
    n = batch * seq_len
    return pl.pallas_call(
        functools.partial(_fox_kernel, tile=tile, n_tiles=nq),
        grid=(batch, n_pairs),
        in_specs=[pl.BlockSpec((seq_len, LANES), lambda b, p: (b, q_col + p)),
                  pl.BlockSpec((seq_len, LANES), lambda b, p: (b, k_col + p)),
                  pl.BlockSpec((seq_len, LANES), lambda b, p: (b, 0)),
                  pl.BlockSpec((nq, LANES, tile), lambda b, p: (b, v_row + p, 0))],
        out_specs=pl.BlockSpec((seq_len, LANES), lambda b, p: (b, p)),
        out_shape=jax.ShapeDtypeStruct((n, n_pairs * LANES), BF16),
        scratch_shapes=[pltpu.VMEM((2, 2, tile, tile), F32)],
        compiler_params=_params("parallel", "parallel"),
        name="fox_attention",
    )(main, main, fs, vt)


def _sb_kernel(q_ref, k_ref, vt_ref, tri_ref, o_ref, acc_ref, stage_ref, *, tile, n_tiles):
    tri = tri_ref[...]
    strict = (lax.broadcasted_iota(jnp.int32, (tile, tile), 0)
              < lax.broadcasted_iota(jnp.int32, (tile, tile), 1))
    lo = _row_lo((LANES, tile))

    def queries(i):
        return _split_pair(q_ref[i * tile:(i + 1) * tile, :], HEAD_DIM ** -0.5)

    def scores(ks, heads):
        return tuple(_dot_nt(ks, heads[a]) for a in range(2))

    def consume(zz, rs, vt, masked):
        sps, base, later, pvs = [], [], [], []
        for a in range(2):
            z = zz[a]
            sp = jnp.maximum(z, 0.0) + jnp.log(1.0 + jnp.exp(-jnp.abs(z)))
            base.append(z - sp)
            if masked:
                sp = jnp.where(strict, sp, 0.0)
            sps.append(sp)
        for a in range(2):
            hi = sps[a].astype(BF16)
            mid = (sps[a] - hi.astype(F32)).astype(BF16)
            later.append(_dot(tri, hi) + _dot(tri, mid))
        for a in range(2):
            wgt = jnp.exp(base[a] - later[a] + rs[a])
            if masked:
                wgt = jnp.where(strict, wgt, 0.0)
            pvs.append(_dot(vt, wgt.astype(BF16)))
        return tuple(rs[a] - jnp.sum(sps[a], axis=0, keepdims=True) for a in range(2)), pvs

    def live(rs):
        return (jnp.max(jnp.maximum(rs[0], rs[1])) > EXP_UNDERFLOW).astype(jnp.int32)

    tasks = [(i, j) for i in range(n_tiles) for j in ((i, i - 1) if i else (i,))]
    qcache = {}

    def task_scores(t):
        i, j = tasks[t]
        if i not in qcache:
            qcache[i] = queries(i)
        zz = scores(k_ref[j * tile:(j + 1) * tile, :], qcache[i])
        for a in range(2):
            stage_ref[t % 2, a] = zz[a]

    zero = jnp.zeros((1, tile), F32)
    survival = {}
    task_scores(0)
    for t, (i, j) in enumerate(tasks):
        if t + 1 < len(tasks):
            task_scores(t + 1)
        zz = [stage_ref[t % 2, a] for a in range(2)]
        if j == i:
            rs, acc = (zero, zero), None
        rs, pvs = consume(zz, rs, vt_ref[j], j == i)
        acc = pvs if acc is None else [acc[a] + pvs[a] for a in range(2)]
        if j == max(i - 1, 0):
            for a in range(2):
                acc_ref[i, a] = acc[a]
            survival[i] = rs

    for i in range(2, n_tiles):
        heads = qcache[i]

        def cond(state):
            j, alive, _ = state
            return (j >= 0) & (alive > 0)

        def body(state, i=i, heads=heads):
            j, _, rs = state
            rs, pvs = consume(scores(k_ref[pl.ds(j * tile, tile), :], heads), rs, vt_ref[j], False)
            for a in range(2):
                acc_ref[i, a] = acc_ref[i, a] + pvs[a]
            return j - 1, live(rs), rs

        lax.while_loop(cond, body, (i - 2, live(survival[i]), survival[i]))

    for i in range(n_tiles):
        o = jnp.where(lo, acc_ref[i, 0], acc_ref[i, 1])
        o_ref[i * tile:(i + 1) * tile, :] = o.T.astype(o_ref.dtype)


def sb_attention(main, vt, batch, seq_len, q_col, k_col, n_pairs, tri, tile=ATT_TILE):
    nq = seq_len // tile
    n = batch * seq_len
    return pl.pallas_call(
        functools.partial(_sb_kernel, tile=tile, n_tiles=nq),
        grid=(batch, n_pairs),
        in_specs=[pl.BlockSpec((seq_len, LANES), lambda b, p: (b, q_col + p)),
                  pl.BlockSpec((seq_len, LANES), lambda b, p: (b, k_col + p)),
                  pl.BlockSpec((nq, LANES, tile), lambda b, p: (b, p, 0)),
                  _full(tri)],
        out_specs=pl.BlockSpec((seq_len, LANES), lambda b, p: (b, p)),
        out_shape=jax.ShapeDtypeStruct((n, n_pairs * LANES), BF16),
        scratch_shapes=[pltpu.VMEM((nq, 2, LANES, tile), F32), pltpu.VMEM((2, 2, tile, tile), F32)],
        compiler_params=_params("parallel", "parallel"),
        name="sb_attention",
    )(main, main, vt, tri)


def _gelu_tanh(x):
    return 0.5 * x * (1.0 + jnp.tanh(math.sqrt(2.0 / math.pi) * (x + 0.044715 * (x * x * x))))


def _compress_kernel(xk_ref, xv_ref, pek_ref, pev_ref, w1k_ref, w1v_ref, w2k_ref, w2v_ref,
                     ok_ref, ov_ref, *, n_cmp):
    def one(x_ref, pe_ref, w1_ref, w2_ref):
        x = x_ref[...].astype(F32)
        xa = (x + pe_ref[0:1, :]).astype(BF16)
        xb = (x + pe_ref[1:2, :]).astype(BF16)
        ha = _dot(xa, w1_ref[0])
        hb = _dot(xb, w1_ref[1])
        rows = ha.shape[0]
        hb = pltpu.roll(hb, rows - 1, 0)
        h = _gelu_tanh(ha + hb)
        out = _dot(h.astype(BF16), w2_ref[...])
        ridx = lax.broadcasted_iota(jnp.int32, out.shape, 0)
        return jnp.where(ridx < n_cmp, out, 0.0)

    ok_ref[0] = one(xk_ref, pek_ref, w1k_ref, w2k_ref).astype(ok_ref.dtype)
    ov_ref[0] = one(xv_ref, pev_ref, w1v_ref, w2v_ref).T.astype(ov_ref.dtype)


def compress(xk, xv, pek, pev, w1k, w1v, w2k, w2v, batch, n_cmp):
    chunks = xk.shape[0] // batch
    row = pl.BlockSpec((chunks, xk.shape[1]), lambda b: (b, 0))
    return pl.pallas_call(
        functools.partial(_compress_kernel, n_cmp=n_cmp),
        grid=(batch,),
        in_specs=[row, row, _full(pek), _full(pev), _full(w1k), _full(w1v), _full(w2k), _full(w2v)],
        out_specs=[pl.BlockSpec((1, chunks, LANES), lambda b: (b, 0, 0)),
                   pl.BlockSpec((1, LANES, chunks), lambda b: (b, 0, 0))],
        out_shape=[jax.ShapeDtypeStruct((batch, chunks, LANES), BF16),
                   jax.ShapeDtypeStruct((batch, LANES, chunks), BF16)],
        compiler_params=_params("parallel"),
        name="nsa_compress",
    )(xk, xv, pek, pev, w1k, w1v, w2k, w2v)


def _nsa_cmp_kernel(q_ref, kc_ref, vct_ref, cb_ref, ov_ref, ocmp_ref, sel_ref, *, tile, n_blocks):
    i = pl.program_id(1)
    kc = kc_ref[0]
    vct = vct_ref[0]
    pcsum = [jnp.zeros((LANES, tile), F32), jnp.zeros((LANES, tile), F32)]
    lo = _row_lo((LANES, tile))
    for r in range(NSA_GQA):
        heads = _split_pair(q_ref[:, r * LANES:(r + 1) * LANES], HEAD_DIM ** -0.5)
        outs = []
        for a in range(2):
            bias = cb_ref[2 * r + a]
            valid = bias > 0.5 * NEG
            s = _dot_nt(kc, heads[a]) + bias
            m = jnp.max(s, axis=0, keepdims=True)
            pr = jnp.where(valid, jnp.exp(s - m), 0.0)
            l = jnp.sum(pr, axis=0, keepdims=True)
            pc = pr / jnp.where(l > 0.0, l, 1.0)
            pcsum[a] = pcsum[a] + pc
            outs.append(_dot(vct, pc.astype(BF16)))
        ocmp_ref[r * LANES:(r + 1) * LANES, :] = jnp.where(lo, outs[0], outs[1])

    score = _dot_01exact(ov_ref[0], pcsum[0]) + _dot_01exact(ov_ref[1], pcsum[1])
    score = score[:2 * n_blocks]
    shape = (2 * n_blocks, tile)
    row = lax.broadcasted_iota(jnp.int32, shape, 0)
    qpos = i * tile + lax.broadcasted_iota(jnp.int32, shape, 1)
    blk = row % n_blocks
    cur = qpos // SLC_BLOCK
    forced = (blk == 0) | (blk == cur) | (blk == cur - 1)
    future = blk > cur
    score = jnp.where(future, -1.0, jnp.where(forced, FORCED_SCORE, score))
    grp0 = row < n_blocks
    cnt = jnp.zeros(shape, F32)
    for mth in range(n_blocks):
        other = jnp.where(grp0, score[mth:mth + 1, :], score[n_blocks + mth:n_blocks + mth + 1, :])
        ahead = (other > score) | ((other == score) & (blk > mth))
        cnt = cnt + jnp.where(ahead, 1.0, 0.0)
    n_top = min(SLC_TOPK, n_blocks)
    sel_ref[...] = jnp.where(cnt < n_top, 0.0, NEG)


def nsa_cmp_select(main, kc, vct, cmp_bias, overlap2, batch, seq_len, q_col, tile=ATT_TILE):
    nq = seq_len // tile
    n = batch * seq_len
    n_blocks = seq_len // SLC_BLOCK
    qw = NSA_GQA * LANES
    return pl.pallas_call(
        functools.partial(_nsa_cmp_kernel, tile=tile, n_blocks=n_blocks),
        grid=(batch, nq),
        in_specs=[pl.BlockSpec((tile, qw), lambda b, i: (b * nq + i, q_col // NSA_GQA)),
                  pl.BlockSpec((1,) + kc.shape[1:], lambda b, i: (b, 0, 0)),
                  pl.BlockSpec((1,) + vct.shape[1:], lambda b, i: (b, 0, 0)),
                  pl.BlockSpec((NSA_HEADS, cmp_bias.shape[1], tile), lambda b, i: (0, 0, i)),
                  _full(overlap2)],
        out_specs=[pl.BlockSpec((qw, tile), lambda b, i: (0, b * nq + i)),
                   pl.BlockSpec((2 * n_blocks, tile), lambda b, i: (0, b * nq + i))],
        out_shape=[jax.ShapeDtypeStruct((qw, n), F32),
                   jax.ShapeDtypeStruct((2 * n_blocks, n), F32)],
        compiler_params=_params("parallel", "parallel"),
        name="nsa_cmp_select",
    )(main, kc, vct, cmp_bias, overlap2)


def _nsa_main_kernel(far_ref, q_ref, sel_ref, ocmp_ref, gate_ref, ks_ref, vst_ref, kw_ref, vwt_ref, tz_ref,
                     o_ref, stage_ref, *, tile, n_tiles, n_blocks):
    r = pl.program_id(1)
    lo = _row_lo((LANES, tile))
    blocks_per_tile = tile // SLC_BLOCK
    far = tuple(far_ref[2 * r + a] for a in range(2))

    def queries(i):
        return _split_pair(q_ref[i * tile:(i + 1) * tile, :], HEAD_DIM ** -0.5)

    def chosen(a, i, j, offset=None):
        rows = []
        for nb in range(blocks_per_tile):
            blk = a * n_blocks + j * blocks_per_tile + nb
            row = sel_ref[blk:blk + 1, i * tile:(i + 1) * tile]
            if offset is not None:
                row = row + offset
            rows.append(jnp.broadcast_to(row, (SLC_BLOCK, tile)))
        return jnp.concatenate(rows, axis=0)

    tasks = []
    for i in range(n_tiles):
        tasks += [("slc", i, j) for j in range(i + 1)]
        tasks += [("win", i, j) for j in ((i - 1, i) if i else (i,))]
    qcache = {}

    def scores(t):
        kind, i, j = tasks[t]
        if i not in qcache:
            qcache[i] = queries(i)
        k_ref_ = ks_ref if kind == "slc" else kw_ref
        kj = k_ref_[j * tile:(j + 1) * tile, :]
        for a in range(2):
            stage_ref[t % 2, a] = _dot_nt(kj, qcache[i][a])

    def fresh():
        return ([(jnp.full((1, tile), NEG, F32), jnp.zeros((1, tile), F32)) for _ in range(2)],
                [jnp.zeros((LANES, tile), F32) for _ in range(2)])

    scores(0)
    for t, (kind, i, j) in enumerate(tasks):
        if t + 1 < len(tasks):
            scores(t + 1)
        qk = [stage_ref[t % 2, a] for a in range(2)]
        first = j == 0 if kind == "slc" else j == max(i - 1, 0)
        if first:
            stats, acc = fresh()
        vt = (vst_ref if kind == "slc" else vwt_ref)[j]
        alphas, pvs = [], []
        for a in range(2):
            if j == i:
                s = qk[a] + tz_ref[a, 0]
            elif j == i - 1:
                s = qk[a] + tz_ref[a, 1 if kind == "slc" else 2]
            if kind == "slc":
                s = s + chosen(a, i, j) if j >= i - 1 else qk[a] + chosen(a, i, j, far[a])
            m_prev, l_prev = stats[a]
            m_new = jnp.maximum(m_prev, jnp.max(s, axis=0, keepdims=True))
            alpha = jnp.exp(m_prev - m_new)
            pr = jnp.exp(s - m_new)
            stats[a] = (m_new, alpha * l_prev + jnp.sum(pr, axis=0, keepdims=True))
            alphas.append(alpha)
            pvs.append(_dot(vt, pr.astype(BF16)))
        for a in range(2):
            acc[a] = alphas[a] * acc[a] + pvs[a]
        if j == i:
            branch = jnp.where(lo, acc[0] / stats[0][1], acc[1] / stats[1][1])
            if kind == "slc":
                o_slc = branch
            else:
                cols = slice(i * tile, (i + 1) * tile)
                gexp = [jnp.where(lo, gate_ref[2 * c:2 * c + 1, cols], gate_ref[2 * c + 1:2 * c + 2, cols])
                        for c in range(3)]
                out = gexp[0] * ocmp_ref[:, cols] + gexp[1] * o_slc + gexp[2] * branch
                o_ref[cols, :] = out.T.astype(o_ref.dtype)


def nsa_main(far_bias, main, sel, ocmp, gates, vt, tz, batch, seq_len, q_col, ks_col, kw_col, vs_row, vw_row,
             tile=ATT_TILE):
    nq = seq_len // tile
    n = batch * seq_len
    n_blocks = seq_len // SLC_BLOCK
    kv = lambda col: pl.BlockSpec((seq_len, LANES), lambda b, r: (b, col))
    vts = lambda rowblk: pl.BlockSpec((nq, LANES, tile), lambda b, r: (b, rowblk, 0))
    return pl.pallas_call(
        functools.partial(_nsa_main_kernel, tile=tile, n_tiles=nq, n_blocks=n_blocks),
        grid=(batch, NSA_GQA),
        in_specs=[pl.BlockSpec(memory_space=pltpu.SMEM),
                  pl.BlockSpec((seq_len, LANES), lambda b, r: (b, q_col + r)),
                  pl.BlockSpec((2 * n_blocks, seq_len), lambda b, r: (0, b)),
                  pl.BlockSpec((LANES, seq_len), lambda b, r: (r, b)),
                  pl.BlockSpec((GATE_ROWS, seq_len), lambda b, r: (r, b)),
                  kv(ks_col), vts(vs_row), kv(kw_col), vts(vw_row),
                  pl.BlockSpec((2, 3, tile, tile), lambda b, r: (r, 0, 0, 0))],
        out_specs=pl.BlockSpec((seq_len, LANES), lambda b, r: (b, r)),
        out_shape=jax.ShapeDtypeStruct((n, NSA_GQA * LANES), BF16),
        scratch_shapes=[pltpu.VMEM((2, 2, tile, tile), F32)],
        compiler_params=_params("parallel", "parallel"),
        name="nsa_main",
    )(far_bias, main, sel, ocmp, gates, main, vt, main, vt, tz)


def _xattn_kernel(q_ref, kv_ref, wo_ref, x_ref, o_ref):
    width = XA_HEADS * XA_HEAD_DIM
    outs = []
    for h in range(XA_HEADS):
        q = q_ref[:, h * XA_HEAD_DIM:(h + 1) * XA_HEAD_DIM]
        k = kv_ref[:, h * XA_HEAD_DIM:(h + 1) * XA_HEAD_DIM]
        v = kv_ref[:, width + h * XA_HEAD_DIM:width + (h + 1) * XA_HEAD_DIM]
        s = _dot_nt(q, k) * (XA_HEAD_DIM ** -0.5)
        m = jnp.max(s, axis=-1, keepdims=True)
        p = jnp.exp(s - m)
        l = jnp.sum(p, axis=-1, keepdims=True)
        outs.append((_dot(p.astype(BF16), v) / l).astype(BF16))
    o = jnp.concatenate(outs, axis=-1)
    o_ref[...] = x_ref[...] + _dot(o, wo_ref[...])


def xattn_residual(q, kv, wo, x, batch, seq_len, mem_len, tm=ROW_TILE):
    n, d = x.shape
    tm = min(tm, seq_len)
    nt = seq_len // tm
    return pl.pallas_call(
        _xattn_kernel,
        grid=(batch, nt),
        in_specs=[pl.BlockSpec((tm, q.shape[1]), lambda b, i: (b * nt + i, 0)),
                  pl.BlockSpec((mem_len, kv.shape[1]), lambda b, i: (b, 0)),
                  _full(wo),
                  pl.BlockSpec((tm, d), lambda b, i: (b * nt + i, 0))],
        out_specs=pl.BlockSpec((tm, d), lambda b, i: (b * nt + i, 0)),
        out_shape=jax.ShapeDtypeStruct((n, d), F32),
        compiler_params=_params("parallel", "parallel"),
        name="xattn_residual",
    )(q, kv, wo, x)


def _mlp_kernel(x_ref, g_ref, w1_ref, w2_ref, gf_ref, o_ref, hn_ref, acc_ref, *, final_norm):
    f = pl.program_id(1)

    @pl.when(f == 0)
    def _():
        hn_ref[...] = _rmsnorm(x_ref[...], g_ref[...]).astype(BF16)
        acc_ref[...] = x_ref[...]

    h = jnp.maximum(_dot(hn_ref[...], w1_ref[...]), 0.0)
    acc_ref[...] += _dot((h * h).astype(BF16), w2_ref[...])

    @pl.when(f == pl.num_programs(1) - 1)
    def _():
        y = acc_ref[...]
        if final_norm:
            y = _rmsnorm(y, gf_ref[...])
        o_ref[...] = y


def mlp_residual(x, g, w1, w2, gf, final_norm, tm=MLP_ROW_TILE, tf=MLP_FF_TILE):
    n, d = x.shape
    ff = w1.shape[1]
    tm = min(tm, n)
    tf = min(tf, ff)
    return pl.pallas_call(
        functools.partial(_mlp_kernel, final_norm=final_norm),
        grid=(n // tm, ff // tf),
        in_specs=[pl.BlockSpec((tm, d), lambda i, f: (i, 0)),
                  pl.BlockSpec((1, d), lambda i, f: (0, 0)),
                  pl.BlockSpec((d, tf), lambda i, f: (0, f)),
                  pl.BlockSpec((tf, d), lambda i, f: (f, 0)),
                  pl.BlockSpec((1, d), lambda i, f: (0, 0))],
        out_specs=pl.BlockSpec((tm, d), lambda i, f: (i, 0)),
        out_shape=jax.ShapeDtypeStruct((n, d), F32),
        scratch_shapes=[pltpu.VMEM((tm, d), BF16), pltpu.VMEM((tm, d), F32)],
        compiler_params=_params("parallel", "arbitrary"),
        name="mlp_residual",
    )(x, g, w1, w2, gf)


def _static_tables(seq_len):
    tile = ATT_TILE
    n_chunks = seq_len // CMP_STRIDE
    n_cmp = (seq_len - CMP_BLOCK) // CMP_STRIDE + 1
    n_slc = seq_len // SLC_BLOCK
    assert n_chunks == LANES and 2 * n_slc <= LANES and seq_len % tile == 0 and WINDOW == tile
    c = np.arange(n_chunks)[:, None]
    t = np.arange(seq_len)[None, :]
    cdist = t - (c * CMP_STRIDE + CMP_BLOCK - 1)
    cmp_idx = np.where((cdist >= 0) & (c < n_cmp), _t5_bucket_np(cdist), -1).astype(np.int32)
    sl = np.arange(tile)[:, None]
    tl = np.arange(tile)[None, :]
    prev = _t5_bucket_np(tl - sl + tile)
    tz_idx = np.concatenate([np.where(sl <= tl, _t5_bucket_np(tl - sl), -1), prev,
                             np.where(sl > tl, prev, -1)], axis=0).astype(np.int32)
    far_bucket = int(_t5_bucket_np(np.array([tile + 1]))[0])
    assert far_bucket == int(_t5_bucket_np(np.array([seq_len]))[0])
    cmp_start = np.arange(n_cmp) * CMP_STRIDE
    cmp_stop = cmp_start + CMP_BLOCK - 1
    slc_start = np.arange(n_slc) * SLC_BLOCK
    slc_stop = slc_start + SLC_BLOCK - 1
    ov = ((cmp_start[:, None] <= slc_stop[None, :]) & (cmp_stop[:, None] >= slc_start[None, :]))
    overlap2 = np.zeros((2, LANES, LANES), np.float32)
    for a in range(2):
        overlap2[a, a * n_slc:(a + 1) * n_slc, :n_cmp] = ov.T
    return cmp_idx, tz_idx, far_bucket, overlap2, n_cmp


def _tri_prefix(nn):
    s = np.arange(nn)[:, None]
    j = np.arange(nn)[None, :]
    return (j <= s).astype(np.float32)


def _tri_later(nn):
    s = np.arange(nn)[:, None]
    j = np.arange(nn)[None, :]
    return (j > s).astype(np.float32)


_NSA_HEAD_PERM = [a * NSA_GQA + r for r in range(NSA_GQA) for a in range(NSA_GROUPS)]


def _perm_head_cols(w, perm):
    d = w.shape[0]
    return w.reshape(d, len(perm), HEAD_DIM)[:, perm, :].reshape(d, len(perm) * HEAD_DIM)


def _block_diag2(m):
    z = jnp.zeros_like(m)
    return jnp.concatenate([jnp.concatenate([m, z], axis=1), jnp.concatenate([z, m], axis=1)], axis=0)


def _compress_weights(pe, w1, w2):
    half = CMP_BLOCK // 2
    pe2 = jnp.concatenate([pe, pe], axis=1).reshape(2, half * LANES)
    w1e = jax.vmap(_block_diag2)(w1).reshape(2, half * LANES, LANES)
    return pe2.astype(F32), w1e.astype(BF16), _block_diag2(w2).astype(BF16)


def kernel(x, mem, rel_bias, mem_norm_g, norm_mix_g, norm_xattn_g, norm_mlp_g, final_norm_g, w_in_even, b_forget, cmp_pe_k, cmp_w1_k, cmp_w2_k, cmp_pe_v, cmp_w1_v, cmp_w2_v, w_out_even, w_in_odd, w_out_odd, xa_wq, xa_wkv, xa_wo, mlp_w1, mlp_w2):
    batch, seq_len, d = x.shape
    mem_len = mem.shape[1]
    depth = norm_mix_g.shape[0]
    n = batch * seq_len
    fox_w = FOX_HEADS * HEAD_DIM
    nsa_w = NSA_HEADS * HEAD_DIM
    kv_w = NSA_GROUPS * HEAD_DIM
    splits = np.cumsum([fox_w, fox_w, fox_w, FOX_HEADS, nsa_w, kv_w, kv_w, kv_w, kv_w, kv_w, kv_w])

    cmp_idx, tz_idx, far_bucket, overlap2, n_cmp = _static_tables(seq_len)
    tab = rel_bias.astype(F32)[:, _NSA_HEAD_PERM]
    cmp_bias = bucket_lookup(tab, jnp.asarray(cmp_idx), rows=cmp_idx.shape[0])
    tz = bucket_lookup(tab, jnp.asarray(tz_idx), rows=ATT_TILE)
    tz = tz.reshape(NSA_HEADS, 3, ATT_TILE, ATT_TILE)
    far_bias = tab[far_bucket]
    overlap2 = jnp.asarray(overlap2, BF16)
    tri_prefix = jnp.asarray(_tri_prefix(min(ROW_TILE, seq_len)), BF16)
    tri_later = jnp.asarray(_tri_later(ATT_TILE), BF16)

    xf = x.reshape(n, d)
    memf = mem.reshape(batch * mem_len, d)
    row = lambda v: v.reshape(1, -1).astype(F32)

    for layer in range(depth):
        g_mix = row(norm_mix_g[layer])
        if layer % 2 == 0:
            e = layer // 2
            (w_fq, w_fk, w_fv, w_ff, w_nq, w_kc, w_vc, w_ks, w_vs, w_kw, w_vw, w_ng) = jnp.split(
                w_in_even[e], splits, axis=1)
            w_main = jnp.concatenate([w_fq, w_fk, _perm_head_cols(w_nq, _NSA_HEAD_PERM), w_ks, w_kw],
                                     axis=1).astype(BF16)
            w_t = jnp.concatenate([w_fv, w_vs, w_vw], axis=1).T.astype(BF16)
            w_ng_t = w_ng.reshape(d, NSA_GROUPS, NSA_GQA, 3).transpose(2, 3, 1, 0).reshape(NSA_GQA, 6, d)
            w_ng_t = jnp.pad(w_ng_t, ((0, 0), (0, GATE_ROWS - 6), (0, 0))).reshape(NSA_GQA * GATE_ROWS, d)
            pad_f = LANES - F_PIECES * FOX_HEADS
            w_ff_p = jnp.pad(jnp.repeat(w_ff, F_PIECES, axis=1), ((0, 0), (0, pad_f)))
            b_f = jnp.pad(jnp.repeat(b_forget[e].astype(F32), F_PIECES), (0, pad_f)).reshape(1, LANES)
            main, vt, kc_in, vc_in, fs, gates = even_proj(
                xf, g_mix, w_main, w_t, w_kc.astype(BF16), w_vc.astype(BF16), w_ff_p.astype(BF16), b_f,
                w_ng_t.astype(BF16), tri_prefix, seq_len)
            nblk = fox_w // LANES
            fox = fox_attention(main, fs, vt, batch, seq_len, 0, nblk, 0, FOX_HEADS // 2)

            pek, w1k, w2k = _compress_weights(cmp_pe_k[e], cmp_w1_k[e], cmp_w2_k[e])
            pev, w1v, w2v = _compress_weights(cmp_pe_v[e], cmp_w1_v[e], cmp_w2_v[e])
            chunk_w = CMP_STRIDE * LANES
            kc, vct = compress(kc_in.reshape(n // CMP_STRIDE, chunk_w), vc_in.reshape(n // CMP_STRIDE, chunk_w),
                               pek, pev, w1k, w1v, w2k, w2v, batch, n_cmp)
            q_col = 2 * nblk
            ocmp, sel = nsa_cmp_select(main, kc, vct, cmp_bias, overlap2, batch, seq_len, q_col)
            nsa = nsa_main(far_bias, main, sel, ocmp, gates, vt, tz, batch, seq_len, q_col,
                           q_col + NSA_GQA, q_col + NSA_GQA + 1, nblk, nblk + 1)
            w_o = w_out_even[e]
            w_o_nsa = w_o[fox_w:].reshape(NSA_HEADS, HEAD_DIM, d)[jnp.asarray(_NSA_HEAD_PERM)].reshape(nsa_w, d)
            a_list = [fox, nsa]
            w_list = [w_o[:fox_w].astype(BF16), w_o_nsa.astype(BF16)]
        else:
            o = layer // 2
            sb_w = SB_HEADS * HEAD_DIM
            w_in = w_in_odd[o]
            main, vt = norm_proj(xf, g_mix, w_in[:, :2 * sb_w].astype(BF16), w_in[:, 2 * sb_w:].T.astype(BF16))
            nblk = sb_w // LANES
            sb = sb_attention(main, vt, batch, seq_len, 0, nblk, SB_HEADS // 2, tri_later)
            a_list = [sb]
            w_list = [w_out_odd[o].astype(BF16)]

        x1, qx = proj_residual_q(a_list, w_list, xf, row(norm_xattn_g[layer]), xa_wq[layer].astype(BF16))
        kv_mem = norm_proj(memf, row(mem_norm_g), xa_wkv[layer].astype(BF16))
        x2 = xattn_residual(qx, kv_mem, xa_wo[layer].astype(BF16), x1, batch, seq_len, mem_len)
        last = layer == depth - 1
        xf = mlp_residual(x2, row(norm_mlp_g[layer]), mlp_w1[layer].astype(BF16), mlp_w2[layer].astype(BF16),
                          row(final_norm_g), final_norm=last)
    return xf.reshape(batch, seq_len, d)
```

```python
import functools
import math

import numpy as np
import jax
import jax.numpy as jnp
from jax import lax
from jax.experimental import pallas as pl
from jax.experimental.pallas import tpu as pltpu

F32 = jnp.float32
BF16 = jnp.bfloat16

LANES = 128
HEAD_DIM = 64
FOX_HEADS = 8
NSA_HEADS = 8
NSA_GROUPS = 2
NSA_GQA = NSA_HEADS // NSA_GROUPS
SB_HEADS = 16
CMP_BLOCK = 32
CMP_STRIDE = 16
SLC_BLOCK = 64
SLC_TOPK = 8
WINDOW = 256
N_BUCKETS = 32
MAX_DISTANCE = 128
XA_HEADS = 4
XA_HEAD_DIM = 128
EPS = 1e-6
NEG = -1e30
FORCED_SCORE = 1e4
EXP_UNDERFLOW = -104.0

ATT_TILE = 256
ROW_TILE = 512
MLP_ROW_TILE = 1024
MLP_FF_TILE = 1024
GATE_ROWS = 8
F_PIECES = 3


def _dot(a, b):
    return jnp.dot(a, b, preferred_element_type=F32)


def _dot_nt(a, b):
    return lax.dot_general(a, b, (((1,), (1,)), ((), ())), preferred_element_type=F32)


def _split3(x):
    hi = x.astype(BF16)
    r1 = x - hi.astype(F32)
    mid = r1.astype(BF16)
    lo = (r1 - mid.astype(F32)).astype(BF16)
    return hi, mid, lo


def _dot_exact01(x, m01):
    hi, mid, lo = _split3(x)
    return _dot(hi, m01) + _dot(mid, m01) + _dot(lo, m01)


def _dot_01exact(m01, x):
    hi, mid, lo = _split3(x)
    return _dot(m01, hi) + _dot(m01, mid) + _dot(m01, lo)


def _rmsnorm(x, g):
    ms = jnp.mean(x * x, axis=-1, keepdims=True)
    return x * lax.rsqrt(ms + EPS) * g


def _sigmoid(x):
    return 1.0 / (1.0 + jnp.exp(-x))


def _softplus(x):
    return jnp.maximum(x, 0.0) + jnp.log1p(jnp.exp(-jnp.abs(x)))


def _split_pair(q, scale):
    lo = lax.broadcasted_iota(jnp.int32, q.shape, 1) < HEAD_DIM
    qs = q * jnp.asarray(scale, q.dtype)
    zero = jnp.zeros_like(qs)
    return jnp.where(lo, qs, zero), jnp.where(lo, zero, qs)


def _row_lo(shape):
    return lax.broadcasted_iota(jnp.int32, shape, 0) < HEAD_DIM


def _t5_bucket_np(dist):
    dist = np.maximum(dist, 0)
    max_exact = N_BUCKETS // 2
    d_f = np.maximum(dist, 1).astype(np.float64)
    large = max_exact + (np.log(d_f / max_exact) / math.log(MAX_DISTANCE / max_exact)
                         * (N_BUCKETS - max_exact)).astype(np.int32)
    large = np.minimum(large, N_BUCKETS - 1)
    return np.where(dist < max_exact, dist, large).astype(np.int32)


def _params(*sem):
    return pltpu.CompilerParams(dimension_semantics=sem)


def _full(a):
    return pl.BlockSpec(a.shape, lambda *_: (0,) * a.ndim)


def _norm_proj_kernel(x_ref, g_ref, w_ref, wt_ref, o_ref, ot_ref, *, tile):
    xn = _rmsnorm(x_ref[...], g_ref[...]).astype(BF16)
    o_ref[...] = _dot(xn, w_ref[...]).astype(o_ref.dtype)
    if wt_ref is not None:
        t = _dot_nt(wt_ref[...], xn).astype(ot_ref.dtype)
        for c in range(t.shape[1] // tile):
            ot_ref[c] = t[:, c * tile:(c + 1) * tile]


def norm_proj(x, g, w, w_t=None, tile=ATT_TILE, tm=ROW_TILE):
    n, d = x.shape
    p = w.shape[1]
    tm = min(tm, n)
    in_specs = [pl.BlockSpec((tm, d), lambda i: (i, 0)), _full(g), _full(w)]
    out_specs = [pl.BlockSpec((tm, p), lambda i: (i, 0))]
    out_shape = [jax.ShapeDtypeStruct((n, p), BF16)]
    args = [x, g, w]
    if w_t is None:
        body = lambda x_ref, g_ref, w_ref, o_ref: _norm_proj_kernel(
            x_ref, g_ref, w_ref, None, o_ref, None, tile=tile)
    else:
        pt = w_t.shape[0]
        in_specs.append(_full(w_t))
        out_specs.append(pl.BlockSpec((tm // tile, pt, tile), lambda i: (i, 0, 0)))
        out_shape.append(jax.ShapeDtypeStruct((n // tile, pt, tile), BF16))
        args.append(w_t)
        body = functools.partial(_norm_proj_kernel, tile=tile)
    out = pl.pallas_call(
        body, grid=(n // tm,), in_specs=in_specs, out_specs=out_specs, out_shape=out_shape,
        compiler_params=_params("parallel"), name="norm_proj",
    )(*args)
    return out if w_t is not None else out[0]


def _even_proj_kernel(x_ref, g_ref, w_ref, wt_ref, wkc_ref, wvc_ref, wff_ref, bf_ref, wng_ref, tri_ref,
                      main_ref, vt_ref, kc_ref, vc_ref, f_ref, gate_ref, carry_ref, *, tiles_per_seq, tile):
    i = pl.program_id(0)
    xn = _rmsnorm(x_ref[...], g_ref[...]).astype(BF16)
    main_ref[...] = _dot(xn, w_ref[...]).astype(BF16)
    t = _dot_nt(wt_ref[...], xn).astype(BF16)
    for c in range(t.shape[1] // tile):
        vt_ref[c] = t[:, c * tile:(c + 1) * tile]
    kc_ref[...] = _dot(xn, wkc_ref[...]).astype(BF16)
    vc_ref[...] = _dot(xn, wvc_ref[...]).astype(BF16)
    gate_ref[...] = _sigmoid(_dot_nt(wng_ref[...], xn))
    ff = _dot(xn, wff_ref[...]) + bf_ref[...]
    logf = jnp.minimum(ff, 0.0) - jnp.log1p(jnp.exp(-jnp.abs(ff)))

    @pl.when(i % tiles_per_seq == 0)
    def _():
        carry_ref[...] = jnp.zeros_like(carry_ref)

    cs = _dot_01exact(tri_ref[...], logf) + carry_ref[0:1, :]
    carry_ref[...] = jnp.broadcast_to(cs[cs.shape[0] - 1:, :], carry_ref.shape)
    hi, mid, lo = _split3(-cs)
    piece = lax.broadcasted_iota(jnp.int32, cs.shape, 1) % F_PIECES
    f_ref[...] = jnp.where(piece == 0, hi, jnp.where(piece == 1, mid, lo))


def even_proj(x, g, w_main, w_t, w_kc, w_vc, w_ff, b_f, w_ng_t, tri_l, seq_len, tile=ATT_TILE, tm=ROW_TILE):
    n, d = x.shape
    tm = min(tm, seq_len)
    pm = w_main.shape[1]
    pt = w_t.shape[0]
    gr = w_ng_t.shape[0]
    row = lambda width: pl.BlockSpec((tm, width), lambda i: (i, 0))
    return pl.pallas_call(
        functools.partial(_even_proj_kernel, tiles_per_seq=seq_len // tm, tile=tile),
        grid=(n // tm,),
        in_specs=[row(d), _full(g), _full(w_main), _full(w_t), _full(w_kc), _full(w_vc), _full(w_ff),
                  _full(b_f), _full(w_ng_t), _full(tri_l)],
        out_specs=[row(pm),
                   pl.BlockSpec((tm // tile, pt, tile), lambda i: (i, 0, 0)),
                   row(LANES), row(LANES), row(LANES),
                   pl.BlockSpec((gr, tm), lambda i: (0, i))],
        out_shape=[jax.ShapeDtypeStruct((n, pm), BF16),
                   jax.ShapeDtypeStruct((n // tile, pt, tile), BF16),
                   jax.ShapeDtypeStruct((n, LANES), BF16),
                   jax.ShapeDtypeStruct((n, LANES), BF16),
                   jax.ShapeDtypeStruct((n, LANES), BF16),
                   jax.ShapeDtypeStruct((gr, n), F32)],
        scratch_shapes=[pltpu.VMEM((8, LANES), F32)],
        compiler_params=_params("arbitrary"),
        name="even_proj",
    )(x, g, w_main, w_t, w_kc, w_vc, w_ff, b_f, w_ng_t, tri_l)


def _bucket_lookup_kernel(tab_ref, idx_ref, o_ref):
    h = pl.program_id(0)
    idx = idx_ref[...]
    out = jnp.full(idx.shape, NEG, F32)
    for b in range(N_BUCKETS):
        out = jnp.where(idx == b, tab_ref[b, h], out)
    o_ref[0] = out


def bucket_lookup(tab, idx, rows):
    r, c = idx.shape
    h = tab.shape[1]
    return pl.pallas_call(
        _bucket_lookup_kernel,
        grid=(h, r // rows),
        in_specs=[pl.BlockSpec(memory_space=pltpu.SMEM),
                  pl.BlockSpec((rows, c), lambda hh, j: (j, 0))],
        out_specs=pl.BlockSpec((1, rows, c), lambda hh, j: (hh, j, 0)),
        out_shape=jax.ShapeDtypeStruct((h, r, c), F32),
        compiler_params=_params("parallel", "parallel"),
        name="bucket_lookup",
    )(tab, idx)


def _fox_kernel(q_ref, k_ref, fs_ref, vt_ref, o_ref, stage_ref, *, tile, n_tiles):
    p = pl.program_id(1)
    lane = lax.broadcasted_iota(jnp.int32, (tile, LANES), 1)
    causal = (lax.broadcasted_iota(jnp.int32, (tile, tile), 0)
              <= lax.broadcasted_iota(jnp.int32, (tile, tile), 1))
    lo = _row_lo((LANES, tile))

    def keys(j):
        sl = slice(j * tile, (j + 1) * tile)
        return jnp.concatenate([k_ref[sl, :], fs_ref[sl, :]], axis=1)

    def queries(i):
        heads = _split_pair(q_ref[i * tile:(i + 1) * tile, :], HEAD_DIM ** -0.5)
        return tuple(jnp.concatenate(
            [heads[a], jnp.where(lane // F_PIECES == 2 * p + a, 1.0, 0.0).astype(BF16)], axis=1)
            for a in range(2))

    tasks = [(i, j) for i in range(n_tiles) for j in range(i + 1)]
    qcache = {}

    def scores(t):
        i, j = tasks[t]
        if i not in qcache:
            qcache[i] = queries(i)
        kj = keys(j)
        for a in range(2):
            stage_ref[t % 2, a] = _dot_nt(kj, qcache[i][a])

    scores(0)
    for t, (i, j) in enumerate(tasks):
        if t + 1 < len(tasks):
            scores(t + 1)
        qk = [stage_ref[t % 2, a] for a in range(2)]
        if j == 0:
            stats = [(jnp.full((1, tile), NEG, F32), jnp.zeros((1, tile), F32)) for _ in range(2)]
            acc = [jnp.zeros((LANES, tile), F32) for _ in range(2)]
        vt = vt_ref[j]
        alphas, pvs = [], []
        for a in range(2):
            m_prev, l_prev = stats[a]
            s = jnp.where(causal, qk[a], NEG) if j == i else qk[a]
            m_new = jnp.maximum(m_prev, jnp.max(s, axis=0, keepdims=True))
            alpha = jnp.exp(m_prev - m_new)
            pr = jnp.exp(s - m_new)
            stats[a] = (m_new, alpha * l_prev + jnp.sum(pr, axis=0, keepdims=True))
            alphas.append(alpha)
            pvs.append(_dot(vt, pr.astype(BF16)))
        for a in range(2):
            acc[a] = alphas[a] * acc[a] + pvs[a]
        if j == i:
            o = jnp.where(lo, acc[0] / stats[0][1], acc[1] / stats[1][1])
            o_ref[i * tile:(i + 1) * tile, :] = o.T.astype(o_ref.dtype)


def fox_attention(main, fs, vt, batch, seq_len, q_col, k_col, v_row, n_pairs, tile=ATT_TILE):
    nq = seq_len // tile
    n = batch * seq_len
    return pl.pallas_call(
        functools.partial(_fox_kernel, tile=tile, n_tiles=nq),
        grid=(batch, n_pairs),
        in_specs=[pl.BlockSpec((seq_len, LANES), lambda b, p: (b, q_col + p)),
                  pl.BlockSpec((seq_len, LANES), lambda b, p: (b, k_col + p)),
                  pl.BlockSpec((seq_len, LANES), lambda b, p: (b, 0)),
                  pl.BlockSpec((nq, LANES, tile), lambda b, p: (b, v_row + p, 0))],
        out_specs=pl.BlockSpec((seq_len, LANES), lambda b, p: (b, p)),
        out_shape=jax.ShapeDtypeStruct((n, n_pairs * LANES), BF16),
        scratch_shapes=[pltpu.VMEM((2, 2, tile, tile), F32)],
        compiler_params=_params("parallel", "parallel"),
        name="fox_attention",
    )(main, main, fs, vt)


def _sb_kernel(q_ref, k_ref, vt_ref, tri_ref, o_ref, acc_ref, stage_ref, *, tile, n_tiles):
    tri = tri_ref[...]
    strict = (lax.broadcasted_iota(jnp.int32, (tile, tile), 0)
              < lax.broadcasted_iota(jnp.int32, (tile, tile), 1))
    lo = _row_lo((LANES, tile))

    def queries(i):
        return _split_pair(q_ref[i * tile:(i + 1) * tile, :], HEAD_DIM ** -0.5)

    def scores(ks, heads):
        return tuple(_dot_nt(ks, heads[a]) for a in range(2))

    def consume(zz, rs, vt, masked):
        sps, base, later, pvs = [], [], [], []
        for a in range(2):
            z = zz[a]
            sp = jnp.maximum(z, 0.0) + jnp.log(1.0 + jnp.exp(-jnp.abs(z)))
            base.append(z - sp)
            if masked:
                sp = jnp.where(strict, sp, 0.0)
            sps.append(sp)
        for a in range(2):
            hi = sps[a].astype(BF16)
            mid = (sps[a] - hi.astype(F32)).astype(BF16)
            later.append(_dot(tri, hi) + _dot(tri, mid))
        for a in range(2):
            wgt = jnp.exp(base[a] - later[a] + rs[a])
            if masked:
                wgt = jnp.where(strict, wgt, 0.0)
            pvs.append(_dot(vt, wgt.astype(BF16)))
        return tuple(rs[a] - jnp.sum(sps[a], axis=0, keepdims=True) for a in range(2)), pvs

    def live(rs):
        return (jnp.max(jnp.maximum(rs[0], rs[1])) > EXP_UNDERFLOW).astype(jnp.int32)

    tasks = [(i, j) for i in range(n_tiles) for j in ((i, i - 1) if i else (i,))]
    qcache = {}

    def task_scores(t):
        i, j = tasks[t]
        if i not in qcache:
            qcache[i] = queries(i)
        zz = scores(k_ref[j * tile:(j + 1) * tile, :], qcache[i])
        for a in range(2):
            stage_ref[t % 2, a] = zz[a]

    zero = jnp.zeros((1, tile), F32)
    survival = {}
    task_scores(0)
    for t, (i, j) in enumerate(tasks):
        if t + 1 < len(tasks):
            task_scores(t + 1)
        zz = [stage_ref[t % 2, a] for a in range(2)]
        if j == i:
            rs, acc = (zero, zero), None
        rs, pvs = consume(zz, rs, vt_ref[j], j == i)
        acc = pvs if acc is None else [acc[a] + pvs[a] for a in range(2)]
        if j == max(i - 1, 0):
            for a in range(2):
                acc_ref[i, a] = acc[a]
            survival[i] = rs

    for i in range(2, n_tiles):
        heads = qcache[i]

        def cond(state):
            j, alive, _ = state
            return (j >= 0) & (alive > 0)

        def body(state, i=i, heads=heads):
            j, _, rs = state
            rs, pvs = consume(scores(k_ref[pl.ds(j * tile, tile), :], heads), rs, vt_ref[j], False)
            for a in range(2):
                acc_ref[i, a] = acc_ref[i, a] + pvs[a]
            return j - 1, live(rs), rs

        lax.while_loop(cond, body, (i - 2, live(survival[i]), survival[i]))

    for i in range(n_tiles):
        o = jnp.where(lo, acc_ref[i, 0], acc_ref[i, 1])
        o_ref[i * tile:(i + 1) * tile, :] = o.T.astype(o_ref.dtype)


def sb_attention(main, vt, batch, seq_len, q_col, k_col, n_pairs, tri, tile=ATT_TILE):
    nq = seq_len // tile
    n = batch * seq_len
    return pl.pallas_call(
        functools.partial(_sb_kernel, tile=tile, n_tiles=nq),
        grid=(batch, n_pairs),
        in_specs=[pl.BlockSpec((seq_len, LANES), lambda b, p: (b, q_col + p)),
                  pl.BlockSpec((seq_len, LANES), lambda b, p: (b, k_col + p)),
                  pl.BlockSpec((nq, LANES, tile), lambda b, p: (b, p, 0)),
                  _full(tri)],
        out_specs=pl.BlockSpec((seq_len, LANES), lambda b, p: (b, p)),
        out_shape=jax.ShapeDtypeStruct((n, n_pairs * LANES), BF16),
        scratch_shapes=[pltpu.VMEM((nq, 2, LANES, tile), F32), pltpu.VMEM((2, 2, tile, tile), F32)],
        compiler_params=_params("parallel", "parallel"),
        name="sb_attention",
    )(main, main, vt, tri)


def _gelu_tanh(x):
    return 0.5 * x * (1.0 + jnp.tanh(math.sqrt(2.0 / math.pi) * (x + 0.044715 * (x * x * x))))


def _compress_kernel(xk_ref, xv_ref, pek_ref, pev_ref, w1k_ref, w1v_ref, w2k_ref, w2v_ref,
                     ok_ref, ov_ref, *, n_cmp):
    def one(x_ref, pe_ref, w1_ref, w2_ref):
        x = x_ref[...].astype(F32)
        xa = (x + pe_ref[0:1, :]).astype(BF16)
        xb = (x + pe_ref[1:2, :]).astype(BF16)
        ha = _dot(xa, w1_ref[0])
        hb = _dot(xb, w1_ref[1])
        rows = ha.shape[0]
        hb = pltpu.roll(hb, rows - 1, 0)
        h = _gelu_tanh(ha + hb)
        out = _dot(h.astype(BF16), w2_ref[...])
        ridx = lax.broadcasted_iota(jnp.int32, out.shape, 0)
        return jnp.where(ridx < n_cmp, out, 0.0)

    ok_ref[0] = one(xk_ref, pek_ref, w1k_ref, w2k_ref).astype(ok_ref.dtype)
    ov_ref[0] = one(xv_ref, pev_ref, w1v_ref, w2v_ref).T.astype(ov_ref.dtype)


def compress(xk, xv, pek, pev, w1k, w1v, w2k, w2v, batch, n_cmp):
    chunks = xk.shape[0] // batch
    row = pl.BlockSpec((chunks, xk.shape[1]), lambda b: (b, 0))
    return pl.pallas_call(
        functools.partial(_compress_kernel, n_cmp=n_cmp),
        grid=(batch,),
        in_specs=[row, row, _full(pek), _full(pev), _full(w1k), _full(w1v), _full(w2k), _full(w2v)],
        out_specs=[pl.BlockSpec((1, chunks, LANES), lambda b: (b, 0, 0)),
                   pl.BlockSpec((1, LANES, chunks), lambda b: (b, 0, 0))],
        out_shape=[jax.ShapeDtypeStruct((batch, chunks, LANES), BF16),
                   jax.ShapeDtypeStruct((batch, LANES, chunks), BF16)],
        compiler_params=_params("parallel"),
        name="nsa_compress",
    )(xk, xv, pek, pev, w1k, w1v, w2k, w2v)


def _nsa_cmp_kernel(q_ref, kc_ref, vct_ref, cb_ref, ov_ref, ocmp_ref, sel_ref, *, tile, n_blocks):
    i = pl.program_id(1)
    kc = kc_ref[0]
    vct = vct_ref[0]
    pcsum = [jnp.zeros((LANES, tile), F32), jnp.zeros((LANES, tile), F32)]
    lo = _row_lo((LANES, tile))
    for r in range(NSA_GQA):
        heads = _split_pair(q_ref[:, r * LANES:(r + 1) * LANES], HEAD_DIM ** -0.5)
        outs = []
        for a in range(2):
            bias = cb_ref[2 * r + a]
            valid = bias > 0.5 * NEG
            s = _dot_nt(kc, heads[a]) + bias
            m = jnp.max(s, axis=0, keepdims=True)
            pr = jnp.where(valid, jnp.exp(s - m), 0.0)
            l = jnp.sum(pr, axis=0, keepdims=True)
            pc = pr / jnp.where(l > 0.0, l, 1.0)
            pcsum[a] = pcsum[a] + pc
            outs.append(_dot(vct, pc.astype(BF16)))
        ocmp_ref[r * LANES:(r + 1) * LANES, :] = jnp.where(lo, outs[0], outs[1])

    score = _dot_01exact(ov_ref[0], pcsum[0]) + _dot_01exact(ov_ref[1], pcsum[1])
    score = score[:2 * n_blocks]
    shape = (2 * n_blocks, tile)
    row = lax.broadcasted_iota(jnp.int32, shape, 0)
    qpos = i * tile + lax.broadcasted_iota(jnp.int32, shape, 1)
    blk = row % n_blocks
    cur = qpos // SLC_BLOCK
    forced = (blk == 0) | (blk == cur) | (blk == cur - 1)
    future = blk > cur
    score = jnp.where(future, -1.0, jnp.where(forced, FORCED_SCORE, score))
    grp0 = row < n_blocks
    cnt = jnp.zeros(shape, F32)
    for mth in range(n_blocks):
        other = jnp.where(grp0, score[mth:mth + 1, :], score[n_blocks + mth:n_blocks + mth + 1, :])
        ahead = (other > score) | ((other == score) & (blk > mth))
        cnt = cnt + jnp.where(ahead, 1.0, 0.0)
    n_top = min(SLC_TOPK, n_blocks)
    sel_ref[...] = jnp.where(cnt < n_top, 0.0, NEG)


def nsa_cmp_select(main, kc, vct, cmp_bias, overlap2, batch, seq_len, q_col, tile=ATT_TILE):
    nq = seq_len // tile
    n = batch * seq_len
    n_blocks = seq_len // SLC_BLOCK
    qw = NSA_GQA * LANES
    return pl.pallas_call(
        functools.partial(_nsa_cmp_kernel, tile=tile, n_blocks=n_blocks),
        grid=(batch, nq),
        in_specs=[pl.BlockSpec((tile, qw), lambda b, i: (b * nq + i, q_col // NSA_GQA)),
                  pl.BlockSpec((1,) + kc.shape[1:], lambda b, i: (b, 0, 0)),
                  pl.BlockSpec((1,) + vct.shape[1:], lambda b, i: (b, 0, 0)),
                  pl.BlockSpec((NSA_HEADS, cmp_bias.shape[1], tile), lambda b, i: (0, 0, i)),
                  _full(overlap2)],
        out_specs=[pl.BlockSpec((qw, tile), lambda b, i: (0, b * nq + i)),
                   pl.BlockSpec((2 * n_blocks, tile), lambda b, i: (0, b * nq + i))],
        out_shape=[jax.ShapeDtypeStruct((qw, n), F32),
                   jax.ShapeDtypeStruct((2 * n_blocks, n), F32)],
        compiler_params=_params("parallel", "parallel"),
        name="nsa_cmp_select",
    )(main, kc, vct, cmp_bias, overlap2)


def _nsa_main_kernel(far_ref, q_ref, sel_ref, ocmp_ref, gate_ref, ks_ref, vst_ref, kw_ref, vwt_ref, tz_ref,
                     o_ref, stage_ref, *, tile, n_tiles, n_blocks):
    r = pl.program_id(1)
    lo = _row_lo((LANES, tile))
    blocks_per_tile = tile // SLC_BLOCK
    far = tuple(far_ref[2 * r + a] for a in range(2))

    def queries(i):
        return _split_pair(q_ref[i * tile:(i + 1) * tile, :], HEAD_DIM ** -0.5)

    def chosen(a, i, j, offset=None):
        rows = []
        for nb in range(blocks_per_tile):
            blk = a * n_blocks + j * blocks_per_tile + nb
            row = sel_ref[blk:blk + 1, i * tile:(i + 1) * tile]
            if offset is not None:
                row = row + offset
            rows.append(jnp.broadcast_to(row, (SLC_BLOCK, tile)))
        return jnp.concatenate(rows, axis=0)

    tasks = []
    for i in range(n_tiles):
        tasks += [("slc", i, j) for j in range(i + 1)]
        tasks += [("win", i, j) for j in ((i - 1, i) if i else (i,))]
    qcache = {}

    def scores(t):
        kind, i, j = tasks[t]
        if i not in qcache:
            qcache[i] = queries(i)
        k_ref_ = ks_ref if kind == "slc" else kw_ref
        kj = k_ref_[j * tile:(j + 1) * tile, :]
        for a in range(2):
            stage_ref[t % 2, a] = _dot_nt(kj, qcache[i][a])

    def fresh():
        return ([(jnp.full((1, tile), NEG, F32), jnp.zeros((1, tile), F32)) for _ in range(2)],
                [jnp.zeros((LANES, tile), F32) for _ in range(2)])

    scores(0)
    for t, (kind, i, j) in enumerate(tasks):
        if t + 1 < len(tasks):
            scores(t + 1)
        qk = [stage_ref[t % 2, a] for a in range(2)]
        first = j == 0 if kind == "slc" else j == max(i - 1, 0)
        if first:
            stats, acc = fresh()
        vt = (vst_ref if kind == "slc" else vwt_ref)[j]
        alphas, pvs = [], []
        for a in range(2):
            if j == i:
                s = qk[a] + tz_ref[a, 0]
            elif j == i - 1:
                s = qk[a] + tz_ref[a, 1 if kind == "slc" else 2]
            if kind == "slc":
                s = s + chosen(a, i, j) if j >= i - 1 else qk[a] + chosen(a, i, j, far[a])
            m_prev, l_prev = stats[a]
            m_new = jnp.maximum(m_prev, jnp.max(s, axis=0, keepdims=True))
            alpha = jnp.exp(m_prev - m_new)
            pr = jnp.exp(s - m_new)
            stats[a] = (m_new, alpha * l_prev + jnp.sum(pr, axis=0, keepdims=True))
            alphas.append(alpha)
            pvs.append(_dot(vt, pr.astype(BF16)))
        for a in range(2):
            acc[a] = alphas[a] * acc[a] + pvs[a]
        if j == i:
            branch = jnp.where(lo, acc[0] / stats[0][1], acc[1] / stats[1][1])
            if kind == "slc":
                o_slc = branch
            else:
                cols = slice(i * tile, (i + 1) * tile)
                gexp = [jnp.where(lo, gate_ref[2 * c:2 * c + 1, cols], gate_ref[2 * c + 1:2 * c + 2, cols])
                        for c in range(3)]
                out = gexp[0] * ocmp_ref[:, cols] + gexp[1] * o_slc + gexp[2] * branch
                o_ref[cols, :] = out.T.astype(o_ref.dtype)


def nsa_main(far_bias, main, sel, ocmp, gates, vt, tz, batch, seq_len, q_col, ks_col, kw_col, vs_row, vw_row,
             tile=ATT_TILE):
    nq = seq_len // tile
    n = batch * seq_len
    n_blocks = seq_len // SLC_BLOCK
    kv = lambda col: pl.BlockSpec((seq_len, LANES), lambda b, r: (b, col))
    vts = lambda rowblk: pl.BlockSpec((nq, LANES, tile), lambda b, r: (b, rowblk, 0))
    return pl.pallas_call(
        functools.partial(_nsa_main_kernel, tile=tile, n_tiles=nq, n_blocks=n_blocks),
        grid=(batch, NSA_GQA),
        in_specs=[pl.BlockSpec(memory_space=pltpu.SMEM),
                  pl.BlockSpec((seq_len, LANES), lambda b, r: (b, q_col + r)),
                  pl.BlockSpec((2 * n_blocks, seq_len), lambda b, r: (0, b)),
                  pl.BlockSpec((LANES, seq_len), lambda b, r: (r, b)),
                  pl.BlockSpec((GATE_ROWS, seq_len), lambda b, r: (r, b)),
                  kv(ks_col), vts(vs_row), kv(kw_col), vts(vw_row),
                  pl.BlockSpec((2, 3, tile, tile), lambda b, r: (r, 0, 0, 0))],
        out_specs=pl.BlockSpec((seq_len, LANES), lambda b, r: (b, r)),
        out_shape=jax.ShapeDtypeStruct((n, NSA_GQA * LANES), BF16),
        scratch_shapes=[pltpu.VMEM((2, 2, tile, tile), F32)],
        compiler_params=_params("parallel", "parallel"),
        name="nsa_main",
    )(far_bias, main, sel, ocmp, gates, main, vt, main, vt, tz)


def _layer_tail_kernel(*refs, n_in, final_norm):
    a_refs = refs[:n_in]
    w_refs = refs[n_in:2 * n_in]
    (x_ref, gx_ref, wq_ref, kv_ref, wo_ref, gm_ref, w1_ref, w2_ref, gf_ref,
     o_ref, hn_ref, acc_ref) = refs[2 * n_in:]
    f = pl.program_id(1)

    @pl.when(f == 0)
    def _():
        x1 = x_ref[...]
        for a_ref, w_ref in zip(a_refs, w_refs):
            x1 = x1 + _dot(a_ref[...], w_ref[...])
        q = _dot(_rmsnorm(x1, gx_ref[...]).astype(BF16), wq_ref[...]).astype(BF16)
        width = XA_HEADS * XA_HEAD_DIM
        outs = []
        for h in range(XA_HEADS):
            hs = slice(h * XA_HEAD_DIM, (h + 1) * XA_HEAD_DIM)
            s = _dot_nt(q[:, hs], kv_ref[:, hs]) * (XA_HEAD_DIM ** -0.5)
            m = jnp.max(s, axis=-1, keepdims=True)
            p = jnp.exp(s - m)
            l = jnp.sum(p, axis=-1, keepdims=True)
            v = kv_ref[:, width + h * XA_HEAD_DIM:width + (h + 1) * XA_HEAD_DIM]
            outs.append((_dot(p.astype(BF16), v) / l).astype(BF16))
        x2 = x1 + _dot(jnp.concatenate(outs, axis=-1), wo_ref[...])
        hn_ref[...] = _rmsnorm(x2, gm_ref[...]).astype(BF16)
        acc_ref[...] = x2

    h = jnp.maximum(_dot(hn_ref[...], w1_ref[...]), 0.0)
    acc_ref[...] += _dot((h * h).astype(BF16), w2_ref[...])

    @pl.when(f == pl.num_programs(1) - 1)
    def _():
        y = acc_ref[...]
        if final_norm:
            y = _rmsnorm(y, gf_ref[...])
        o_ref[...] = y


def layer_tail(a_list, w_list, x, gx, wq, kv, wo, gm, w1, w2, gf, final_norm, seq_len, mem_len,
               tm=MLP_ROW_TILE, tf=MLP_FF_TILE):
    n, d = x.shape
    ff = w1.shape[1]
    tm = min(tm, seq_len)
    tf = min(tf, ff)
    tiles_per_seq = seq_len // tm
    once = pl.Buffered(1)
    row = lambda a: pl.BlockSpec((tm, a.shape[1]), lambda i, f: (i, 0))
    const = lambda a: pl.BlockSpec(a.shape, lambda i, f: (0,) * a.ndim, pipeline_mode=once)
    return pl.pallas_call(
        functools.partial(_layer_tail_kernel, n_in=len(a_list), final_norm=final_norm),
        grid=(n // tm, ff // tf),
        in_specs=[row(a) for a in a_list] + [const(w) for w in w_list]
        + [row(x), const(gx), const(wq),
           pl.BlockSpec((mem_len, kv.shape[1]), lambda i, f: (i // tiles_per_seq, 0)),
           const(wo), const(gm),
           pl.BlockSpec((d, tf), lambda i, f: (0, f)),
           pl.BlockSpec((tf, d), lambda i, f: (f, 0)),
           const(gf)],
        out_specs=pl.BlockSpec((tm, d), lambda i, f: (i, 0)),
        out_shape=jax.ShapeDtypeStruct((n, d), F32),
        scratch_shapes=[pltpu.VMEM((tm, d), BF16), pltpu.VMEM((tm, d), F32)],
        compiler_params=_params("parallel", "arbitrary"),
        name="layer_tail",
    )(*a_list, *w_list, x, gx, wq, kv, wo, gm, w1, w2, gf)


def _static_tables(seq_len):
    tile = ATT_TILE
    n_chunks = seq_len // CMP_STRIDE
    n_cmp = (seq_len - CMP_BLOCK) // CMP_STRIDE + 1
    n_slc = seq_len // SLC_BLOCK
    assert n_chunks == LANES and 2 * n_slc <= LANES and seq_len % tile == 0 and WINDOW == tile
    c = np.arange(n_chunks)[:, None]
    t = np.arange(seq_len)[None, :]
    cdist = t - (c * CMP_STRIDE + CMP_BLOCK - 1)
    cmp_idx = np.where((cdist >= 0) & (c < n_cmp), _t5_bucket_np(cdist), -1).astype(np.int32)
    sl = np.arange(tile)[:, None]
    tl = np.arange(tile)[None, :]
    prev = _t5_bucket_np(tl - sl + tile)
    tz_idx = np.concatenate([np.where(sl <= tl, _t5_bucket_np(tl - sl), -1), prev,
                             np.where(sl > tl, prev, -1)], axis=0).astype(np.int32)
    far_bucket = int(_t5_bucket_np(np.array([tile + 1]))[0])
    assert far_bucket == int(_t5_bucket_np(np.array([seq_len]))[0])
    cmp_start = np.arange(n_cmp) * CMP_STRIDE
    cmp_stop = cmp_start + CMP_BLOCK - 1
    slc_start = np.arange(n_slc) * SLC_BLOCK
    slc_stop = slc_start + SLC_BLOCK - 1
    ov = ((cmp_start[:, None] <= slc_stop[None, :]) & (cmp_stop[:, None] >= slc_start[None, :]))
    overlap2 = np.zeros((2, LANES, LANES), np.float32)
    for a in range(2):
        overlap2[a, a * n_slc:(a + 1) * n_slc, :n_cmp] = ov.T
    return cmp_idx, tz_idx, far_bucket, overlap2, n_cmp


def _tri_prefix(nn):
    s = np.arange(nn)[:, None]
    j = np.arange(nn)[None, :]
    return (j <= s).astype(np.float32)


def _tri_later(nn):
    s = np.arange(nn)[:, None]
    j = np.arange(nn)[None, :]
    return (j > s).astype(np.float32)


_NSA_HEAD_PERM = [a * NSA_GQA + r for r in range(NSA_GQA) for a in range(NSA_GROUPS)]


def _perm_head_cols(w, perm):
    d = w.shape[0]
    return w.reshape(d, len(perm), HEAD_DIM)[:, perm, :].reshape(d, len(perm) * HEAD_DIM)


def _block_diag2(m):
    z = jnp.zeros_like(m)
    return jnp.concatenate([jnp.concatenate([m, z], axis=1), jnp.concatenate([z, m], axis=1)], axis=0)


def _compress_weights(pe, w1, w2):
    half = CMP_BLOCK // 2
    pe2 = jnp.concatenate([pe, pe], axis=1).reshape(2, half * LANES)
    w1e = jax.vmap(_block_diag2)(w1).reshape(2, half * LANES, LANES)
    return pe2.astype(F32), w1e.astype(BF16), _block_diag2(w2).astype(BF16)


def kernel(x, mem, rel_bias, mem_norm_g, norm_mix_g, norm_xattn_g, norm_mlp_g, final_norm_g, w_in_even, b_forget, cmp_pe_k, cmp_w1_k, cmp_w2_k, cmp_pe_v, cmp_w1_v, cmp_w2_v, w_out_even, w_in_odd, w_out_odd, xa_wq, xa_wkv, xa_wo, mlp_w1, mlp_w2):
    batch, seq_len, d = x.shape
    mem_len = mem.shape[1]
    depth = norm_mix_g.shape[0]
    n = batch * seq_len
    fox_w = FOX_HEADS * HEAD_DIM
    nsa_w = NSA_HEADS * HEAD_DIM
    kv_w = NSA_GROUPS * HEAD_DIM
    splits = np.cumsum([fox_w, fox_w, fox_w, FOX_HEADS, nsa_w, kv_w, kv_w, kv_w, kv_w, kv_w, kv_w])

    cmp_idx, tz_idx, far_bucket, overlap2, n_cmp = _static_tables(seq_len)
    tab = rel_bias.astype(F32)[:, _NSA_HEAD_PERM]
    cmp_bias = bucket_lookup(tab, jnp.asarray(cmp_idx), rows=cmp_idx.shape[0])
    tz = bucket_lookup(tab, jnp.asarray(tz_idx), rows=ATT_TILE)
    tz = tz.reshape(NSA_HEADS, 3, ATT_TILE, ATT_TILE)
    far_bias = tab[far_bucket]
    overlap2 = jnp.asarray(overlap2, BF16)
    tri_prefix = jnp.asarray(_tri_prefix(min(ROW_TILE, seq_len)), BF16)
    tri_later = jnp.asarray(_tri_later(ATT_TILE), BF16)

    xf = x.reshape(n, d)
    memf = mem.reshape(batch * mem_len, d)
    row = lambda v: v.reshape(1, -1).astype(F32)

    for layer in range(depth):
        g_mix = row(norm_mix_g[layer])
        if layer % 2 == 0:
            e = layer // 2
            (w_fq, w_fk, w_fv, w_ff, w_nq, w_kc, w_vc, w_ks, w_vs, w_kw, w_vw, w_ng) = jnp.split(
                w_in_even[e], splits, axis=1)
            w_main = jnp.concatenate([w_fq, w_fk, _perm_head_cols(w_nq, _NSA_HEAD_PERM), w_ks, w_kw],
                                     axis=1).astype(BF16)
            w_t = jnp.concatenate([w_fv, w_vs, w_vw], axis=1).T.astype(BF16)
            w_ng_t = w_ng.reshape(d, NSA_GROUPS, NSA_GQA, 3).transpose(2, 3, 1, 0).reshape(NSA_GQA, 6, d)
            w_ng_t = jnp.pad(w_ng_t, ((0, 0), (0, GATE_ROWS - 6), (0, 0))).reshape(NSA_GQA * GATE_ROWS, d)
            pad_f = LANES - F_PIECES * FOX_HEADS
            w_ff_p = jnp.pad(jnp.repeat(w_ff, F_PIECES, axis=1), ((0, 0), (0, pad_f)))
            b_f = jnp.pad(jnp.repeat(b_forget[e].astype(F32), F_PIECES), (0, pad_f)).reshape(1, LANES)
            main, vt, kc_in, vc_in, fs, gates = even_proj(
                xf, g_mix, w_main, w_t, w_kc.astype(BF16), w_vc.astype(BF16), w_ff_p.astype(BF16), b_f,
                w_ng_t.astype(BF16), tri_prefix, seq_len)
            nblk = fox_w // LANES
            fox = fox_attention(main, fs, vt, batch, seq_len, 0, nblk, 0, FOX_HEADS // 2)

            pek, w1k, w2k = _compress_weights(cmp_pe_k[e], cmp_w1_k[e], cmp_w2_k[e])
            pev, w1v, w2v = _compress_weights(cmp_pe_v[e], cmp_w1_v[e], cmp_w2_v[e])
            chunk_w = CMP_STRIDE * LANES
            kc, vct = compress(kc_in.reshape(n // CMP_STRIDE, chunk_w), vc_in.reshape(n // CMP_STRIDE, chunk_w),
                               pek, pev, w1k, w1v, w2k, w2v, batch, n_cmp)
            q_col = 2 * nblk
            ocmp, sel = nsa_cmp_select(main, kc, vct, cmp_bias, overlap2, batch, seq_len, q_col)
            nsa = nsa_main(far_bias, main, sel, ocmp, gates, vt, tz, batch, seq_len, q_col,
                           q_col + NSA_GQA, q_col + NSA_GQA + 1, nblk, nblk + 1)
            w_o = w_out_even[e]
            w_o_nsa = w_o[fox_w:].reshape(NSA_HEADS, HEAD_DIM, d)[jnp.asarray(_NSA_HEAD_PERM)].reshape(nsa_w, d)
            a_list = [fox, nsa]
            w_list = [w_o[:fox_w].astype(BF16), w_o_nsa.astype(BF16)]
        else:
            o = layer // 2
            sb_w = SB_HEADS * HEAD_DIM
            w_in = w_in_odd[o]
            main, vt = norm_proj(xf, g_mix, w_in[:, :2 * sb_w].astype(BF16), w_in[:, 2 * sb_w:].T.astype(BF16))
            nblk = sb_w // LANES
            sb = sb_attention(main, vt, batch, seq_len, 0, nblk, SB_HEADS // 2, tri_later)
            a_list = [sb]
            w_list = [w_out_odd[o].astype(BF16)]

        kv_mem = norm_proj(memf, row(mem_norm_g), xa_wkv[layer].astype(BF16))
        xf = layer_tail(a_list, w_list, xf, row(norm_xattn_g[layer]), xa_wq[layer].astype(BF16), kv_mem,
                        xa_wo[layer].astype(BF16), row(norm_mlp_g[layer]), mlp_w1[layer].astype(BF16),
                        mlp_w2[layer].astype(BF16), row(final_norm_g), layer == depth - 1, seq_len, mem_len)
    return xf.reshape(batch, seq_len, d)
```

```python
import functools
import math

import numpy as np
import jax
import jax.numpy as jnp
from jax import lax
from jax.experimental import pallas as pl
from jax.experimental.pallas import tpu as pltpu

F32 = jnp.float32
BF16 = jnp.bfloat16

LANES = 128
HEAD_DIM = 64
FOX_HEADS = 8
NSA_HEADS = 8
NSA_GROUPS = 2
NSA_GQA = NSA_HEADS // NSA_GROUPS
SB_HEADS = 16
CMP_BLOCK = 32
CMP_STRIDE = 16
SLC_BLOCK = 64
SLC_TOPK = 8
WINDOW = 256
N_BUCKETS = 32
MAX_DISTANCE = 128
XA_HEADS = 4
XA_HEAD_DIM = 128
EPS = 1e-6
NEG = -1e30
FORCED_SCORE = 1e4
EXP_UNDERFLOW = -104.0

ATT_TILE = 256
ROW_TILE = 512
MLP_ROW_TILE = 1024
MLP_FF_TILE = 1024
GATE_ROWS = 8
F_PIECES = 3


def _dot(a, b):
    return jnp.dot(a, b, preferred_element_type=F32)


def _dot_nt(a, b):
    return lax.dot_general(a, b, (((1,), (1,)), ((), ())), preferred_element_type=F32)


def _split3(x):
    hi = x.astype(BF16)
    r1 = x - hi.astype(F32)
    mid = r1.astype(BF16)
    lo = (r1 - mid.astype(F32)).astype(BF16)
    return hi, mid, lo


def _dot_exact01(x, m01):
    hi, mid, lo = _split3(x)
    return _dot(hi, m01) + _dot(mid, m01) + _dot(lo, m01)


def _dot_01exact(m01, x):
    hi, mid, lo = _split3(x)
    return _dot(m01, hi) + _dot(m01, mid) + _dot(m01, lo)


def _rmsnorm(x, g):
    ms = jnp.mean(x * x, axis=-1, keepdims=True)
    return x * lax.rsqrt(ms + EPS) * g


def _sigmoid(x):
    return 1.0 / (1.0 + jnp.exp(-x))


def _softplus(x):
    return jnp.maximum(x, 0.0) + jnp.log1p(jnp.exp(-jnp.abs(x)))


def _split_pair(q, scale):
    lo = lax.broadcasted_iota(jnp.int32, q.shape, 1) < HEAD_DIM
    qs = q * jnp.asarray(scale, q.dtype)
    zero = jnp.zeros_like(qs)
    return jnp.where(lo, qs, zero), jnp.where(lo, zero, qs)


def _row_lo(shape):
    return lax.broadcasted_iota(jnp.int32, shape, 0) < HEAD_DIM


def _t5_bucket_np(dist):
    dist = np.maximum(dist, 0)
    max_exact = N_BUCKETS // 2
    d_f = np.maximum(dist, 1).astype(np.float64)
    large = max_exact + (np.log(d_f / max_exact) / math.log(MAX_DISTANCE / max_exact)
                         * (N_BUCKETS - max_exact)).astype(np.int32)
    large = np.minimum(large, N_BUCKETS - 1)
    return np.where(dist < max_exact, dist, large).astype(np.int32)


def _params(*sem):
    return pltpu.CompilerParams(dimension_semantics=sem)


def _full(a):
    return pl.BlockSpec(a.shape, lambda *_: (0,) * a.ndim)


def _norm_proj_kernel(x_ref, g_ref, w_ref, wt_ref, o_ref, ot_ref, *, tile):
    xn = _rmsnorm(x_ref[...], g_ref[...]).astype(BF16)
    o_ref[...] = _dot(xn, w_ref[...]).astype(o_ref.dtype)
    if wt_ref is not None:
        t = _dot_nt(wt_ref[...], xn).astype(ot_ref.dtype)
        for c in range(t.shape[1] // tile):
            ot_ref[c] = t[:, c * tile:(c + 1) * tile]


def norm_proj(x, g, w, w_t=None, tile=ATT_TILE, tm=ROW_TILE):
    n, d = x.shape
    p = w.shape[1]
    tm = min(tm, n)
    in_specs = [pl.BlockSpec((tm, d), lambda i: (i, 0)), _full(g), _full(w)]
    out_specs = [pl.BlockSpec((tm, p), lambda i: (i, 0))]
    out_shape = [jax.ShapeDtypeStruct((n, p), BF16)]
    args = [x, g, w]
    if w_t is None:
        body = lambda x_ref, g_ref, w_ref, o_ref: _norm_proj_kernel(
            x_ref, g_ref, w_ref, None, o_ref, None, tile=tile)
    else:
        pt = w_t.shape[0]
        in_specs.append(_full(w_t))
        out_specs.append(pl.BlockSpec((tm // tile, pt, tile), lambda i: (i, 0, 0)))
        out_shape.append(jax.ShapeDtypeStruct((n // tile, pt, tile), BF16))
        args.append(w_t)
        body = functools.partial(_norm_proj_kernel, tile=tile)
    out = pl.pallas_call(
        body, grid=(n // tm,), in_specs=in_specs, out_specs=out_specs, out_shape=out_shape,
        compiler_params=_params("parallel"), name="norm_proj",
    )(*args)
    return out if w_t is not None else out[0]


def _even_proj_kernel(x_ref, g_ref, w_ref, wt_ref, wkc_ref, wvc_ref, wff_ref, bf_ref, wng_ref, tri_ref,
                      main_ref, vt_ref, kc_ref, vc_ref, f_ref, gate_ref, carry_ref, *, tiles_per_seq, tile):
    i = pl.program_id(0)
    xn = _rmsnorm(x_ref[...], g_ref[...]).astype(BF16)
    main_ref[...] = _dot(xn, w_ref[...]).astype(BF16)
    t = _dot_nt(wt_ref[...], xn).astype(BF16)
    for c in range(t.shape[1] // tile):
        vt_ref[c] = t[:, c * tile:(c + 1) * tile]
    kc_ref[...] = _dot(xn, wkc_ref[...]).astype(BF16)
    vc_ref[...] = _dot(xn, wvc_ref[...]).astype(BF16)
    gate_ref[...] = _sigmoid(_dot_nt(wng_ref[...], xn))
    ff = _dot(xn, wff_ref[...]) + bf_ref[...]
    logf = jnp.minimum(ff, 0.0) - jnp.log1p(jnp.exp(-jnp.abs(ff)))

    @pl.when(i % tiles_per_seq == 0)
    def _():
        carry_ref[...] = jnp.zeros_like(carry_ref)

    cs = _dot_01exact(tri_ref[...], logf) + carry_ref[0:1, :]
    carry_ref[...] = jnp.broadcast_to(cs[cs.shape[0] - 1:, :], carry_ref.shape)
    hi, mid, lo = _split3(-cs)
    piece = lax.broadcasted_iota(jnp.int32, cs.shape, 1) % F_PIECES
    f_ref[...] = jnp.where(piece == 0, hi, jnp.where(piece == 1, mid, lo))


def even_proj(x, g, w_main, w_t, w_kc, w_vc, w_ff, b_f, w_ng_t, tri_l, seq_len, tile=ATT_TILE, tm=ROW_TILE):
    n, d = x.shape
    tm = min(tm, seq_len)
    pm = w_main.shape[1]
    pt = w_t.shape[0]
    gr = w_ng_t.shape[0]
    row = lambda width: pl.BlockSpec((tm, width), lambda i: (i, 0))
    return pl.pallas_call(
        functools.partial(_even_proj_kernel, tiles_per_seq=seq_len // tm, tile=tile),
        grid=(n // tm,),
        in_specs=[row(d), _full(g), _full(w_main), _full(w_t), _full(w_kc), _full(w_vc), _full(w_ff),
                  _full(b_f), _full(w_ng_t), _full(tri_l)],
        out_specs=[row(pm),
                   pl.BlockSpec((tm // tile, pt, tile), lambda i: (i, 0, 0)),
                   row(LANES), row(LANES), row(LANES),
                   pl.BlockSpec((gr, tm), lambda i: (0, i))],
        out_shape=[jax.ShapeDtypeStruct((n, pm), BF16),
                   jax.ShapeDtypeStruct((n // tile, pt, tile), BF16),
                   jax.ShapeDtypeStruct((n, LANES), BF16),
                   jax.ShapeDtypeStruct((n, LANES), BF16),
                   jax.ShapeDtypeStruct((n, LANES), BF16),
                   jax.ShapeDtypeStruct((gr, n), F32)],
        scratch_shapes=[pltpu.VMEM((8, LANES), F32)],
        compiler_params=_params("arbitrary"),
        name="even_proj",
    )(x, g, w_main, w_t, w_kc, w_vc, w_ff, b_f, w_ng_t, tri_l)


def _bucket_lookup_kernel(tab_ref, idx_ref, o_ref):
    h = pl.program_id(0)
    idx = idx_ref[...]
    out = jnp.full(idx.shape, NEG, F32)
    for b in range(N_BUCKETS):
        out = jnp.where(idx == b, tab_ref[b, h], out)
    o_ref[0] = out


def bucket_lookup(tab, idx, rows):
    r, c = idx.shape
    h = tab.shape[1]
    return pl.pallas_call(
        _bucket_lookup_kernel,
        grid=(h, r // rows),
        in_specs=[pl.BlockSpec(memory_space=pltpu.SMEM),
                  pl.BlockSpec((rows, c), lambda hh, j: (j, 0))],
        out_specs=pl.BlockSpec((1, rows, c), lambda hh, j: (hh, j, 0)),
        out_shape=jax.ShapeDtypeStruct((h, r, c), F32),
        compiler_params=_params("parallel", "parallel"),
        name="bucket_lookup",
    )(tab, idx)


def _fox_kernel(q_ref, k_ref, fs_ref, vt_ref, o_ref, stage_ref, *, tile, n_tiles):
    p = pl.program_id(1)
    lane = lax.broadcasted_iota(jnp.int32, (tile, LANES), 1)
    causal = (lax.broadcasted_iota(jnp.int32, (tile, tile), 0)
              <= lax.broadcasted_iota(jnp.int32, (tile, tile), 1))
    lo = _row_lo((LANES, tile))

    def keys(j):
        sl = slice(j * tile, (j + 1) * tile)
        return jnp.concatenate([k_ref[sl, :], fs_ref[sl, :]], axis=1)

    def queries(i):
        heads = _split_pair(q_ref[i * tile:(i + 1) * tile, :], HEAD_DIM ** -0.5)
        return tuple(jnp.concatenate(
            [heads[a], jnp.where(lane // F_PIECES == 2 * p + a, 1.0, 0.0).astype(BF16)], axis=1)
            for a in range(2))

    tasks = [(i, j) for i in range(n_tiles) for j in range(i + 1)]
    qcache = {}

    def scores(t):
        i, j = tasks[t]
        if i not in qcache:
            qcache[i] = queries(i)
        kj = keys(j)
        for a in range(2):
            stage_ref[t % 2, a] = _dot_nt(kj, qcache[i][a])

    scores(0)
    for t, (i, j) in enumerate(tasks):
        if t + 1 < len(tasks):
            scores(t + 1)
        qk = [stage_ref[t % 2, a] for a in range(2)]
        if j == 0:
            stats = [(jnp.full((1, tile), NEG, F32), jnp.zeros((1, tile), F32)) for _ in range(2)]
            acc = [jnp.zeros((LANES, tile), F32) for _ in range(2)]
        vt = vt_ref[j]
        alphas, pvs = [], []
        for a in range(2):
            m_prev, l_prev = stats[a]
            s = jnp.where(causal, qk[a], NEG) if j == i else qk[a]
            m_new = jnp.maximum(m_prev, jnp.max(s, axis=0, keepdims=True))
            alpha = jnp.exp(m_prev - m_new)
            pr = jnp.exp(s - m_new)
            stats[a] = (m_new, alpha * l_prev + jnp.sum(pr, axis=0, keepdims=True))
            alphas.append(alpha)
            pvs.append(_dot(vt, pr.astype(BF16)))
        for a in range(2):
            acc[a] = alphas[a] * acc[a] + pvs[a]
        if j == i:
            o = jnp.where(lo, acc[0] / stats[0][1], acc[1] / stats[1][1])
            o_ref[i * tile:(i + 1) * tile, :] = o.T.astype(o_ref.dtype)


def fox_attention(main, fs, vt, batch, seq_len, q_col, k_col, v_row, n_pairs, tile=ATT_TILE):
    nq = seq_len // tile
    n = batch * seq_len
    return pl.pallas_call(
        functools.partial(_fox_kernel, tile=tile, n_tiles=nq),
        grid=(batch, n_pairs),
        in_specs=[pl.BlockSpec((seq_len, LANES), lambda b, p: (b, q_col + p)),
                  pl.BlockSpec((seq_len, LANES), lambda b, p: (b, k_col + p)),
                  pl.BlockSpec((seq_len, LANES), lambda b, p: (b, 0)),
                  pl.BlockSpec((nq, LANES, tile), lambda b, p: (b, v_row + p, 0))],
        out_specs=pl.BlockSpec((seq_len, LANES), lambda b, p: (b, p)),
        out_shape=jax.ShapeDtypeStruct((n, n_pairs * LANES), BF16),
        scratch_shapes=[pltpu.VMEM((2, 2, tile, tile), F32)],
        compiler_params=_params("parallel", "parallel"),
        name="fox_attention",
    )(main, main, fs, vt)


def _sb_kernel(q_ref, k_ref, vt_ref, tri_ref, o_ref, acc_ref, stage_ref, *, tile, n_tiles):
    tri = tri_ref[...]
    strict = (lax.broadcasted_iota(jnp.int32, (tile, tile), 0)
              < lax.broadcasted_iota(jnp.int32, (tile, tile), 1))
    lo = _row_lo((LANES, tile))

    def queries(i):
        return _split_pair(q_ref[i * tile:(i + 1) * tile, :], HEAD_DIM ** -0.5)

    def scores(ks, heads):
        return tuple(_dot_nt(ks, heads[a]) for a in range(2))

    def consume(zz, rs, vt, masked):
        sps, base, later, pvs = [], [], [], []
        for a in range(2):
            z = zz[a]
            sp = jnp.maximum(z, 0.0) + jnp.log(1.0 + jnp.exp(-jnp.abs(z)))
            base.append(z - sp)
            if masked:
                sp = jnp.where(strict, sp, 0.0)
            sps.append(sp)
        for a in range(2):
            later.append(_dot(tri, sps[a].astype(BF16)))
        for a in range(2):
            wgt = jnp.exp(base[a] - later[a] + rs[a])
            if masked:
                wgt = jnp.where(strict, wgt, 0.0)
            pvs.append(_dot(vt, wgt.astype(BF16)))
        return tuple(rs[a] - jnp.sum(sps[a], axis=0, keepdims=True) for a in range(2)), pvs

    def live(rs):
        return (jnp.max(jnp.maximum(rs[0], rs[1])) > EXP_UNDERFLOW).astype(jnp.int32)

    tasks = [(i, j) for i in range(n_tiles) for j in ((i, i - 1) if i else (i,))]
    qcache = {}

    def task_scores(t):
        i, j = tasks[t]
        if i not in qcache:
            qcache[i] = queries(i)
        zz = scores(k_ref[j * tile:(j + 1) * tile, :], qcache[i])
        for a in range(2):
            stage_ref[t % 2, a] = zz[a]

    zero = jnp.zeros((1, tile), F32)
    survival = {}
    task_scores(0)
    for t, (i, j) in enumerate(tasks):
        if t + 1 < len(tasks):
            task_scores(t + 1)
        zz = [stage_ref[t % 2, a] for a in range(2)]
        if j == i:
            rs, acc = (zero, zero), None
        rs, pvs = consume(zz, rs, vt_ref[j], j == i)
        acc = pvs if acc is None else [acc[a] + pvs[a] for a in range(2)]
        if j == max(i - 1, 0):
            for a in range(2):
                acc_ref[i, a] = acc[a]
            survival[i] = rs

    for i in range(2, n_tiles):
        heads = qcache[i]

        def cond(state):
            j, alive, _ = state
            return (j >= 0) & (alive > 0)

        def body(state, i=i, heads=heads):
            j, _, rs = state
            rs, pvs = consume(scores(k_ref[pl.ds(j * tile, tile), :], heads), rs, vt_ref[j], False)
            for a in range(2):
                acc_ref[i, a] = acc_ref[i, a] + pvs[a]
            return j - 1, live(rs), rs

        lax.while_loop(cond, body, (i - 2, live(survival[i]), survival[i]))

    for i in range(n_tiles):
        o = jnp.where(lo, acc_ref[i, 0], acc_ref[i, 1])
        o_ref[i * tile:(i + 1) * tile, :] = o.T.astype(o_ref.dtype)


def sb_attention(main, vt, batch, seq_len, q_col, k_col, n_pairs, tri, tile=ATT_TILE):
    nq = seq_len // tile
    n = batch * seq_len
    return pl.pallas_call(
        functools.partial(_sb_kernel, tile=tile, n_tiles=nq),
        grid=(batch, n_pairs),
        in_specs=[pl.BlockSpec((seq_len, LANES), lambda b, p: (b, q_col + p)),
                  pl.BlockSpec((seq_len, LANES), lambda b, p: (b, k_col + p)),
                  pl.BlockSpec((nq, LANES, tile), lambda b, p: (b, p, 0)),
                  _full(tri)],
        out_specs=pl.BlockSpec((seq_len, LANES), lambda b, p: (b, p)),
        out_shape=jax.ShapeDtypeStruct((n, n_pairs * LANES), BF16),
        scratch_shapes=[pltpu.VMEM((nq, 2, LANES, tile), F32), pltpu.VMEM((2, 2, tile, tile), F32)],
        compiler_params=_params("parallel", "parallel"),
        name="sb_attention",
    )(main, main, vt, tri)


def _gelu_tanh(x):
    return 0.5 * x * (1.0 + jnp.tanh(math.sqrt(2.0 / math.pi) * (x + 0.044715 * (x * x * x))))


def _compress_kernel(xk_ref, xv_ref, pek_ref, pev_ref, w1k_ref, w1v_ref, w2k_ref, w2v_ref,
                     ok_ref, ov_ref, *, n_cmp):
    def one(x_ref, pe_ref, w1_ref, w2_ref):
        x = x_ref[...].astype(F32)
        xa = (x + pe_ref[0:1, :]).astype(BF16)
        xb = (x + pe_ref[1:2, :]).astype(BF16)
        ha = _dot(xa, w1_ref[0])
        hb = _dot(xb, w1_ref[1])
        rows = ha.shape[0]
        hb = pltpu.roll(hb, rows - 1, 0)
        h = _gelu_tanh(ha + hb)
        out = _dot(h.astype(BF16), w2_ref[...])
        ridx = lax.broadcasted_iota(jnp.int32, out.shape, 0)
        return jnp.where(ridx < n_cmp, out, 0.0)

    ok_ref[0] = one(xk_ref, pek_ref, w1k_ref, w2k_ref).astype(ok_ref.dtype)
    ov_ref[0] = one(xv_ref, pev_ref, w1v_ref, w2v_ref).T.astype(ov_ref.dtype)


def compress(xk, xv, pek, pev, w1k, w1v, w2k, w2v, batch, n_cmp):
    chunks = xk.shape[0] // batch
    row = pl.BlockSpec((chunks, xk.shape[1]), lambda b: (b, 0))
    return pl.pallas_call(
        functools.partial(_compress_kernel, n_cmp=n_cmp),
        grid=(batch,),
        in_specs=[row, row, _full(pek), _full(pev), _full(w1k), _full(w1v), _full(w2k), _full(w2v)],
        out_specs=[pl.BlockSpec((1, chunks, LANES), lambda b: (b, 0, 0)),
                   pl.BlockSpec((1, LANES, chunks), lambda b: (b, 0, 0))],
        out_shape=[jax.ShapeDtypeStruct((batch, chunks, LANES), BF16),
                   jax.ShapeDtypeStruct((batch, LANES, chunks), BF16)],
        compiler_params=_params("parallel"),
        name="nsa_compress",
    )(xk, xv, pek, pev, w1k, w1v, w2k, w2v)


def _nsa_cmp_kernel(q_ref, kc_ref, vct_ref, cb_ref, ov_ref, ocmp_ref, sel_ref, *, tile, n_blocks):
    i = pl.program_id(1)
    kc = kc_ref[0]
    vct = vct_ref[0]
    pcsum = [jnp.zeros((LANES, tile), F32), jnp.zeros((LANES, tile), F32)]
    lo = _row_lo((LANES, tile))
    for r in range(NSA_GQA):
        heads = _split_pair(q_ref[:, r * LANES:(r + 1) * LANES], HEAD_DIM ** -0.5)
        outs = []
        for a in range(2):
            bias = cb_ref[2 * r + a]
            valid = bias > 0.5 * NEG
            s = _dot_nt(kc, heads[a]) + bias
            m = jnp.max(s, axis=0, keepdims=True)
            pr = jnp.where(valid, jnp.exp(s - m), 0.0)
            l = jnp.sum(pr, axis=0, keepdims=True)
            pc = pr / jnp.where(l > 0.0, l, 1.0)
            pcsum[a] = pcsum[a] + pc
            outs.append(_dot(vct, pc.astype(BF16)))
        ocmp_ref[r * LANES:(r + 1) * LANES, :] = jnp.where(lo, outs[0], outs[1])

    score = _dot_01exact(ov_ref[0], pcsum[0]) + _dot_01exact(ov_ref[1], pcsum[1])
    score = score[:2 * n_blocks]
    shape = (2 * n_blocks, tile)
    row = lax.broadcasted_iota(jnp.int32, shape, 0)
    qpos = i * tile + lax.broadcasted_iota(jnp.int32, shape, 1)
    blk = row % n_blocks
    cur = qpos // SLC_BLOCK
    forced = (blk == 0) | (blk == cur) | (blk == cur - 1)
    future = blk > cur
    score = jnp.where(future, -1.0, jnp.where(forced, FORCED_SCORE, score))
    grp0 = row < n_blocks
    cnt = jnp.zeros(shape, F32)
    for mth in range(n_blocks):
        other = jnp.where(grp0, score[mth:mth + 1, :], score[n_blocks + mth:n_blocks + mth + 1, :])
        ahead = (other > score) | ((other == score) & (blk > mth))
        cnt = cnt + jnp.where(ahead, 1.0, 0.0)
    n_top = min(SLC_TOPK, n_blocks)
    sel_ref[...] = jnp.where(cnt < n_top, 0.0, NEG)


def nsa_cmp_select(main, kc, vct, cmp_bias, overlap2, batch, seq_len, q_col, tile=ATT_TILE):
    nq = seq_len // tile
    n = batch * seq_len
    n_blocks = seq_len // SLC_BLOCK
    qw = NSA_GQA * LANES
    return pl.pallas_call(
        functools.partial(_nsa_cmp_kernel, tile=tile, n_blocks=n_blocks),
        grid=(batch, nq),
        in_specs=[pl.BlockSpec((tile, qw), lambda b, i: (b * nq + i, q_col // NSA_GQA)),
                  pl.BlockSpec((1,) + kc.shape[1:], lambda b, i: (b, 0, 0)),
                  pl.BlockSpec((1,) + vct.shape[1:], lambda b, i: (b, 0, 0)),
                  pl.BlockSpec((NSA_HEADS, cmp_bias.shape[1], tile), lambda b, i: (0, 0, i)),
                  _full(overlap2)],
        out_specs=[pl.BlockSpec((qw, tile), lambda b, i: (0, b * nq + i)),
                   pl.BlockSpec((2 * n_blocks, tile), lambda b, i: (0, b * nq + i))],
        out_shape=[jax.ShapeDtypeStruct((qw, n), F32),
                   jax.ShapeDtypeStruct((2 * n_blocks, n), F32)],
        compiler_params=_params("parallel", "parallel"),
        name="nsa_cmp_select",
    )(main, kc, vct, cmp_bias, overlap2)


def _nsa_main_kernel(far_ref, q_ref, sel_ref, ocmp_ref, gate_ref, ks_ref, vst_ref, kw_ref, vwt_ref, tz_ref,
                     o_ref, stage_ref, *, tile, n_tiles, n_blocks):
    r = pl.program_id(1)
    lo = _row_lo((LANES, tile))
    blocks_per_tile = tile // SLC_BLOCK
    far = tuple(far_ref[2 * r + a] for a in range(2))

    def queries(i):
        return _split_pair(q_ref[i * tile:(i + 1) * tile, :], HEAD_DIM ** -0.5)

    def chosen(a, i, j, offset=None):
        rows = []
        for nb in range(blocks_per_tile):
            blk = a * n_blocks + j * blocks_per_tile + nb
            row = sel_ref[blk:blk + 1, i * tile:(i + 1) * tile]
            if offset is not None:
                row = row + offset
            rows.append(jnp.broadcast_to(row, (SLC_BLOCK, tile)))
        return jnp.concatenate(rows, axis=0)

    tasks = []
    for i in range(n_tiles):
        tasks += [("slc", i, j) for j in range(i + 1)]
        tasks += [("win", i, j) for j in ((i - 1, i) if i else (i,))]
    qcache = {}

    def scores(t):
        kind, i, j = tasks[t]
        if i not in qcache:
            qcache[i] = queries(i)
        k_ref_ = ks_ref if kind == "slc" else kw_ref
        kj = k_ref_[j * tile:(j + 1) * tile, :]
        for a in range(2):
            stage_ref[t % 2, a] = _dot_nt(kj, qcache[i][a])

    def fresh():
        return ([(jnp.full((1, tile), NEG, F32), jnp.zeros((1, tile), F32)) for _ in range(2)],
                [jnp.zeros((LANES, tile), F32) for _ in range(2)])

    scores(0)
    for t, (kind, i, j) in enumerate(tasks):
        if t + 1 < len(tasks):
            scores(t + 1)
        qk = [stage_ref[t % 2, a] for a in range(2)]
        first = j == 0 if kind == "slc" else j == max(i - 1, 0)
        if first:
            stats, acc = fresh()
        vt = (vst_ref if kind == "slc" else vwt_ref)[j]
        alphas, pvs = [], []
        for a in range(2):
            if j == i:
                s = qk[a] + tz_ref[a, 0]
            elif j == i - 1:
                s = qk[a] + tz_ref[a, 1 if kind == "slc" else 2]
            if kind == "slc":
                s = s + chosen(a, i, j) if j >= i - 1 else qk[a] + chosen(a, i, j, far[a])
            m_prev, l_prev = stats[a]
            m_new = jnp.maximum(m_prev, jnp.max(s, axis=0, keepdims=True))
            alpha = jnp.exp(m_prev - m_new)
            pr = jnp.exp(s - m_new)
            stats[a] = (m_new, alpha * l_prev + jnp.sum(pr, axis=0, keepdims=True))
            alphas.append(alpha)
            pvs.append(_dot(vt, pr.astype(BF16)))
        for a in range(2):
            acc[a] = alphas[a] * acc[a] + pvs[a]
        if j == i:
            branch = jnp.where(lo, acc[0] / stats[0][1], acc[1] / stats[1][1])
            if kind == "slc":
                o_slc = branch
            else:
                cols = slice(i * tile, (i + 1) * tile)
                gexp = [jnp.where(lo, gate_ref[2 * c:2 * c + 1, cols], gate_ref[2 * c + 1:2 * c + 2, cols])
                        for c in range(3)]
                out = gexp[0] * ocmp_ref[:, cols] + gexp[1] * o_slc + gexp[2] * branch
                o_ref[cols, :] = out.T.astype(o_ref.dtype)


def nsa_main(far_bias, main, sel, ocmp, gates, vt, tz, batch, seq_len, q_col, ks_col, kw_col, vs_row, vw_row,
             tile=ATT_TILE):
    nq = seq_len // tile
    n = batch * seq_len
    n_blocks = seq_len // SLC_BLOCK
    kv = lambda col: pl.BlockSpec((seq_len, LANES), lambda b, r: (b, col))
    vts = lambda rowblk: pl.BlockSpec((nq, LANES, tile), lambda b, r: (b, rowblk, 0))
    return pl.pallas_call(
        functools.partial(_nsa_main_kernel, tile=tile, n_tiles=nq, n_blocks=n_blocks),
        grid=(batch, NSA_GQA),
        in_specs=[pl.BlockSpec(memory_space=pltpu.SMEM),
                  pl.BlockSpec((seq_len, LANES), lambda b, r: (b, q_col + r)),
                  pl.BlockSpec((2 * n_blocks, seq_len), lambda b, r: (0, b)),
                  pl.BlockSpec((LANES, seq_len), lambda b, r: (r, b)),
                  pl.BlockSpec((GATE_ROWS, seq_len), lambda b, r: (r, b)),
                  kv(ks_col), vts(vs_row), kv(kw_col), vts(vw_row),
                  pl.BlockSpec((2, 3, tile, tile), lambda b, r: (r, 0, 0, 0))],
        out_specs=pl.BlockSpec((seq_len, LANES), lambda b, r: (b, r)),
        out_shape=jax.ShapeDtypeStruct((n, NSA_GQA * LANES), BF16),
        scratch_shapes=[pltpu.VMEM((2, 2, tile, tile), F32)],
        compiler_params=_params("parallel", "parallel"),
        name="nsa_main",
    )(far_bias, main, sel, ocmp, gates, main, vt, main, vt, tz)


def _layer_tail_kernel(*refs, n_in, final_norm):
    a_refs = refs[:n_in]
    w_refs = refs[n_in:2 * n_in]
    (x_ref, gx_ref, wq_ref, kv_ref, wo_ref, gm_ref, w1_ref, w2_ref, gf_ref,
     o_ref, hn_ref, acc_ref) = refs[2 * n_in:]
    f = pl.program_id(1)

    @pl.when(f == 0)
    def _():
        x1 = x_ref[...]
        for a_ref, w_ref in zip(a_refs, w_refs):
            x1 = x1 + _dot(a_ref[...], w_ref[...])
        q = _dot(_rmsnorm(x1, gx_ref[...]).astype(BF16), wq_ref[...]).astype(BF16)
        width = XA_HEADS * XA_HEAD_DIM
        outs = []
        for h in range(XA_HEADS):
            hs = slice(h * XA_HEAD_DIM, (h + 1) * XA_HEAD_DIM)
            s = _dot_nt(q[:, hs], kv_ref[:, hs]) * (XA_HEAD_DIM ** -0.5)
            m = jnp.max(s, axis=-1, keepdims=True)
            p = jnp.exp(s - m)
            l = jnp.sum(p, axis=-1, keepdims=True)
            v = kv_ref[:, width + h * XA_HEAD_DIM:width + (h + 1) * XA_HEAD_DIM]
            outs.append((_dot(p.astype(BF16), v) / l).astype(BF16))
        x2 = x1 + _dot(jnp.concatenate(outs, axis=-1), wo_ref[...])
        hn_ref[...] = _rmsnorm(x2, gm_ref[...]).astype(BF16)
        acc_ref[...] = x2

    h = jnp.maximum(_dot(hn_ref[...], w1_ref[...]), 0.0)
    acc_ref[...] += _dot((h * h).astype(BF16), w2_ref[...])

    @pl.when(f == pl.num_programs(1) - 1)
    def _():
        y = acc_ref[...]
        if final_norm:
            y = _rmsnorm(y, gf_ref[...])
        o_ref[...] = y


def layer_tail(a_list, w_list, x, gx, wq, kv, wo, gm, w1, w2, gf, final_norm, seq_len, mem_len,
               tm=MLP_ROW_TILE, tf=MLP_FF_TILE):
    n, d = x.shape
    ff = w1.shape[1]
    tm = min(tm, seq_len)
    tf = min(tf, ff)
    tiles_per_seq = seq_len // tm
    once = pl.Buffered(1)
    row = lambda a: pl.BlockSpec((tm, a.shape[1]), lambda i, f: (i, 0))
    const = lambda a: pl.BlockSpec(a.shape, lambda i, f: (0,) * a.ndim, pipeline_mode=once)
    return pl.pallas_call(
        functools.partial(_layer_tail_kernel, n_in=len(a_list), final_norm=final_norm),
        grid=(n // tm, ff // tf),
        in_specs=[row(a) for a in a_list] + [const(w) for w in w_list]
        + [row(x), const(gx), const(wq),
           pl.BlockSpec((mem_len, kv.shape[1]), lambda i, f: (i // tiles_per_seq, 0)),
           const(wo), const(gm),
           pl.BlockSpec((d, tf), lambda i, f: (0, f)),
           pl.BlockSpec((tf, d), lambda i, f: (f, 0)),
           const(gf)],
        out_specs=pl.BlockSpec((tm, d), lambda i, f: (i, 0)),
        out_shape=jax.ShapeDtypeStruct((n, d), F32),
        scratch_shapes=[pltpu.VMEM((tm, d), BF16), pltpu.VMEM((tm, d), F32)],
        compiler_params=_params("parallel", "arbitrary"),
        name="layer_tail",
    )(*a_list, *w_list, x, gx, wq, kv, wo, gm, w1, w2, gf)


def _static_tables(seq_len):
    tile = ATT_TILE
    n_chunks = seq_len // CMP_STRIDE
    n_cmp = (seq_len - CMP_BLOCK) // CMP_STRIDE + 1
    n_slc = seq_len // SLC_BLOCK
    assert n_chunks == LANES and 2 * n_slc <= LANES and seq_len % tile == 0 and WINDOW == tile
    c = np.arange(n_chunks)[:, None]
    t = np.arange(seq_len)[None, :]
    cdist = t - (c * CMP_STRIDE + CMP_BLOCK - 1)
    cmp_idx = np.where((cdist >= 0) & (c < n_cmp), _t5_bucket_np(cdist), -1).astype(np.int32)
    sl = np.arange(tile)[:, None]
    tl = np.arange(tile)[None, :]
    prev = _t5_bucket_np(tl - sl + tile)
    tz_idx = np.concatenate([np.where(sl <= tl, _t5_bucket_np(tl - sl), -1), prev,
                             np.where(sl > tl, prev, -1)], axis=0).astype(np.int32)
    far_bucket = int(_t5_bucket_np(np.array([tile + 1]))[0])
    assert far_bucket == int(_t5_bucket_np(np.array([seq_len]))[0])
    cmp_start = np.arange(n_cmp) * CMP_STRIDE
    cmp_stop = cmp_start + CMP_BLOCK - 1
    slc_start = np.arange(n_slc) * SLC_BLOCK
    slc_stop = slc_start + SLC_BLOCK - 1
    ov = ((cmp_start[:, None] <= slc_stop[None, :]) & (cmp_stop[:, None] >= slc_start[None, :]))
    overlap2 = np.zeros((2, LANES, LANES), np.float32)
    for a in range(2):
        overlap2[a, a * n_slc:(a + 1) * n_slc, :n_cmp] = ov.T
    return cmp_idx, tz_idx, far_bucket, overlap2, n_cmp


def _tri_prefix(nn):
    s = np.arange(nn)[:, None]
    j = np.arange(nn)[None, :]
    return (j <= s).astype(np.float32)


def _tri_later(nn):
    s = np.arange(nn)[:, None]
    j = np.arange(nn)[None, :]
    return (j > s).astype(np.float32)


_NSA_HEAD_PERM = [a * NSA_GQA + r for r in range(NSA_GQA) for a in range(NSA_GROUPS)]


def _perm_head_cols(w, perm):
    d = w.shape[0]
    return w.reshape(d, len(perm), HEAD_DIM)[:, perm, :].reshape(d, len(perm) * HEAD_DIM)


def _block_diag2(m):
    z = jnp.zeros_like(m)
    return jnp.concatenate([jnp.concatenate([m, z], axis=1), jnp.concatenate([z, m], axis=1)], axis=0)


def _compress_weights(pe, w1, w2):
    half = CMP_BLOCK // 2
    pe2 = jnp.concatenate([pe, pe], axis=1).reshape(2, half * LANES)
    w1e = jax.vmap(_block_diag2)(w1).reshape(2, half * LANES, LANES)
    return pe2.astype(F32), w1e.astype(BF16), _block_diag2(w2).astype(BF16)


def kernel(x, mem, rel_bias, mem_norm_g, norm_mix_g, norm_xattn_g, norm_mlp_g, final_norm_g, w_in_even, b_forget, cmp_pe_k, cmp_w1_k, cmp_w2_k, cmp_pe_v, cmp_w1_v, cmp_w2_v, w_out_even, w_in_odd, w_out_odd, xa_wq, xa_wkv, xa_wo, mlp_w1, mlp_w2):
    batch, seq_len, d = x.shape
    mem_len = mem.shape[1]
    depth = norm_mix_g.shape[0]
    n = batch * seq_len
    fox_w = FOX_HEADS * HEAD_DIM
    nsa_w = NSA_HEADS * HEAD_DIM
    kv_w = NSA_GROUPS * HEAD_DIM
    splits = np.cumsum([fox_w, fox_w, fox_w, FOX_HEADS, nsa_w, kv_w, kv_w, kv_w, kv_w, kv_w, kv_w])

    cmp_idx, tz_idx, far_bucket, overlap2, n_cmp = _static_tables(seq_len)
    tab = rel_bias.astype(F32)[:, _NSA_HEAD_PERM]
    cmp_bias = bucket_lookup(tab, jnp.asarray(cmp_idx), rows=cmp_idx.shape[0])
    tz = bucket_lookup(tab, jnp.asarray(tz_idx), rows=ATT_TILE)
    tz = tz.reshape(NSA_HEADS, 3, ATT_TILE, ATT_TILE)
    far_bias = tab[far_bucket]
    overlap2 = jnp.asarray(overlap2, BF16)
    tri_prefix = jnp.asarray(_tri_prefix(min(ROW_TILE, seq_len)), BF16)
    tri_later = jnp.asarray(_tri_later(ATT_TILE), BF16)

    xf = x.reshape(n, d)
    memf = mem.reshape(batch * mem_len, d)
    row = lambda v: v.reshape(1, -1).astype(F32)

    for layer in range(depth):
        g_mix = row(norm_mix_g[layer])
        if layer % 2 == 0:
            e = layer // 2
            (w_fq, w_fk, w_fv, w_ff, w_nq, w_kc, w_vc, w_ks, w_vs, w_kw, w_vw, w_ng) = jnp.split(
                w_in_even[e], splits, axis=1)
            w_main = jnp.concatenate([w_fq, w_fk, _perm_head_cols(w_nq, _NSA_HEAD_PERM), w_ks, w_kw],
                                     axis=1).astype(BF16)
            w_t = jnp.concatenate([w_fv, w_vs, w_vw], axis=1).T.astype(BF16)
            w_ng_t = w_ng.reshape(d, NSA_GROUPS, NSA_GQA, 3).transpose(2, 3, 1, 0).reshape(NSA_GQA, 6, d)
            w_ng_t = jnp.pad(w_ng_t, ((0, 0), (0, GATE_ROWS - 6), (0, 0))).reshape(NSA_GQA * GATE_ROWS, d)
            pad_f = LANES - F_PIECES * FOX_HEADS
            w_ff_p = jnp.pad(jnp.repeat(w_ff, F_PIECES, axis=1), ((0, 0), (0, pad_f)))
            b_f = jnp.pad(jnp.repeat(b_forget[e].astype(F32), F_PIECES), (0, pad_f)).reshape(1, LANES)
            main, vt, kc_in, vc_in, fs, gates = even_proj(
                xf, g_mix, w_main, w_t, w_kc.astype(BF16), w_vc.astype(BF16), w_ff_p.astype(BF16), b_f,
                w_ng_t.astype(BF16), tri_prefix, seq_len)
            nblk = fox_w // LANES
            fox = fox_attention(main, fs, vt, batch, seq_len, 0, nblk, 0, FOX_HEADS // 2)

            pek, w1k, w2k = _compress_weights(cmp_pe_k[e], cmp_w1_k[e], cmp_w2_k[e])
            pev, w1v, w2v = _compress_weights(cmp_pe_v[e], cmp_w1_v[e], cmp_w2_v[e])
            chunk_w = CMP_STRIDE * LANES
            kc, vct = compress(kc_in.reshape(n // CMP_STRIDE, chunk_w), vc_in.reshape(n // CMP_STRIDE, chunk_w),
                               pek, pev, w1k, w1v, w2k, w2v, batch, n_cmp)
            q_col = 2 * nblk
            ocmp, sel = nsa_cmp_select(main, kc, vct, cmp_bias, overlap2, batch, seq_len, q_col)
            nsa = nsa_main(far_bias, main, sel, ocmp, gates, vt, tz, batch, seq_len, q_col,
                           q_col + NSA_GQA, q_col + NSA_GQA + 1, nblk, nblk + 1)
            w_o = w_out_even[e]
            w_o_nsa = w_o[fox_w:].reshape(NSA_HEADS, HEAD_DIM, d)[jnp.asarray(_NSA_HEAD_PERM)].reshape(nsa_w, d)
            a_list = [fox, nsa]
            w_list = [w_o[:fox_w].astype(BF16), w_o_nsa.astype(BF16)]
        else:
            o = layer // 2
            sb_w = SB_HEADS * HEAD_DIM
            w_in = w_in_odd[o]
            main, vt = norm_proj(xf, g_mix, w_in[:, :2 * sb_w].astype(BF16), w_in[:, 2 * sb_w:].T.astype(BF16))
            nblk = sb_w // LANES
            sb = sb_attention(main, vt, batch, seq_len, 0, nblk, SB_HEADS // 2, tri_later)
            a_list = [sb]
            w_list = [w_out_odd[o].astype(BF16)]

        kv_mem = norm_proj(memf, row(mem_norm_g), xa_wkv[layer].astype(BF16))
        xf = layer_tail(a_list, w_list, xf, row(norm_xattn_g[layer]), xa_wq[layer].astype(BF16), kv_mem,
                        xa_wo[layer].astype(BF16), row(norm_mlp_g[layer]), mlp_w1[layer].astype(BF16),
                        mlp_w2[layer].astype(BF16), row(final_norm_g), layer == depth - 1, seq_len, mem_len)
    return xf.reshape(batch, seq_len, d)
```

```python
import functools
import math

import numpy as np
import jax
import jax.numpy as jnp
from jax import lax
from jax.experimental import pallas as pl
from jax.experimental.pallas import tpu as pltpu

F32 = jnp.float32
BF16 = jnp.bfloat16

LANES = 128
HEAD_DIM = 64
FOX_HEADS = 8
NSA_HEADS = 8
NSA_GROUPS = 2
NSA_GQA = NSA_HEADS // NSA_GROUPS
SB_HEADS = 16
CMP_BLOCK = 32
CMP_STRIDE = 16
SLC_BLOCK = 64
SLC_TOPK = 8
WINDOW = 256
N_BUCKETS = 32
MAX_DISTANCE = 128
XA_HEADS = 4
XA_HEAD_DIM = 128
EPS = 1e-6
NEG = -1e30
FORCED_SCORE = 1e4
EXP_UNDERFLOW = -104.0

ATT_TILE = 256
ROW_TILE = 512
MLP_ROW_TILE = 1024
MLP_FF_TILE = 1024
GATE_ROWS = 8
F_PIECES = 3


def _dot(a, b):
    return jnp.dot(a, b, preferred_element_type=F32)


def _dot_nt(a, b):
    return lax.dot_general(a, b, (((1,), (1,)), ((), ())), preferred_element_type=F32)


def _split3(x):
    hi = x.astype(BF16)
    r1 = x - hi.astype(F32)
    mid = r1.astype(BF16)
    lo = (r1 - mid.astype(F32)).astype(BF16)
    return hi, mid, lo


def _dot_exact01(x, m01):
    hi, mid, lo = _split3(x)
    return _dot(hi, m01) + _dot(mid, m01) + _dot(lo, m01)


def _dot_01exact(m01, x):
    hi, mid, lo = _split3(x)
    return _dot(m01, hi) + _dot(m01, mid) + _dot(m01, lo)


def _rmsnorm(x, g):
    ms = jnp.mean(x * x, axis=-1, keepdims=True)
    return x * lax.rsqrt(ms + EPS) * g


def _sigmoid(x):
    return 1.0 / (1.0 + jnp.exp(-x))


def _softplus(x):
    return jnp.maximum(x, 0.0) + jnp.log1p(jnp.exp(-jnp.abs(x)))


def _split_pair(q, scale):
    lo = lax.broadcasted_iota(jnp.int32, q.shape, 1) < HEAD_DIM
    qs = q * jnp.asarray(scale, q.dtype)
    zero = jnp.zeros_like(qs)
    return jnp.where(lo, qs, zero), jnp.where(lo, zero, qs)


def _row_lo(shape):
    return lax.broadcasted_iota(jnp.int32, shape, 0) < HEAD_DIM


def _t5_bucket_np(dist):
    dist = np.maximum(dist, 0)
    max_exact = N_BUCKETS // 2
    d_f = np.maximum(dist, 1).astype(np.float64)
    large = max_exact + (np.log(d_f / max_exact) / math.log(MAX_DISTANCE / max_exact)
                         * (N_BUCKETS - max_exact)).astype(np.int32)
    large = np.minimum(large, N_BUCKETS - 1)
    return np.where(dist < max_exact, dist, large).astype(np.int32)


def _params(*sem):
    return pltpu.CompilerParams(dimension_semantics=sem)


def _full(a):
    return pl.BlockSpec(a.shape, lambda *_: (0,) * a.ndim)


def _norm_proj_kernel(x_ref, g_ref, w_ref, wt_ref, o_ref, ot_ref, *, tile):
    xn = _rmsnorm(x_ref[...], g_ref[...]).astype(BF16)
    o_ref[...] = _dot(xn, w_ref[...]).astype(o_ref.dtype)
    if wt_ref is not None:
        t = _dot_nt(wt_ref[...], xn).astype(ot_ref.dtype)
        for c in range(t.shape[1] // tile):
            ot_ref[c] = t[:, c * tile:(c + 1) * tile]


def norm_proj(x, g, w, w_t=None, tile=ATT_TILE, tm=ROW_TILE):
    n, d = x.shape
    p = w.shape[1]
    tm = min(tm, n)
    in_specs = [pl.BlockSpec((tm, d), lambda i: (i, 0)), _full(g), _full(w)]
    out_specs = [pl.BlockSpec((tm, p), lambda i: (i, 0))]
    out_shape = [jax.ShapeDtypeStruct((n, p), BF16)]
    args = [x, g, w]
    if w_t is None:
        body = lambda x_ref, g_ref, w_ref, o_ref: _norm_proj_kernel(
            x_ref, g_ref, w_ref, None, o_ref, None, tile=tile)
    else:
        pt = w_t.shape[0]
        in_specs.append(_full(w_t))
        out_specs.append(pl.BlockSpec((tm // tile, pt, tile), lambda i: (i, 0, 0)))
        out_shape.append(jax.ShapeDtypeStruct((n // tile, pt, tile), BF16))
        args.append(w_t)
        body = functools.partial(_norm_proj_kernel, tile=tile)
    out = pl.pallas_call(
        body, grid=(n // tm,), in_specs=in_specs, out_specs=out_specs, out_shape=out_shape,
        compiler_params=_params("parallel"), name="norm_proj",
    )(*args)
    return out if w_t is not None else out[0]


def _even_proj_kernel(x_ref, g_ref, w_ref, wt_ref, wkc_ref, wvc_ref, wff_ref, bf_ref, wng_ref, tri_ref,
                      main_ref, vt_ref, kc_ref, vc_ref, f_ref, gate_ref, carry_ref, *, tiles_per_seq, tile):
    i = pl.program_id(0)
    xn = _rmsnorm(x_ref[...], g_ref[...]).astype(BF16)
    main_ref[...] = _dot(xn, w_ref[...]).astype(BF16)
    t = _dot_nt(wt_ref[...], xn).astype(BF16)
    for c in range(t.shape[1] // tile):
        vt_ref[c] = t[:, c * tile:(c + 1) * tile]
    kc_ref[...] = _dot(xn, wkc_ref[...]).astype(BF16)
    vc_ref[...] = _dot(xn, wvc_ref[...]).astype(BF16)
    gate_ref[...] = _sigmoid(_dot_nt(wng_ref[...], xn))
    ff = _dot(xn, wff_ref[...]) + bf_ref[...]
    logf = jnp.minimum(ff, 0.0) - jnp.log1p(jnp.exp(-jnp.abs(ff)))

    @pl.when(i % tiles_per_seq == 0)
    def _():
        carry_ref[...] = jnp.zeros_like(carry_ref)

    cs = _dot_01exact(tri_ref[...], logf) + carry_ref[0:1, :]
    carry_ref[...] = jnp.broadcast_to(cs[cs.shape[0] - 1:, :], carry_ref.shape)
    hi, mid, lo = _split3(-cs)
    piece = lax.broadcasted_iota(jnp.int32, cs.shape, 1) % F_PIECES
    f_ref[...] = jnp.where(piece == 0, hi, jnp.where(piece == 1, mid, lo))


def even_proj(x, g, w_main, w_t, w_kc, w_vc, w_ff, b_f, w_ng_t, tri_l, seq_len, tile=ATT_TILE, tm=ROW_TILE):
    n, d = x.shape
    tm = min(tm, seq_len)
    pm = w_main.shape[1]
    pt = w_t.shape[0]
    gr = w_ng_t.shape[0]
    row = lambda width: pl.BlockSpec((tm, width), lambda i: (i, 0))
    return pl.pallas_call(
        functools.partial(_even_proj_kernel, tiles_per_seq=seq_len // tm, tile=tile),
        grid=(n // tm,),
        in_specs=[row(d), _full(g), _full(w_main), _full(w_t), _full(w_kc), _full(w_vc), _full(w_ff),
                  _full(b_f), _full(w_ng_t), _full(tri_l)],
        out_specs=[row(pm),
                   pl.BlockSpec((tm // tile, pt, tile), lambda i: (i, 0, 0)),
                   row(LANES), row(LANES), row(LANES),
                   pl.BlockSpec((gr, tm), lambda i: (0, i))],
        out_shape=[jax.ShapeDtypeStruct((n, pm), BF16),
                   jax.ShapeDtypeStruct((n // tile, pt, tile), BF16),
                   jax.ShapeDtypeStruct((n, LANES), BF16),
                   jax.ShapeDtypeStruct((n, LANES), BF16),
                   jax.ShapeDtypeStruct((n, LANES), BF16),
                   jax.ShapeDtypeStruct((gr, n), F32)],
        scratch_shapes=[pltpu.VMEM((8, LANES), F32)],
        compiler_params=_params("arbitrary"),
        name="even_proj",
    )(x, g, w_main, w_t, w_kc, w_vc, w_ff, b_f, w_ng_t, tri_l)


def _bucket_lookup_kernel(tab_ref, idx_ref, o_ref):
    h = pl.program_id(0)
    idx = idx_ref[...]
    out = jnp.full(idx.shape, NEG, F32)
    for b in range(N_BUCKETS):
        out = jnp.where(idx == b, tab_ref[b, h], out)
    o_ref[0] = out


def bucket_lookup(tab, idx, rows):
    r, c = idx.shape
    h = tab.shape[1]
    return pl.pallas_call(
        _bucket_lookup_kernel,
        grid=(h, r // rows),
        in_specs=[pl.BlockSpec(memory_space=pltpu.SMEM),
                  pl.BlockSpec((rows, c), lambda hh, j: (j, 0))],
        out_specs=pl.BlockSpec((1, rows, c), lambda hh, j: (hh, j, 0)),
        out_shape=jax.ShapeDtypeStruct((h, r, c), F32),
        compiler_params=_params("parallel", "parallel"),
        name="bucket_lookup",
    )(tab, idx)


def _fox_kernel(q_ref, k_ref, fs_ref, vt_ref, o_ref, stage_ref, *, tile, n_tiles):
    p = pl.program_id(1)
    lane = lax.broadcasted_iota(jnp.int32, (tile, LANES), 1)
    causal = (lax.broadcasted_iota(jnp.int32, (tile, tile), 0)
              <= lax.broadcasted_iota(jnp.int32, (tile, tile), 1))
    lo = _row_lo((LANES, tile))

    def keys(j):
        sl = slice(j * tile, (j + 1) * tile)
        return jnp.concatenate([k_ref[sl, :], fs_ref[sl, :]], axis=1)

    def queries(i):
        heads = _split_pair(q_ref[i * tile:(i + 1) * tile, :], HEAD_DIM ** -0.5)
        return tuple(jnp.concatenate(
            [heads[a], jnp.where(lane // F_PIECES == 2 * p + a, 1.0, 0.0).astype(BF16)], axis=1)
            for a in range(2))

    tasks = [(i, j) for i in range(n_tiles) for j in range(i + 1)]
    qcache = {}

    def scores(t):
        i, j = tasks[t]
        if i not in qcache:
            qcache[i] = queries(i)
        kj = keys(j)
        for a in range(2):
            stage_ref[t % 2, a] = _dot_nt(kj, qcache[i][a])

    scores(0)
    for t, (i, j) in enumerate(tasks):
        if t + 1 < len(tasks):
            scores(t + 1)
        qk = [stage_ref[t % 2, a] for a in range(2)]
        if j == 0:
            stats = [(jnp.full((1, tile), NEG, F32), jnp.zeros((1, tile), F32)) for _ in range(2)]
            acc = [jnp.zeros((LANES, tile), F32) for _ in range(2)]
        vt = vt_ref[j]
        alphas, pvs = [], []
        for a in range(2):
            m_prev, l_prev = stats[a]
            s = jnp.where(causal, qk[a], NEG) if j == i else qk[a]
            m_new = jnp.maximum(m_prev, jnp.max(s, axis=0, keepdims=True))
            alpha = jnp.exp(m_prev - m_new)
            pr = jnp.exp(s - m_new)
            stats[a] = (m_new, alpha * l_prev + jnp.sum(pr, axis=0, keepdims=True))
            alphas.append(alpha)
            pvs.append(_dot(vt, pr.astype(BF16)))
        for a in range(2):
            acc[a] = alphas[a] * acc[a] + pvs[a]
        if j == i:
            o = jnp.where(lo, acc[0] / stats[0][1], acc[1] / stats[1][1])
            o_ref[i * tile:(i + 1) * tile, :] = o.T.astype(o_ref.dtype)


def fox_attention(main, fs, vt, batch, seq_len, q_col, k_col, v_row, n_pairs, tile=ATT_TILE):
    nq = seq_len // tile
    n = batch * seq_len
    return pl.pallas_call(
        functools.partial(_fox_kernel, tile=tile, n_tiles=nq),
        grid=(batch, n_pairs),
        in_specs=[pl.BlockSpec((seq_len, LANES), lambda b, p: (b, q_col + p)),
                  pl.BlockSpec((seq_len, LANES), lambda b, p: (b, k_col + p)),
                  pl.BlockSpec((seq_len, LANES), lambda b, p: (b, 0)),
                  pl.BlockSpec((nq, LANES, tile), lambda b, p: (b, v_row + p, 0))],
        out_specs=pl.BlockSpec((seq_len, LANES), lambda b, p: (b, p)),
        out_shape=jax.ShapeDtypeStruct((n, n_pairs * LANES), BF16),
        scratch_shapes=[pltpu.VMEM((2, 2, tile, tile), F32)],
        compiler_params=_params("parallel", "parallel"),
        name="fox_attention",
    )(main, main, fs, vt)


def _sb_kernel(q_ref, k_ref, vt_ref, tri_ref, o_ref, acc_ref, stage_ref, *, tile, n_tiles):
    tri = tri_ref[...]
    strict = (lax.broadcasted_iota(jnp.int32, (tile, tile), 0)
              < lax.broadcasted_iota(jnp.int32, (tile, tile), 1))
    lo = _row_lo((LANES, tile))

    def queries(i):
        return _split_pair(q_ref[i * tile:(i + 1) * tile, :], HEAD_DIM ** -0.5)

    def scores(ks, heads):
        return tuple(_dot_nt(ks, heads[a]) for a in range(2))

    def consume(zz, rs, vt, masked):
        sps, base, later, pvs = [], [], [], []
        for a in range(2):
            z = zz[a]
            sp = jnp.maximum(z, 0.0) + jnp.log(1.0 + jnp.exp(-jnp.abs(z)))
            base.append(z - sp)
            if masked:
                sp = jnp.where(strict, sp, 0.0)
            sps.append(sp)
        for a in range(2):
            later.append(_dot(tri, sps[a].astype(BF16)))
        for a in range(2):
            wgt = jnp.exp(base[a] - later[a] + rs[a])
            if masked:
                wgt = jnp.where(strict, wgt, 0.0)
            pvs.append(_dot(vt, wgt.astype(BF16)))
        return tuple(rs[a] - jnp.sum(sps[a], axis=0, keepdims=True) for a in range(2)), pvs

    def live(rs):
        return (jnp.max(jnp.maximum(rs[0], rs[1])) > EXP_UNDERFLOW).astype(jnp.int32)

    tasks = [(i, j) for i in range(n_tiles) for j in ((i, i - 1) if i else (i,))]
    qcache = {}

    def task_scores(t):
        i, j = tasks[t]
        if i not in qcache:
            qcache[i] = queries(i)
        zz = scores(k_ref[j * tile:(j + 1) * tile, :], qcache[i])
        for a in range(2):
            stage_ref[t % 2, a] = zz[a]

    zero = jnp.zeros((1, tile), F32)
    survival = {}
    task_scores(0)
    for t, (i, j) in enumerate(tasks):
        if t + 1 < len(tasks):
            task_scores(t + 1)
        zz = [stage_ref[t % 2, a] for a in range(2)]
        if j == i:
            rs, acc = (zero, zero), None
        rs, pvs = consume(zz, rs, vt_ref[j], j == i)
        acc = pvs if acc is None else [acc[a] + pvs[a] for a in range(2)]
        if j == max(i - 1, 0):
            for a in range(2):
                acc_ref[i, a] = acc[a]
            survival[i] = rs

    for i in range(2, n_tiles):
        heads = qcache[i]

        def cond(state):
            j, alive, _ = state
            return (j >= 0) & (alive > 0)

        def body(state, i=i, heads=heads):
            j, _, rs = state
            rs, pvs = consume(scores(k_ref[pl.ds(j * tile, tile), :], heads), rs, vt_ref[j], False)
            for a in range(2):
                acc_ref[i, a] = acc_ref[i, a] + pvs[a]
            return j - 1, live(rs), rs

        lax.while_loop(cond, body, (i - 2, live(survival[i]), survival[i]))

    for i in range(n_tiles):
        o = jnp.where(lo, acc_ref[i, 0], acc_ref[i, 1])
        o_ref[i * tile:(i + 1) * tile, :] = o.T.astype(o_ref.dtype)


def sb_attention(main, vt, batch, seq_len, q_col, k_col, n_pairs, tri, tile=ATT_TILE):
    nq = seq_len // tile
    n = batch * seq_len
    return pl.pallas_call(
        functools.partial(_sb_kernel, tile=tile, n_tiles=nq),
        grid=(batch, n_pairs),
        in_specs=[pl.BlockSpec((seq_len, LANES), lambda b, p: (b, q_col + p)),
                  pl.BlockSpec((seq_len, LANES), lambda b, p: (b, k_col + p)),
                  pl.BlockSpec((nq, LANES, tile), lambda b, p: (b, p, 0)),
                  _full(tri)],
        out_specs=pl.BlockSpec((seq_len, LANES), lambda b, p: (b, p)),
        out_shape=jax.ShapeDtypeStruct((n, n_pairs * LANES), BF16),
        scratch_shapes=[pltpu.VMEM((nq, 2, LANES, tile), F32), pltpu.VMEM((2, 2, tile, tile), F32)],
        compiler_params=_params("parallel", "parallel"),
        name="sb_attention",
    )(main, main, vt, tri)


def _gelu_tanh(x):
    return 0.5 * x * (1.0 + jnp.tanh(math.sqrt(2.0 / math.pi) * (x + 0.044715 * (x * x * x))))


def _compress_kernel(xk_ref, xv_ref, pek_ref, pev_ref, w1k_ref, w1v_ref, w2k_ref, w2v_ref,
                     ok_ref, ov_ref, *, n_cmp):
    def one(x_ref, pe_ref, w1_ref, w2_ref):
        x = x_ref[...].astype(F32)
        xa = (x + pe_ref[0:1, :]).astype(BF16)
        xb = (x + pe_ref[1:2, :]).astype(BF16)
        ha = _dot(xa, w1_ref[0])
        hb = _dot(xb, w1_ref[1])
        rows = ha.shape[0]
        hb = pltpu.roll(hb, rows - 1, 0)
        h = _gelu_tanh(ha + hb)
        out = _dot(h.astype(BF16), w2_ref[...])
        ridx = lax.broadcasted_iota(jnp.int32, out.shape, 0)
        return jnp.where(ridx < n_cmp, out, 0.0)

    ok_ref[0] = one(xk_ref, pek_ref, w1k_ref, w2k_ref).astype(ok_ref.dtype)
    ov_ref[0] = one(xv_ref, pev_ref, w1v_ref, w2v_ref).T.astype(ov_ref.dtype)


def compress(xk, xv, pek, pev, w1k, w1v, w2k, w2v, batch, n_cmp):
    chunks = xk.shape[0] // batch
    row = pl.BlockSpec((chunks, xk.shape[1]), lambda b: (b, 0))
    return pl.pallas_call(
        functools.partial(_compress_kernel, n_cmp=n_cmp),
        grid=(batch,),
        in_specs=[row, row, _full(pek), _full(pev), _full(w1k), _full(w1v), _full(w2k), _full(w2v)],
        out_specs=[pl.BlockSpec((1, chunks, LANES), lambda b: (b, 0, 0)),
                   pl.BlockSpec((1, LANES, chunks), lambda b: (b, 0, 0))],
        out_shape=[jax.ShapeDtypeStruct((batch, chunks, LANES), BF16),
                   jax.ShapeDtypeStruct((batch, LANES, chunks), BF16)],
        compiler_params=_params("parallel"),
        name="nsa_compress",
    )(xk, xv, pek, pev, w1k, w1v, w2k, w2v)


def _nsa_cmp_kernel(q_ref, kc_ref, vct_ref, cb_ref, ov_ref, ocmp_ref, sel_ref, *, tile, n_blocks):
    i = pl.program_id(1)
    kc = kc_ref[0]
    vct = vct_ref[0]
    pcsum = [jnp.zeros((LANES, tile), F32), jnp.zeros((LANES, tile), F32)]
    lo = _row_lo((LANES, tile))
    for r in range(NSA_GQA):
        heads = _split_pair(q_ref[:, r * LANES:(r + 1) * LANES], HEAD_DIM ** -0.5)
        outs = []
        for a in range(2):
            bias = cb_ref[2 * r + a]
            valid = bias > 0.5 * NEG
            s = _dot_nt(kc, heads[a]) + bias
            m = jnp.max(s, axis=0, keepdims=True)
            pr = jnp.where(valid, jnp.exp(s - m), 0.0)
            l = jnp.sum(pr, axis=0, keepdims=True)
            pc = pr / jnp.where(l > 0.0, l, 1.0)
            pcsum[a] = pcsum[a] + pc
            outs.append(_dot(vct, pc.astype(BF16)))
        ocmp_ref[r * LANES:(r + 1) * LANES, :] = jnp.where(lo, outs[0], outs[1])

    score = _dot_01exact(ov_ref[0], pcsum[0]) + _dot_01exact(ov_ref[1], pcsum[1])
    score = score[:2 * n_blocks]
    shape = (2 * n_blocks, tile)
    row = lax.broadcasted_iota(jnp.int32, shape, 0)
    qpos = i * tile + lax.broadcasted_iota(jnp.int32, shape, 1)
    blk = row % n_blocks
    cur = qpos // SLC_BLOCK
    forced = (blk == 0) | (blk == cur) | (blk == cur - 1)
    future = blk > cur
    score = jnp.where(future, -1.0, jnp.where(forced, FORCED_SCORE, score))
    grp0 = row < n_blocks
    cnt = jnp.zeros(shape, F32)
    for mth in range(n_blocks):
        other = jnp.where(grp0, score[mth:mth + 1, :], score[n_blocks + mth:n_blocks + mth + 1, :])
        ahead = (other > score) | ((other == score) & (blk > mth))
        cnt = cnt + jnp.where(ahead, 1.0, 0.0)
    n_top = min(SLC_TOPK, n_blocks)
    sel_ref[...] = jnp.where(cnt < n_top, 0.0, NEG)


def nsa_cmp_select(main, kc, vct, cmp_bias, overlap2, batch, seq_len, q_col, tile=ATT_TILE):
    nq = seq_len // tile
    n = batch * seq_len
    n_blocks = seq_len // SLC_BLOCK
    qw = NSA_GQA * LANES
    return pl.pallas_call(
        functools.partial(_nsa_cmp_kernel, tile=tile, n_blocks=n_blocks),
        grid=(batch, nq),
        in_specs=[pl.BlockSpec((tile, qw), lambda b, i: (b * nq + i, q_col // NSA_GQA)),
                  pl.BlockSpec((1,) + kc.shape[1:], lambda b, i: (b, 0, 0)),
                  pl.BlockSpec((1,) + vct.shape[1:], lambda b, i: (b, 0, 0)),
                  pl.BlockSpec((NSA_HEADS, cmp_bias.shape[1], tile), lambda b, i: (0, 0, i)),
                  _full(overlap2)],
        out_specs=[pl.BlockSpec((qw, tile), lambda b, i: (0, b * nq + i)),
                   pl.BlockSpec((2 * n_blocks, tile), lambda b, i: (0, b * nq + i))],
        out_shape=[jax.ShapeDtypeStruct((qw, n), F32),
                   jax.ShapeDtypeStruct((2 * n_blocks, n), F32)],
        compiler_params=_params("parallel", "parallel"),
        name="nsa_cmp_select",
    )(main, kc, vct, cmp_bias, overlap2)


def _nsa_main_kernel(far_ref, q_ref, sel_ref, ocmp_ref, gate_ref, ks_ref, vst_ref, kw_ref, vwt_ref, tz_ref,
                     o_ref, stage_ref, *, tile, n_tiles, n_blocks):
    r = pl.program_id(1)
    lo = _row_lo((LANES, tile))
    blocks_per_tile = tile // SLC_BLOCK
    far = tuple(far_ref[2 * r + a] for a in range(2))

    def queries(i):
        return _split_pair(q_ref[i * tile:(i + 1) * tile, :], HEAD_DIM ** -0.5)

    def chosen(a, i, j, offset=None):
        rows = []
        for nb in range(blocks_per_tile):
            blk = a * n_blocks + j * blocks_per_tile + nb
            row = sel_ref[blk:blk + 1, i * tile:(i + 1) * tile]
            if offset is not None:
                row = row + offset
            rows.append(jnp.broadcast_to(row, (SLC_BLOCK, tile)))
        return jnp.concatenate(rows, axis=0)

    tasks = []
    for i in range(n_tiles):
        tasks += [("slc", i, j) for j in range(i + 1)]
        tasks += [("win", i, j) for j in ((i - 1, i) if i else (i,))]
    qcache = {}

    def scores(t):
        kind, i, j = tasks[t]
        if i not in qcache:
            qcache[i] = queries(i)
        k_ref_ = ks_ref if kind == "slc" else kw_ref
        kj = k_ref_[j * tile:(j + 1) * tile, :]
        for a in range(2):
            stage_ref[t % 2, a] = _dot_nt(kj, qcache[i][a])

    def fresh():
        return ([(jnp.full((1, tile), NEG, F32), jnp.zeros((1, tile), F32)) for _ in range(2)],
                [jnp.zeros((LANES, tile), F32) for _ in range(2)])

    scores(0)
    for t, (kind, i, j) in enumerate(tasks):
        if t + 1 < len(tasks):
            scores(t + 1)
        qk = [stage_ref[t % 2, a] for a in range(2)]
        first = j == 0 if kind == "slc" else j == max(i - 1, 0)
        if first:
            stats, acc = fresh()
        vt = (vst_ref if kind == "slc" else vwt_ref)[j]
        alphas, pvs = [], []
        for a in range(2):
            if j == i:
                s = qk[a] + tz_ref[a, 0]
            elif j == i - 1:
                s = qk[a] + tz_ref[a, 1 if kind == "slc" else 2]
            if kind == "slc":
                s = s + chosen(a, i, j) if j >= i - 1 else qk[a] + chosen(a, i, j, far[a])
            m_prev, l_prev = stats[a]
            m_new = jnp.maximum(m_prev, jnp.max(s, axis=0, keepdims=True))
            alpha = jnp.exp(m_prev - m_new)
            pr = jnp.exp(s - m_new)
            stats[a] = (m_new, alpha * l_prev + jnp.sum(pr, axis=0, keepdims=True))
            alphas.append(alpha)
            pvs.append(_dot(vt, pr.astype(BF16)))
        for a in range(2):
            acc[a] = alphas[a] * acc[a] + pvs[a]
        if j == i:
            branch = jnp.where(lo, acc[0] / stats[0][1], acc[1] / stats[1][1])
            if kind == "slc":
                o_slc = branch
            else:
                cols = slice(i * tile, (i + 1) * tile)
                gexp = [jnp.where(lo, gate_ref[2 * c:2 * c + 1, cols], gate_ref[2 * c + 1:2 * c + 2, cols])
                        for c in range(3)]
                out = gexp[0] * ocmp_ref[:, cols] + gexp[1] * o_slc + gexp[2] * branch
                o_ref[cols, :] = out.T.astype(o_ref.dtype)


def nsa_main(far_bias, main, sel, ocmp, gates, vt, tz, batch, seq_len, q_col, ks_col, kw_col, vs_row, vw_row,
             tile=ATT_TILE):
    nq = seq_len // tile
    n = batch * seq_len
    n_blocks = seq_len // SLC_BLOCK
    kv = lambda col: pl.BlockSpec((seq_len, LANES), lambda b, r: (b, col))
    vts = lambda rowblk: pl.BlockSpec((nq, LANES, tile), lambda b, r: (b, rowblk, 0))
    return pl.pallas_call(
        functools.partial(_nsa_main_kernel, tile=tile, n_tiles=nq, n_blocks=n_blocks),
        grid=(batch, NSA_GQA),
        in_specs=[pl.BlockSpec(memory_space=pltpu.SMEM),
                  pl.BlockSpec((seq_len, LANES), lambda b, r: (b, q_col + r)),
                  pl.BlockSpec((2 * n_blocks, seq_len), lambda b, r: (0, b)),
                  pl.BlockSpec((LANES, seq_len), lambda b, r: (r, b)),
                  pl.BlockSpec((GATE_ROWS, seq_len), lambda b, r: (r, b)),
                  kv(ks_col), vts(vs_row), kv(kw_col), vts(vw_row),
                  pl.BlockSpec((2, 3, tile, tile), lambda b, r: (r, 0, 0, 0))],
        out_specs=pl.BlockSpec((seq_len, LANES), lambda b, r: (b, r)),
        out_shape=jax.ShapeDtypeStruct((n, NSA_GQA * LANES), BF16),
        scratch_shapes=[pltpu.VMEM((2, 2, tile, tile), F32)],
        compiler_params=_params("parallel", "parallel"),
        name="nsa_main",
    )(far_bias, main, sel, ocmp, gates, main, vt, main, vt, tz)


def _layer_tail_kernel(*refs, n_in, final_norm):
    a_refs = refs[:n_in]
    w_refs = refs[n_in:2 * n_in]
    (x_ref, gx_ref, wq_ref, kv_ref, wo_ref, gm_ref, w1_ref, w2_ref, gf_ref,
     o_ref, hn_ref, acc_ref) = refs[2 * n_in:]
    f = pl.program_id(1)

    @pl.when(f == 0)
    def _():
        x1 = x_ref[...]
        for a_ref, w_ref in zip(a_refs, w_refs):
            x1 = x1 + _dot(a_ref[...], w_ref[...])
        q = _dot(_rmsnorm(x1, gx_ref[...]).astype(BF16), wq_ref[...]).astype(BF16)
        width = XA_HEADS * XA_HEAD_DIM
        outs = []
        for h in range(XA_HEADS):
            hs = slice(h * XA_HEAD_DIM, (h + 1) * XA_HEAD_DIM)
            s = _dot_nt(q[:, hs], kv_ref[:, hs]) * (XA_HEAD_DIM ** -0.5)
            m = jnp.max(s, axis=-1, keepdims=True)
            p = jnp.exp(s - m)
            l = jnp.sum(p, axis=-1, keepdims=True)
            v = kv_ref[:, width + h * XA_HEAD_DIM:width + (h + 1) * XA_HEAD_DIM]
            outs.append((_dot(p.astype(BF16), v) / l).astype(BF16))
        x2 = x1 + _dot(jnp.concatenate(outs, axis=-1), wo_ref[...])
        hn_ref[...] = _rmsnorm(x2, gm_ref[...]).astype(BF16)
        acc_ref[...] = x2

    h = jnp.maximum(_dot(hn_ref[...], w1_ref[...]), 0.0)
    acc_ref[...] += _dot((h * h).astype(BF16), w2_ref[...])

    @pl.when(f == pl.num_programs(1) - 1)
    def _():
        y = acc_ref[...]
        if final_norm:
            y = _rmsnorm(y, gf_ref[...])
        o_ref[...] = y


def layer_tail(a_list, w_list, x, gx, wq, kv, kv_col, wo, gm, w1, w2, gf, final_norm, seq_len, mem_len,
               tm=MLP_ROW_TILE, tf=MLP_FF_TILE):
    n, d = x.shape
    ff = w1.shape[1]
    tm = min(tm, seq_len)
    tf = min(tf, ff)
    tiles_per_seq = seq_len // tm
    once = pl.Buffered(1)
    row = lambda a: pl.BlockSpec((tm, a.shape[1]), lambda i, f: (i, 0))
    const = lambda a: pl.BlockSpec(a.shape, lambda i, f: (0,) * a.ndim, pipeline_mode=once)
    return pl.pallas_call(
        functools.partial(_layer_tail_kernel, n_in=len(a_list), final_norm=final_norm),
        grid=(n // tm, ff // tf),
        in_specs=[row(a) for a in a_list] + [const(w) for w in w_list]
        + [row(x), const(gx), const(wq),
           pl.BlockSpec((mem_len, 2 * XA_HEADS * XA_HEAD_DIM), lambda i, f: (i // tiles_per_seq, kv_col)),
           const(wo), const(gm),
           pl.BlockSpec((d, tf), lambda i, f: (0, f)),
           pl.BlockSpec((tf, d), lambda i, f: (f, 0)),
           const(gf)],
        out_specs=pl.BlockSpec((tm, d), lambda i, f: (i, 0)),
        out_shape=jax.ShapeDtypeStruct((n, d), F32),
        scratch_shapes=[pltpu.VMEM((tm, d), BF16), pltpu.VMEM((tm, d), F32)],
        compiler_params=_params("parallel", "arbitrary"),
        name="layer_tail",
    )(*a_list, *w_list, x, gx, wq, kv, wo, gm, w1, w2, gf)


def _static_tables(seq_len):
    tile = ATT_TILE
    n_chunks = seq_len // CMP_STRIDE
    n_cmp = (seq_len - CMP_BLOCK) // CMP_STRIDE + 1
    n_slc = seq_len // SLC_BLOCK
    assert n_chunks == LANES and 2 * n_slc <= LANES and seq_len % tile == 0 and WINDOW == tile
    c = np.arange(n_chunks)[:, None]
    t = np.arange(seq_len)[None, :]
    cdist = t - (c * CMP_STRIDE + CMP_BLOCK - 1)
    cmp_idx = np.where((cdist >= 0) & (c < n_cmp), _t5_bucket_np(cdist), -1).astype(np.int32)
    sl = np.arange(tile)[:, None]
    tl = np.arange(tile)[None, :]
    prev = _t5_bucket_np(tl - sl + tile)
    tz_idx = np.concatenate([np.where(sl <= tl, _t5_bucket_np(tl - sl), -1), prev,
                             np.where(sl > tl, prev, -1)], axis=0).astype(np.int32)
    far_bucket = int(_t5_bucket_np(np.array([tile + 1]))[0])
    assert far_bucket == int(_t5_bucket_np(np.array([seq_len]))[0])
    cmp_start = np.arange(n_cmp) * CMP_STRIDE
    cmp_stop = cmp_start + CMP_BLOCK - 1
    slc_start = np.arange(n_slc) * SLC_BLOCK
    slc_stop = slc_start + SLC_BLOCK - 1
    ov = ((cmp_start[:, None] <= slc_stop[None, :]) & (cmp_stop[:, None] >= slc_start[None, :]))
    overlap2 = np.zeros((2, LANES, LANES), np.float32)
    for a in range(2):
        overlap2[a, a * n_slc:(a + 1) * n_slc, :n_cmp] = ov.T
    return cmp_idx, tz_idx, far_bucket, overlap2, n_cmp


def _tri_prefix(nn):
    s = np.arange(nn)[:, None]
    j = np.arange(nn)[None, :]
    return (j <= s).astype(np.float32)


def _tri_later(nn):
    s = np.arange(nn)[:, None]
    j = np.arange(nn)[None, :]
    return (j > s).astype(np.float32)


_NSA_HEAD_PERM = [a * NSA_GQA + r for r in range(NSA_GQA) for a in range(NSA_GROUPS)]


def _block_diag2(m):
    z = jnp.zeros_like(m)
    return jnp.concatenate([jnp.concatenate([m, z], axis=1), jnp.concatenate([z, m], axis=1)], axis=0)


def _compress_weights(pe, w1, w2):
    half = CMP_BLOCK // 2
    pe2 = jnp.concatenate([pe, pe], axis=1).reshape(2, half * LANES)
    w1e = jax.vmap(_block_diag2)(w1).reshape(2, half * LANES, LANES)
    return pe2.astype(F32), w1e.astype(BF16), _block_diag2(w2).astype(BF16)


def kernel(x, mem, rel_bias, mem_norm_g, norm_mix_g, norm_xattn_g, norm_mlp_g, final_norm_g, w_in_even, b_forget, cmp_pe_k, cmp_w1_k, cmp_w2_k, cmp_pe_v, cmp_w1_v, cmp_w2_v, w_out_even, w_in_odd, w_out_odd, xa_wq, xa_wkv, xa_wo, mlp_w1, mlp_w2):
    batch, seq_len, d = x.shape
    mem_len = mem.shape[1]
    depth = norm_mix_g.shape[0]
    n = batch * seq_len
    fox_w = FOX_HEADS * HEAD_DIM
    nsa_w = NSA_HEADS * HEAD_DIM
    kv_w = NSA_GROUPS * HEAD_DIM
    splits = np.cumsum([fox_w, fox_w, fox_w, FOX_HEADS, nsa_w, kv_w, kv_w, kv_w, kv_w, kv_w, kv_w])

    cmp_idx, tz_idx, far_bucket, overlap2, n_cmp = _static_tables(seq_len)
    tab = rel_bias.astype(F32)[:, _NSA_HEAD_PERM]
    cmp_bias = bucket_lookup(tab, jnp.asarray(cmp_idx), rows=cmp_idx.shape[0])
    tz = bucket_lookup(tab, jnp.asarray(tz_idx), rows=ATT_TILE)
    tz = tz.reshape(NSA_HEADS, 3, ATT_TILE, ATT_TILE)
    far_bias = tab[far_bucket]
    overlap2 = jnp.asarray(overlap2, BF16)
    tri_prefix = jnp.asarray(_tri_prefix(min(ROW_TILE, seq_len)), BF16)
    tri_later = jnp.asarray(_tri_later(ATT_TILE), BF16)

    xf = x.reshape(n, d)
    memf = mem.reshape(batch * mem_len, d)
    row = lambda v: v.reshape(1, -1).astype(F32)
    rows = lambda v: v.reshape(v.shape[0], 1, -1).astype(F32)
    g_mix_all, g_xa_all, g_mlp_all = rows(norm_mix_g), rows(norm_xattn_g), rows(norm_mlp_g)
    g_mem, g_final = row(mem_norm_g), row(final_norm_g)

    ne = w_in_even.shape[0]
    (w_fq, w_fk, w_fv, w_ff, w_nq, w_kc, w_vc, w_ks, w_vs, w_kw, w_vw, w_ng) = jnp.split(w_in_even, splits, axis=2)
    w_nq = w_nq.reshape(ne, d, NSA_HEADS, HEAD_DIM)[:, :, _NSA_HEAD_PERM, :].reshape(ne, d, nsa_w)
    w_main_all = jnp.concatenate([w_fq, w_fk, w_nq, w_ks, w_kw], axis=2).astype(BF16)
    w_t_all = jnp.concatenate([w_fv, w_vs, w_vw], axis=2).transpose(0, 2, 1).astype(BF16)
    w_kc_all, w_vc_all = w_kc.astype(BF16), w_vc.astype(BF16)
    w_ng_t = w_ng.reshape(ne, d, NSA_GROUPS, NSA_GQA, 3).transpose(0, 3, 4, 2, 1).reshape(ne, NSA_GQA, 6, d)
    w_ng_all = jnp.pad(w_ng_t, ((0, 0), (0, 0), (0, GATE_ROWS - 6), (0, 0))).reshape(
        ne, NSA_GQA * GATE_ROWS, d).astype(BF16)
    pad_f = LANES - F_PIECES * FOX_HEADS
    w_ff_all = jnp.pad(jnp.repeat(w_ff, F_PIECES, axis=2), ((0, 0), (0, 0), (0, pad_f))).astype(BF16)
    b_f_all = jnp.pad(jnp.repeat(b_forget.astype(F32), F_PIECES, axis=1), ((0, 0), (0, pad_f))).reshape(ne, 1, LANES)
    cmp_k = jax.vmap(_compress_weights)(cmp_pe_k, cmp_w1_k, cmp_w2_k)
    cmp_v = jax.vmap(_compress_weights)(cmp_pe_v, cmp_w1_v, cmp_w2_v)
    w_o_fox_all = w_out_even[:, :fox_w].astype(BF16)
    w_o_nsa_all = w_out_even[:, fox_w:].reshape(ne, NSA_HEADS, HEAD_DIM, d)[:, _NSA_HEAD_PERM].reshape(
        ne, nsa_w, d).astype(BF16)
    sb_w = SB_HEADS * HEAD_DIM
    w_odd_qk_all = w_in_odd[:, :, :2 * sb_w].astype(BF16)
    w_odd_vt_all = w_in_odd[:, :, 2 * sb_w:].transpose(0, 2, 1).astype(BF16)
    w_o_odd_all = w_out_odd.astype(BF16)
    xa_wq_all, xa_wo_all = xa_wq.astype(BF16), xa_wo.astype(BF16)
    mlp_w1_all, mlp_w2_all = mlp_w1.astype(BF16), mlp_w2.astype(BF16)
    xa_wkv_cat = xa_wkv.transpose(1, 0, 2).reshape(d, depth * xa_wkv.shape[2]).astype(BF16)
    kv_mem = norm_proj(memf, g_mem, xa_wkv_cat)

    for layer in range(depth):
        g_mix = g_mix_all[layer]
        if layer % 2 == 0:
            e = layer // 2
            main, vt, kc_in, vc_in, fs, gates = even_proj(
                xf, g_mix, w_main_all[e], w_t_all[e], w_kc_all[e], w_vc_all[e], w_ff_all[e], b_f_all[e],
                w_ng_all[e], tri_prefix, seq_len)
            nblk = fox_w // LANES
            fox = fox_attention(main, fs, vt, batch, seq_len, 0, nblk, 0, FOX_HEADS // 2)

            chunk_w = CMP_STRIDE * LANES
            kc, vct = compress(kc_in.reshape(n // CMP_STRIDE, chunk_w), vc_in.reshape(n // CMP_STRIDE, chunk_w),
                               cmp_k[0][e], cmp_v[0][e], cmp_k[1][e], cmp_v[1][e], cmp_k[2][e], cmp_v[2][e],
                               batch, n_cmp)
            q_col = 2 * nblk
            ocmp, sel = nsa_cmp_select(main, kc, vct, cmp_bias, overlap2, batch, seq_len, q_col)
            nsa = nsa_main(far_bias, main, sel, ocmp, gates, vt, tz, batch, seq_len, q_col,
                           q_col + NSA_GQA, q_col + NSA_GQA + 1, nblk, nblk + 1)
            a_list = [fox, nsa]
            w_list = [w_o_fox_all[e], w_o_nsa_all[e]]
        else:
            o = layer // 2
            main, vt = norm_proj(xf, g_mix, w_odd_qk_all[o], w_odd_vt_all[o])
            nblk = sb_w // LANES
            sb = sb_attention(main, vt, batch, seq_len, 0, nblk, SB_HEADS // 2, tri_later)
            a_list = [sb]
            w_list = [w_o_odd_all[o]]

        xf = layer_tail(a_list, w_list, xf, g_xa_all[layer], xa_wq_all[layer], kv_mem, layer, xa_wo_all[layer],
                        g_mlp_all[layer], mlp_w1_all[layer], mlp_w2_all[layer], g_final, layer == depth - 1,
                        seq_len, mem_len)
    return xf.reshape(batch, seq_len, d)
```

```python
import functools
import math

import numpy as np
import jax
import jax.numpy as jnp
from jax import lax
from jax.experimental import pallas as pl
from jax.experimental.pallas import tpu as pltpu

F32 = jnp.float32
BF16 = jnp.bfloat16

LANES = 128
HEAD_DIM = 64
FOX_HEADS = 8
NSA_HEADS = 8
NSA_GROUPS = 2
NSA_GQA = NSA_HEADS // NSA_GROUPS
SB_HEADS = 16
CMP_BLOCK = 32
CMP_STRIDE = 16
SLC_BLOCK = 64
SLC_TOPK = 8
WINDOW = 256
N_BUCKETS = 32
MAX_DISTANCE = 128
XA_HEADS = 4
XA_HEAD_DIM = 128
EPS = 1e-6
NEG = -1e30
FORCED_SCORE = 1e4
EXP_UNDERFLOW = -104.0

ATT_TILE = 256
ROW_TILE = 512
MLP_ROW_TILE = 1024
MLP_FF_TILE = 1024
GATE_ROWS = 8
F_PIECES = 3


def _dot(a, b):
    return jnp.dot(a, b, preferred_element_type=F32)


def _dot_nt(a, b):
    return lax.dot_general(a, b, (((1,), (1,)), ((), ())), preferred_element_type=F32)


def _split3(x):
    hi = x.astype(BF16)
    r1 = x - hi.astype(F32)
    mid = r1.astype(BF16)
    lo = (r1 - mid.astype(F32)).astype(BF16)
    return hi, mid, lo


def _dot_exact01(x, m01):
    hi, mid, lo = _split3(x)
    return _dot(hi, m01) + _dot(mid, m01) + _dot(lo, m01)


def _dot_01exact(m01, x):
    hi, mid, lo = _split3(x)
    return _dot(m01, hi) + _dot(m01, mid) + _dot(m01, lo)


def _rmsnorm(x, g):
    ms = jnp.mean(x * x, axis=-1, keepdims=True)
    return x * lax.rsqrt(ms + EPS) * g


def _sigmoid(x):
    return 1.0 / (1.0 + jnp.exp(-x))


def _softplus(x):
    return jnp.maximum(x, 0.0) + jnp.log1p(jnp.exp(-jnp.abs(x)))


def _split_pair(q, scale):
    lo = lax.broadcasted_iota(jnp.int32, q.shape, 1) < HEAD_DIM
    qs = q * jnp.asarray(scale, q.dtype)
    zero = jnp.zeros_like(qs)
    return jnp.where(lo, qs, zero), jnp.where(lo, zero, qs)


def _row_lo(shape):
    return lax.broadcasted_iota(jnp.int32, shape, 0) < HEAD_DIM


def _t5_bucket_np(dist):
    dist = np.maximum(dist, 0)
    max_exact = N_BUCKETS // 2
    d_f = np.maximum(dist, 1).astype(np.float64)
    large = max_exact + (np.log(d_f / max_exact) / math.log(MAX_DISTANCE / max_exact)
                         * (N_BUCKETS - max_exact)).astype(np.int32)
    large = np.minimum(large, N_BUCKETS - 1)
    return np.where(dist < max_exact, dist, large).astype(np.int32)


def _params(*sem):
    return pltpu.CompilerParams(dimension_semantics=sem)


def _full(a):
    return pl.BlockSpec(a.shape, lambda *_: (0,) * a.ndim)


def _pick(a, k):
    return pl.BlockSpec((None,) + a.shape[1:], lambda *_: (k,) + (0,) * (a.ndim - 1))


def _norm_proj_kernel(x_ref, g_ref, w_ref, wt_ref, o_ref, ot_ref, *, tile):
    xn = _rmsnorm(x_ref[...], g_ref[...]).astype(BF16)
    o_ref[...] = _dot(xn, w_ref[...]).astype(o_ref.dtype)
    if wt_ref is not None:
        t = _dot_nt(wt_ref[...], xn).astype(ot_ref.dtype)
        for c in range(t.shape[1] // tile):
            ot_ref[c] = t[:, c * tile:(c + 1) * tile]


def norm_proj(x, g, g_layer, w, w_t=None, layer=0, tile=ATT_TILE, tm=ROW_TILE):
    n, d = x.shape
    p = w.shape[2]
    tm = min(tm, n)
    in_specs = [pl.BlockSpec((tm, d), lambda i: (i, 0)), _pick(g, g_layer), _pick(w, layer)]
    out_specs = [pl.BlockSpec((tm, p), lambda i: (i, 0))]
    out_shape = [jax.ShapeDtypeStruct((n, p), BF16)]
    args = [x, g, w]
    if w_t is None:
        body = lambda x_ref, g_ref, w_ref, o_ref: _norm_proj_kernel(
            x_ref, g_ref, w_ref, None, o_ref, None, tile=tile)
    else:
        pt = w_t.shape[1]
        in_specs.append(_pick(w_t, layer))
        out_specs.append(pl.BlockSpec((tm // tile, pt, tile), lambda i: (i, 0, 0)))
        out_shape.append(jax.ShapeDtypeStruct((n // tile, pt, tile), BF16))
        args.append(w_t)
        body = functools.partial(_norm_proj_kernel, tile=tile)
    out = pl.pallas_call(
        body, grid=(n // tm,), in_specs=in_specs, out_specs=out_specs, out_shape=out_shape,
        compiler_params=_params("parallel"), name="norm_proj",
    )(*args)
    return out if w_t is not None else out[0]


def _even_proj_kernel(x_ref, g_ref, w_ref, wt_ref, wkc_ref, wvc_ref, wff_ref, bf_ref, wng_ref, tri_ref,
                      main_ref, vt_ref, kc_ref, vc_ref, f_ref, gate_ref, carry_ref, chunk_ref,
                      *, tiles_per_seq, tile):
    i = pl.program_id(0)
    xn = _rmsnorm(x_ref[...], g_ref[...]).astype(BF16)
    main_ref[...] = _dot(xn, w_ref[...]).astype(BF16)
    t = _dot_nt(wt_ref[...], xn).astype(BF16)
    for c in range(t.shape[1] // tile):
        vt_ref[c] = t[:, c * tile:(c + 1) * tile]
    chunks = chunk_ref.shape[0] // CMP_STRIDE
    for w_c_ref, out_ref in ((wkc_ref, kc_ref), (wvc_ref, vc_ref)):
        chunk_ref[...] = _dot(xn, w_c_ref[...])
        for pos in range(CMP_STRIDE):
            out_ref[:, pos * LANES:(pos + 1) * LANES] = chunk_ref[
                pl.ds(pos, chunks, stride=CMP_STRIDE), :].astype(BF16)
    gate_ref[...] = _sigmoid(_dot_nt(wng_ref[...], xn))
    ff = _dot(xn, wff_ref[...]) + bf_ref[...]
    logf = jnp.minimum(ff, 0.0) - jnp.log1p(jnp.exp(-jnp.abs(ff)))

    @pl.when(i % tiles_per_seq == 0)
    def _():
        carry_ref[...] = jnp.zeros_like(carry_ref)

    cs = _dot_01exact(tri_ref[...], logf) + carry_ref[0:1, :]
    carry_ref[...] = jnp.broadcast_to(cs[cs.shape[0] - 1:, :], carry_ref.shape)
    hi, mid, lo = _split3(-cs)
    piece = lax.broadcasted_iota(jnp.int32, cs.shape, 1) % F_PIECES
    f_ref[...] = jnp.where(piece == 0, hi, jnp.where(piece == 1, mid, lo))


def even_proj(x, g, layer, w_main, w_t, w_kc, w_vc, w_ff, b_f, w_ng_t, e, tri_l, seq_len,
              tile=ATT_TILE, tm=ROW_TILE):
    n, d = x.shape
    tm = min(tm, seq_len)
    pm = w_main.shape[2]
    pt = w_t.shape[1]
    gr = w_ng_t.shape[1]
    row = lambda width: pl.BlockSpec((tm, width), lambda i: (i, 0))
    chunked = pl.BlockSpec((tm // CMP_STRIDE, CMP_STRIDE * LANES), lambda i: (i, 0))
    chunked_shape = jax.ShapeDtypeStruct((n // CMP_STRIDE, CMP_STRIDE * LANES), BF16)
    return pl.pallas_call(
        functools.partial(_even_proj_kernel, tiles_per_seq=seq_len // tm, tile=tile),
        grid=(n // tm,),
        in_specs=[row(d), _pick(g, layer), _pick(w_main, e), _pick(w_t, e), _pick(w_kc, e), _pick(w_vc, e),
                  _pick(w_ff, e), _pick(b_f, e), _pick(w_ng_t, e), _full(tri_l)],
        out_specs=[row(pm),
                   pl.BlockSpec((tm // tile, pt, tile), lambda i: (i, 0, 0)),
                   chunked, chunked, row(LANES),
                   pl.BlockSpec((gr, tm), lambda i: (0, i))],
        out_shape=[jax.ShapeDtypeStruct((n, pm), BF16),
                   jax.ShapeDtypeStruct((n // tile, pt, tile), BF16),
                   chunked_shape, chunked_shape,
                   jax.ShapeDtypeStruct((n, LANES), BF16),
                   jax.ShapeDtypeStruct((gr, n), F32)],
        scratch_shapes=[pltpu.VMEM((8, LANES), F32), pltpu.VMEM((tm, LANES), F32)],
        compiler_params=_params("arbitrary"),
        name="even_proj",
    )(x, g, w_main, w_t, w_kc, w_vc, w_ff, b_f, w_ng_t, tri_l)


def _bucket_lookup_kernel(tab_ref, idx_ref, o_ref):
    h = pl.program_id(0)
    idx = idx_ref[...]
    out = jnp.full(idx.shape, NEG, F32)
    for b in range(N_BUCKETS):
        out = jnp.where(idx == b, tab_ref[b, h], out)
    o_ref[0] = out


def bucket_lookup(tab, idx, rows):
    r, c = idx.shape
    h = tab.shape[1]
    return pl.pallas_call(
        _bucket_lookup_kernel,
        grid=(h, r // rows),
        in_specs=[pl.BlockSpec(memory_space=pltpu.SMEM),
                  pl.BlockSpec((rows, c), lambda hh, j: (j, 0))],
        out_specs=pl.BlockSpec((1, rows, c), lambda hh, j: (hh, j, 0)),
        out_shape=jax.ShapeDtypeStruct((h, r, c), F32),
        compiler_params=_params("parallel", "parallel"),
        name="bucket_lookup",
    )(tab, idx)


def _fox_kernel(q_ref, k_ref, fs_ref, vt_ref, o_ref, stage_ref, *, tile, n_tiles):
    p = pl.program_id(1)
    lane = lax.broadcasted_iota(jnp.int32, (tile, LANES), 1)
    causal = (lax.broadcasted_iota(jnp.int32, (tile, tile), 0)
              <= lax.broadcasted_iota(jnp.int32, (tile, tile), 1))
    lo = _row_lo((LANES, tile))

    def keys(j):
        sl = slice(j * tile, (j + 1) * tile)
        return jnp.concatenate([k_ref[sl, :], fs_ref[sl, :]], axis=1)

    def queries(i):
        heads = _split_pair(q_ref[i * tile:(i + 1) * tile, :], HEAD_DIM ** -0.5)
        return tuple(jnp.concatenate(
            [heads[a], jnp.where(lane // F_PIECES == 2 * p + a, 1.0, 0.0).astype(BF16)], axis=1)
            for a in range(2))

    tasks = [(i, j) for i in range(n_tiles) for j in range(i + 1)]
    qcache = {}

    def scores(t):
        i, j = tasks[t]
        if i not in qcache:
            qcache[i] = queries(i)
        kj = keys(j)
        for a in range(2):
            stage_ref[t % 2, a] = _dot_nt(kj, qcache[i][a])

    scores(0)
    for t, (i, j) in enumerate(tasks):
        if t + 1 < len(tasks):
            scores(t + 1)
        qk = [stage_ref[t % 2, a] for a in range(2)]
        if j == 0:
            stats = [(jnp.full((1, tile), NEG, F32), jnp.zeros((1, tile), F32)) for _ in range(2)]
            acc = [jnp.zeros((LANES, tile), F32) for _ in range(2)]
        vt = vt_ref[j]
        alphas, pvs = [], []
        for a in range(2):
            m_prev, l_prev = stats[a]
            s = jnp.where(causal, qk[a], NEG) if j == i else qk[a]
            m_new = jnp.maximum(m_prev, jnp.max(s, axis=0, keepdims=True))
            alpha = jnp.exp(m_prev - m_new)
            pr = jnp.exp(s - m_new)
            stats[a] = (m_new, alpha * l_prev + jnp.sum(pr, axis=0, keepdims=True))
            alphas.append(alpha)
            pvs.append(_dot(vt, pr.astype(BF16)))
        for a in range(2):
            acc[a] = alphas[a] * acc[a] + pvs[a]
        if j == i:
            o = jnp.where(lo, acc[0] / stats[0][1], acc[1] / stats[1][1])
            o_ref[i * tile:(i + 1) * tile, :] = o.T.astype(o_ref.dtype)


def fox_attention(main, fs, vt, batch, seq_len, q_col, k_col, v_row, n_pairs, tile=ATT_TILE):
    nq = seq_len // tile
    n = batch * seq_len
    return pl.pallas_call(
        functools.partial(_fox_kernel, tile=tile, n_tiles=nq),
        grid=(batch, n_pairs),
        in_specs=[pl.BlockSpec((seq_len, LANES), lambda b, p: (b, q_col + p)),
                  pl.BlockSpec((seq_len, LANES), lambda b, p: (b, k_col + p)),
                  pl.BlockSpec((seq_len, LANES), lambda b, p: (b, 0)),
                  pl.BlockSpec((nq, LANES, tile), lambda b, p: (b, v_row + p, 0))],
        out_specs=pl.BlockSpec((seq_len, LANES), lambda b, p: (b, p)),
        out_shape=jax.ShapeDtypeStruct((n, n_pairs * LANES), BF16),
        scratch_shapes=[pltpu.VMEM((2, 2, tile, tile), F32)],
        compiler_params=_params("parallel", "parallel"),
        name="fox_attention",
    )(main, main, fs, vt)


def _sb_kernel(q_ref, k_ref, vt_ref, tri_ref, o_ref, acc_ref, stage_ref, *, tile, n_tiles):
    tri = tri_ref[...]
    strict = (lax.broadcasted_iota(jnp.int32, (tile, tile), 0)
              < lax.broadcasted_iota(jnp.int32, (tile, tile), 1))
    lo = _row_lo((LANES, tile))

    def queries(i):
        return _split_pair(q_ref[i * tile:(i + 1) * tile, :], HEAD_DIM ** -0.5)

    def scores(ks, heads):
        return tuple(_dot_nt(ks, heads[a]) for a in range(2))

    def consume(zz, rs, vt, masked):
        sps, base, later, pvs = [], [], [], []
        for a in range(2):
            z = zz[a]
            sp = jnp.maximum(z, 0.0) + jnp.log(1.0 + jnp.exp(-jnp.abs(z)))
            base.append(z - sp)
            if masked:
                sp = jnp.where(strict, sp, 0.0)
            sps.append(sp)
        for a in range(2):
            later.append(_dot(tri, sps[a].astype(BF16)))
        for a in range(2):
            wgt = jnp.exp(base[a] - later[a] + rs[a])
            if masked:
                wgt = jnp.where(strict, wgt, 0.0)
            pvs.append(_dot(vt, wgt.astype(BF16)))
        return tuple(rs[a] - jnp.sum(sps[a], axis=0, keepdims=True) for a in range(2)), pvs

    def live(rs):
        return (jnp.max(jnp.maximum(rs[0], rs[1])) > EXP_UNDERFLOW).astype(jnp.int32)

    tasks = [(i, j) for i in range(n_tiles) for j in ((i, i - 1) if i else (i,))]
    qcache = {}

    def task_scores(t):
        i, j = tasks[t]
        if i not in qcache:
            qcache[i] = queries(i)
        zz = scores(k_ref[j * tile:(j + 1) * tile, :], qcache[i])
        for a in range(2):
            stage_ref[t % 2, a] = zz[a]

    zero = jnp.zeros((1, tile), F32)
    survival = {}
    task_scores(0)
    for t, (i, j) in enumerate(tasks):
        if t + 1 < len(tasks):
            task_scores(t + 1)
        zz = [stage_ref[t % 2, a] for a in range(2)]
        if j == i:
            rs, acc = (zero, zero), None
        rs, pvs = consume(zz, rs, vt_ref[j], j == i)
        acc = pvs if acc is None else [acc[a] + pvs[a] for a in range(2)]
        if j == max(i - 1, 0):
            for a in range(2):
                acc_ref[i, a] = acc[a]
            survival[i] = rs

    for i in range(2, n_tiles):
        heads = qcache[i]

        def cond(state):
            j, alive, _ = state
            return (j >= 0) & (alive > 0)

        def body(state, i=i, heads=heads):
            j, _, rs = state
            rs, pvs = consume(scores(k_ref[pl.ds(j * tile, tile), :], heads), rs, vt_ref[j], False)
            for a in range(2):
                acc_ref[i, a] = acc_ref[i, a] + pvs[a]
            return j - 1, live(rs), rs

        lax.while_loop(cond, body, (i - 2, live(survival[i]), survival[i]))

    for i in range(n_tiles):
        o = jnp.where(lo, acc_ref[i, 0], acc_ref[i, 1])
        o_ref[i * tile:(i + 1) * tile, :] = o.T.astype(o_ref.dtype)


def sb_attention(main, vt, batch, seq_len, q_col, k_col, n_pairs, tri, tile=ATT_TILE):
    nq = seq_len // tile
    n = batch * seq_len
    return pl.pallas_call(
        functools.partial(_sb_kernel, tile=tile, n_tiles=nq),
        grid=(batch, n_pairs),
        in_specs=[pl.BlockSpec((seq_len, LANES), lambda b, p: (b, q_col + p)),
                  pl.BlockSpec((seq_len, LANES), lambda b, p: (b, k_col + p)),
                  pl.BlockSpec((nq, LANES, tile), lambda b, p: (b, p, 0)),
                  _full(tri)],
        out_specs=pl.BlockSpec((seq_len, LANES), lambda b, p: (b, p)),
        out_shape=jax.ShapeDtypeStruct((n, n_pairs * LANES), BF16),
        scratch_shapes=[pltpu.VMEM((nq, 2, LANES, tile), F32), pltpu.VMEM((2, 2, tile, tile), F32)],
        compiler_params=_params("parallel", "parallel"),
        name="sb_attention",
    )(main, main, vt, tri)


def _gelu_tanh(x):
    return 0.5 * x * (1.0 + jnp.tanh(math.sqrt(2.0 / math.pi) * (x + 0.044715 * (x * x * x))))


def _compress_kernel(xk_ref, xv_ref, pek_ref, pev_ref, w1k_ref, w1v_ref, w2k_ref, w2v_ref,
                     ok_ref, ov_ref, *, n_cmp):
    def one(x_ref, pe_ref, w1_ref, w2_ref):
        x = x_ref[...].astype(F32)
        xa = (x + pe_ref[0:1, :]).astype(BF16)
        xb = (x + pe_ref[1:2, :]).astype(BF16)
        ha = _dot(xa, w1_ref[0])
        hb = _dot(xb, w1_ref[1])
        rows = ha.shape[0]
        hb = pltpu.roll(hb, rows - 1, 0)
        h = _gelu_tanh(ha + hb)
        out = _dot(h.astype(BF16), w2_ref[...])
        ridx = lax.broadcasted_iota(jnp.int32, out.shape, 0)
        return jnp.where(ridx < n_cmp, out, 0.0)

    ok_ref[0] = one(xk_ref, pek_ref, w1k_ref, w2k_ref).astype(ok_ref.dtype)
    ov_ref[0] = one(xv_ref, pev_ref, w1v_ref, w2v_ref).T.astype(ov_ref.dtype)


def compress(xk, xv, pek, pev, w1k, w1v, w2k, w2v, batch, n_cmp):
    chunks = xk.shape[0] // batch
    row = pl.BlockSpec((chunks, xk.shape[1]), lambda b: (b, 0))
    return pl.pallas_call(
        functools.partial(_compress_kernel, n_cmp=n_cmp),
        grid=(batch,),
        in_specs=[row, row, _full(pek), _full(pev), _full(w1k), _full(w1v), _full(w2k), _full(w2v)],
        out_specs=[pl.BlockSpec((1, chunks, LANES), lambda b: (b, 0, 0)),
                   pl.BlockSpec((1, LANES, chunks), lambda b: (b, 0, 0))],
        out_shape=[jax.ShapeDtypeStruct((batch, chunks, LANES), BF16),
                   jax.ShapeDtypeStruct((batch, LANES, chunks), BF16)],
        compiler_params=_params("parallel"),
        name="nsa_compress",
    )(xk, xv, pek, pev, w1k, w1v, w2k, w2v)


def _nsa_cmp_kernel(q_ref, kc_ref, vct_ref, cb_ref, ov_ref, ocmp_ref, sel_ref, *, tile, n_blocks):
    i = pl.program_id(1)
    kc = kc_ref[0]
    vct = vct_ref[0]
    pcsum = [jnp.zeros((LANES, tile), F32), jnp.zeros((LANES, tile), F32)]
    lo = _row_lo((LANES, tile))
    for r in range(NSA_GQA):
        heads = _split_pair(q_ref[:, r * LANES:(r + 1) * LANES], HEAD_DIM ** -0.5)
        outs = []
        for a in range(2):
            bias = cb_ref[2 * r + a]
            valid = bias > 0.5 * NEG
            s = _dot_nt(kc, heads[a]) + bias
            m = jnp.max(s, axis=0, keepdims=True)
            pr = jnp.where(valid, jnp.exp(s - m), 0.0)
            l = jnp.sum(pr, axis=0, keepdims=True)
            pc = pr / jnp.where(l > 0.0, l, 1.0)
            pcsum[a] = pcsum[a] + pc
            outs.append(_dot(vct, pc.astype(BF16)))
        ocmp_ref[r * LANES:(r + 1) * LANES, :] = jnp.where(lo, outs[0], outs[1])

    score = _dot_01exact(ov_ref[0], pcsum[0]) + _dot_01exact(ov_ref[1], pcsum[1])
    score = score[:2 * n_blocks]
    shape = (2 * n_blocks, tile)
    row = lax.broadcasted_iota(jnp.int32, shape, 0)
    qpos = i * tile + lax.broadcasted_iota(jnp.int32, shape, 1)
    blk = row % n_blocks
    cur = qpos // SLC_BLOCK
    forced = (blk == 0) | (blk == cur) | (blk == cur - 1)
    future = blk > cur
    score = jnp.where(future, -1.0, jnp.where(forced, FORCED_SCORE, score))
    grp0 = row < n_blocks
    cnt = jnp.zeros(shape, F32)
    for mth in range(n_blocks):
        other = jnp.where(grp0, score[mth:mth + 1, :], score[n_blocks + mth:n_blocks + mth + 1, :])
        ahead = (other > score) | ((other == score) & (blk > mth))
        cnt = cnt + jnp.where(ahead, 1.0, 0.0)
    n_top = min(SLC_TOPK, n_blocks)
    sel_ref[...] = jnp.where(cnt < n_top, 0.0, NEG)


def nsa_cmp_select(main, kc, vct, cmp_bias, overlap2, batch, seq_len, q_col, tile=ATT_TILE):
    nq = seq_len // tile
    n = batch * seq_len
    n_blocks = seq_len // SLC_BLOCK
    qw = NSA_GQA * LANES
    return pl.pallas_call(
        functools.partial(_nsa_cmp_kernel, tile=tile, n_blocks=n_blocks),
        grid=(batch, nq),
        in_specs=[pl.BlockSpec((tile, qw), lambda b, i: (b * nq + i, q_col // NSA_GQA)),
                  pl.BlockSpec((1,) + kc.shape[1:], lambda b, i: (b, 0, 0)),
                  pl.BlockSpec((1,) + vct.shape[1:], lambda b, i: (b, 0, 0)),
                  pl.BlockSpec((NSA_HEADS, cmp_bias.shape[1], tile), lambda b, i: (0, 0, i)),
                  _full(overlap2)],
        out_specs=[pl.BlockSpec((qw, tile), lambda b, i: (0, b * nq + i)),
                   pl.BlockSpec((2 * n_blocks, tile), lambda b, i: (0, b * nq + i))],
        out_shape=[jax.ShapeDtypeStruct((qw, n), F32),
                   jax.ShapeDtypeStruct((2 * n_blocks, n), F32)],
        compiler_params=_params("parallel", "parallel"),
        name="nsa_cmp_select",
    )(main, kc, vct, cmp_bias, overlap2)


def _nsa_main_kernel(far_ref, q_ref, sel_ref, ocmp_ref, gate_ref, ks_ref, vst_ref, kw_ref, vwt_ref, tz_ref,
                     o_ref, stage_ref, *, tile, n_tiles, n_blocks):
    r = pl.program_id(1)
    lo = _row_lo((LANES, tile))
    blocks_per_tile = tile // SLC_BLOCK
    far = tuple(far_ref[2 * r + a] for a in range(2))

    def queries(i):
        return _split_pair(q_ref[i * tile:(i + 1) * tile, :], HEAD_DIM ** -0.5)

    def chosen(a, i, j, offset=None):
        rows = []
        for nb in range(blocks_per_tile):
            blk = a * n_blocks + j * blocks_per_tile + nb
            row = sel_ref[blk:blk + 1, i * tile:(i + 1) * tile]
            if offset is not None:
                row = row + offset
            rows.append(jnp.broadcast_to(row, (SLC_BLOCK, tile)))
        return jnp.concatenate(rows, axis=0)

    tasks = []
    for i in range(n_tiles):
        tasks += [("slc", i, j) for j in range(i + 1)]
        tasks += [("win", i, j) for j in ((i - 1, i) if i else (i,))]
    qcache = {}

    def scores(t):
        kind, i, j = tasks[t]
        if i not in qcache:
            qcache[i] = queries(i)
        k_ref_ = ks_ref if kind == "slc" else kw_ref
        kj = k_ref_[j * tile:(j + 1) * tile, :]
        for a in range(2):
            stage_ref[t % 2, a] = _dot_nt(kj, qcache[i][a])

    def fresh():
        return ([(jnp.full((1, tile), NEG, F32), jnp.zeros((1, tile), F32)) for _ in range(2)],
                [jnp.zeros((LANES, tile), F32) for _ in range(2)])

    scores(0)
    for t, (kind, i, j) in enumerate(tasks):
        if t + 1 < len(tasks):
            scores(t + 1)
        qk = [stage_ref[t % 2, a] for a in range(2)]
        first = j == 0 if kind == "slc" else j == max(i - 1, 0)
        if first:
            stats, acc = fresh()
        vt = (vst_ref if kind == "slc" else vwt_ref)[j]
        alphas, pvs = [], []
        for a in range(2):
            if j == i:
                s = qk[a] + tz_ref[a, 0]
            elif j == i - 1:
                s = qk[a] + tz_ref[a, 1 if kind == "slc" else 2]
            if kind == "slc":
                s = s + chosen(a, i, j) if j >= i - 1 else qk[a] + chosen(a, i, j, far[a])
            m_prev, l_prev = stats[a]
            m_new = jnp.maximum(m_prev, jnp.max(s, axis=0, keepdims=True))
            alpha = jnp.exp(m_prev - m_new)
            pr = jnp.exp(s - m_new)
            stats[a] = (m_new, alpha * l_prev + jnp.sum(pr, axis=0, keepdims=True))
            alphas.append(alpha)
            pvs.append(_dot(vt, pr.astype(BF16)))
        for a in range(2):
            acc[a] = alphas[a] * acc[a] + pvs[a]
        if j == i:
            branch = jnp.where(lo, acc[0] / stats[0][1], acc[1] / stats[1][1])
            if kind == "slc":
                o_slc = branch
            else:
                cols = slice(i * tile, (i + 1) * tile)
                gexp = [jnp.where(lo, gate_ref[2 * c:2 * c + 1, cols], gate_ref[2 * c + 1:2 * c + 2, cols])
                        for c in range(3)]
                out = gexp[0] * ocmp_ref[:, cols] + gexp[1] * o_slc + gexp[2] * branch
                o_ref[cols, :] = out.T.astype(o_ref.dtype)


def nsa_main(far_bias, main, sel, ocmp, gates, vt, tz, batch, seq_len, q_col, ks_col, kw_col, vs_row, vw_row,
             tile=ATT_TILE):
    nq = seq_len // tile
    n = batch * seq_len
    n_blocks = seq_len // SLC_BLOCK
    kv = lambda col: pl.BlockSpec((seq_len, LANES), lambda b, r: (b, col))
    vts = lambda rowblk: pl.BlockSpec((nq, LANES, tile), lambda b, r: (b, rowblk, 0))
    return pl.pallas_call(
        functools.partial(_nsa_main_kernel, tile=tile, n_tiles=nq, n_blocks=n_blocks),
        grid=(batch, NSA_GQA),
        in_specs=[pl.BlockSpec(memory_space=pltpu.SMEM),
                  pl.BlockSpec((seq_len, LANES), lambda b, r: (b, q_col + r)),
                  pl.BlockSpec((2 * n_blocks, seq_len), lambda b, r: (0, b)),
                  pl.BlockSpec((LANES, seq_len), lambda b, r: (r, b)),
                  pl.BlockSpec((GATE_ROWS, seq_len), lambda b, r: (r, b)),
                  kv(ks_col), vts(vs_row), kv(kw_col), vts(vw_row),
                  pl.BlockSpec((2, 3, tile, tile), lambda b, r: (r, 0, 0, 0))],
        out_specs=pl.BlockSpec((seq_len, LANES), lambda b, r: (b, r)),
        out_shape=jax.ShapeDtypeStruct((n, NSA_GQA * LANES), BF16),
        scratch_shapes=[pltpu.VMEM((2, 2, tile, tile), F32)],
        compiler_params=_params("parallel", "parallel"),
        name="nsa_main",
    )(far_bias, main, sel, ocmp, gates, main, vt, main, vt, tz)


def _layer_tail_kernel(*refs, n_in, final_norm):
    a_refs = refs[:n_in]
    w_refs = refs[n_in:2 * n_in]
    (x_ref, gx_ref, wq_ref, kv_ref, wo_ref, gm_ref, w1_ref, w2_ref, gf_ref,
     o_ref, hn_ref, acc_ref) = refs[2 * n_in:]
    f = pl.program_id(1)

    @pl.when(f == 0)
    def _():
        x1 = x_ref[...]
        for a_ref, w_ref in zip(a_refs, w_refs):
            x1 = x1 + _dot(a_ref[...], w_ref[...])
        q = _dot(_rmsnorm(x1, gx_ref[...]).astype(BF16), wq_ref[...]).astype(BF16)
        width = XA_HEADS * XA_HEAD_DIM
        outs = []
        for h in range(XA_HEADS):
            hs = slice(h * XA_HEAD_DIM, (h + 1) * XA_HEAD_DIM)
            s = _dot_nt(q[:, hs], kv_ref[:, hs]) * (XA_HEAD_DIM ** -0.5)
            m = jnp.max(s, axis=-1, keepdims=True)
            p = jnp.exp(s - m)
            l = jnp.sum(p, axis=-1, keepdims=True)
            v = kv_ref[:, width + h * XA_HEAD_DIM:width + (h + 1) * XA_HEAD_DIM]
            outs.append((_dot(p.astype(BF16), v) / l).astype(BF16))
        x2 = x1 + _dot(jnp.concatenate(outs, axis=-1), wo_ref[...])
        hn_ref[...] = _rmsnorm(x2, gm_ref[...]).astype(BF16)
        acc_ref[...] = x2

    h = jnp.maximum(_dot(hn_ref[...], w1_ref[...]), 0.0)
    acc_ref[...] += _dot((h * h).astype(BF16), w2_ref[...])

    @pl.when(f == pl.num_programs(1) - 1)
    def _():
        y = acc_ref[...]
        if final_norm:
            y = _rmsnorm(y, gf_ref[...])
        o_ref[...] = y


def layer_tail(a_list, w_list, w_idx, x, layer, gx, wq, kv, wo, gm, w1, w2, gf, final_norm, seq_len, mem_len,
               tm=MLP_ROW_TILE, tf=MLP_FF_TILE):
    n, d = x.shape
    ff = w1.shape[2]
    tm = min(tm, seq_len)
    tf = min(tf, ff)
    tiles_per_seq = seq_len // tm
    once = pl.Buffered(1)
    row = lambda a: pl.BlockSpec((tm, a.shape[1]), lambda i, f: (i, 0))
    pick = lambda a, k: pl.BlockSpec((None,) + a.shape[1:], lambda i, f: (k,) + (0,) * (a.ndim - 1),
                                     pipeline_mode=once)
    return pl.pallas_call(
        functools.partial(_layer_tail_kernel, n_in=len(a_list), final_norm=final_norm),
        grid=(n // tm, ff // tf),
        in_specs=[row(a) for a in a_list] + [pick(w, w_idx) for w in w_list]
        + [row(x), pick(gx, layer), pick(wq, layer),
           pl.BlockSpec((mem_len, 2 * XA_HEADS * XA_HEAD_DIM), lambda i, f: (i // tiles_per_seq, layer)),
           pick(wo, layer), pick(gm, layer),
           pl.BlockSpec((None, d, tf), lambda i, f: (layer, 0, f)),
           pl.BlockSpec((None, tf, d), lambda i, f: (layer, f, 0)),
           pl.BlockSpec(gf.shape, lambda i, f: (0, 0), pipeline_mode=once)],
        out_specs=pl.BlockSpec((tm, d), lambda i, f: (i, 0)),
        out_shape=jax.ShapeDtypeStruct((n, d), F32),
        scratch_shapes=[pltpu.VMEM((tm, d), BF16), pltpu.VMEM((tm, d), F32)],
        compiler_params=_params("parallel", "arbitrary"),
        name="layer_tail",
    )(*a_list, *w_list, x, gx, wq, kv, wo, gm, w1, w2, gf)


def _static_tables(seq_len):
    tile = ATT_TILE
    n_chunks = seq_len // CMP_STRIDE
    n_cmp = (seq_len - CMP_BLOCK) // CMP_STRIDE + 1
    n_slc = seq_len // SLC_BLOCK
    assert n_chunks == LANES and 2 * n_slc <= LANES and seq_len % tile == 0 and WINDOW == tile
    c = np.arange(n_chunks)[:, None]
    t = np.arange(seq_len)[None, :]
    cdist = t - (c * CMP_STRIDE + CMP_BLOCK - 1)
    cmp_idx = np.where((cdist >= 0) & (c < n_cmp), _t5_bucket_np(cdist), -1).astype(np.int32)
    sl = np.arange(tile)[:, None]
    tl = np.arange(tile)[None, :]
    prev = _t5_bucket_np(tl - sl + tile)
    tz_idx = np.concatenate([np.where(sl <= tl, _t5_bucket_np(tl - sl), -1), prev,
                             np.where(sl > tl, prev, -1)], axis=0).astype(np.int32)
    far_bucket = int(_t5_bucket_np(np.array([tile + 1]))[0])
    assert far_bucket == int(_t5_bucket_np(np.array([seq_len]))[0])
    cmp_start = np.arange(n_cmp) * CMP_STRIDE
    cmp_stop = cmp_start + CMP_BLOCK - 1
    slc_start = np.arange(n_slc) * SLC_BLOCK
    slc_stop = slc_start + SLC_BLOCK - 1
    ov = ((cmp_start[:, None] <= slc_stop[None, :]) & (cmp_stop[:, None] >= slc_start[None, :]))
    overlap2 = np.zeros((2, LANES, LANES), np.float32)
    for a in range(2):
        overlap2[a, a * n_slc:(a + 1) * n_slc, :n_cmp] = ov.T
    return cmp_idx, tz_idx, far_bucket, overlap2, n_cmp


def _tri_prefix(nn):
    s = np.arange(nn)[:, None]
    j = np.arange(nn)[None, :]
    return (j <= s).astype(np.float32)


def _tri_later(nn):
    s = np.arange(nn)[:, None]
    j = np.arange(nn)[None, :]
    return (j > s).astype(np.float32)


_NSA_HEAD_PERM = [a * NSA_GQA + r for r in range(NSA_GQA) for a in range(NSA_GROUPS)]


def _block_diag2(m):
    z = jnp.zeros_like(m)
    return jnp.concatenate([jnp.concatenate([m, z], axis=1), jnp.concatenate([z, m], axis=1)], axis=0)


def _compress_weights(pe, w1, w2):
    half = CMP_BLOCK // 2
    pe2 = jnp.concatenate([pe, pe], axis=1).reshape(2, half * LANES)
    w1e = jax.vmap(_block_diag2)(w1).reshape(2, half * LANES, LANES)
    return pe2.astype(F32), w1e.astype(BF16), _block_diag2(w2).astype(BF16)


def kernel(x, mem, rel_bias, mem_norm_g, norm_mix_g, norm_xattn_g, norm_mlp_g, final_norm_g, w_in_even, b_forget, cmp_pe_k, cmp_w1_k, cmp_w2_k, cmp_pe_v, cmp_w1_v, cmp_w2_v, w_out_even, w_in_odd, w_out_odd, xa_wq, xa_wkv, xa_wo, mlp_w1, mlp_w2):
    batch, seq_len, d = x.shape
    mem_len = mem.shape[1]
    depth = norm_mix_g.shape[0]
    n = batch * seq_len
    fox_w = FOX_HEADS * HEAD_DIM
    nsa_w = NSA_HEADS * HEAD_DIM
    kv_w = NSA_GROUPS * HEAD_DIM
    splits = np.cumsum([fox_w, fox_w, fox_w, FOX_HEADS, nsa_w, kv_w, kv_w, kv_w, kv_w, kv_w, kv_w])

    cmp_idx, tz_idx, far_bucket, overlap2, n_cmp = _static_tables(seq_len)
    tab = rel_bias.astype(F32)[:, _NSA_HEAD_PERM]
    cmp_bias = bucket_lookup(tab, jnp.asarray(cmp_idx), rows=cmp_idx.shape[0])
    tz = bucket_lookup(tab, jnp.asarray(tz_idx), rows=ATT_TILE)
    tz = tz.reshape(NSA_HEADS, 3, ATT_TILE, ATT_TILE)
    far_bias = tab[far_bucket]
    overlap2 = jnp.asarray(overlap2, BF16)
    tri_prefix = jnp.asarray(_tri_prefix(min(ROW_TILE, seq_len)), BF16)
    tri_later = jnp.asarray(_tri_later(ATT_TILE), BF16)

    xf = x.reshape(n, d)
    memf = mem.reshape(batch * mem_len, d)
    row = lambda v: v.reshape(1, -1).astype(F32)
    rows = lambda v: v.reshape(v.shape[0], 1, -1).astype(F32)
    g_mix_all, g_xa_all, g_mlp_all = rows(norm_mix_g), rows(norm_xattn_g), rows(norm_mlp_g)
    g_mem, g_final = row(mem_norm_g), row(final_norm_g)

    ne = w_in_even.shape[0]
    (w_fq, w_fk, w_fv, w_ff, w_nq, w_kc, w_vc, w_ks, w_vs, w_kw, w_vw, w_ng) = jnp.split(w_in_even, splits, axis=2)
    w_nq = w_nq.reshape(ne, d, NSA_HEADS, HEAD_DIM)[:, :, _NSA_HEAD_PERM, :].reshape(ne, d, nsa_w)
    w_main_all = jnp.concatenate([w_fq, w_fk, w_nq, w_ks, w_kw], axis=2).astype(BF16)
    w_t_all = jnp.concatenate([w_fv, w_vs, w_vw], axis=2).transpose(0, 2, 1).astype(BF16)
    w_kc_all, w_vc_all = w_kc.astype(BF16), w_vc.astype(BF16)
    w_ng_t = w_ng.reshape(ne, d, NSA_GROUPS, NSA_GQA, 3).transpose(0, 3, 4, 2, 1).reshape(ne, NSA_GQA, 6, d)
    w_ng_all = jnp.pad(w_ng_t, ((0, 0), (0, 0), (0, GATE_ROWS - 6), (0, 0))).reshape(
        ne, NSA_GQA * GATE_ROWS, d).astype(BF16)
    pad_f = LANES - F_PIECES * FOX_HEADS
    w_ff_all = jnp.pad(jnp.repeat(w_ff, F_PIECES, axis=2), ((0, 0), (0, 0), (0, pad_f))).astype(BF16)
    b_f_all = jnp.pad(jnp.repeat(b_forget.astype(F32), F_PIECES, axis=1), ((0, 0), (0, pad_f))).reshape(ne, 1, LANES)
    cmp_k = jax.vmap(_compress_weights)(cmp_pe_k, cmp_w1_k, cmp_w2_k)
    cmp_v = jax.vmap(_compress_weights)(cmp_pe_v, cmp_w1_v, cmp_w2_v)
    w_o_fox_all = w_out_even[:, :fox_w].astype(BF16)
    w_o_nsa_all = w_out_even[:, fox_w:].reshape(ne, NSA_HEADS, HEAD_DIM, d)[:, _NSA_HEAD_PERM].reshape(
        ne, nsa_w, d).astype(BF16)
    sb_w = SB_HEADS * HEAD_DIM
    w_odd_qk_all = w_in_odd[:, :, :2 * sb_w].astype(BF16)
    w_odd_vt_all = w_in_odd[:, :, 2 * sb_w:].transpose(0, 2, 1).astype(BF16)
    w_o_odd_all = w_out_odd.astype(BF16)
    xa_wq_all, xa_wo_all = xa_wq.astype(BF16), xa_wo.astype(BF16)
    mlp_w1_all, mlp_w2_all = mlp_w1.astype(BF16), mlp_w2.astype(BF16)
    xa_wkv_cat = xa_wkv.transpose(1, 0, 2).reshape(d, depth * xa_wkv.shape[2]).astype(BF16)
    kv_mem = norm_proj(memf, g_mem[None], 0, xa_wkv_cat[None])

    for layer in range(depth):
        if layer % 2 == 0:
            e = layer // 2
            main, vt, kc_in, vc_in, fs, gates = even_proj(
                xf, g_mix_all, layer, w_main_all, w_t_all, w_kc_all, w_vc_all, w_ff_all, b_f_all, w_ng_all, e,
                tri_prefix, seq_len)
            nblk = fox_w // LANES
            fox = fox_attention(main, fs, vt, batch, seq_len, 0, nblk, 0, FOX_HEADS // 2)

            kc, vct = compress(kc_in, vc_in,
                               cmp_k[0][e], cmp_v[0][e], cmp_k[1][e], cmp_v[1][e], cmp_k[2][e], cmp_v[2][e],
                               batch, n_cmp)
            q_col = 2 * nblk
            ocmp, sel = nsa_cmp_select(main, kc, vct, cmp_bias, overlap2, batch, seq_len, q_col)
            nsa = nsa_main(far_bias, main, sel, ocmp, gates, vt, tz, batch, seq_len, q_col,
                           q_col + NSA_GQA, q_col + NSA_GQA + 1, nblk, nblk + 1)
            a_list, w_list, w_idx = [fox, nsa], [w_o_fox_all, w_o_nsa_all], e
        else:
            o = layer // 2
            main, vt = norm_proj(xf, g_mix_all, layer, w_odd_qk_all, w_odd_vt_all, o)
            nblk = sb_w // LANES
            sb = sb_attention(main, vt, batch, seq_len, 0, nblk, SB_HEADS // 2, tri_later)
            a_list, w_list, w_idx = [sb], [w_o_odd_all], o

        xf = layer_tail(a_list, w_list, w_idx, xf, layer, g_xa_all, xa_wq_all, kv_mem, xa_wo_all, g_mlp_all,
                        mlp_w1_all, mlp_w2_all, g_final, layer == depth - 1, seq_len, mem_len)
    return xf.reshape(batch, seq_len, d)
```

```python
import functools
import math

import numpy as np
import jax
import jax.numpy as jnp
from jax import lax
from jax.experimental import pallas as pl
from jax.experimental.pallas import tpu as pltpu

F32 = jnp.float32
BF16 = jnp.bfloat16

LANES = 128
HEAD_DIM = 64
FOX_HEADS = 8
NSA_HEADS = 8
NSA_GROUPS = 2
NSA_GQA = NSA_HEADS // NSA_GROUPS
SB_HEADS = 16
CMP_BLOCK = 32
CMP_STRIDE = 16
SLC_BLOCK = 64
SLC_TOPK = 8
WINDOW = 256
N_BUCKETS = 32
MAX_DISTANCE = 128
XA_HEADS = 4
XA_HEAD_DIM = 128
EPS = 1e-6
NEG = -1e30
FORCED_SCORE = 1e4
EXP_UNDERFLOW = -104.0

ATT_TILE = 256
ROW_TILE = 512
MLP_ROW_TILE = 1024
MLP_FF_TILE = 1024
GATE_ROWS = 8
F_PIECES = 3


def _dot(a, b):
    return jnp.dot(a, b, preferred_element_type=F32)


def _dot_nt(a, b):
    return lax.dot_general(a, b, (((1,), (1,)), ((), ())), preferred_element_type=F32)


def _split3(x):
    hi = x.astype(BF16)
    r1 = x - hi.astype(F32)
    mid = r1.astype(BF16)
    lo = (r1 - mid.astype(F32)).astype(BF16)
    return hi, mid, lo


def _dot_exact01(x, m01):
    hi, mid, lo = _split3(x)
    return _dot(hi, m01) + _dot(mid, m01) + _dot(lo, m01)


def _dot_01exact(m01, x):
    hi, mid, lo = _split3(x)
    return _dot(m01, hi) + _dot(m01, mid) + _dot(m01, lo)


def _rmsnorm(x, g):
    ms = jnp.mean(x * x, axis=-1, keepdims=True)
    return x * lax.rsqrt(ms + EPS) * g


def _sigmoid(x):
    return 1.0 / (1.0 + jnp.exp(-x))


def _softplus(x):
    return jnp.maximum(x, 0.0) + jnp.log1p(jnp.exp(-jnp.abs(x)))


def _split_pair(q, scale):
    lo = lax.broadcasted_iota(jnp.int32, q.shape, 1) < HEAD_DIM
    qs = q * jnp.asarray(scale, q.dtype)
    zero = jnp.zeros_like(qs)
    return jnp.where(lo, qs, zero), jnp.where(lo, zero, qs)


def _row_lo(shape):
    return lax.broadcasted_iota(jnp.int32, shape, 0) < HEAD_DIM


def _t5_bucket_np(dist):
    dist = np.maximum(dist, 0)
    max_exact = N_BUCKETS // 2
    d_f = np.maximum(dist, 1).astype(np.float64)
    large = max_exact + (np.log(d_f / max_exact) / math.log(MAX_DISTANCE / max_exact)
                         * (N_BUCKETS - max_exact)).astype(np.int32)
    large = np.minimum(large, N_BUCKETS - 1)
    return np.where(dist < max_exact, dist, large).astype(np.int32)


def _params(*sem):
    return pltpu.CompilerParams(dimension_semantics=sem)


def _full(a):
    return pl.BlockSpec(a.shape, lambda *_: (0,) * a.ndim)


def _store_col_blocks(o_ref, val):
    for c in range(o_ref.shape[0]):
        o_ref[c] = val[:, c * LANES:(c + 1) * LANES]


def _pick(a, k):
    return pl.BlockSpec((None,) + a.shape[1:], lambda *_: (k,) + (0,) * (a.ndim - 1))


def _norm_proj_kernel(x_ref, g_ref, w_ref, wt_ref, o_ref, ot_ref, *, tile):
    xn = _rmsnorm(x_ref[...], g_ref[...]).astype(BF16)
    res = _dot(xn, w_ref[...]).astype(o_ref.dtype)
    if len(o_ref.shape) == 3:
        _store_col_blocks(o_ref, res)
    else:
        o_ref[...] = res
    if wt_ref is not None:
        t = _dot_nt(wt_ref[...], xn).astype(ot_ref.dtype)
        for c in range(t.shape[1] // tile):
            ot_ref[c] = t[:, c * tile:(c + 1) * tile]


def norm_proj(x, g, g_layer, w, w_t=None, layer=0, col_blocks=False, tile=ATT_TILE, tm=ROW_TILE):
    n, d = x.shape
    p = w.shape[2]
    tm = min(tm, n)
    in_specs = [pl.BlockSpec((tm, d), lambda i: (i, 0)), _pick(g, g_layer), _pick(w, layer)]
    if col_blocks:
        out_specs = [pl.BlockSpec((p // LANES, tm, LANES), lambda i: (0, i, 0))]
        out_shape = [jax.ShapeDtypeStruct((p // LANES, n, LANES), BF16)]
    else:
        out_specs = [pl.BlockSpec((tm, p), lambda i: (i, 0))]
        out_shape = [jax.ShapeDtypeStruct((n, p), BF16)]
    args = [x, g, w]
    if w_t is None:
        body = lambda x_ref, g_ref, w_ref, o_ref: _norm_proj_kernel(
            x_ref, g_ref, w_ref, None, o_ref, None, tile=tile)
    else:
        pt = w_t.shape[1]
        in_specs.append(_pick(w_t, layer))
        out_specs.append(pl.BlockSpec((tm // tile, pt, tile), lambda i: (i, 0, 0)))
        out_shape.append(jax.ShapeDtypeStruct((n // tile, pt, tile), BF16))
        args.append(w_t)
        body = functools.partial(_norm_proj_kernel, tile=tile)
    out = pl.pallas_call(
        body, grid=(n // tm,), in_specs=in_specs, out_specs=out_specs, out_shape=out_shape,
        compiler_params=_params("parallel"), name="norm_proj",
    )(*args)
    return out if w_t is not None else out[0]


def _even_proj_kernel(x_ref, g_ref, w_ref, wt_ref, wkc_ref, wvc_ref, wff_ref, bf_ref, wng_ref, tri_ref,
                      main_ref, vt_ref, kc_ref, vc_ref, f_ref, gate_ref, carry_ref, chunk_ref,
                      *, tiles_per_seq, tile):
    i = pl.program_id(0)
    xn = _rmsnorm(x_ref[...], g_ref[...]).astype(BF16)
    _store_col_blocks(main_ref, _dot(xn, w_ref[...]).astype(BF16))
    t = _dot_nt(wt_ref[...], xn).astype(BF16)
    for c in range(t.shape[1] // tile):
        vt_ref[c] = t[:, c * tile:(c + 1) * tile]
    chunks = chunk_ref.shape[0] // CMP_STRIDE
    for w_c_ref, out_ref in ((wkc_ref, kc_ref), (wvc_ref, vc_ref)):
        chunk_ref[...] = _dot(xn, w_c_ref[...])
        for pos in range(CMP_STRIDE):
            out_ref[:, pos * LANES:(pos + 1) * LANES] = chunk_ref[
                pl.ds(pos, chunks, stride=CMP_STRIDE), :].astype(BF16)
    gate_ref[...] = _sigmoid(_dot_nt(wng_ref[...], xn))
    ff = _dot(xn, wff_ref[...]) + bf_ref[...]
    logf = jnp.minimum(ff, 0.0) - jnp.log1p(jnp.exp(-jnp.abs(ff)))

    @pl.when(i % tiles_per_seq == 0)
    def _():
        carry_ref[...] = jnp.zeros_like(carry_ref)

    cs = _dot_01exact(tri_ref[...], logf) + carry_ref[0:1, :]
    carry_ref[...] = jnp.broadcast_to(cs[cs.shape[0] - 1:, :], carry_ref.shape)
    hi, mid, lo = _split3(-cs)
    piece = lax.broadcasted_iota(jnp.int32, cs.shape, 1) % F_PIECES
    f_ref[...] = jnp.where(piece == 0, hi, jnp.where(piece == 1, mid, lo))


def even_proj(x, g, layer, w_main, w_t, w_kc, w_vc, w_ff, b_f, w_ng_t, e, tri_l, seq_len,
              tile=ATT_TILE, tm=ROW_TILE):
    n, d = x.shape
    tm = min(tm, seq_len)
    pm = w_main.shape[2]
    pt = w_t.shape[1]
    gr = w_ng_t.shape[1]
    row = lambda width: pl.BlockSpec((tm, width), lambda i: (i, 0))
    chunked = pl.BlockSpec((tm // CMP_STRIDE, CMP_STRIDE * LANES), lambda i: (i, 0))
    chunked_shape = jax.ShapeDtypeStruct((n // CMP_STRIDE, CMP_STRIDE * LANES), BF16)
    return pl.pallas_call(
        functools.partial(_even_proj_kernel, tiles_per_seq=seq_len // tm, tile=tile),
        grid=(n // tm,),
        in_specs=[row(d), _pick(g, layer), _pick(w_main, e), _pick(w_t, e), _pick(w_kc, e), _pick(w_vc, e),
                  _pick(w_ff, e), _pick(b_f, e), _pick(w_ng_t, e), _full(tri_l)],
        out_specs=[pl.BlockSpec((pm // LANES, tm, LANES), lambda i: (0, i, 0)),
                   pl.BlockSpec((tm // tile, pt, tile), lambda i: (i, 0, 0)),
                   chunked, chunked, row(LANES),
                   pl.BlockSpec((gr, tm), lambda i: (0, i))],
        out_shape=[jax.ShapeDtypeStruct((pm // LANES, n, LANES), BF16),
                   jax.ShapeDtypeStruct((n // tile, pt, tile), BF16),
                   chunked_shape, chunked_shape,
                   jax.ShapeDtypeStruct((n, LANES), BF16),
                   jax.ShapeDtypeStruct((gr, n), F32)],
        scratch_shapes=[pltpu.VMEM((8, LANES), F32), pltpu.VMEM((tm, LANES), F32)],
        compiler_params=_params("arbitrary"),
        name="even_proj",
    )(x, g, w_main, w_t, w_kc, w_vc, w_ff, b_f, w_ng_t, tri_l)


def _bucket_lookup_kernel(tab_ref, idx_ref, o_ref):
    h = pl.program_id(0)
    idx = idx_ref[...]
    out = jnp.full(idx.shape, NEG, F32)
    for b in range(N_BUCKETS):
        out = jnp.where(idx == b, tab_ref[b, h], out)
    o_ref[0] = out


def bucket_lookup(tab, idx, rows):
    r, c = idx.shape
    h = tab.shape[1]
    return pl.pallas_call(
        _bucket_lookup_kernel,
        grid=(h, r // rows),
        in_specs=[pl.BlockSpec(memory_space=pltpu.SMEM),
                  pl.BlockSpec((rows, c), lambda hh, j: (j, 0))],
        out_specs=pl.BlockSpec((1, rows, c), lambda hh, j: (hh, j, 0)),
        out_shape=jax.ShapeDtypeStruct((h, r, c), F32),
        compiler_params=_params("parallel", "parallel"),
        name="bucket_lookup",
    )(tab, idx)


def _fox_kernel(q_ref, k_ref, fs_ref, vt_ref, o_ref, stage_ref, *, tile, n_tiles):
    p = pl.program_id(1)
    lane = lax.broadcasted_iota(jnp.int32, (tile, LANES), 1)
    causal = (lax.broadcasted_iota(jnp.int32, (tile, tile), 0)
              <= lax.broadcasted_iota(jnp.int32, (tile, tile), 1))
    lo = _row_lo((LANES, tile))

    def keys(j):
        sl = slice(j * tile, (j + 1) * tile)
        return jnp.concatenate([k_ref[sl, :], fs_ref[sl, :]], axis=1)

    def queries(i):
        heads = _split_pair(q_ref[i * tile:(i + 1) * tile, :], HEAD_DIM ** -0.5)
        return tuple(jnp.concatenate(
            [heads[a], jnp.where(lane // F_PIECES == 2 * p + a, 1.0, 0.0).astype(BF16)], axis=1)
            for a in range(2))

    tasks = [(i, j) for i in range(n_tiles) for j in range(i + 1)]
    qcache = {}

    def scores(t):
        i, j = tasks[t]
        if i not in qcache:
            qcache[i] = queries(i)
        kj = keys(j)
        for a in range(2):
            stage_ref[t % 2, a] = _dot_nt(kj, qcache[i][a])

    scores(0)
    for t, (i, j) in enumerate(tasks):
        if t + 1 < len(tasks):
            scores(t + 1)
        qk = [stage_ref[t % 2, a] for a in range(2)]
        if j == 0:
            stats = [(jnp.full((1, tile), NEG, F32), jnp.zeros((1, tile), F32)) for _ in range(2)]
            acc = [jnp.zeros((LANES, tile), F32) for _ in range(2)]
        vt = vt_ref[j]
        alphas, pvs = [], []
        for a in range(2):
            m_prev, l_prev = stats[a]
            s = jnp.where(causal, qk[a], NEG) if j == i else qk[a]
            m_new = jnp.maximum(m_prev, jnp.max(s, axis=0, keepdims=True))
            alpha = jnp.exp(m_prev - m_new)
            pr = jnp.exp(s - m_new)
            stats[a] = (m_new, alpha * l_prev + jnp.sum(pr, axis=0, keepdims=True))
            alphas.append(alpha)
            pvs.append(_dot(vt, pr.astype(BF16)))
        for a in range(2):
            acc[a] = alphas[a] * acc[a] + pvs[a]
        if j == i:
            o = jnp.where(lo, acc[0] / stats[0][1], acc[1] / stats[1][1])
            o_ref[i * tile:(i + 1) * tile, :] = o.T.astype(o_ref.dtype)


def fox_attention(main, fs, vt, batch, seq_len, q_col, k_col, v_row, n_pairs, tile=ATT_TILE):
    nq = seq_len // tile
    n = batch * seq_len
    return pl.pallas_call(
        functools.partial(_fox_kernel, tile=tile, n_tiles=nq),
        grid=(batch, n_pairs),
        in_specs=[pl.BlockSpec((None, seq_len, LANES), lambda b, p: (q_col + p, b, 0)),
                  pl.BlockSpec((None, seq_len, LANES), lambda b, p: (k_col + p, b, 0)),
                  pl.BlockSpec((seq_len, LANES), lambda b, p: (b, 0)),
                  pl.BlockSpec((nq, LANES, tile), lambda b, p: (b, v_row + p, 0))],
        out_specs=pl.BlockSpec((None, seq_len, LANES), lambda b, p: (p, b, 0)),
        out_shape=jax.ShapeDtypeStruct((n_pairs, n, LANES), BF16),
        scratch_shapes=[pltpu.VMEM((2, 2, tile, tile), F32)],
        compiler_params=_params("parallel", "parallel"),
        name="fox_attention",
    )(main, main, fs, vt)


def _sb_kernel(q_ref, k_ref, vt_ref, tri_ref, o_ref, acc_ref, stage_ref, *, tile, n_tiles):
    tri = tri_ref[...]
    strict = (lax.broadcasted_iota(jnp.int32, (tile, tile), 0)
              < lax.broadcasted_iota(jnp.int32, (tile, tile), 1))
    lo = _row_lo((LANES, tile))

    def queries(i):
        return _split_pair(q_ref[i * tile:(i + 1) * tile, :], HEAD_DIM ** -0.5)

    def scores(ks, heads):
        return tuple(_dot_nt(ks, heads[a]) for a in range(2))

    def consume(zz, rs, vt, masked):
        sps, base, later, pvs = [], [], [], []
        for a in range(2):
            z = zz[a]
            sp = jnp.maximum(z, 0.0) + jnp.log(1.0 + jnp.exp(-jnp.abs(z)))
            base.append(z - sp)
            if masked:
                sp = jnp.where(strict, sp, 0.0)
            sps.append(sp)
        for a in range(2):
            later.append(_dot(tri, sps[a].astype(BF16)))
        for a in range(2):
            wgt = jnp.exp(base[a] - later[a] + rs[a])
            if masked:
                wgt = jnp.where(strict, wgt, 0.0)
            pvs.append(_dot(vt, wgt.astype(BF16)))
        return tuple(rs[a] - jnp.sum(sps[a], axis=0, keepdims=True) for a in range(2)), pvs

    def live(rs):
        return (jnp.max(jnp.maximum(rs[0], rs[1])) > EXP_UNDERFLOW).astype(jnp.int32)

    tasks = [(i, j) for i in range(n_tiles) for j in ((i, i - 1) if i else (i,))]
    qcache = {}

    def task_scores(t):
        i, j = tasks[t]
        if i not in qcache:
            qcache[i] = queries(i)
        zz = scores(k_ref[j * tile:(j + 1) * tile, :], qcache[i])
        for a in range(2):
            stage_ref[t % 2, a] = zz[a]

    zero = jnp.zeros((1, tile), F32)
    survival = {}
    task_scores(0)
    for t, (i, j) in enumerate(tasks):
        if t + 1 < len(tasks):
            task_scores(t + 1)
        zz = [stage_ref[t % 2, a] for a in range(2)]
        if j == i:
            rs, acc = (zero, zero), None
        rs, pvs = consume(zz, rs, vt_ref[j], j == i)
        acc = pvs if acc is None else [acc[a] + pvs[a] for a in range(2)]
        if j == max(i - 1, 0):
            for a in range(2):
                acc_ref[i, a] = acc[a]
            survival[i] = rs

    for i in range(2, n_tiles):
        heads = qcache[i]

        def cond(state):
            j, alive, _ = state
            return (j >= 0) & (alive > 0)

        def body(state, i=i, heads=heads):
            j, _, rs = state
            rs, pvs = consume(scores(k_ref[pl.ds(j * tile, tile), :], heads), rs, vt_ref[j], False)
            for a in range(2):
                acc_ref[i, a] = acc_ref[i, a] + pvs[a]
            return j - 1, live(rs), rs

        lax.while_loop(cond, body, (i - 2, live(survival[i]), survival[i]))

    for i in range(n_tiles):
        o = jnp.where(lo, acc_ref[i, 0], acc_ref[i, 1])
        o_ref[i * tile:(i + 1) * tile, :] = o.T.astype(o_ref.dtype)


def sb_attention(main, vt, batch, seq_len, q_col, k_col, n_pairs, tri, tile=ATT_TILE):
    nq = seq_len // tile
    n = batch * seq_len
    return pl.pallas_call(
        functools.partial(_sb_kernel, tile=tile, n_tiles=nq),
        grid=(batch, n_pairs),
        in_specs=[pl.BlockSpec((None, seq_len, LANES), lambda b, p: (q_col + p, b, 0)),
                  pl.BlockSpec((None, seq_len, LANES), lambda b, p: (k_col + p, b, 0)),
                  pl.BlockSpec((nq, LANES, tile), lambda b, p: (b, p, 0)),
                  _full(tri)],
        out_specs=pl.BlockSpec((None, seq_len, LANES), lambda b, p: (p, b, 0)),
        out_shape=jax.ShapeDtypeStruct((n_pairs, n, LANES), BF16),
        scratch_shapes=[pltpu.VMEM((nq, 2, LANES, tile), F32), pltpu.VMEM((2, 2, tile, tile), F32)],
        compiler_params=_params("parallel", "parallel"),
        name="sb_attention",
    )(main, main, vt, tri)


def _gelu_tanh(x):
    return 0.5 * x * (1.0 + jnp.tanh(math.sqrt(2.0 / math.pi) * (x + 0.044715 * (x * x * x))))


def _compress_kernel(xk_ref, xv_ref, pek_ref, pev_ref, w1k_ref, w1v_ref, w2k_ref, w2v_ref,
                     ok_ref, ov_ref, *, n_cmp):
    def one(x_ref, pe_ref, w1_ref, w2_ref):
        x = x_ref[...].astype(F32)
        xa = (x + pe_ref[0:1, :]).astype(BF16)
        xb = (x + pe_ref[1:2, :]).astype(BF16)
        ha = _dot(xa, w1_ref[0])
        hb = _dot(xb, w1_ref[1])
        rows = ha.shape[0]
        hb = pltpu.roll(hb, rows - 1, 0)
        h = _gelu_tanh(ha + hb)
        out = _dot(h.astype(BF16), w2_ref[...])
        ridx = lax.broadcasted_iota(jnp.int32, out.shape, 0)
        return jnp.where(ridx < n_cmp, out, 0.0)

    ok_ref[0] = one(xk_ref, pek_ref, w1k_ref, w2k_ref).astype(ok_ref.dtype)
    ov_ref[0] = one(xv_ref, pev_ref, w1v_ref, w2v_ref).T.astype(ov_ref.dtype)


def compress(xk, xv, pek, pev, w1k, w1v, w2k, w2v, batch, n_cmp):
    chunks = xk.shape[0] // batch
    row = pl.BlockSpec((chunks, xk.shape[1]), lambda b: (b, 0))
    return pl.pallas_call(
        functools.partial(_compress_kernel, n_cmp=n_cmp),
        grid=(batch,),
        in_specs=[row, row, _full(pek), _full(pev), _full(w1k), _full(w1v), _full(w2k), _full(w2v)],
        out_specs=[pl.BlockSpec((1, chunks, LANES), lambda b: (b, 0, 0)),
                   pl.BlockSpec((1, LANES, chunks), lambda b: (b, 0, 0))],
        out_shape=[jax.ShapeDtypeStruct((batch, chunks, LANES), BF16),
                   jax.ShapeDtypeStruct((batch, LANES, chunks), BF16)],
        compiler_params=_params("parallel"),
        name="nsa_compress",
    )(xk, xv, pek, pev, w1k, w1v, w2k, w2v)


def _nsa_cmp_kernel(q_ref, kc_ref, vct_ref, cb_ref, ov_ref, ocmp_ref, sel_ref, *, tile, n_blocks):
    i = pl.program_id(1)
    kc = kc_ref[0]
    vct = vct_ref[0]
    pcsum = [jnp.zeros((LANES, tile), F32), jnp.zeros((LANES, tile), F32)]
    lo = _row_lo((LANES, tile))
    for r in range(NSA_GQA):
        heads = _split_pair(q_ref[r], HEAD_DIM ** -0.5)
        outs = []
        for a in range(2):
            bias = cb_ref[2 * r + a]
            valid = bias > 0.5 * NEG
            s = _dot_nt(kc, heads[a]) + bias
            m = jnp.max(s, axis=0, keepdims=True)
            pr = jnp.where(valid, jnp.exp(s - m), 0.0)
            l = jnp.sum(pr, axis=0, keepdims=True)
            pc = pr / jnp.where(l > 0.0, l, 1.0)
            pcsum[a] = pcsum[a] + pc
            outs.append(_dot(vct, pc.astype(BF16)))
        ocmp_ref[r * LANES:(r + 1) * LANES, :] = jnp.where(lo, outs[0], outs[1])

    score = _dot_01exact(ov_ref[0], pcsum[0]) + _dot_01exact(ov_ref[1], pcsum[1])
    score = score[:2 * n_blocks]
    shape = (2 * n_blocks, tile)
    row = lax.broadcasted_iota(jnp.int32, shape, 0)
    qpos = i * tile + lax.broadcasted_iota(jnp.int32, shape, 1)
    blk = row % n_blocks
    cur = qpos // SLC_BLOCK
    forced = (blk == 0) | (blk == cur) | (blk == cur - 1)
    future = blk > cur
    score = jnp.where(future, -1.0, jnp.where(forced, FORCED_SCORE, score))
    grp0 = row < n_blocks
    cnt = jnp.zeros(shape, F32)
    for mth in range(n_blocks):
        other = jnp.where(grp0, score[mth:mth + 1, :], score[n_blocks + mth:n_blocks + mth + 1, :])
        ahead = (other > score) | ((other == score) & (blk > mth))
        cnt = cnt + jnp.where(ahead, 1.0, 0.0)
    n_top = min(SLC_TOPK, n_blocks)
    sel_ref[...] = jnp.where(cnt < n_top, 0.0, NEG)


def nsa_cmp_select(main, kc, vct, cmp_bias, overlap2, batch, seq_len, q_col, tile=ATT_TILE):
    nq = seq_len // tile
    n = batch * seq_len
    n_blocks = seq_len // SLC_BLOCK
    qw = NSA_GQA * LANES
    return pl.pallas_call(
        functools.partial(_nsa_cmp_kernel, tile=tile, n_blocks=n_blocks),
        grid=(batch, nq),
        in_specs=[pl.BlockSpec((NSA_GQA, tile, LANES), lambda b, i: (q_col // NSA_GQA, b * nq + i, 0)),
                  pl.BlockSpec((1,) + kc.shape[1:], lambda b, i: (b, 0, 0)),
                  pl.BlockSpec((1,) + vct.shape[1:], lambda b, i: (b, 0, 0)),
                  pl.BlockSpec((NSA_HEADS, cmp_bias.shape[1], tile), lambda b, i: (0, 0, i)),
                  _full(overlap2)],
        out_specs=[pl.BlockSpec((qw, tile), lambda b, i: (0, b * nq + i)),
                   pl.BlockSpec((2 * n_blocks, tile), lambda b, i: (0, b * nq + i))],
        out_shape=[jax.ShapeDtypeStruct((qw, n), F32),
                   jax.ShapeDtypeStruct((2 * n_blocks, n), F32)],
        compiler_params=_params("parallel", "parallel"),
        name="nsa_cmp_select",
    )(main, kc, vct, cmp_bias, overlap2)


def _nsa_main_kernel(far_ref, q_ref, sel_ref, ocmp_ref, gate_ref, ks_ref, vst_ref, kw_ref, vwt_ref, tz_ref,
                     o_ref, stage_ref, *, tile, n_tiles, n_blocks):
    r = pl.program_id(1)
    lo = _row_lo((LANES, tile))
    blocks_per_tile = tile // SLC_BLOCK
    far = tuple(far_ref[2 * r + a] for a in range(2))

    def queries(i):
        return _split_pair(q_ref[i * tile:(i + 1) * tile, :], HEAD_DIM ** -0.5)

    def chosen(a, i, j, offset=None):
        rows = []
        for nb in range(blocks_per_tile):
            blk = a * n_blocks + j * blocks_per_tile + nb
            row = sel_ref[blk:blk + 1, i * tile:(i + 1) * tile]
            if offset is not None:
                row = row + offset
            rows.append(jnp.broadcast_to(row, (SLC_BLOCK, tile)))
        return jnp.concatenate(rows, axis=0)

    tasks = []
    for i in range(n_tiles):
        tasks += [("slc", i, j) for j in range(i + 1)]
        tasks += [("win", i, j) for j in ((i - 1, i) if i else (i,))]
    qcache = {}

    def scores(t):
        kind, i, j = tasks[t]
        if i not in qcache:
            qcache[i] = queries(i)
        k_ref_ = ks_ref if kind == "slc" else kw_ref
        kj = k_ref_[j * tile:(j + 1) * tile, :]
        for a in range(2):
            stage_ref[t % 2, a] = _dot_nt(kj, qcache[i][a])

    def fresh():
        return ([(jnp.full((1, tile), NEG, F32), jnp.zeros((1, tile), F32)) for _ in range(2)],
                [jnp.zeros((LANES, tile), F32) for _ in range(2)])

    scores(0)
    for t, (kind, i, j) in enumerate(tasks):
        if t + 1 < len(tasks):
            scores(t + 1)
        qk = [stage_ref[t % 2, a] for a in range(2)]
        first = j == 0 if kind == "slc" else j == max(i - 1, 0)
        if first:
            stats, acc = fresh()
        vt = (vst_ref if kind == "slc" else vwt_ref)[j]
        alphas, pvs = [], []
        for a in range(2):
            if j == i:
                s = qk[a] + tz_ref[a, 0]
            elif j == i - 1:
                s = qk[a] + tz_ref[a, 1 if kind == "slc" else 2]
            if kind == "slc":
                s = s + chosen(a, i, j) if j >= i - 1 else qk[a] + chosen(a, i, j, far[a])
            m_prev, l_prev = stats[a]
            m_new = jnp.maximum(m_prev, jnp.max(s, axis=0, keepdims=True))
            alpha = jnp.exp(m_prev - m_new)
            pr = jnp.exp(s - m_new)
            stats[a] = (m_new, alpha * l_prev + jnp.sum(pr, axis=0, keepdims=True))
            alphas.append(alpha)
            pvs.append(_dot(vt, pr.astype(BF16)))
        for a in range(2):
            acc[a] = alphas[a] * acc[a] + pvs[a]
        if j == i:
            branch = jnp.where(lo, acc[0] / stats[0][1], acc[1] / stats[1][1])
            if kind == "slc":
                o_slc = branch
            else:
                cols = slice(i * tile, (i + 1) * tile)
                gexp = [jnp.where(lo, gate_ref[2 * c:2 * c + 1, cols], gate_ref[2 * c + 1:2 * c + 2, cols])
                        for c in range(3)]
                out = gexp[0] * ocmp_ref[:, cols] + gexp[1] * o_slc + gexp[2] * branch
                o_ref[cols, :] = out.T.astype(o_ref.dtype)


def nsa_main(far_bias, main, sel, ocmp, gates, vt, tz, batch, seq_len, q_col, ks_col, kw_col, vs_row, vw_row,
             tile=ATT_TILE):
    nq = seq_len // tile
    n = batch * seq_len
    n_blocks = seq_len // SLC_BLOCK
    kv = lambda col: pl.BlockSpec((None, seq_len, LANES), lambda b, r: (col, b, 0))
    vts = lambda rowblk: pl.BlockSpec((nq, LANES, tile), lambda b, r: (b, rowblk, 0))
    return pl.pallas_call(
        functools.partial(_nsa_main_kernel, tile=tile, n_tiles=nq, n_blocks=n_blocks),
        grid=(batch, NSA_GQA),
        in_specs=[pl.BlockSpec(memory_space=pltpu.SMEM),
                  pl.BlockSpec((None, seq_len, LANES), lambda b, r: (q_col + r, b, 0)),
                  pl.BlockSpec((2 * n_blocks, seq_len), lambda b, r: (0, b)),
                  pl.BlockSpec((LANES, seq_len), lambda b, r: (r, b)),
                  pl.BlockSpec((GATE_ROWS, seq_len), lambda b, r: (r, b)),
                  kv(ks_col), vts(vs_row), kv(kw_col), vts(vw_row),
                  pl.BlockSpec((2, 3, tile, tile), lambda b, r: (r, 0, 0, 0))],
        out_specs=pl.BlockSpec((None, seq_len, LANES), lambda b, r: (r, b, 0)),
        out_shape=jax.ShapeDtypeStruct((NSA_GQA, n, LANES), BF16),
        scratch_shapes=[pltpu.VMEM((2, 2, tile, tile), F32)],
        compiler_params=_params("parallel", "parallel"),
        name="nsa_main",
    )(far_bias, main, sel, ocmp, gates, main, vt, main, vt, tz)


def _layer_tail_kernel(*refs, n_in, final_norm):
    a_refs = refs[:n_in]
    w_refs = refs[n_in:2 * n_in]
    (x_ref, gx_ref, wq_ref, kv_ref, wo_ref, gm_ref, w1_ref, w2_ref, gf_ref,
     o_ref, hn_ref, acc_ref) = refs[2 * n_in:]
    f = pl.program_id(1)

    @pl.when(f == 0)
    def _():
        x1 = x_ref[...]
        for a_ref, w_ref in zip(a_refs, w_refs):
            a = jnp.concatenate([a_ref[c] for c in range(a_ref.shape[0])], axis=1)
            x1 = x1 + _dot(a, w_ref[...])
        q = _dot(_rmsnorm(x1, gx_ref[...]).astype(BF16), wq_ref[...]).astype(BF16)
        width = XA_HEADS * XA_HEAD_DIM
        outs = []
        for h in range(XA_HEADS):
            hs = slice(h * XA_HEAD_DIM, (h + 1) * XA_HEAD_DIM)
            s = _dot_nt(q[:, hs], kv_ref[:, hs]) * (XA_HEAD_DIM ** -0.5)
            m = jnp.max(s, axis=-1, keepdims=True)
            p = jnp.exp(s - m)
            l = jnp.sum(p, axis=-1, keepdims=True)
            v = kv_ref[:, width + h * XA_HEAD_DIM:width + (h + 1) * XA_HEAD_DIM]
            outs.append((_dot(p.astype(BF16), v) / l).astype(BF16))
        x2 = x1 + _dot(jnp.concatenate(outs, axis=-1), wo_ref[...])
        hn_ref[...] = _rmsnorm(x2, gm_ref[...]).astype(BF16)
        acc_ref[...] = x2

    h = jnp.maximum(_dot(hn_ref[...], w1_ref[...]), 0.0)
    acc_ref[...] += _dot((h * h).astype(BF16), w2_ref[...])

    @pl.when(f == pl.num_programs(1) - 1)
    def _():
        y = acc_ref[...]
        if final_norm:
            y = _rmsnorm(y, gf_ref[...])
        o_ref[...] = y


def layer_tail(a_list, w_list, w_idx, x, layer, gx, wq, kv, wo, gm, w1, w2, gf, final_norm, seq_len, mem_len,
               tm=MLP_ROW_TILE, tf=MLP_FF_TILE):
    n, d = x.shape
    ff = w1.shape[2]
    tm = min(tm, seq_len)
    tf = min(tf, ff)
    tiles_per_seq = seq_len // tm
    once = pl.Buffered(1)
    row = lambda a: pl.BlockSpec((tm, a.shape[1]), lambda i, f: (i, 0))
    pick = lambda a, k: pl.BlockSpec((None,) + a.shape[1:], lambda i, f: (k,) + (0,) * (a.ndim - 1),
                                     pipeline_mode=once)
    return pl.pallas_call(
        functools.partial(_layer_tail_kernel, n_in=len(a_list), final_norm=final_norm),
        grid=(n // tm, ff // tf),
        in_specs=[pl.BlockSpec((a.shape[0], tm, LANES), lambda i, f: (0, i, 0)) for a in a_list]
        + [pick(w, w_idx) for w in w_list]
        + [row(x), pick(gx, layer), pick(wq, layer),
           pl.BlockSpec((mem_len, 2 * XA_HEADS * XA_HEAD_DIM), lambda i, f: (i // tiles_per_seq, layer)),
           pick(wo, layer), pick(gm, layer),
           pl.BlockSpec((None, d, tf), lambda i, f: (layer, 0, f)),
           pl.BlockSpec((None, tf, d), lambda i, f: (layer, f, 0)),
           pl.BlockSpec(gf.shape, lambda i, f: (0, 0), pipeline_mode=once)],
        out_specs=pl.BlockSpec((tm, d), lambda i, f: (i, 0)),
        out_shape=jax.ShapeDtypeStruct((n, d), F32),
        scratch_shapes=[pltpu.VMEM((tm, d), BF16), pltpu.VMEM((tm, d), F32)],
        compiler_params=_params("parallel", "arbitrary"),
        name="layer_tail",
    )(*a_list, *w_list, x, gx, wq, kv, wo, gm, w1, w2, gf)


def _static_tables(seq_len):
    tile = ATT_TILE
    n_chunks = seq_len // CMP_STRIDE
    n_cmp = (seq_len - CMP_BLOCK) // CMP_STRIDE + 1
    n_slc = seq_len // SLC_BLOCK
    assert n_chunks == LANES and 2 * n_slc <= LANES and seq_len % tile == 0 and WINDOW == tile
    c = np.arange(n_chunks)[:, None]
    t = np.arange(seq_len)[None, :]
    cdist = t - (c * CMP_STRIDE + CMP_BLOCK - 1)
    cmp_idx = np.where((cdist >= 0) & (c < n_cmp), _t5_bucket_np(cdist), -1).astype(np.int32)
    sl = np.arange(tile)[:, None]
    tl = np.arange(tile)[None, :]
    prev = _t5_bucket_np(tl - sl + tile)
    tz_idx = np.concatenate([np.where(sl <= tl, _t5_bucket_np(tl - sl), -1), prev,
                             np.where(sl > tl, prev, -1)], axis=0).astype(np.int32)
    far_bucket = int(_t5_bucket_np(np.array([tile + 1]))[0])
    assert far_bucket == int(_t5_bucket_np(np.array([seq_len]))[0])
    cmp_start = np.arange(n_cmp) * CMP_STRIDE
    cmp_stop = cmp_start + CMP_BLOCK - 1
    slc_start = np.arange(n_slc) * SLC_BLOCK
    slc_stop = slc_start + SLC_BLOCK - 1
    ov = ((cmp_start[:, None] <= slc_stop[None, :]) & (cmp_stop[:, None] >= slc_start[None, :]))
    overlap2 = np.zeros((2, LANES, LANES), np.float32)
    for a in range(2):
        overlap2[a, a * n_slc:(a + 1) * n_slc, :n_cmp] = ov.T
    return cmp_idx, tz_idx, far_bucket, overlap2, n_cmp


def _tri_prefix(nn):
    s = np.arange(nn)[:, None]
    j = np.arange(nn)[None, :]
    return (j <= s).astype(np.float32)


def _tri_later(nn):
    s = np.arange(nn)[:, None]
    j = np.arange(nn)[None, :]
    return (j > s).astype(np.float32)


_NSA_HEAD_PERM = [a * NSA_GQA + r for r in range(NSA_GQA) for a in range(NSA_GROUPS)]


def _block_diag2(m):
    z = jnp.zeros_like(m)
    return jnp.concatenate([jnp.concatenate([m, z], axis=1), jnp.concatenate([z, m], axis=1)], axis=0)


def _compress_weights(pe, w1, w2):
    half = CMP_BLOCK // 2
    pe2 = jnp.concatenate([pe, pe], axis=1).reshape(2, half * LANES)
    w1e = jax.vmap(_block_diag2)(w1).reshape(2, half * LANES, LANES)
    return pe2.astype(F32), w1e.astype(BF16), _block_diag2(w2).astype(BF16)


def kernel(x, mem, rel_bias, mem_norm_g, norm_mix_g, norm_xattn_g, norm_mlp_g, final_norm_g, w_in_even, b_forget, cmp_pe_k, cmp_w1_k, cmp_w2_k, cmp_pe_v, cmp_w1_v, cmp_w2_v, w_out_even, w_in_odd, w_out_odd, xa_wq, xa_wkv, xa_wo, mlp_w1, mlp_w2):
    batch, seq_len, d = x.shape
    mem_len = mem.shape[1]
    depth = norm_mix_g.shape[0]
    n = batch * seq_len
    fox_w = FOX_HEADS * HEAD_DIM
    nsa_w = NSA_HEADS * HEAD_DIM
    kv_w = NSA_GROUPS * HEAD_DIM
    splits = np.cumsum([fox_w, fox_w, fox_w, FOX_HEADS, nsa_w, kv_w, kv_w, kv_w, kv_w, kv_w, kv_w])

    cmp_idx, tz_idx, far_bucket, overlap2, n_cmp = _static_tables(seq_len)
    tab = rel_bias.astype(F32)[:, _NSA_HEAD_PERM]
    cmp_bias = bucket_lookup(tab, jnp.asarray(cmp_idx), rows=cmp_idx.shape[0])
    tz = bucket_lookup(tab, jnp.asarray(tz_idx), rows=ATT_TILE)
    tz = tz.reshape(NSA_HEADS, 3, ATT_TILE, ATT_TILE)
    far_bias = tab[far_bucket]
    overlap2 = jnp.asarray(overlap2, BF16)
    tri_prefix = jnp.asarray(_tri_prefix(min(ROW_TILE, seq_len)), BF16)
    tri_later = jnp.asarray(_tri_later(ATT_TILE), BF16)

    xf = x.reshape(n, d)
    memf = mem.reshape(batch * mem_len, d)
    row = lambda v: v.reshape(1, -1).astype(F32)
    rows = lambda v: v.reshape(v.shape[0], 1, -1).astype(F32)
    g_mix_all, g_xa_all, g_mlp_all = rows(norm_mix_g), rows(norm_xattn_g), rows(norm_mlp_g)
    g_mem, g_final = row(mem_norm_g), row(final_norm_g)

    ne = w_in_even.shape[0]
    (w_fq, w_fk, w_fv, w_ff, w_nq, w_kc, w_vc, w_ks, w_vs, w_kw, w_vw, w_ng) = jnp.split(w_in_even, splits, axis=2)
    w_nq = w_nq.reshape(ne, d, NSA_HEADS, HEAD_DIM)[:, :, _NSA_HEAD_PERM, :].reshape(ne, d, nsa_w)
    w_main_all = jnp.concatenate([w_fq, w_fk, w_nq, w_ks, w_kw], axis=2).astype(BF16)
    w_t_all = jnp.concatenate([w_fv, w_vs, w_vw], axis=2).transpose(0, 2, 1).astype(BF16)
    w_kc_all, w_vc_all = w_kc.astype(BF16), w_vc.astype(BF16)
    w_ng_t = w_ng.reshape(ne, d, NSA_GROUPS, NSA_GQA, 3).transpose(0, 3, 4, 2, 1).reshape(ne, NSA_GQA, 6, d)
    w_ng_all = jnp.pad(w_ng_t, ((0, 0), (0, 0), (0, GATE_ROWS - 6), (0, 0))).reshape(
        ne, NSA_GQA * GATE_ROWS, d).astype(BF16)
    pad_f = LANES - F_PIECES * FOX_HEADS
    w_ff_all = jnp.pad(jnp.repeat(w_ff, F_PIECES, axis=2), ((0, 0), (0, 0), (0, pad_f))).astype(BF16)
    b_f_all = jnp.pad(jnp.repeat(b_forget.astype(F32), F_PIECES, axis=1), ((0, 0), (0, pad_f))).reshape(ne, 1, LANES)
    cmp_k = jax.vmap(_compress_weights)(cmp_pe_k, cmp_w1_k, cmp_w2_k)
    cmp_v = jax.vmap(_compress_weights)(cmp_pe_v, cmp_w1_v, cmp_w2_v)
    w_o_fox_all = w_out_even[:, :fox_w].astype(BF16)
    w_o_nsa_all = w_out_even[:, fox_w:].reshape(ne, NSA_HEADS, HEAD_DIM, d)[:, _NSA_HEAD_PERM].reshape(
        ne, nsa_w, d).astype(BF16)
    sb_w = SB_HEADS * HEAD_DIM
    w_odd_qk_all = w_in_odd[:, :, :2 * sb_w].astype(BF16)
    w_odd_vt_all = w_in_odd[:, :, 2 * sb_w:].transpose(0, 2, 1).astype(BF16)
    w_o_odd_all = w_out_odd.astype(BF16)
    xa_wq_all, xa_wo_all = xa_wq.astype(BF16), xa_wo.astype(BF16)
    mlp_w1_all, mlp_w2_all = mlp_w1.astype(BF16), mlp_w2.astype(BF16)
    xa_wkv_cat = xa_wkv.transpose(1, 0, 2).reshape(d, depth * xa_wkv.shape[2]).astype(BF16)
    kv_mem = norm_proj(memf, g_mem[None], 0, xa_wkv_cat[None])

    for layer in range(depth):
        if layer % 2 == 0:
            e = layer // 2
            main, vt, kc_in, vc_in, fs, gates = even_proj(
                xf, g_mix_all, layer, w_main_all, w_t_all, w_kc_all, w_vc_all, w_ff_all, b_f_all, w_ng_all, e,
                tri_prefix, seq_len)
            nblk = fox_w // LANES
            fox = fox_attention(main, fs, vt, batch, seq_len, 0, nblk, 0, FOX_HEADS // 2)

            kc, vct = compress(kc_in, vc_in,
                               cmp_k[0][e], cmp_v[0][e], cmp_k[1][e], cmp_v[1][e], cmp_k[2][e], cmp_v[2][e],
                               batch, n_cmp)
            q_col = 2 * nblk
            ocmp, sel = nsa_cmp_select(main, kc, vct, cmp_bias, overlap2, batch, seq_len, q_col)
            nsa = nsa_main(far_bias, main, sel, ocmp, gates, vt, tz, batch, seq_len, q_col,
                           q_col + NSA_GQA, q_col + NSA_GQA + 1, nblk, nblk + 1)
            a_list, w_list, w_idx = [fox, nsa], [w_o_fox_all, w_o_nsa_all], e
        else:
            o = layer // 2
            main, vt = norm_proj(xf, g_mix_all, layer, w_odd_qk_all, w_odd_vt_all, o, col_blocks=True)
            nblk = sb_w // LANES
            sb = sb_attention(main, vt, batch, seq_len, 0, nblk, SB_HEADS // 2, tri_later)
            a_list, w_list, w_idx = [sb], [w_o_odd_all], o

        xf = layer_tail(a_list, w_list, w_idx, xf, layer, g_xa_all, xa_wq_all, kv_mem, xa_wo_all, g_mlp_all,
                        mlp_w1_all, mlp_w2_all, g_final, layer == depth - 1, seq_len, mem_len)
    return xf.reshape(batch, seq_len, d)
```

```python
import functools
import math

import numpy as np
import jax
import jax.numpy as jnp
from jax import lax
from jax.experimental import pallas as pl
from jax.experimental.pallas import tpu as pltpu

F32 = jnp.float32
BF16 = jnp.bfloat16

LANES = 128
HEAD_DIM = 64
FOX_HEADS = 8
NSA_HEADS = 8
NSA_GROUPS = 2
NSA_GQA = NSA_HEADS // NSA_GROUPS
SB_HEADS = 16
CMP_BLOCK = 32
CMP_STRIDE = 16
SLC_BLOCK = 64
SLC_TOPK = 8
WINDOW = 256
N_BUCKETS = 32
MAX_DISTANCE = 128
XA_HEADS = 4
XA_HEAD_DIM = 128
EPS = 1e-6
NEG = -1e30
FORCED_SCORE = 1e4
EXP_UNDERFLOW = -104.0
LOG2E = math.log2(math.e)

ATT_TILE = 256
ROW_TILE = 512
MLP_ROW_TILE = 1024
MLP_FF_TILE = 1024
GATE_ROWS = 8
F_PIECES = 3


def _dot(a, b):
    return jnp.dot(a, b, preferred_element_type=F32)


def _dot_nt(a, b):
    return lax.dot_general(a, b, (((1,), (1,)), ((), ())), preferred_element_type=F32)


def _split3(x):
    hi = x.astype(BF16)
    r1 = x - hi.astype(F32)
    mid = r1.astype(BF16)
    lo = (r1 - mid.astype(F32)).astype(BF16)
    return hi, mid, lo


def _dot_exact01(x, m01):
    hi, mid, lo = _split3(x)
    return _dot(hi, m01) + _dot(mid, m01) + _dot(lo, m01)


def _dot_01exact(m01, x):
    hi, mid, lo = _split3(x)
    return _dot(m01, hi) + _dot(m01, mid) + _dot(m01, lo)


def _rmsnorm(x, g):
    ms = jnp.mean(x * x, axis=-1, keepdims=True)
    return x * lax.rsqrt(ms + EPS) * g


def _sigmoid(x):
    return 1.0 / (1.0 + jnp.exp(-x))


def _softplus(x):
    return jnp.maximum(x, 0.0) + jnp.log1p(jnp.exp(-jnp.abs(x)))


def _split_pair(q, scale):
    lo = lax.broadcasted_iota(jnp.int32, q.shape, 1) < HEAD_DIM
    qs = q * jnp.asarray(scale, q.dtype)
    zero = jnp.zeros_like(qs)
    return jnp.where(lo, qs, zero), jnp.where(lo, zero, qs)


def _row_lo(shape):
    return lax.broadcasted_iota(jnp.int32, shape, 0) < HEAD_DIM


def _t5_bucket_np(dist):
    dist = np.maximum(dist, 0)
    max_exact = N_BUCKETS // 2
    d_f = np.maximum(dist, 1).astype(np.float64)
    large = max_exact + (np.log(d_f / max_exact) / math.log(MAX_DISTANCE / max_exact)
                         * (N_BUCKETS - max_exact)).astype(np.int32)
    large = np.minimum(large, N_BUCKETS - 1)
    return np.where(dist < max_exact, dist, large).astype(np.int32)


def _params(*sem):
    return pltpu.CompilerParams(dimension_semantics=sem)


def _full(a):
    return pl.BlockSpec(a.shape, lambda *_: (0,) * a.ndim)


def _store_col_blocks(o_ref, val):
    for c in range(o_ref.shape[0]):
        o_ref[c] = val[:, c * LANES:(c + 1) * LANES]


def _pick(a, k):
    return pl.BlockSpec((None,) + a.shape[1:], lambda *_: (k,) + (0,) * (a.ndim - 1))


def _norm_proj_kernel(x_ref, g_ref, w_ref, wt_ref, o_ref, ot_ref, *, tile):
    xn = _rmsnorm(x_ref[...], g_ref[...]).astype(BF16)
    res = _dot(xn, w_ref[...]).astype(o_ref.dtype)
    if len(o_ref.shape) == 3:
        _store_col_blocks(o_ref, res)
    else:
        o_ref[...] = res
    if wt_ref is not None:
        t = _dot_nt(wt_ref[...], xn).astype(ot_ref.dtype)
        for c in range(t.shape[1] // tile):
            ot_ref[c] = t[:, c * tile:(c + 1) * tile]


def norm_proj(x, g, g_layer, w, w_t=None, layer=0, col_blocks=False, tile=ATT_TILE, tm=ROW_TILE):
    n, d = x.shape
    p = w.shape[2]
    tm = min(tm, n)
    in_specs = [pl.BlockSpec((tm, d), lambda i: (i, 0)), _pick(g, g_layer), _pick(w, layer)]
    if col_blocks:
        out_specs = [pl.BlockSpec((p // LANES, tm, LANES), lambda i: (0, i, 0))]
        out_shape = [jax.ShapeDtypeStruct((p // LANES, n, LANES), BF16)]
    else:
        out_specs = [pl.BlockSpec((tm, p), lambda i: (i, 0))]
        out_shape = [jax.ShapeDtypeStruct((n, p), BF16)]
    args = [x, g, w]
    if w_t is None:
        body = lambda x_ref, g_ref, w_ref, o_ref: _norm_proj_kernel(
            x_ref, g_ref, w_ref, None, o_ref, None, tile=tile)
    else:
        pt = w_t.shape[1]
        in_specs.append(_pick(w_t, layer))
        out_specs.append(pl.BlockSpec((tm // tile, pt, tile), lambda i: (i, 0, 0)))
        out_shape.append(jax.ShapeDtypeStruct((n // tile, pt, tile), BF16))
        args.append(w_t)
        body = functools.partial(_norm_proj_kernel, tile=tile)
    out = pl.pallas_call(
        body, grid=(n // tm,), in_specs=in_specs, out_specs=out_specs, out_shape=out_shape,
        compiler_params=_params("parallel"), name="norm_proj",
    )(*args)
    return out if w_t is not None else out[0]


def _even_proj_kernel(x_ref, g_ref, w_ref, wt_ref, wkc_ref, wvc_ref, wff_ref, bf_ref, wng_ref, tri_ref,
                      main_ref, vt_ref, kc_ref, vc_ref, f_ref, gate_ref, carry_ref, chunk_ref,
                      *, tiles_per_seq, tile):
    i = pl.program_id(0)
    xn = _rmsnorm(x_ref[...], g_ref[...]).astype(BF16)
    _store_col_blocks(main_ref, _dot(xn, w_ref[...]).astype(BF16))
    t = _dot_nt(wt_ref[...], xn).astype(BF16)
    for c in range(t.shape[1] // tile):
        vt_ref[c] = t[:, c * tile:(c + 1) * tile]
    chunks = chunk_ref.shape[0] // CMP_STRIDE
    for w_c_ref, out_ref in ((wkc_ref, kc_ref), (wvc_ref, vc_ref)):
        chunk_ref[...] = _dot(xn, w_c_ref[...])
        for pos in range(CMP_STRIDE):
            out_ref[:, pos * LANES:(pos + 1) * LANES] = chunk_ref[
                pl.ds(pos, chunks, stride=CMP_STRIDE), :].astype(BF16)
    gate_ref[...] = _sigmoid(_dot_nt(wng_ref[...], xn))
    ff = _dot(xn, wff_ref[...]) + bf_ref[...]
    logf = jnp.minimum(ff, 0.0) - jnp.log1p(jnp.exp(-jnp.abs(ff)))

    @pl.when(i % tiles_per_seq == 0)
    def _():
        carry_ref[...] = jnp.zeros_like(carry_ref)

    cs = _dot_01exact(tri_ref[...], logf) + carry_ref[0:1, :]
    carry_ref[...] = jnp.broadcast_to(cs[cs.shape[0] - 1:, :], carry_ref.shape)
    hi, mid, lo = _split3(-LOG2E * cs)
    piece = lax.broadcasted_iota(jnp.int32, cs.shape, 1) % F_PIECES
    f_ref[...] = jnp.where(piece == 0, hi, jnp.where(piece == 1, mid, lo))


def even_proj(x, g, layer, w_main, w_t, w_kc, w_vc, w_ff, b_f, w_ng_t, e, tri_l, seq_len,
              tile=ATT_TILE, tm=ROW_TILE):
    n, d = x.shape
    tm = min(tm, seq_len)
    pm = w_main.shape[2]
    pt = w_t.shape[1]
    gr = w_ng_t.shape[1]
    row = lambda width: pl.BlockSpec((tm, width), lambda i: (i, 0))
    chunked = pl.BlockSpec((tm // CMP_STRIDE, CMP_STRIDE * LANES), lambda i: (i, 0))
    chunked_shape = jax.ShapeDtypeStruct((n // CMP_STRIDE, CMP_STRIDE * LANES), BF16)
    return pl.pallas_call(
        functools.partial(_even_proj_kernel, tiles_per_seq=seq_len // tm, tile=tile),
        grid=(n // tm,),
        in_specs=[row(d), _pick(g, layer), _pick(w_main, e), _pick(w_t, e), _pick(w_kc, e), _pick(w_vc, e),
                  _pick(w_ff, e), _pick(b_f, e), _pick(w_ng_t, e), _full(tri_l)],
        out_specs=[pl.BlockSpec((pm // LANES, tm, LANES), lambda i: (0, i, 0)),
                   pl.BlockSpec((tm // tile, pt, tile), lambda i: (i, 0, 0)),
                   chunked, chunked, row(LANES),
                   pl.BlockSpec((gr, tm), lambda i: (0, i))],
        out_shape=[jax.ShapeDtypeStruct((pm // LANES, n, LANES), BF16),
                   jax.ShapeDtypeStruct((n // tile, pt, tile), BF16),
                   chunked_shape, chunked_shape,
                   jax.ShapeDtypeStruct((n, LANES), BF16),
                   jax.ShapeDtypeStruct((gr, n), F32)],
        scratch_shapes=[pltpu.VMEM((8, LANES), F32), pltpu.VMEM((tm, LANES), F32)],
        compiler_params=_params("arbitrary"),
        name="even_proj",
    )(x, g, w_main, w_t, w_kc, w_vc, w_ff, b_f, w_ng_t, tri_l)


def _bucket_lookup_kernel(tab_ref, idx_ref, o_ref):
    h = pl.program_id(0)
    idx = idx_ref[...]
    out = jnp.full(idx.shape, NEG, F32)
    for b in range(N_BUCKETS):
        out = jnp.where(idx == b, tab_ref[b, h], out)
    o_ref[0] = out


def bucket_lookup(tab, idx, rows):
    r, c = idx.shape
    h = tab.shape[1]
    return pl.pallas_call(
        _bucket_lookup_kernel,
        grid=(h, r // rows),
        in_specs=[pl.BlockSpec(memory_space=pltpu.SMEM),
                  pl.BlockSpec((rows, c), lambda hh, j: (j, 0))],
        out_specs=pl.BlockSpec((1, rows, c), lambda hh, j: (hh, j, 0)),
        out_shape=jax.ShapeDtypeStruct((h, r, c), F32),
        compiler_params=_params("parallel", "parallel"),
        name="bucket_lookup",
    )(tab, idx)


def _fox_kernel(q_ref, k_ref, fs_ref, vt_ref, o_ref, stage_ref, *, tile, n_tiles):
    p = pl.program_id(1)
    lane = lax.broadcasted_iota(jnp.int32, (tile, LANES), 1)
    causal = (lax.broadcasted_iota(jnp.int32, (tile, tile), 0)
              <= lax.broadcasted_iota(jnp.int32, (tile, tile), 1))
    lo = _row_lo((LANES, tile))

    def keys(j):
        sl = slice(j * tile, (j + 1) * tile)
        return jnp.concatenate([k_ref[sl, :], fs_ref[sl, :]], axis=1)

    def queries(i):
        heads = _split_pair(q_ref[i * tile:(i + 1) * tile, :], HEAD_DIM ** -0.5)
        return tuple(jnp.concatenate(
            [heads[a], jnp.where(lane // F_PIECES == 2 * p + a, 1.0, 0.0).astype(BF16)], axis=1)
            for a in range(2))

    tasks = [(i, j) for i in range(n_tiles) for j in range(i + 1)]
    qcache = {}

    def scores(t):
        i, j = tasks[t]
        if i not in qcache:
            qcache[i] = queries(i)
        kj = keys(j)
        for a in range(2):
            stage_ref[t % 2, a] = _dot_nt(kj, qcache[i][a])

    scores(0)
    for t, (i, j) in enumerate(tasks):
        if t + 1 < len(tasks):
            scores(t + 1)
        qk = [stage_ref[t % 2, a] for a in range(2)]
        if j == 0:
            stats = [(jnp.full((1, tile), NEG, F32), jnp.zeros((1, tile), F32)) for _ in range(2)]
            acc = [jnp.zeros((LANES, tile), F32) for _ in range(2)]
        vt = vt_ref[j]
        alphas, pvs = [], []
        for a in range(2):
            m_prev, l_prev = stats[a]
            s = jnp.where(causal, qk[a], NEG) if j == i else qk[a]
            m_new = jnp.maximum(m_prev, jnp.max(s, axis=0, keepdims=True))
            alpha = jnp.exp2(m_prev - m_new)
            pr = jnp.exp2(s - m_new)
            stats[a] = (m_new, alpha * l_prev + jnp.sum(pr, axis=0, keepdims=True))
            alphas.append(alpha)
            pvs.append(_dot(vt, pr.astype(BF16)))
        for a in range(2):
            acc[a] = alphas[a] * acc[a] + pvs[a]
        if j == i:
            o = jnp.where(lo, acc[0] / stats[0][1], acc[1] / stats[1][1])
            o_ref[i * tile:(i + 1) * tile, :] = o.T.astype(o_ref.dtype)


def fox_attention(main, fs, vt, batch, seq_len, q_col, k_col, v_row, n_pairs, tile=ATT_TILE):
    nq = seq_len // tile
    n = batch * seq_len
    return pl.pallas_call(
        functools.partial(_fox_kernel, tile=tile, n_tiles=nq),
        grid=(batch, n_pairs),
        in_specs=[pl.BlockSpec((None, seq_len, LANES), lambda b, p: (q_col + p, b, 0)),
                  pl.BlockSpec((None, seq_len, LANES), lambda b, p: (k_col + p, b, 0)),
                  pl.BlockSpec((seq_len, LANES), lambda b, p: (b, 0)),
                  pl.BlockSpec((nq, LANES, tile), lambda b, p: (b, v_row + p, 0))],
        out_specs=pl.BlockSpec((None, seq_len, LANES), lambda b, p: (p, b, 0)),
        out_shape=jax.ShapeDtypeStruct((n_pairs, n, LANES), BF16),
        scratch_shapes=[pltpu.VMEM((2, 2, tile, tile), F32)],
        compiler_params=_params("parallel", "parallel"),
        name="fox_attention",
    )(main, main, fs, vt)


def _sb_kernel(q_ref, k_ref, vt_ref, tri_ref, o_ref, acc_ref, stage_ref, *, tile, n_tiles):
    tri = tri_ref[...]
    strict = (lax.broadcasted_iota(jnp.int32, (tile, tile), 0)
              < lax.broadcasted_iota(jnp.int32, (tile, tile), 1))
    lo = _row_lo((LANES, tile))

    def queries(i):
        return _split_pair(q_ref[i * tile:(i + 1) * tile, :], HEAD_DIM ** -0.5)

    def scores(ks, heads):
        return tuple(_dot_nt(ks, heads[a]) for a in range(2))

    def consume(zz, rs, vt, masked):
        sps, base, later, pvs = [], [], [], []
        for a in range(2):
            z = zz[a]
            sp = jnp.maximum(z, 0.0) + jnp.log(1.0 + jnp.exp(-jnp.abs(z)))
            base.append(z - sp)
            if masked:
                sp = jnp.where(strict, sp, 0.0)
            sps.append(sp)
        for a in range(2):
            later.append(_dot(tri, sps[a].astype(BF16)))
        for a in range(2):
            wgt = jnp.exp(base[a] - later[a] + rs[a])
            if masked:
                wgt = jnp.where(strict, wgt, 0.0)
            pvs.append(_dot(vt, wgt.astype(BF16)))
        return tuple(rs[a] - jnp.sum(sps[a], axis=0, keepdims=True) for a in range(2)), pvs

    def live(rs):
        return (jnp.max(jnp.maximum(rs[0], rs[1])) > EXP_UNDERFLOW).astype(jnp.int32)

    tasks = [(i, j) for i in range(n_tiles) for j in ((i, i - 1) if i else (i,))]
    qcache = {}

    def task_scores(t):
        i, j = tasks[t]
        if i not in qcache:
            qcache[i] = queries(i)
        zz = scores(k_ref[j * tile:(j + 1) * tile, :], qcache[i])
        for a in range(2):
            stage_ref[t % 2, a] = zz[a]

    zero = jnp.zeros((1, tile), F32)
    survival = {}
    task_scores(0)
    for t, (i, j) in enumerate(tasks):
        if t + 1 < len(tasks):
            task_scores(t + 1)
        zz = [stage_ref[t % 2, a] for a in range(2)]
        if j == i:
            rs, acc = (zero, zero), None
        rs, pvs = consume(zz, rs, vt_ref[j], j == i)
        acc = pvs if acc is None else [acc[a] + pvs[a] for a in range(2)]
        if j == max(i - 1, 0):
            for a in range(2):
                acc_ref[i, a] = acc[a]
            survival[i] = rs

    for i in range(2, n_tiles):
        heads = qcache[i]

        def cond(state):
            j, alive, _ = state
            return (j >= 0) & (alive > 0)

        def body(state, i=i, heads=heads):
            j, _, rs = state
            rs, pvs = consume(scores(k_ref[pl.ds(j * tile, tile), :], heads), rs, vt_ref[j], False)
            for a in range(2):
                acc_ref[i, a] = acc_ref[i, a] + pvs[a]
            return j - 1, live(rs), rs

        lax.while_loop(cond, body, (i - 2, live(survival[i]), survival[i]))

    for i in range(n_tiles):
        o = jnp.where(lo, acc_ref[i, 0], acc_ref[i, 1])
        o_ref[i * tile:(i + 1) * tile, :] = o.T.astype(o_ref.dtype)


def sb_attention(main, vt, batch, seq_len, q_col, k_col, n_pairs, tri, tile=ATT_TILE):
    nq = seq_len // tile
    n = batch * seq_len
    return pl.pallas_call(
        functools.partial(_sb_kernel, tile=tile, n_tiles=nq),
        grid=(batch, n_pairs),
        in_specs=[pl.BlockSpec((None, seq_len, LANES), lambda b, p: (q_col + p, b, 0)),
                  pl.BlockSpec((None, seq_len, LANES), lambda b, p: (k_col + p, b, 0)),
                  pl.BlockSpec((nq, LANES, tile), lambda b, p: (b, p, 0)),
                  _full(tri)],
        out_specs=pl.BlockSpec((None, seq_len, LANES), lambda b, p: (p, b, 0)),
        out_shape=jax.ShapeDtypeStruct((n_pairs, n, LANES), BF16),
        scratch_shapes=[pltpu.VMEM((nq, 2, LANES, tile), F32), pltpu.VMEM((2, 2, tile, tile), F32)],
        compiler_params=_params("parallel", "parallel"),
        name="sb_attention",
    )(main, main, vt, tri)


def _gelu_tanh(x):
    return 0.5 * x * (1.0 + jnp.tanh(math.sqrt(2.0 / math.pi) * (x + 0.044715 * (x * x * x))))


def _compress_kernel(xk_ref, xv_ref, pek_ref, pev_ref, w1k_ref, w1v_ref, w2k_ref, w2v_ref,
                     ok_ref, ov_ref, *, n_cmp):
    def one(x_ref, pe_ref, w1_ref, w2_ref):
        x = x_ref[...].astype(F32)
        xa = (x + pe_ref[0:1, :]).astype(BF16)
        xb = (x + pe_ref[1:2, :]).astype(BF16)
        ha = _dot(xa, w1_ref[0])
        hb = _dot(xb, w1_ref[1])
        rows = ha.shape[0]
        hb = pltpu.roll(hb, rows - 1, 0)
        h = _gelu_tanh(ha + hb)
        out = _dot(h.astype(BF16), w2_ref[...])
        ridx = lax.broadcasted_iota(jnp.int32, out.shape, 0)
        return jnp.where(ridx < n_cmp, out, 0.0)

    ok_ref[0] = one(xk_ref, pek_ref, w1k_ref, w2k_ref).astype(ok_ref.dtype)
    ov_ref[0] = one(xv_ref, pev_ref, w1v_ref, w2v_ref).T.astype(ov_ref.dtype)


def compress(xk, xv, pek, pev, w1k, w1v, w2k, w2v, batch, n_cmp):
    chunks = xk.shape[0] // batch
    row = pl.BlockSpec((chunks, xk.shape[1]), lambda b: (b, 0))
    return pl.pallas_call(
        functools.partial(_compress_kernel, n_cmp=n_cmp),
        grid=(batch,),
        in_specs=[row, row, _full(pek), _full(pev), _full(w1k), _full(w1v), _full(w2k), _full(w2v)],
        out_specs=[pl.BlockSpec((1, chunks, LANES), lambda b: (b, 0, 0)),
                   pl.BlockSpec((1, LANES, chunks), lambda b: (b, 0, 0))],
        out_shape=[jax.ShapeDtypeStruct((batch, chunks, LANES), BF16),
                   jax.ShapeDtypeStruct((batch, LANES, chunks), BF16)],
        compiler_params=_params("parallel"),
        name="nsa_compress",
    )(xk, xv, pek, pev, w1k, w1v, w2k, w2v)


def _nsa_cmp_kernel(q_ref, kc_ref, vct_ref, cb_ref, ov_ref, ocmp_ref, sel_ref, *, tile, n_blocks):
    i = pl.program_id(1)
    kc = kc_ref[0]
    vct = vct_ref[0]
    pcsum = [jnp.zeros((LANES, tile), F32), jnp.zeros((LANES, tile), F32)]
    lo = _row_lo((LANES, tile))
    for r in range(NSA_GQA):
        heads = _split_pair(q_ref[r], HEAD_DIM ** -0.5)
        outs = []
        for a in range(2):
            bias = cb_ref[2 * r + a]
            valid = bias > 0.5 * NEG
            s = _dot_nt(kc, heads[a]) + bias
            m = jnp.max(s, axis=0, keepdims=True)
            pr = jnp.where(valid, jnp.exp2(s - m), 0.0)
            l = jnp.sum(pr, axis=0, keepdims=True)
            pc = pr / jnp.where(l > 0.0, l, 1.0)
            pcsum[a] = pcsum[a] + pc
            outs.append(_dot(vct, pc.astype(BF16)))
        ocmp_ref[r * LANES:(r + 1) * LANES, :] = jnp.where(lo, outs[0], outs[1])

    score = _dot_01exact(ov_ref[0], pcsum[0]) + _dot_01exact(ov_ref[1], pcsum[1])
    score = score[:2 * n_blocks]
    shape = (2 * n_blocks, tile)
    row = lax.broadcasted_iota(jnp.int32, shape, 0)
    qpos = i * tile + lax.broadcasted_iota(jnp.int32, shape, 1)
    blk = row % n_blocks
    cur = qpos // SLC_BLOCK
    forced = (blk == 0) | (blk == cur) | (blk == cur - 1)
    future = blk > cur
    score = jnp.where(future, -1.0, jnp.where(forced, FORCED_SCORE, score))
    grp0 = row < n_blocks
    cnt = jnp.zeros(shape, F32)
    for mth in range(n_blocks):
        other = jnp.where(grp0, score[mth:mth + 1, :], score[n_blocks + mth:n_blocks + mth + 1, :])
        ahead = (other > score) | ((other == score) & (blk > mth))
        cnt = cnt + jnp.where(ahead, 1.0, 0.0)
    n_top = min(SLC_TOPK, n_blocks)
    sel_ref[...] = jnp.where(cnt < n_top, 0.0, NEG)


def nsa_cmp_select(main, kc, vct, cmp_bias, overlap2, batch, seq_len, q_col, tile=ATT_TILE):
    nq = seq_len // tile
    n = batch * seq_len
    n_blocks = seq_len // SLC_BLOCK
    qw = NSA_GQA * LANES
    return pl.pallas_call(
        functools.partial(_nsa_cmp_kernel, tile=tile, n_blocks=n_blocks),
        grid=(batch, nq),
        in_specs=[pl.BlockSpec((NSA_GQA, tile, LANES), lambda b, i: (q_col // NSA_GQA, b * nq + i, 0)),
                  pl.BlockSpec((1,) + kc.shape[1:], lambda b, i: (b, 0, 0)),
                  pl.BlockSpec((1,) + vct.shape[1:], lambda b, i: (b, 0, 0)),
                  pl.BlockSpec((NSA_HEADS, cmp_bias.shape[1], tile), lambda b, i: (0, 0, i)),
                  _full(overlap2)],
        out_specs=[pl.BlockSpec((qw, tile), lambda b, i: (0, b * nq + i)),
                   pl.BlockSpec((2 * n_blocks, tile), lambda b, i: (0, b * nq + i))],
        out_shape=[jax.ShapeDtypeStruct((qw, n), F32),
                   jax.ShapeDtypeStruct((2 * n_blocks, n), F32)],
        compiler_params=_params("parallel", "parallel"),
        name="nsa_cmp_select",
    )(main, kc, vct, cmp_bias, overlap2)


def _nsa_main_kernel(far_ref, q_ref, sel_ref, ocmp_ref, gate_ref, ks_ref, vst_ref, kw_ref, vwt_ref, tz_ref,
                     o_ref, stage_ref, *, tile, n_tiles, n_blocks):
    r = pl.program_id(1)
    lo = _row_lo((LANES, tile))
    blocks_per_tile = tile // SLC_BLOCK
    far = tuple(far_ref[2 * r + a] for a in range(2))

    def queries(i):
        return _split_pair(q_ref[i * tile:(i + 1) * tile, :], HEAD_DIM ** -0.5)

    def chosen(a, i, j, offset=None):
        rows = []
        for nb in range(blocks_per_tile):
            blk = a * n_blocks + j * blocks_per_tile + nb
            row = sel_ref[blk:blk + 1, i * tile:(i + 1) * tile]
            if offset is not None:
                row = row + offset
            rows.append(jnp.broadcast_to(row, (SLC_BLOCK, tile)))
        return jnp.concatenate(rows, axis=0)

    tasks = []
    for i in range(n_tiles):
        tasks += [("slc", i, j) for j in range(i + 1)]
        tasks += [("win", i, j) for j in ((i - 1, i) if i else (i,))]
    qcache = {}

    def scores(t):
        kind, i, j = tasks[t]
        if i not in qcache:
            qcache[i] = queries(i)
        k_ref_ = ks_ref if kind == "slc" else kw_ref
        kj = k_ref_[j * tile:(j + 1) * tile, :]
        for a in range(2):
            stage_ref[t % 2, a] = _dot_nt(kj, qcache[i][a])

    def fresh():
        return ([(jnp.full((1, tile), NEG, F32), jnp.zeros((1, tile), F32)) for _ in range(2)],
                [jnp.zeros((LANES, tile), F32) for _ in range(2)])

    scores(0)
    for t, (kind, i, j) in enumerate(tasks):
        if t + 1 < len(tasks):
            scores(t + 1)
        qk = [stage_ref[t % 2, a] for a in range(2)]
        first = j == 0 if kind == "slc" else j == max(i - 1, 0)
        if first:
            stats, acc = fresh()
        vt = (vst_ref if kind == "slc" else vwt_ref)[j]
        alphas, pvs = [], []
        for a in range(2):
            if j == i:
                s = qk[a] + tz_ref[a, 0]
            elif j == i - 1:
                s = qk[a] + tz_ref[a, 1 if kind == "slc" else 2]
            if kind == "slc":
                s = s + chosen(a, i, j) if j >= i - 1 else qk[a] + chosen(a, i, j, far[a])
            m_prev, l_prev = stats[a]
            m_new = jnp.maximum(m_prev, jnp.max(s, axis=0, keepdims=True))
            alpha = jnp.exp2(m_prev - m_new)
            pr = jnp.exp2(s - m_new)
            stats[a] = (m_new, alpha * l_prev + jnp.sum(pr, axis=0, keepdims=True))
            alphas.append(alpha)
            pvs.append(_dot(vt, pr.astype(BF16)))
        for a in range(2):
            acc[a] = alphas[a] * acc[a] + pvs[a]
        if j == i:
            branch = jnp.where(lo, acc[0] / stats[0][1], acc[1] / stats[1][1])
            if kind == "slc":
                o_slc = branch
            else:
                cols = slice(i * tile, (i + 1) * tile)
                gexp = [jnp.where(lo, gate_ref[2 * c:2 * c + 1, cols], gate_ref[2 * c + 1:2 * c + 2, cols])
                        for c in range(3)]
                out = gexp[0] * ocmp_ref[:, cols] + gexp[1] * o_slc + gexp[2] * branch
                o_ref[cols, :] = out.T.astype(o_ref.dtype)


def nsa_main(far_bias, main, sel, ocmp, gates, vt, tz, batch, seq_len, q_col, ks_col, kw_col, vs_row, vw_row,
             tile=ATT_TILE):
    nq = seq_len // tile
    n = batch * seq_len
    n_blocks = seq_len // SLC_BLOCK
    kv = lambda col: pl.BlockSpec((None, seq_len, LANES), lambda b, r: (col, b, 0))
    vts = lambda rowblk: pl.BlockSpec((nq, LANES, tile), lambda b, r: (b, rowblk, 0))
    return pl.pallas_call(
        functools.partial(_nsa_main_kernel, tile=tile, n_tiles=nq, n_blocks=n_blocks),
        grid=(batch, NSA_GQA),
        in_specs=[pl.BlockSpec(memory_space=pltpu.SMEM),
                  pl.BlockSpec((None, seq_len, LANES), lambda b, r: (q_col + r, b, 0)),
                  pl.BlockSpec((2 * n_blocks, seq_len), lambda b, r: (0, b)),
                  pl.BlockSpec((LANES, seq_len), lambda b, r: (r, b)),
                  pl.BlockSpec((GATE_ROWS, seq_len), lambda b, r: (r, b)),
                  kv(ks_col), vts(vs_row), kv(kw_col), vts(vw_row),
                  pl.BlockSpec((2, 3, tile, tile), lambda b, r: (r, 0, 0, 0))],
        out_specs=pl.BlockSpec((None, seq_len, LANES), lambda b, r: (r, b, 0)),
        out_shape=jax.ShapeDtypeStruct((NSA_GQA, n, LANES), BF16),
        scratch_shapes=[pltpu.VMEM((2, 2, tile, tile), F32)],
        compiler_params=_params("parallel", "parallel"),
        name="nsa_main",
    )(far_bias, main, sel, ocmp, gates, main, vt, main, vt, tz)


def _layer_tail_kernel(*refs, n_in, final_norm):
    a_refs = refs[:n_in]
    w_refs = refs[n_in:2 * n_in]
    (x_ref, gx_ref, wq_ref, kv_ref, wo_ref, gm_ref, w1_ref, w2_ref, gf_ref,
     o_ref, hn_ref, acc_ref) = refs[2 * n_in:]
    f = pl.program_id(1)

    @pl.when(f == 0)
    def _():
        x1 = x_ref[...]
        for a_ref, w_ref in zip(a_refs, w_refs):
            a = jnp.concatenate([a_ref[c] for c in range(a_ref.shape[0])], axis=1)
            x1 = x1 + _dot(a, w_ref[...])
        q = _dot(_rmsnorm(x1, gx_ref[...]).astype(BF16), wq_ref[...]).astype(BF16)
        width = XA_HEADS * XA_HEAD_DIM
        outs = []
        for h in range(XA_HEADS):
            hs = slice(h * XA_HEAD_DIM, (h + 1) * XA_HEAD_DIM)
            s = _dot_nt(q[:, hs], kv_ref[:, hs]) * (XA_HEAD_DIM ** -0.5)
            m = jnp.max(s, axis=-1, keepdims=True)
            p = jnp.exp(s - m)
            l = jnp.sum(p, axis=-1, keepdims=True)
            v = kv_ref[:, width + h * XA_HEAD_DIM:width + (h + 1) * XA_HEAD_DIM]
            outs.append((_dot(p.astype(BF16), v) / l).astype(BF16))
        x2 = x1 + _dot(jnp.concatenate(outs, axis=-1), wo_ref[...])
        hn_ref[...] = _rmsnorm(x2, gm_ref[...]).astype(BF16)
        acc_ref[...] = x2

    h = jnp.maximum(_dot(hn_ref[...], w1_ref[...]), 0.0)
    acc_ref[...] += _dot((h * h).astype(BF16), w2_ref[...])

    @pl.when(f == pl.num_programs(1) - 1)
    def _():
        y = acc_ref[...]
        if final_norm:
            y = _rmsnorm(y, gf_ref[...])
        o_ref[...] = y


def layer_tail(a_list, w_list, w_idx, x, layer, gx, wq, kv, wo, gm, w1, w2, gf, final_norm, seq_len, mem_len,
               tm=MLP_ROW_TILE, tf=MLP_FF_TILE):
    n, d = x.shape
    ff = w1.shape[2]
    tm = min(tm, seq_len)
    tf = min(tf, ff)
    tiles_per_seq = seq_len // tm
    once = pl.Buffered(1)
    row = lambda a: pl.BlockSpec((tm, a.shape[1]), lambda i, f: (i, 0))
    pick = lambda a, k: pl.BlockSpec((None,) + a.shape[1:], lambda i, f: (k,) + (0,) * (a.ndim - 1),
                                     pipeline_mode=once)
    return pl.pallas_call(
        functools.partial(_layer_tail_kernel, n_in=len(a_list), final_norm=final_norm),
        grid=(n // tm, ff // tf),
        in_specs=[pl.BlockSpec((a.shape[0], tm, LANES), lambda i, f: (0, i, 0)) for a in a_list]
        + [pick(w, w_idx) for w in w_list]
        + [row(x), pick(gx, layer), pick(wq, layer),
           pl.BlockSpec((mem_len, 2 * XA_HEADS * XA_HEAD_DIM), lambda i, f: (i // tiles_per_seq, layer)),
           pick(wo, layer), pick(gm, layer),
           pl.BlockSpec((None, d, tf), lambda i, f: (layer, 0, f)),
           pl.BlockSpec((None, tf, d), lambda i, f: (layer, f, 0)),
           pl.BlockSpec(gf.shape, lambda i, f: (0, 0), pipeline_mode=once)],
        out_specs=pl.BlockSpec((tm, d), lambda i, f: (i, 0)),
        out_shape=jax.ShapeDtypeStruct((n, d), F32),
        scratch_shapes=[pltpu.VMEM((tm, d), BF16), pltpu.VMEM((tm, d), F32)],
        compiler_params=_params("parallel", "arbitrary"),
        name="layer_tail",
    )(*a_list, *w_list, x, gx, wq, kv, wo, gm, w1, w2, gf)


def _static_tables(seq_len):
    tile = ATT_TILE
    n_chunks = seq_len // CMP_STRIDE
    n_cmp = (seq_len - CMP_BLOCK) // CMP_STRIDE + 1
    n_slc = seq_len // SLC_BLOCK
    assert n_chunks == LANES and 2 * n_slc <= LANES and seq_len % tile == 0 and WINDOW == tile
    c = np.arange(n_chunks)[:, None]
    t = np.arange(seq_len)[None, :]
    cdist = t - (c * CMP_STRIDE + CMP_BLOCK - 1)
    cmp_idx = np.where((cdist >= 0) & (c < n_cmp), _t5_bucket_np(cdist), -1).astype(np.int32)
    sl = np.arange(tile)[:, None]
    tl = np.arange(tile)[None, :]
    prev = _t5_bucket_np(tl - sl + tile)
    tz_idx = np.concatenate([np.where(sl <= tl, _t5_bucket_np(tl - sl), -1), prev,
                             np.where(sl > tl, prev, -1)], axis=0).astype(np.int32)
    far_bucket = int(_t5_bucket_np(np.array([tile + 1]))[0])
    assert far_bucket == int(_t5_bucket_np(np.array([seq_len]))[0])
    cmp_start = np.arange(n_cmp) * CMP_STRIDE
    cmp_stop = cmp_start + CMP_BLOCK - 1
    slc_start = np.arange(n_slc) * SLC_BLOCK
    slc_stop = slc_start + SLC_BLOCK - 1
    ov = ((cmp_start[:, None] <= slc_stop[None, :]) & (cmp_stop[:, None] >= slc_start[None, :]))
    overlap2 = np.zeros((2, LANES, LANES), np.float32)
    for a in range(2):
        overlap2[a, a * n_slc:(a + 1) * n_slc, :n_cmp] = ov.T
    return cmp_idx, tz_idx, far_bucket, overlap2, n_cmp


def _tri_prefix(nn):
    s = np.arange(nn)[:, None]
    j = np.arange(nn)[None, :]
    return (j <= s).astype(np.float32)


def _tri_later(nn):
    s = np.arange(nn)[:, None]
    j = np.arange(nn)[None, :]
    return (j > s).astype(np.float32)


_NSA_HEAD_PERM = [a * NSA_GQA + r for r in range(NSA_GQA) for a in range(NSA_GROUPS)]


def _block_diag2(m):
    z = jnp.zeros_like(m)
    return jnp.concatenate([jnp.concatenate([m, z], axis=1), jnp.concatenate([z, m], axis=1)], axis=0)


def _compress_weights(pe, w1, w2):
    half = CMP_BLOCK // 2
    pe2 = jnp.concatenate([pe, pe], axis=1).reshape(2, half * LANES)
    w1e = jax.vmap(_block_diag2)(w1).reshape(2, half * LANES, LANES)
    return pe2.astype(F32), w1e.astype(BF16), _block_diag2(w2).astype(BF16)


def kernel(x, mem, rel_bias, mem_norm_g, norm_mix_g, norm_xattn_g, norm_mlp_g, final_norm_g, w_in_even, b_forget, cmp_pe_k, cmp_w1_k, cmp_w2_k, cmp_pe_v, cmp_w1_v, cmp_w2_v, w_out_even, w_in_odd, w_out_odd, xa_wq, xa_wkv, xa_wo, mlp_w1, mlp_w2):
    batch, seq_len, d = x.shape
    mem_len = mem.shape[1]
    depth = norm_mix_g.shape[0]
    n = batch * seq_len
    fox_w = FOX_HEADS * HEAD_DIM
    nsa_w = NSA_HEADS * HEAD_DIM
    kv_w = NSA_GROUPS * HEAD_DIM
    splits = np.cumsum([fox_w, fox_w, fox_w, FOX_HEADS, nsa_w, kv_w, kv_w, kv_w, kv_w, kv_w, kv_w])

    cmp_idx, tz_idx, far_bucket, overlap2, n_cmp = _static_tables(seq_len)
    tab = LOG2E * rel_bias.astype(F32)[:, _NSA_HEAD_PERM]
    cmp_bias = bucket_lookup(tab, jnp.asarray(cmp_idx), rows=cmp_idx.shape[0])
    tz = bucket_lookup(tab, jnp.asarray(tz_idx), rows=ATT_TILE)
    tz = tz.reshape(NSA_HEADS, 3, ATT_TILE, ATT_TILE)
    far_bias = tab[far_bucket]
    overlap2 = jnp.asarray(overlap2, BF16)
    tri_prefix = jnp.asarray(_tri_prefix(min(ROW_TILE, seq_len)), BF16)
    tri_later = jnp.asarray(_tri_later(ATT_TILE), BF16)

    xf = x.reshape(n, d)
    memf = mem.reshape(batch * mem_len, d)
    row = lambda v: v.reshape(1, -1).astype(F32)
    rows = lambda v: v.reshape(v.shape[0], 1, -1).astype(F32)
    g_mix_all, g_xa_all, g_mlp_all = rows(norm_mix_g), rows(norm_xattn_g), rows(norm_mlp_g)
    g_mem, g_final = row(mem_norm_g), row(final_norm_g)

    ne = w_in_even.shape[0]
    (w_fq, w_fk, w_fv, w_ff, w_nq, w_kc, w_vc, w_ks, w_vs, w_kw, w_vw, w_ng) = jnp.split(w_in_even, splits, axis=2)
    w_fq = LOG2E * w_fq
    w_nq = LOG2E * w_nq.reshape(ne, d, NSA_HEADS, HEAD_DIM)[:, :, _NSA_HEAD_PERM, :].reshape(ne, d, nsa_w)
    w_main_all = jnp.concatenate([w_fq, w_fk, w_nq, w_ks, w_kw], axis=2).astype(BF16)
    w_t_all = jnp.concatenate([w_fv, w_vs, w_vw], axis=2).transpose(0, 2, 1).astype(BF16)
    w_kc_all, w_vc_all = w_kc.astype(BF16), w_vc.astype(BF16)
    w_ng_t = w_ng.reshape(ne, d, NSA_GROUPS, NSA_GQA, 3).transpose(0, 3, 4, 2, 1).reshape(ne, NSA_GQA, 6, d)
    w_ng_all = jnp.pad(w_ng_t, ((0, 0), (0, 0), (0, GATE_ROWS - 6), (0, 0))).reshape(
        ne, NSA_GQA * GATE_ROWS, d).astype(BF16)
    pad_f = LANES - F_PIECES * FOX_HEADS
    w_ff_all = jnp.pad(jnp.repeat(w_ff, F_PIECES, axis=2), ((0, 0), (0, 0), (0, pad_f))).astype(BF16)
    b_f_all = jnp.pad(jnp.repeat(b_forget.astype(F32), F_PIECES, axis=1), ((0, 0), (0, pad_f))).reshape(ne, 1, LANES)
    cmp_k = jax.vmap(_compress_weights)(cmp_pe_k, cmp_w1_k, cmp_w2_k)
    cmp_v = jax.vmap(_compress_weights)(cmp_pe_v, cmp_w1_v, cmp_w2_v)
    w_o_fox_all = w_out_even[:, :fox_w].astype(BF16)
    w_o_nsa_all = w_out_even[:, fox_w:].reshape(ne, NSA_HEADS, HEAD_DIM, d)[:, _NSA_HEAD_PERM].reshape(
        ne, nsa_w, d).astype(BF16)
    sb_w = SB_HEADS * HEAD_DIM
    w_odd_qk_all = w_in_odd[:, :, :2 * sb_w].astype(BF16)
    w_odd_vt_all = w_in_odd[:, :, 2 * sb_w:].transpose(0, 2, 1).astype(BF16)
    w_o_odd_all = w_out_odd.astype(BF16)
    xa_wq_all, xa_wo_all = xa_wq.astype(BF16), xa_wo.astype(BF16)
    mlp_w1_all, mlp_w2_all = mlp_w1.astype(BF16), mlp_w2.astype(BF16)
    xa_wkv_cat = xa_wkv.transpose(1, 0, 2).reshape(d, depth * xa_wkv.shape[2]).astype(BF16)
    kv_mem = norm_proj(memf, g_mem[None], 0, xa_wkv_cat[None])

    for layer in range(depth):
        if layer % 2 == 0:
            e = layer // 2
            main, vt, kc_in, vc_in, fs, gates = even_proj(
                xf, g_mix_all, layer, w_main_all, w_t_all, w_kc_all, w_vc_all, w_ff_all, b_f_all, w_ng_all, e,
                tri_prefix, seq_len)
            nblk = fox_w // LANES
            fox = fox_attention(main, fs, vt, batch, seq_len, 0, nblk, 0, FOX_HEADS // 2)

            kc, vct = compress(kc_in, vc_in,
                               cmp_k[0][e], cmp_v[0][e], cmp_k[1][e], cmp_v[1][e], cmp_k[2][e], cmp_v[2][e],
                               batch, n_cmp)
            q_col = 2 * nblk
            ocmp, sel = nsa_cmp_select(main, kc, vct, cmp_bias, overlap2, batch, seq_len, q_col)
            nsa = nsa_main(far_bias, main, sel, ocmp, gates, vt, tz, batch, seq_len, q_col,
                           q_col + NSA_GQA, q_col + NSA_GQA + 1, nblk, nblk + 1)
            a_list, w_list, w_idx = [fox, nsa], [w_o_fox_all, w_o_nsa_all], e
        else:
            o = layer // 2
            main, vt = norm_proj(xf, g_mix_all, layer, w_odd_qk_all, w_odd_vt_all, o, col_blocks=True)
            nblk = sb_w // LANES
            sb = sb_attention(main, vt, batch, seq_len, 0, nblk, SB_HEADS // 2, tri_later)
            a_list, w_list, w_idx = [sb], [w_o_odd_all], o

        xf = layer_tail(a_list, w_list, w_idx, xf, layer, g_xa_all, xa_wq_all, kv_mem, xa_wo_all, g_mlp_all,
                        mlp_w1_all, mlp_w2_all, g_final, layer == depth - 1, seq_len, mem_len)
    return xf.reshape(batch, seq_len, d)
```

```python
import functools
import math

import numpy as np
import jax
import jax.numpy as jnp
from jax import lax
from jax.experimental import pallas as pl
from jax.experimental.pallas import tpu as pltpu

F32 = jnp.float32
BF16 = jnp.bfloat16

LANES = 128
HEAD_DIM = 64
FOX_HEADS = 8
NSA_HEADS = 8
NSA_GROUPS = 2
NSA_GQA = NSA_HEADS // NSA_GROUPS
SB_HEADS = 16
CMP_BLOCK = 32
CMP_STRIDE = 16
SLC_BLOCK = 64
SLC_TOPK = 8
WINDOW = 256
N_BUCKETS = 32
MAX_DISTANCE = 128
XA_HEADS = 4
XA_HEAD_DIM = 128
EPS = 1e-6
NEG = -1e30
FORCED_SCORE = 1e4
EXP_UNDERFLOW = -104.0
LOG2E = math.log2(math.e)

ATT_TILE = 256
ROW_TILE = 512
MLP_ROW_TILE = 1024
MLP_FF_TILE = 1024
GATE_ROWS = 8
F_PIECES = 3


def _dot(a, b):
    return jnp.dot(a, b, preferred_element_type=F32)


def _dot_nt(a, b):
    return lax.dot_general(a, b, (((1,), (1,)), ((), ())), preferred_element_type=F32)


def _split3(x):
    hi = x.astype(BF16)
    r1 = x - hi.astype(F32)
    mid = r1.astype(BF16)
    lo = (r1 - mid.astype(F32)).astype(BF16)
    return hi, mid, lo


def _dot_exact01(x, m01):
    hi, mid, lo = _split3(x)
    return _dot(hi, m01) + _dot(mid, m01) + _dot(lo, m01)


def _dot_01exact(m01, x):
    hi, mid, lo = _split3(x)
    return _dot(m01, hi) + _dot(m01, mid) + _dot(m01, lo)


def _rmsnorm(x, g):
    ms = jnp.mean(x * x, axis=-1, keepdims=True)
    return x * lax.rsqrt(ms + EPS) * g


def _sigmoid(x):
    return 1.0 / (1.0 + jnp.exp(-x))


def _softplus(x):
    return jnp.maximum(x, 0.0) + jnp.log1p(jnp.exp(-jnp.abs(x)))


def _split_pair(q, scale):
    lo = lax.broadcasted_iota(jnp.int32, q.shape, 1) < HEAD_DIM
    qs = q * jnp.asarray(scale, q.dtype)
    zero = jnp.zeros_like(qs)
    return jnp.where(lo, qs, zero), jnp.where(lo, zero, qs)


def _row_lo(shape):
    return lax.broadcasted_iota(jnp.int32, shape, 0) < HEAD_DIM


def _t5_bucket_np(dist):
    dist = np.maximum(dist, 0)
    max_exact = N_BUCKETS // 2
    d_f = np.maximum(dist, 1).astype(np.float64)
    large = max_exact + (np.log(d_f / max_exact) / math.log(MAX_DISTANCE / max_exact)
                         * (N_BUCKETS - max_exact)).astype(np.int32)
    large = np.minimum(large, N_BUCKETS - 1)
    return np.where(dist < max_exact, dist, large).astype(np.int32)


def _params(*sem):
    return pltpu.CompilerParams(dimension_semantics=sem)


def _full(a):
    return pl.BlockSpec(a.shape, lambda *_: (0,) * a.ndim)


def _store_col_blocks(o_ref, val):
    for c in range(o_ref.shape[0]):
        o_ref[c] = val[:, c * LANES:(c + 1) * LANES]


def _pick(a, k):
    return pl.BlockSpec((None,) + a.shape[1:], lambda *_: (k,) + (0,) * (a.ndim - 1))


def _norm_proj_kernel(x_ref, g_ref, w_ref, wt_ref, o_ref, ot_ref, *, tile):
    xn = _rmsnorm(x_ref[...], g_ref[...]).astype(BF16)
    res = _dot(xn, w_ref[...]).astype(o_ref.dtype)
    if len(o_ref.shape) == 3:
        _store_col_blocks(o_ref, res)
    else:
        o_ref[...] = res
    if wt_ref is not None:
        t = _dot_nt(wt_ref[...], xn).astype(ot_ref.dtype)
        for c in range(t.shape[1] // tile):
            ot_ref[c] = t[:, c * tile:(c + 1) * tile]


def norm_proj(x, g, g_layer, w, w_t=None, layer=0, col_blocks=False, tile=ATT_TILE, tm=ROW_TILE):
    n, d = x.shape
    p = w.shape[2]
    tm = min(tm, n)
    in_specs = [pl.BlockSpec((tm, d), lambda i: (i, 0)), _pick(g, g_layer), _pick(w, layer)]
    if col_blocks:
        out_specs = [pl.BlockSpec((p // LANES, tm, LANES), lambda i: (0, i, 0))]
        out_shape = [jax.ShapeDtypeStruct((p // LANES, n, LANES), BF16)]
    else:
        out_specs = [pl.BlockSpec((tm, p), lambda i: (i, 0))]
        out_shape = [jax.ShapeDtypeStruct((n, p), BF16)]
    args = [x, g, w]
    if w_t is None:
        body = lambda x_ref, g_ref, w_ref, o_ref: _norm_proj_kernel(
            x_ref, g_ref, w_ref, None, o_ref, None, tile=tile)
    else:
        pt = w_t.shape[1]
        in_specs.append(_pick(w_t, layer))
        out_specs.append(pl.BlockSpec((tm // tile, pt, tile), lambda i: (i, 0, 0)))
        out_shape.append(jax.ShapeDtypeStruct((n // tile, pt, tile), BF16))
        args.append(w_t)
        body = functools.partial(_norm_proj_kernel, tile=tile)
    out = pl.pallas_call(
        body, grid=(n // tm,), in_specs=in_specs, out_specs=out_specs, out_shape=out_shape,
        compiler_params=_params("parallel"), name="norm_proj",
    )(*args)
    return out if w_t is not None else out[0]


def _even_proj_kernel(x_ref, g_ref, w_ref, wt_ref, wkvc_ref, wff_ref, bf_ref, wng_ref, tri_ref,
                      main_ref, vt_ref, kc_ref, vc_ref, f_ref, gate_ref, carry_ref, chunk_ref,
                      *, tiles_per_seq, tile):
    i = pl.program_id(0)
    xn = _rmsnorm(x_ref[...], g_ref[...]).astype(BF16)
    _store_col_blocks(main_ref, _dot(xn, w_ref[...]).astype(BF16))
    t = _dot_nt(wt_ref[...], xn).astype(BF16)
    for c in range(t.shape[1] // tile):
        vt_ref[c] = t[:, c * tile:(c + 1) * tile]
    chunks = chunk_ref.shape[1] // CMP_STRIDE
    kvc = _dot(xn, wkvc_ref[...])
    for half, out_ref in enumerate((kc_ref, vc_ref)):
        chunk_ref[half] = kvc[:, half * LANES:(half + 1) * LANES]
        for pos in range(CMP_STRIDE):
            out_ref[:, pos * LANES:(pos + 1) * LANES] = chunk_ref[
                half, pl.ds(pos, chunks, stride=CMP_STRIDE), :].astype(BF16)
    gate_ref[...] = _sigmoid(_dot_nt(wng_ref[...], xn))
    ff = _dot(xn, wff_ref[...]) + bf_ref[...]
    logf = jnp.minimum(ff, 0.0) - jnp.log1p(jnp.exp(-jnp.abs(ff)))

    @pl.when(i % tiles_per_seq == 0)
    def _():
        carry_ref[...] = jnp.zeros_like(carry_ref)

    cs = _dot_01exact(tri_ref[...], logf) + carry_ref[0:1, :]
    carry_ref[...] = jnp.broadcast_to(cs[cs.shape[0] - 1:, :], carry_ref.shape)
    hi, mid, lo = _split3(-LOG2E * cs)
    piece = lax.broadcasted_iota(jnp.int32, cs.shape, 1) % F_PIECES
    f_ref[...] = jnp.where(piece == 0, hi, jnp.where(piece == 1, mid, lo))


def even_proj(x, g, layer, w_main, w_t, w_kvc, w_ff, b_f, w_ng_t, e, tri_l, seq_len,
              tile=ATT_TILE, tm=ROW_TILE):
    n, d = x.shape
    tm = min(tm, seq_len)
    pm = w_main.shape[2]
    pt = w_t.shape[1]
    gr = w_ng_t.shape[1]
    row = lambda width: pl.BlockSpec((tm, width), lambda i: (i, 0))
    chunked = pl.BlockSpec((tm // CMP_STRIDE, CMP_STRIDE * LANES), lambda i: (i, 0))
    chunked_shape = jax.ShapeDtypeStruct((n // CMP_STRIDE, CMP_STRIDE * LANES), BF16)
    return pl.pallas_call(
        functools.partial(_even_proj_kernel, tiles_per_seq=seq_len // tm, tile=tile),
        grid=(n // tm,),
        in_specs=[row(d), _pick(g, layer), _pick(w_main, e), _pick(w_t, e), _pick(w_kvc, e),
                  _pick(w_ff, e), _pick(b_f, e), _pick(w_ng_t, e), _full(tri_l)],
        out_specs=[pl.BlockSpec((pm // LANES, tm, LANES), lambda i: (0, i, 0)),
                   pl.BlockSpec((tm // tile, pt, tile), lambda i: (i, 0, 0)),
                   chunked, chunked, row(LANES),
                   pl.BlockSpec((gr, tm), lambda i: (0, i))],
        out_shape=[jax.ShapeDtypeStruct((pm // LANES, n, LANES), BF16),
                   jax.ShapeDtypeStruct((n // tile, pt, tile), BF16),
                   chunked_shape, chunked_shape,
                   jax.ShapeDtypeStruct((n, LANES), BF16),
                   jax.ShapeDtypeStruct((gr, n), F32)],
        scratch_shapes=[pltpu.VMEM((8, LANES), F32), pltpu.VMEM((2, tm, LANES), F32)],
        compiler_params=_params("arbitrary"),
        name="even_proj",
    )(x, g, w_main, w_t, w_kvc, w_ff, b_f, w_ng_t, tri_l)


def _bucket_lookup_kernel(tab_ref, idx_ref, o_ref):
    h = pl.program_id(0)
    idx = idx_ref[...]
    out = jnp.full(idx.shape, NEG, F32)
    for b in range(N_BUCKETS):
        out = jnp.where(idx == b, tab_ref[b, h], out)
    o_ref[0] = out


def bucket_lookup(tab, idx, rows):
    r, c = idx.shape
    h = tab.shape[1]
    return pl.pallas_call(
        _bucket_lookup_kernel,
        grid=(h, r // rows),
        in_specs=[pl.BlockSpec(memory_space=pltpu.SMEM),
                  pl.BlockSpec((rows, c), lambda hh, j: (j, 0))],
        out_specs=pl.BlockSpec((1, rows, c), lambda hh, j: (hh, j, 0)),
        out_shape=jax.ShapeDtypeStruct((h, r, c), F32),
        compiler_params=_params("parallel", "parallel"),
        name="bucket_lookup",
    )(tab, idx)


def _fox_kernel(q_ref, k_ref, fs_ref, vt_ref, o_ref, stage_ref, *, tile, n_tiles):
    p = pl.program_id(1)
    lane = lax.broadcasted_iota(jnp.int32, (tile, LANES), 1)
    causal = (lax.broadcasted_iota(jnp.int32, (tile, tile), 0)
              <= lax.broadcasted_iota(jnp.int32, (tile, tile), 1))
    lo = _row_lo((LANES, tile))

    def keys(j):
        sl = slice(j * tile, (j + 1) * tile)
        return jnp.concatenate([k_ref[sl, :], fs_ref[sl, :]], axis=1)

    def queries(i):
        heads = _split_pair(q_ref[i * tile:(i + 1) * tile, :], HEAD_DIM ** -0.5)
        return tuple(jnp.concatenate(
            [heads[a], jnp.where(lane // F_PIECES == 2 * p + a, 1.0, 0.0).astype(BF16)], axis=1)
            for a in range(2))

    tasks = [(i, j) for i in range(n_tiles) for j in range(i + 1)]
    qcache = {}

    def scores(t):
        i, j = tasks[t]
        if i not in qcache:
            qcache[i] = queries(i)
        kj = keys(j)
        for a in range(2):
            stage_ref[t % 2, a] = _dot_nt(kj, qcache[i][a])

    scores(0)
    for t, (i, j) in enumerate(tasks):
        if t + 1 < len(tasks):
            scores(t + 1)
        qk = [stage_ref[t % 2, a] for a in range(2)]
        if j == 0:
            stats = [(jnp.full((1, tile), NEG, F32), jnp.zeros((1, tile), F32)) for _ in range(2)]
            acc = [jnp.zeros((LANES, tile), F32) for _ in range(2)]
        vt = vt_ref[j]
        alphas, pvs = [], []
        for a in range(2):
            m_prev, l_prev = stats[a]
            s = jnp.where(causal, qk[a], NEG) if j == i else qk[a]
            m_new = jnp.maximum(m_prev, jnp.max(s, axis=0, keepdims=True))
            alpha = jnp.exp2(m_prev - m_new)
            pr = jnp.exp2(s - m_new)
            stats[a] = (m_new, alpha * l_prev + jnp.sum(pr, axis=0, keepdims=True))
            alphas.append(alpha)
            pvs.append(_dot(vt, pr.astype(BF16)))
        for a in range(2):
            acc[a] = alphas[a] * acc[a] + pvs[a]
        if j == i:
            o = jnp.where(lo, acc[0] / stats[0][1], acc[1] / stats[1][1])
            o_ref[i * tile:(i + 1) * tile, :] = o.T.astype(o_ref.dtype)


def fox_attention(main, fs, vt, batch, seq_len, q_col, k_col, v_row, n_pairs, tile=ATT_TILE):
    nq = seq_len // tile
    n = batch * seq_len
    return pl.pallas_call(
        functools.partial(_fox_kernel, tile=tile, n_tiles=nq),
        grid=(batch, n_pairs),
        in_specs=[pl.BlockSpec((None, seq_len, LANES), lambda b, p: (q_col + p, b, 0)),
                  pl.BlockSpec((None, seq_len, LANES), lambda b, p: (k_col + p, b, 0)),
                  pl.BlockSpec((seq_len, LANES), lambda b, p: (b, 0)),
                  pl.BlockSpec((nq, LANES, tile), lambda b, p: (b, v_row + p, 0))],
        out_specs=pl.BlockSpec((None, seq_len, LANES), lambda b, p: (p, b, 0)),
        out_shape=jax.ShapeDtypeStruct((n_pairs, n, LANES), BF16),
        scratch_shapes=[pltpu.VMEM((2, 2, tile, tile), F32)],
        compiler_params=_params("parallel", "parallel"),
        name="fox_attention",
    )(main, main, fs, vt)


def _sb_kernel(q_ref, k_ref, vt_ref, tri_ref, o_ref, acc_ref, stage_ref, *, tile, n_tiles):
    tri = tri_ref[...]
    strict = (lax.broadcasted_iota(jnp.int32, (tile, tile), 0)
              < lax.broadcasted_iota(jnp.int32, (tile, tile), 1))
    lo = _row_lo((LANES, tile))

    def queries(i):
        return _split_pair(q_ref[i * tile:(i + 1) * tile, :], HEAD_DIM ** -0.5)

    def scores(ks, heads):
        return tuple(_dot_nt(ks, heads[a]) for a in range(2))

    def consume(zz, rs, vt, masked):
        sps, base, later, pvs = [], [], [], []
        for a in range(2):
            z = zz[a]
            sp = jnp.maximum(z, 0.0) + jnp.log(1.0 + jnp.exp(-jnp.abs(z)))
            base.append(z - sp)
            if masked:
                sp = jnp.where(strict, sp, 0.0)
            sps.append(sp)
        for a in range(2):
            later.append(_dot(tri, sps[a].astype(BF16)))
        for a in range(2):
            wgt = jnp.exp(base[a] - later[a] + rs[a])
            if masked:
                wgt = jnp.where(strict, wgt, 0.0)
            pvs.append(_dot(vt, wgt.astype(BF16)))
        return tuple(rs[a] - jnp.sum(sps[a], axis=0, keepdims=True) for a in range(2)), pvs

    def live(rs):
        return (jnp.max(jnp.maximum(rs[0], rs[1])) > EXP_UNDERFLOW).astype(jnp.int32)

    tasks = [(i, j) for i in range(n_tiles) for j in ((i, i - 1) if i else (i,))]
    qcache = {}

    def task_scores(t):
        i, j = tasks[t]
        if i not in qcache:
            qcache[i] = queries(i)
        zz = scores(k_ref[j * tile:(j + 1) * tile, :], qcache[i])
        for a in range(2):
            stage_ref[t % 2, a] = zz[a]

    zero = jnp.zeros((1, tile), F32)
    survival = {}
    task_scores(0)
    for t, (i, j) in enumerate(tasks):
        if t + 1 < len(tasks):
            task_scores(t + 1)
        zz = [stage_ref[t % 2, a] for a in range(2)]
        if j == i:
            rs, acc = (zero, zero), None
        rs, pvs = consume(zz, rs, vt_ref[j], j == i)
        acc = pvs if acc is None else [acc[a] + pvs[a] for a in range(2)]
        if j == max(i - 1, 0):
            for a in range(2):
                acc_ref[i, a] = acc[a]
            survival[i] = rs

    for i in range(2, n_tiles):
        heads = qcache[i]

        def cond(state):
            j, alive, _ = state
            return (j >= 0) & (alive > 0)

        def body(state, i=i, heads=heads):
            j, _, rs = state
            rs, pvs = consume(scores(k_ref[pl.ds(j * tile, tile), :], heads), rs, vt_ref[j], False)
            for a in range(2):
                acc_ref[i, a] = acc_ref[i, a] + pvs[a]
            return j - 1, live(rs), rs

        lax.while_loop(cond, body, (i - 2, live(survival[i]), survival[i]))

    for i in range(n_tiles):
        o = jnp.where(lo, acc_ref[i, 0], acc_ref[i, 1])
        o_ref[i * tile:(i + 1) * tile, :] = o.T.astype(o_ref.dtype)


def sb_attention(main, vt, batch, seq_len, q_col, k_col, n_pairs, tri, tile=ATT_TILE):
    nq = seq_len // tile
    n = batch * seq_len
    return pl.pallas_call(
        functools.partial(_sb_kernel, tile=tile, n_tiles=nq),
        grid=(batch, n_pairs),
        in_specs=[pl.BlockSpec((None, seq_len, LANES), lambda b, p: (q_col + p, b, 0)),
                  pl.BlockSpec((None, seq_len, LANES), lambda b, p: (k_col + p, b, 0)),
                  pl.BlockSpec((nq, LANES, tile), lambda b, p: (b, p, 0)),
                  _full(tri)],
        out_specs=pl.BlockSpec((None, seq_len, LANES), lambda b, p: (p, b, 0)),
        out_shape=jax.ShapeDtypeStruct((n_pairs, n, LANES), BF16),
        scratch_shapes=[pltpu.VMEM((nq, 2, LANES, tile), F32), pltpu.VMEM((2, 2, tile, tile), F32)],
        compiler_params=_params("parallel", "parallel"),
        name="sb_attention",
    )(main, main, vt, tri)


def _gelu_tanh(x):
    return 0.5 * x * (1.0 + jnp.tanh(math.sqrt(2.0 / math.pi) * (x + 0.044715 * (x * x * x))))


def _compress_kernel(xk_ref, xv_ref, pek_ref, pev_ref, w1k_ref, w1v_ref, w2k_ref, w2v_ref,
                     ok_ref, ov_ref, *, n_cmp):
    def one(x_ref, pe_ref, w1_ref, w2_ref):
        x = x_ref[...].astype(F32)
        xa = (x + pe_ref[0:1, :]).astype(BF16)
        xb = (x + pe_ref[1:2, :]).astype(BF16)
        ha = _dot(xa, w1_ref[0])
        hb = _dot(xb, w1_ref[1])
        rows = ha.shape[0]
        hb = pltpu.roll(hb, rows - 1, 0)
        h = _gelu_tanh(ha + hb)
        out = _dot(h.astype(BF16), w2_ref[...])
        ridx = lax.broadcasted_iota(jnp.int32, out.shape, 0)
        return jnp.where(ridx < n_cmp, out, 0.0)

    ok_ref[0] = one(xk_ref, pek_ref, w1k_ref, w2k_ref).astype(ok_ref.dtype)
    ov_ref[0] = one(xv_ref, pev_ref, w1v_ref, w2v_ref).T.astype(ov_ref.dtype)


def compress(xk, xv, pek, pev, w1k, w1v, w2k, w2v, batch, n_cmp):
    chunks = xk.shape[0] // batch
    row = pl.BlockSpec((chunks, xk.shape[1]), lambda b: (b, 0))
    return pl.pallas_call(
        functools.partial(_compress_kernel, n_cmp=n_cmp),
        grid=(batch,),
        in_specs=[row, row, _full(pek), _full(pev), _full(w1k), _full(w1v), _full(w2k), _full(w2v)],
        out_specs=[pl.BlockSpec((1, chunks, LANES), lambda b: (b, 0, 0)),
                   pl.BlockSpec((1, LANES, chunks), lambda b: (b, 0, 0))],
        out_shape=[jax.ShapeDtypeStruct((batch, chunks, LANES), BF16),
                   jax.ShapeDtypeStruct((batch, LANES, chunks), BF16)],
        compiler_params=_params("parallel"),
        name="nsa_compress",
    )(xk, xv, pek, pev, w1k, w1v, w2k, w2v)


def _nsa_cmp_kernel(q_ref, kc_ref, vct_ref, cb_ref, ov_ref, ocmp_ref, sel_ref, stage_ref, *, tile, n_blocks):
    i = pl.program_id(1)
    kc = kc_ref[0]
    vct = vct_ref[0]
    pcsum = [jnp.zeros((LANES, tile), F32), jnp.zeros((LANES, tile), F32)]
    lo = _row_lo((LANES, tile))
    for r in range(NSA_GQA):
        heads = _split_pair(q_ref[r], HEAD_DIM ** -0.5)
        for a in range(2):
            stage_ref[2 * r + a] = _dot_nt(kc, heads[a])
    for r in range(NSA_GQA):
        outs = []
        for a in range(2):
            bias = cb_ref[2 * r + a]
            valid = bias > 0.5 * NEG
            s = stage_ref[2 * r + a] + bias
            m = jnp.max(s, axis=0, keepdims=True)
            pr = jnp.where(valid, jnp.exp2(s - m), 0.0)
            l = jnp.sum(pr, axis=0, keepdims=True)
            pc = pr / jnp.where(l > 0.0, l, 1.0)
            pcsum[a] = pcsum[a] + pc
            outs.append(_dot(vct, pc.astype(BF16)))
        ocmp_ref[r * LANES:(r + 1) * LANES, :] = jnp.where(lo, outs[0], outs[1])

    score = _dot_01exact(ov_ref[0], pcsum[0]) + _dot_01exact(ov_ref[1], pcsum[1])
    score = score[:2 * n_blocks]
    shape = (2 * n_blocks, tile)
    row = lax.broadcasted_iota(jnp.int32, shape, 0)
    qpos = i * tile + lax.broadcasted_iota(jnp.int32, shape, 1)
    blk = row % n_blocks
    cur = qpos // SLC_BLOCK
    forced = (blk == 0) | (blk == cur) | (blk == cur - 1)
    future = blk > cur
    score = jnp.where(future, -1.0, jnp.where(forced, FORCED_SCORE, score))
    grp0 = row < n_blocks
    cnt = jnp.zeros(shape, F32)
    for mth in range(n_blocks):
        other = jnp.where(grp0, score[mth:mth + 1, :], score[n_blocks + mth:n_blocks + mth + 1, :])
        ahead = (other > score) | ((other == score) & (blk > mth))
        cnt = cnt + jnp.where(ahead, 1.0, 0.0)
    n_top = min(SLC_TOPK, n_blocks)
    sel_ref[...] = jnp.where(cnt < n_top, 0.0, NEG)


def nsa_cmp_select(main, kc, vct, cmp_bias, overlap2, batch, seq_len, q_col, tile=ATT_TILE):
    nq = seq_len // tile
    n = batch * seq_len
    n_blocks = seq_len // SLC_BLOCK
    qw = NSA_GQA * LANES
    return pl.pallas_call(
        functools.partial(_nsa_cmp_kernel, tile=tile, n_blocks=n_blocks),
        grid=(batch, nq),
        in_specs=[pl.BlockSpec((NSA_GQA, tile, LANES), lambda b, i: (q_col // NSA_GQA, b * nq + i, 0)),
                  pl.BlockSpec((1,) + kc.shape[1:], lambda b, i: (b, 0, 0)),
                  pl.BlockSpec((1,) + vct.shape[1:], lambda b, i: (b, 0, 0)),
                  pl.BlockSpec((NSA_HEADS, cmp_bias.shape[1], tile), lambda b, i: (0, 0, i)),
                  _full(overlap2)],
        out_specs=[pl.BlockSpec((qw, tile), lambda b, i: (0, b * nq + i)),
                   pl.BlockSpec((2 * n_blocks, tile), lambda b, i: (0, b * nq + i))],
        out_shape=[jax.ShapeDtypeStruct((qw, n), F32),
                   jax.ShapeDtypeStruct((2 * n_blocks, n), F32)],
        scratch_shapes=[pltpu.VMEM((NSA_HEADS, kc.shape[1], tile), F32)],
        compiler_params=_params("parallel", "parallel"),
        name="nsa_cmp_select",
    )(main, kc, vct, cmp_bias, overlap2)


def _nsa_main_kernel(far_ref, q_ref, sel_ref, ocmp_ref, gate_ref, ks_ref, vst_ref, kw_ref, vwt_ref, tz_ref,
                     o_ref, stage_ref, *, tile, n_tiles, n_blocks):
    r = pl.program_id(1)
    lo = _row_lo((LANES, tile))
    blocks_per_tile = tile // SLC_BLOCK
    far = tuple(far_ref[2 * r + a] for a in range(2))

    def queries(i):
        return _split_pair(q_ref[i * tile:(i + 1) * tile, :], HEAD_DIM ** -0.5)

    def chosen(a, i, j, offset=None):
        rows = []
        for nb in range(blocks_per_tile):
            blk = a * n_blocks + j * blocks_per_tile + nb
            row = sel_ref[blk:blk + 1, i * tile:(i + 1) * tile]
            if offset is not None:
                row = row + offset
            rows.append(jnp.broadcast_to(row, (SLC_BLOCK, tile)))
        return jnp.concatenate(rows, axis=0)

    tasks = []
    for i in range(n_tiles):
        tasks += [("slc", i, j) for j in range(i + 1)]
        tasks += [("win", i, j) for j in ((i - 1, i) if i else (i,))]
    qcache = {}

    def scores(t):
        kind, i, j = tasks[t]
        if i not in qcache:
            qcache[i] = queries(i)
        k_ref_ = ks_ref if kind == "slc" else kw_ref
        kj = k_ref_[j * tile:(j + 1) * tile, :]
        for a in range(2):
            stage_ref[t % 2, a] = _dot_nt(kj, qcache[i][a])

    def fresh():
        return ([(jnp.full((1, tile), NEG, F32), jnp.zeros((1, tile), F32)) for _ in range(2)],
                [jnp.zeros((LANES, tile), F32) for _ in range(2)])

    scores(0)
    for t, (kind, i, j) in enumerate(tasks):
        if t + 1 < len(tasks):
            scores(t + 1)
        qk = [stage_ref[t % 2, a] for a in range(2)]
        first = j == 0 if kind == "slc" else j == max(i - 1, 0)
        if first:
            stats, acc = fresh()
        vt = (vst_ref if kind == "slc" else vwt_ref)[j]
        alphas, pvs = [], []
        for a in range(2):
            if j == i:
                s = qk[a] + tz_ref[a, 0]
            elif j == i - 1:
                s = qk[a] + tz_ref[a, 1 if kind == "slc" else 2]
            if kind == "slc":
                s = s + chosen(a, i, j) if j >= i - 1 else qk[a] + chosen(a, i, j, far[a])
            m_prev, l_prev = stats[a]
            m_new = jnp.maximum(m_prev, jnp.max(s, axis=0, keepdims=True))
            alpha = jnp.exp2(m_prev - m_new)
            pr = jnp.exp2(s - m_new)
            stats[a] = (m_new, alpha * l_prev + jnp.sum(pr, axis=0, keepdims=True))
            alphas.append(alpha)
            pvs.append(_dot(vt, pr.astype(BF16)))
        for a in range(2):
            acc[a] = alphas[a] * acc[a] + pvs[a]
        if j == i:
            branch = jnp.where(lo, acc[0] / stats[0][1], acc[1] / stats[1][1])
            if kind == "slc":
                o_slc = branch
            else:
                cols = slice(i * tile, (i + 1) * tile)
                gexp = [jnp.where(lo, gate_ref[2 * c:2 * c + 1, cols], gate_ref[2 * c + 1:2 * c + 2, cols])
                        for c in range(3)]
                out = gexp[0] * ocmp_ref[:, cols] + gexp[1] * o_slc + gexp[2] * branch
                o_ref[cols, :] = out.T.astype(o_ref.dtype)


def nsa_main(far_bias, main, sel, ocmp, gates, vt, tz, batch, seq_len, q_col, ks_col, kw_col, vs_row, vw_row,
             tile=ATT_TILE):
    nq = seq_len // tile
    n = batch * seq_len
    n_blocks = seq_len // SLC_BLOCK
    kv = lambda col: pl.BlockSpec((None, seq_len, LANES), lambda b, r: (col, b, 0))
    vts = lambda rowblk: pl.BlockSpec((nq, LANES, tile), lambda b, r: (b, rowblk, 0))
    return pl.pallas_call(
        functools.partial(_nsa_main_kernel, tile=tile, n_tiles=nq, n_blocks=n_blocks),
        grid=(batch, NSA_GQA),
        in_specs=[pl.BlockSpec(memory_space=pltpu.SMEM),
                  pl.BlockSpec((None, seq_len, LANES), lambda b, r: (q_col + r, b, 0)),
                  pl.BlockSpec((2 * n_blocks, seq_len), lambda b, r: (0, b)),
                  pl.BlockSpec((LANES, seq_len), lambda b, r: (r, b)),
                  pl.BlockSpec((GATE_ROWS, seq_len), lambda b, r: (r, b)),
                  kv(ks_col), vts(vs_row), kv(kw_col), vts(vw_row),
                  pl.BlockSpec((2, 3, tile, tile), lambda b, r: (r, 0, 0, 0))],
        out_specs=pl.BlockSpec((None, seq_len, LANES), lambda b, r: (r, b, 0)),
        out_shape=jax.ShapeDtypeStruct((NSA_GQA, n, LANES), BF16),
        scratch_shapes=[pltpu.VMEM((2, 2, tile, tile), F32)],
        compiler_params=_params("parallel", "parallel"),
        name="nsa_main",
    )(far_bias, main, sel, ocmp, gates, main, vt, main, vt, tz)


def _layer_tail_kernel(*refs, n_in, final_norm):
    a_refs = refs[:n_in]
    w_refs = refs[n_in:2 * n_in]
    (x_ref, gx_ref, wq_ref, kv_ref, wo_ref, gm_ref, w1_ref, w2_ref, gf_ref,
     o_ref, hn_ref, acc_ref) = refs[2 * n_in:]
    f = pl.program_id(1)

    @pl.when(f == 0)
    def _():
        x1 = x_ref[...]
        for a_ref, w_ref in zip(a_refs, w_refs):
            a = jnp.concatenate([a_ref[c] for c in range(a_ref.shape[0])], axis=1)
            x1 = x1 + _dot(a, w_ref[...])
        q = _dot(_rmsnorm(x1, gx_ref[...]).astype(BF16), wq_ref[...]).astype(BF16)
        width = XA_HEADS * XA_HEAD_DIM
        outs = []
        for h in range(XA_HEADS):
            hs = slice(h * XA_HEAD_DIM, (h + 1) * XA_HEAD_DIM)
            s = _dot_nt(q[:, hs], kv_ref[:, hs]) * (XA_HEAD_DIM ** -0.5)
            m = jnp.max(s, axis=-1, keepdims=True)
            p = jnp.exp(s - m)
            l = jnp.sum(p, axis=-1, keepdims=True)
            v = kv_ref[:, width + h * XA_HEAD_DIM:width + (h + 1) * XA_HEAD_DIM]
            outs.append((_dot(p.astype(BF16), v) / l).astype(BF16))
        x2 = x1 + _dot(jnp.concatenate(outs, axis=-1), wo_ref[...])
        hn_ref[...] = _rmsnorm(x2, gm_ref[...]).astype(BF16)
        acc_ref[...] = x2

    h = jnp.maximum(_dot(hn_ref[...], w1_ref[...]), 0.0)
    acc_ref[...] += _dot((h * h).astype(BF16), w2_ref[...])

    @pl.when(f == pl.num_programs(1) - 1)
    def _():
        y = acc_ref[...]
        if final_norm:
            y = _rmsnorm(y, gf_ref[...])
        o_ref[...] = y


def layer_tail(a_list, w_list, w_idx, x, layer, gx, wq, kv, wo, gm, w1, w2, gf, final_norm, seq_len, mem_len,
               tm=MLP_ROW_TILE, tf=MLP_FF_TILE):
    n, d = x.shape
    ff = w1.shape[2]
    tm = min(tm, seq_len)
    tf = min(tf, ff)
    tiles_per_seq = seq_len // tm
    once = pl.Buffered(1)
    row = lambda a: pl.BlockSpec((tm, a.shape[1]), lambda i, f: (i, 0))
    pick = lambda a, k: pl.BlockSpec((None,) + a.shape[1:], lambda i, f: (k,) + (0,) * (a.ndim - 1),
                                     pipeline_mode=once)
    return pl.pallas_call(
        functools.partial(_layer_tail_kernel, n_in=len(a_list), final_norm=final_norm),
        grid=(n // tm, ff // tf),
        in_specs=[pl.BlockSpec((a.shape[0], tm, LANES), lambda i, f: (0, i, 0)) for a in a_list]
        + [pick(w, w_idx) for w in w_list]
        + [row(x), pick(gx, layer), pick(wq, layer),
           pl.BlockSpec((mem_len, 2 * XA_HEADS * XA_HEAD_DIM), lambda i, f: (i // tiles_per_seq, layer)),
           pick(wo, layer), pick(gm, layer),
           pl.BlockSpec((None, d, tf), lambda i, f: (layer, 0, f)),
           pl.BlockSpec((None, tf, d), lambda i, f: (layer, f, 0)),
           pl.BlockSpec(gf.shape, lambda i, f: (0, 0), pipeline_mode=once)],
        out_specs=pl.BlockSpec((tm, d), lambda i, f: (i, 0)),
        out_shape=jax.ShapeDtypeStruct((n, d), F32),
        scratch_shapes=[pltpu.VMEM((tm, d), BF16), pltpu.VMEM((tm, d), F32)],
        compiler_params=_params("parallel", "arbitrary"),
        name="layer_tail",
    )(*a_list, *w_list, x, gx, wq, kv, wo, gm, w1, w2, gf)


def _static_tables(seq_len):
    tile = ATT_TILE
    n_chunks = seq_len // CMP_STRIDE
    n_cmp = (seq_len - CMP_BLOCK) // CMP_STRIDE + 1
    n_slc = seq_len // SLC_BLOCK
    assert n_chunks == LANES and 2 * n_slc <= LANES and seq_len % tile == 0 and WINDOW == tile
    c = np.arange(n_chunks)[:, None]
    t = np.arange(seq_len)[None, :]
    cdist = t - (c * CMP_STRIDE + CMP_BLOCK - 1)
    cmp_idx = np.where((cdist >= 0) & (c < n_cmp), _t5_bucket_np(cdist), -1).astype(np.int32)
    sl = np.arange(tile)[:, None]
    tl = np.arange(tile)[None, :]
    prev = _t5_bucket_np(tl - sl + tile)
    tz_idx = np.concatenate([np.where(sl <= tl, _t5_bucket_np(tl - sl), -1), prev,
                             np.where(sl > tl, prev, -1)], axis=0).astype(np.int32)
    far_bucket = int(_t5_bucket_np(np.array([tile + 1]))[0])
    assert far_bucket == int(_t5_bucket_np(np.array([seq_len]))[0])
    cmp_start = np.arange(n_cmp) * CMP_STRIDE
    cmp_stop = cmp_start + CMP_BLOCK - 1
    slc_start = np.arange(n_slc) * SLC_BLOCK
    slc_stop = slc_start + SLC_BLOCK - 1
    ov = ((cmp_start[:, None] <= slc_stop[None, :]) & (cmp_stop[:, None] >= slc_start[None, :]))
    overlap2 = np.zeros((2, LANES, LANES), np.float32)
    for a in range(2):
        overlap2[a, a * n_slc:(a + 1) * n_slc, :n_cmp] = ov.T
    return cmp_idx, tz_idx, far_bucket, overlap2, n_cmp


def _tri_prefix(nn):
    s = np.arange(nn)[:, None]
    j = np.arange(nn)[None, :]
    return (j <= s).astype(np.float32)


def _tri_later(nn):
    s = np.arange(nn)[:, None]
    j = np.arange(nn)[None, :]
    return (j > s).astype(np.float32)


_NSA_HEAD_PERM = [a * NSA_GQA + r for r in range(NSA_GQA) for a in range(NSA_GROUPS)]


def _block_diag2(m):
    z = jnp.zeros_like(m)
    return jnp.concatenate([jnp.concatenate([m, z], axis=1), jnp.concatenate([z, m], axis=1)], axis=0)


def _compress_weights(pe, w1, w2):
    half = CMP_BLOCK // 2
    pe2 = jnp.concatenate([pe, pe], axis=1).reshape(2, half * LANES)
    w1e = jax.vmap(_block_diag2)(w1).reshape(2, half * LANES, LANES)
    return pe2.astype(F32), w1e.astype(BF16), _block_diag2(w2).astype(BF16)


def kernel(x, mem, rel_bias, mem_norm_g, norm_mix_g, norm_xattn_g, norm_mlp_g, final_norm_g, w_in_even, b_forget, cmp_pe_k, cmp_w1_k, cmp_w2_k, cmp_pe_v, cmp_w1_v, cmp_w2_v, w_out_even, w_in_odd, w_out_odd, xa_wq, xa_wkv, xa_wo, mlp_w1, mlp_w2):
    batch, seq_len, d = x.shape
    mem_len = mem.shape[1]
    depth = norm_mix_g.shape[0]
    n = batch * seq_len
    fox_w = FOX_HEADS * HEAD_DIM
    nsa_w = NSA_HEADS * HEAD_DIM
    kv_w = NSA_GROUPS * HEAD_DIM
    splits = np.cumsum([fox_w, fox_w, fox_w, FOX_HEADS, nsa_w, kv_w, kv_w, kv_w, kv_w, kv_w, kv_w])

    cmp_idx, tz_idx, far_bucket, overlap2, n_cmp = _static_tables(seq_len)
    tab = LOG2E * rel_bias.astype(F32)[:, _NSA_HEAD_PERM]
    cmp_bias = bucket_lookup(tab, jnp.asarray(cmp_idx), rows=cmp_idx.shape[0])
    tz = bucket_lookup(tab, jnp.asarray(tz_idx), rows=ATT_TILE)
    tz = tz.reshape(NSA_HEADS, 3, ATT_TILE, ATT_TILE)
    far_bias = tab[far_bucket]
    overlap2 = jnp.asarray(overlap2, BF16)
    tri_prefix = jnp.asarray(_tri_prefix(min(ROW_TILE, seq_len)), BF16)
    tri_later = jnp.asarray(_tri_later(ATT_TILE), BF16)

    xf = x.reshape(n, d)
    memf = mem.reshape(batch * mem_len, d)
    row = lambda v: v.reshape(1, -1).astype(F32)
    rows = lambda v: v.reshape(v.shape[0], 1, -1).astype(F32)
    g_mix_all, g_xa_all, g_mlp_all = rows(norm_mix_g), rows(norm_xattn_g), rows(norm_mlp_g)
    g_mem, g_final = row(mem_norm_g), row(final_norm_g)

    ne = w_in_even.shape[0]
    (w_fq, w_fk, w_fv, w_ff, w_nq, w_kc, w_vc, w_ks, w_vs, w_kw, w_vw, w_ng) = jnp.split(w_in_even, splits, axis=2)
    w_fq = LOG2E * w_fq
    w_nq = LOG2E * w_nq.reshape(ne, d, NSA_HEADS, HEAD_DIM)[:, :, _NSA_HEAD_PERM, :].reshape(ne, d, nsa_w)
    w_main_all = jnp.concatenate([w_fq, w_fk, w_nq, w_ks, w_kw], axis=2).astype(BF16)
    w_t_all = jnp.concatenate([w_fv, w_vs, w_vw], axis=2).transpose(0, 2, 1).astype(BF16)
    w_kvc_all = jnp.concatenate([w_kc, w_vc], axis=2).astype(BF16)
    w_ng_t = w_ng.reshape(ne, d, NSA_GROUPS, NSA_GQA, 3).transpose(0, 3, 4, 2, 1).reshape(ne, NSA_GQA, 6, d)
    w_ng_all = jnp.pad(w_ng_t, ((0, 0), (0, 0), (0, GATE_ROWS - 6), (0, 0))).reshape(
        ne, NSA_GQA * GATE_ROWS, d).astype(BF16)
    pad_f = LANES - F_PIECES * FOX_HEADS
    w_ff_all = jnp.pad(jnp.repeat(w_ff, F_PIECES, axis=2), ((0, 0), (0, 0), (0, pad_f))).astype(BF16)
    b_f_all = jnp.pad(jnp.repeat(b_forget.astype(F32), F_PIECES, axis=1), ((0, 0), (0, pad_f))).reshape(ne, 1, LANES)
    cmp_k = jax.vmap(_compress_weights)(cmp_pe_k, cmp_w1_k, cmp_w2_k)
    cmp_v = jax.vmap(_compress_weights)(cmp_pe_v, cmp_w1_v, cmp_w2_v)
    w_o_fox_all = w_out_even[:, :fox_w].astype(BF16)
    w_o_nsa_all = w_out_even[:, fox_w:].reshape(ne, NSA_HEADS, HEAD_DIM, d)[:, _NSA_HEAD_PERM].reshape(
        ne, nsa_w, d).astype(BF16)
    sb_w = SB_HEADS * HEAD_DIM
    w_odd_qk_all = w_in_odd[:, :, :2 * sb_w].astype(BF16)
    w_odd_vt_all = w_in_odd[:, :, 2 * sb_w:].transpose(0, 2, 1).astype(BF16)
    w_o_odd_all = w_out_odd.astype(BF16)
    xa_wq_all, xa_wo_all = xa_wq.astype(BF16), xa_wo.astype(BF16)
    mlp_w1_all, mlp_w2_all = mlp_w1.astype(BF16), mlp_w2.astype(BF16)
    xa_wkv_cat = xa_wkv.transpose(1, 0, 2).reshape(d, depth * xa_wkv.shape[2]).astype(BF16)
    kv_mem = norm_proj(memf, g_mem[None], 0, xa_wkv_cat[None])

    for layer in range(depth):
        if layer % 2 == 0:
            e = layer // 2
            main, vt, kc_in, vc_in, fs, gates = even_proj(
                xf, g_mix_all, layer, w_main_all, w_t_all, w_kvc_all, w_ff_all, b_f_all, w_ng_all, e,
                tri_prefix, seq_len)
            nblk = fox_w // LANES
            fox = fox_attention(main, fs, vt, batch, seq_len, 0, nblk, 0, FOX_HEADS // 2)

            kc, vct = compress(kc_in, vc_in,
                               cmp_k[0][e], cmp_v[0][e], cmp_k[1][e], cmp_v[1][e], cmp_k[2][e], cmp_v[2][e],
                               batch, n_cmp)
            q_col = 2 * nblk
            ocmp, sel = nsa_cmp_select(main, kc, vct, cmp_bias, overlap2, batch, seq_len, q_col)
            nsa = nsa_main(far_bias, main, sel, ocmp, gates, vt, tz, batch, seq_len, q_col,
                           q_col + NSA_GQA, q_col + NSA_GQA + 1, nblk, nblk + 1)
            a_list, w_list, w_idx = [fox, nsa], [w_o_fox_all, w_o_nsa_all], e
        else:
            o = layer // 2
            main, vt = norm_proj(xf, g_mix_all, layer, w_odd_qk_all, w_odd_vt_all, o, col_blocks=True)
            nblk = sb_w // LANES
            sb = sb_attention(main, vt, batch, seq_len, 0, nblk, SB_HEADS // 2, tri_later)
            a_list, w_list, w_idx = [sb], [w_o_odd_all], o

        xf = layer_tail(a_list, w_list, w_idx, xf, layer, g_xa_all, xa_wq_all, kv_mem, xa_wo_all, g_mlp_all,
                        mlp_w1_all, mlp_w2_all, g_final, layer == depth - 1, seq_len, mem_len)
    return xf.reshape(batch, seq_len, d)
```

```python
import functools
import math

import numpy as np
import jax
import jax.numpy as jnp
from jax import lax
from jax.experimental import pallas as pl
from jax.experimental.pallas import tpu as pltpu

F32 = jnp.float32
BF16 = jnp.bfloat16

LANES = 128
HEAD_DIM = 64
FOX_HEADS = 8
NSA_HEADS = 8
NSA_GROUPS = 2
NSA_GQA = NSA_HEADS // NSA_GROUPS
SB_HEADS = 16
CMP_BLOCK = 32
CMP_STRIDE = 16
SLC_BLOCK = 64
SLC_TOPK = 8
WINDOW = 256
N_BUCKETS = 32
MAX_DISTANCE = 128
XA_HEADS = 4
XA_HEAD_DIM = 128
EPS = 1e-6
NEG = -1e30
FORCED_SCORE = 1e4
EXP_UNDERFLOW = -104.0
LOG2E = math.log2(math.e)
SOFTPLUS_CLAMP = 80.0

ATT_TILE = 256
ROW_TILE = 512
MLP_ROW_TILE = 1024
MLP_FF_TILE = 1024
GATE_ROWS = 8
F_PIECES = 3


def _dot(a, b):
    return jnp.dot(a, b, preferred_element_type=F32)


def _dot_nt(a, b):
    return lax.dot_general(a, b, (((1,), (1,)), ((), ())), preferred_element_type=F32)


def _split3(x):
    hi = x.astype(BF16)
    r1 = x - hi.astype(F32)
    mid = r1.astype(BF16)
    lo = (r1 - mid.astype(F32)).astype(BF16)
    return hi, mid, lo


def _dot_exact01(x, m01):
    hi, mid, lo = _split3(x)
    return _dot(hi, m01) + _dot(mid, m01) + _dot(lo, m01)


def _dot_01exact(m01, x):
    hi, mid, lo = _split3(x)
    return _dot(m01, hi) + _dot(m01, mid) + _dot(m01, lo)


def _rmsnorm(x, g):
    ms = jnp.mean(x * x, axis=-1, keepdims=True)
    return x * lax.rsqrt(ms + EPS) * g


def _sigmoid(x):
    return 1.0 / (1.0 + jnp.exp(-x))


def _softplus(x):
    return jnp.maximum(x, 0.0) + jnp.log1p(jnp.exp(-jnp.abs(x)))


def _split_pair(q, scale):
    lo = lax.broadcasted_iota(jnp.int32, q.shape, 1) < HEAD_DIM
    qs = q * jnp.asarray(scale, q.dtype)
    zero = jnp.zeros_like(qs)
    return jnp.where(lo, qs, zero), jnp.where(lo, zero, qs)


def _row_lo(shape):
    return lax.broadcasted_iota(jnp.int32, shape, 0) < HEAD_DIM


def _t5_bucket_np(dist):
    dist = np.maximum(dist, 0)
    max_exact = N_BUCKETS // 2
    d_f = np.maximum(dist, 1).astype(np.float64)
    large = max_exact + (np.log(d_f / max_exact) / math.log(MAX_DISTANCE / max_exact)
                         * (N_BUCKETS - max_exact)).astype(np.int32)
    large = np.minimum(large, N_BUCKETS - 1)
    return np.where(dist < max_exact, dist, large).astype(np.int32)


def _params(*sem):
    return pltpu.CompilerParams(dimension_semantics=sem)


def _full(a):
    return pl.BlockSpec(a.shape, lambda *_: (0,) * a.ndim)


def _store_col_blocks(o_ref, val):
    for c in range(o_ref.shape[0]):
        o_ref[c] = val[:, c * LANES:(c + 1) * LANES]


def _pick(a, k):
    return pl.BlockSpec((None,) + a.shape[1:], lambda *_: (k,) + (0,) * (a.ndim - 1))


def _norm_proj_kernel(x_ref, g_ref, w_ref, wt_ref, o_ref, ot_ref, *, tile):
    xn = _rmsnorm(x_ref[...], g_ref[...]).astype(BF16)
    res = _dot(xn, w_ref[...]).astype(o_ref.dtype)
    if len(o_ref.shape) == 3:
        _store_col_blocks(o_ref, res)
    else:
        o_ref[...] = res
    if wt_ref is not None:
        t = _dot_nt(wt_ref[...], xn).astype(ot_ref.dtype)
        for c in range(t.shape[1] // tile):
            ot_ref[c] = t[:, c * tile:(c + 1) * tile]


def norm_proj(x, g, g_layer, w, w_t=None, layer=0, col_blocks=False, tile=ATT_TILE, tm=ROW_TILE):
    n, d = x.shape
    p = w.shape[2]
    tm = min(tm, n)
    in_specs = [pl.BlockSpec((tm, d), lambda i: (i, 0)), _pick(g, g_layer), _pick(w, layer)]
    if col_blocks:
        out_specs = [pl.BlockSpec((p // LANES, tm, LANES), lambda i: (0, i, 0))]
        out_shape = [jax.ShapeDtypeStruct((p // LANES, n, LANES), BF16)]
    else:
        out_specs = [pl.BlockSpec((tm, p), lambda i: (i, 0))]
        out_shape = [jax.ShapeDtypeStruct((n, p), BF16)]
    args = [x, g, w]
    if w_t is None:
        body = lambda x_ref, g_ref, w_ref, o_ref: _norm_proj_kernel(
            x_ref, g_ref, w_ref, None, o_ref, None, tile=tile)
    else:
        pt = w_t.shape[1]
        in_specs.append(_pick(w_t, layer))
        out_specs.append(pl.BlockSpec((tm // tile, pt, tile), lambda i: (i, 0, 0)))
        out_shape.append(jax.ShapeDtypeStruct((n // tile, pt, tile), BF16))
        args.append(w_t)
        body = functools.partial(_norm_proj_kernel, tile=tile)
    out = pl.pallas_call(
        body, grid=(n // tm,), in_specs=in_specs, out_specs=out_specs, out_shape=out_shape,
        compiler_params=_params("parallel"), name="norm_proj",
    )(*args)
    return out if w_t is not None else out[0]


def _even_proj_kernel(x_ref, g_ref, w_ref, wt_ref, wkvc_ref, wff_ref, bf_ref, wng_ref, tri_ref,
                      main_ref, vt_ref, kc_ref, vc_ref, f_ref, gate_ref, carry_ref, chunk_ref,
                      *, tiles_per_seq, tile):
    i = pl.program_id(0)
    xn = _rmsnorm(x_ref[...], g_ref[...]).astype(BF16)
    _store_col_blocks(main_ref, _dot(xn, w_ref[...]).astype(BF16))
    t = _dot_nt(wt_ref[...], xn).astype(BF16)
    for c in range(t.shape[1] // tile):
        vt_ref[c] = t[:, c * tile:(c + 1) * tile]
    chunks = chunk_ref.shape[1] // CMP_STRIDE
    kvc = _dot(xn, wkvc_ref[...])
    for half, out_ref in enumerate((kc_ref, vc_ref)):
        chunk_ref[half] = kvc[:, half * LANES:(half + 1) * LANES]
        for pos in range(CMP_STRIDE):
            out_ref[:, pos * LANES:(pos + 1) * LANES] = chunk_ref[
                half, pl.ds(pos, chunks, stride=CMP_STRIDE), :].astype(BF16)
    gate_ref[...] = _sigmoid(_dot_nt(wng_ref[...], xn))
    ff = _dot(xn, wff_ref[...]) + bf_ref[...]
    logf = jnp.minimum(ff, 0.0) - jnp.log1p(jnp.exp(-jnp.abs(ff)))

    @pl.when(i % tiles_per_seq == 0)
    def _():
        carry_ref[...] = jnp.zeros_like(carry_ref)

    cs = _dot_01exact(tri_ref[...], logf) + carry_ref[0:1, :]
    carry_ref[...] = jnp.broadcast_to(cs[cs.shape[0] - 1:, :], carry_ref.shape)
    hi, mid, lo = _split3(-LOG2E * cs)
    piece = lax.broadcasted_iota(jnp.int32, cs.shape, 1) % F_PIECES
    f_ref[...] = jnp.where(piece == 0, hi, jnp.where(piece == 1, mid, lo))


def even_proj(x, g, layer, w_main, w_t, w_kvc, w_ff, b_f, w_ng_t, e, tri_l, seq_len,
              tile=ATT_TILE, tm=ROW_TILE):
    n, d = x.shape
    tm = min(tm, seq_len)
    pm = w_main.shape[2]
    pt = w_t.shape[1]
    gr = w_ng_t.shape[1]
    row = lambda width: pl.BlockSpec((tm, width), lambda i: (i, 0))
    chunked = pl.BlockSpec((tm // CMP_STRIDE, CMP_STRIDE * LANES), lambda i: (i, 0))
    chunked_shape = jax.ShapeDtypeStruct((n // CMP_STRIDE, CMP_STRIDE * LANES), BF16)
    return pl.pallas_call(
        functools.partial(_even_proj_kernel, tiles_per_seq=seq_len // tm, tile=tile),
        grid=(n // tm,),
        in_specs=[row(d), _pick(g, layer), _pick(w_main, e), _pick(w_t, e), _pick(w_kvc, e),
                  _pick(w_ff, e), _pick(b_f, e), _pick(w_ng_t, e), _full(tri_l)],
        out_specs=[pl.BlockSpec((pm // LANES, tm, LANES), lambda i: (0, i, 0)),
                   pl.BlockSpec((tm // tile, pt, tile), lambda i: (i, 0, 0)),
                   chunked, chunked, row(LANES),
                   pl.BlockSpec((gr, tm), lambda i: (0, i))],
        out_shape=[jax.ShapeDtypeStruct((pm // LANES, n, LANES), BF16),
                   jax.ShapeDtypeStruct((n // tile, pt, tile), BF16),
                   chunked_shape, chunked_shape,
                   jax.ShapeDtypeStruct((n, LANES), BF16),
                   jax.ShapeDtypeStruct((gr, n), F32)],
        scratch_shapes=[pltpu.VMEM((8, LANES), F32), pltpu.VMEM((2, tm, LANES), F32)],
        compiler_params=_params("arbitrary"),
        name="even_proj",
    )(x, g, w_main, w_t, w_kvc, w_ff, b_f, w_ng_t, tri_l)


def _bucket_lookup_kernel(tab_ref, idx_ref, o_ref):
    h = pl.program_id(0)
    idx = idx_ref[...]
    out = jnp.full(idx.shape, NEG, F32)
    for b in range(N_BUCKETS):
        out = jnp.where(idx == b, tab_ref[b, h], out)
    o_ref[0] = out


def bucket_lookup(tab, idx, rows):
    r, c = idx.shape
    h = tab.shape[1]
    return pl.pallas_call(
        _bucket_lookup_kernel,
        grid=(h, r // rows),
        in_specs=[pl.BlockSpec(memory_space=pltpu.SMEM),
                  pl.BlockSpec((rows, c), lambda hh, j: (j, 0))],
        out_specs=pl.BlockSpec((1, rows, c), lambda hh, j: (hh, j, 0)),
        out_shape=jax.ShapeDtypeStruct((h, r, c), F32),
        compiler_params=_params("parallel", "parallel"),
        name="bucket_lookup",
    )(tab, idx)


def _fox_kernel(q_ref, k_ref, fs_ref, vt_ref, o_ref, stage_ref, *, tile, n_tiles):
    p = pl.program_id(1)
    lane = lax.broadcasted_iota(jnp.int32, (tile, LANES), 1)
    causal = (lax.broadcasted_iota(jnp.int32, (tile, tile), 0)
              <= lax.broadcasted_iota(jnp.int32, (tile, tile), 1))
    lo = _row_lo((LANES, tile))

    def keys(j):
        sl = slice(j * tile, (j + 1) * tile)
        return jnp.concatenate([k_ref[sl, :], fs_ref[sl, :]], axis=1)

    def queries(i):
        heads = _split_pair(q_ref[i * tile:(i + 1) * tile, :], HEAD_DIM ** -0.5)
        return tuple(jnp.concatenate(
            [heads[a], jnp.where(lane // F_PIECES == 2 * p + a, 1.0, 0.0).astype(BF16)], axis=1)
            for a in range(2))

    tasks = [(i, j) for i in range(n_tiles) for j in range(i + 1)]
    qcache = {}

    def scores(t):
        i, j = tasks[t]
        if i not in qcache:
            qcache[i] = queries(i)
        kj = keys(j)
        for a in range(2):
            stage_ref[t % 2, a] = _dot_nt(kj, qcache[i][a])

    scores(0)
    for t, (i, j) in enumerate(tasks):
        if t + 1 < len(tasks):
            scores(t + 1)
        qk = [stage_ref[t % 2, a] for a in range(2)]
        if j == 0:
            stats = [(jnp.full((1, tile), NEG, F32), jnp.zeros((1, tile), F32)) for _ in range(2)]
            acc = [jnp.zeros((LANES, tile), F32) for _ in range(2)]
        vt = vt_ref[j]
        alphas, pvs = [], []
        for a in range(2):
            m_prev, l_prev = stats[a]
            s = jnp.where(causal, qk[a], NEG) if j == i else qk[a]
            m_new = jnp.maximum(m_prev, jnp.max(s, axis=0, keepdims=True))
            alpha = jnp.exp2(m_prev - m_new)
            pr = jnp.exp2(s - m_new)
            stats[a] = (m_new, alpha * l_prev + jnp.sum(pr, axis=0, keepdims=True))
            alphas.append(alpha)
            pvs.append(_dot(vt, pr.astype(BF16)))
        for a in range(2):
            acc[a] = alphas[a] * acc[a] + pvs[a]
        if j == i:
            o = jnp.where(lo, acc[0] / stats[0][1], acc[1] / stats[1][1])
            o_ref[i * tile:(i + 1) * tile, :] = o.T.astype(o_ref.dtype)


def fox_attention(main, fs, vt, batch, seq_len, q_col, k_col, v_row, n_pairs, tile=ATT_TILE):
    nq = seq_len // tile
    n = batch * seq_len
    return pl.pallas_call(
        functools.partial(_fox_kernel, tile=tile, n_tiles=nq),
        grid=(batch, n_pairs),
        in_specs=[pl.BlockSpec((None, seq_len, LANES), lambda b, p: (q_col + p, b, 0)),
                  pl.BlockSpec((None, seq_len, LANES), lambda b, p: (k_col + p, b, 0)),
                  pl.BlockSpec((seq_len, LANES), lambda b, p: (b, 0)),
                  pl.BlockSpec((nq, LANES, tile), lambda b, p: (b, v_row + p, 0))],
        out_specs=pl.BlockSpec((None, seq_len, LANES), lambda b, p: (p, b, 0)),
        out_shape=jax.ShapeDtypeStruct((n_pairs, n, LANES), BF16),
        scratch_shapes=[pltpu.VMEM((2, 2, tile, tile), F32)],
        compiler_params=_params("parallel", "parallel"),
        name="fox_attention",
    )(main, main, fs, vt)


def _sb_kernel(q_ref, k_ref, vt_ref, tri_ref, o_ref, acc_ref, stage_ref, *, tile, n_tiles):
    tri = tri_ref[...]
    strict = (lax.broadcasted_iota(jnp.int32, (tile, tile), 0)
              < lax.broadcasted_iota(jnp.int32, (tile, tile), 1))
    lo = _row_lo((LANES, tile))

    def queries(i):
        return _split_pair(q_ref[i * tile:(i + 1) * tile, :], HEAD_DIM ** -0.5)

    def scores(ks, heads):
        return tuple(_dot_nt(ks, heads[a]) for a in range(2))

    def consume(zz, rs, vt, masked):
        sps, base, later, pvs = [], [], [], []
        for a in range(2):
            z = zz[a]
            sp = jnp.maximum(z, jnp.log(1.0 + jnp.exp(jnp.minimum(z, SOFTPLUS_CLAMP))))
            base.append(z - sp)
            if masked:
                sp = jnp.where(strict, sp, 0.0)
            sps.append(sp)
        for a in range(2):
            later.append(_dot(tri, sps[a].astype(BF16)))
        for a in range(2):
            wgt = jnp.exp(base[a] - later[a] + rs[a])
            if masked:
                wgt = jnp.where(strict, wgt, 0.0)
            pvs.append(_dot(vt, wgt.astype(BF16)))
        return tuple(rs[a] - jnp.sum(sps[a], axis=0, keepdims=True) for a in range(2)), pvs

    def live(rs):
        return (jnp.max(jnp.maximum(rs[0], rs[1])) > EXP_UNDERFLOW).astype(jnp.int32)

    tasks = [(i, j) for i in range(n_tiles) for j in ((i, i - 1) if i else (i,))]
    qcache = {}

    def task_scores(t):
        i, j = tasks[t]
        if i not in qcache:
            qcache[i] = queries(i)
        zz = scores(k_ref[j * tile:(j + 1) * tile, :], qcache[i])
        for a in range(2):
            stage_ref[t % 2, a] = zz[a]

    zero = jnp.zeros((1, tile), F32)
    survival = {}
    task_scores(0)
    for t, (i, j) in enumerate(tasks):
        if t + 1 < len(tasks):
            task_scores(t + 1)
        zz = [stage_ref[t % 2, a] for a in range(2)]
        if j == i:
            rs, acc = (zero, zero), None
        rs, pvs = consume(zz, rs, vt_ref[j], j == i)
        acc = pvs if acc is None else [acc[a] + pvs[a] for a in range(2)]
        if j == max(i - 1, 0):
            for a in range(2):
                acc_ref[i, a] = acc[a]
            survival[i] = rs

    for i in range(2, n_tiles):
        heads = qcache[i]

        def cond(state):
            j, alive, _ = state
            return (j >= 0) & (alive > 0)

        def body(state, i=i, heads=heads):
            j, _, rs = state
            rs, pvs = consume(scores(k_ref[pl.ds(j * tile, tile), :], heads), rs, vt_ref[j], False)
            for a in range(2):
                acc_ref[i, a] = acc_ref[i, a] + pvs[a]
            return j - 1, live(rs), rs

        lax.while_loop(cond, body, (i - 2, live(survival[i]), survival[i]))

    for i in range(n_tiles):
        o = jnp.where(lo, acc_ref[i, 0], acc_ref[i, 1])
        o_ref[i * tile:(i + 1) * tile, :] = o.T.astype(o_ref.dtype)


def sb_attention(main, vt, batch, seq_len, q_col, k_col, n_pairs, tri, tile=ATT_TILE):
    nq = seq_len // tile
    n = batch * seq_len
    return pl.pallas_call(
        functools.partial(_sb_kernel, tile=tile, n_tiles=nq),
        grid=(batch, n_pairs),
        in_specs=[pl.BlockSpec((None, seq_len, LANES), lambda b, p: (q_col + p, b, 0)),
                  pl.BlockSpec((None, seq_len, LANES), lambda b, p: (k_col + p, b, 0)),
                  pl.BlockSpec((nq, LANES, tile), lambda b, p: (b, p, 0)),
                  _full(tri)],
        out_specs=pl.BlockSpec((None, seq_len, LANES), lambda b, p: (p, b, 0)),
        out_shape=jax.ShapeDtypeStruct((n_pairs, n, LANES), BF16),
        scratch_shapes=[pltpu.VMEM((nq, 2, LANES, tile), F32), pltpu.VMEM((2, 2, tile, tile), F32)],
        compiler_params=_params("parallel", "parallel"),
        name="sb_attention",
    )(main, main, vt, tri)


def _gelu_tanh(x):
    return 0.5 * x * (1.0 + jnp.tanh(math.sqrt(2.0 / math.pi) * (x + 0.044715 * (x * x * x))))


def _compress_kernel(xk_ref, xv_ref, pek_ref, pev_ref, w1k_ref, w1v_ref, w2k_ref, w2v_ref,
                     ok_ref, ov_ref, *, n_cmp):
    def one(x_ref, pe_ref, w1_ref, w2_ref):
        x = x_ref[...].astype(F32)
        xa = (x + pe_ref[0:1, :]).astype(BF16)
        xb = (x + pe_ref[1:2, :]).astype(BF16)
        ha = _dot(xa, w1_ref[0])
        hb = _dot(xb, w1_ref[1])
        rows = ha.shape[0]
        hb = pltpu.roll(hb, rows - 1, 0)
        h = _gelu_tanh(ha + hb)
        out = _dot(h.astype(BF16), w2_ref[...])
        ridx = lax.broadcasted_iota(jnp.int32, out.shape, 0)
        return jnp.where(ridx < n_cmp, out, 0.0)

    ok_ref[0] = one(xk_ref, pek_ref, w1k_ref, w2k_ref).astype(ok_ref.dtype)
    ov_ref[0] = one(xv_ref, pev_ref, w1v_ref, w2v_ref).T.astype(ov_ref.dtype)


def compress(xk, xv, pek, pev, w1k, w1v, w2k, w2v, batch, n_cmp):
    chunks = xk.shape[0] // batch
    row = pl.BlockSpec((chunks, xk.shape[1]), lambda b: (b, 0))
    return pl.pallas_call(
        functools.partial(_compress_kernel, n_cmp=n_cmp),
        grid=(batch,),
        in_specs=[row, row, _full(pek), _full(pev), _full(w1k), _full(w1v), _full(w2k), _full(w2v)],
        out_specs=[pl.BlockSpec((1, chunks, LANES), lambda b: (b, 0, 0)),
                   pl.BlockSpec((1, LANES, chunks), lambda b: (b, 0, 0))],
        out_shape=[jax.ShapeDtypeStruct((batch, chunks, LANES), BF16),
                   jax.ShapeDtypeStruct((batch, LANES, chunks), BF16)],
        compiler_params=_params("parallel"),
        name="nsa_compress",
    )(xk, xv, pek, pev, w1k, w1v, w2k, w2v)


def _nsa_cmp_kernel(q_ref, kc_ref, vct_ref, cb_ref, ov_ref, ocmp_ref, sel_ref, stage_ref, *, tile, n_blocks):
    i = pl.program_id(1)
    kc = kc_ref[0]
    vct = vct_ref[0]
    pcsum = [jnp.zeros((LANES, tile), F32), jnp.zeros((LANES, tile), F32)]
    lo = _row_lo((LANES, tile))
    for r in range(NSA_GQA):
        heads = _split_pair(q_ref[r], HEAD_DIM ** -0.5)
        for a in range(2):
            stage_ref[2 * r + a] = _dot_nt(kc, heads[a])
    for r in range(NSA_GQA):
        outs = []
        for a in range(2):
            bias = cb_ref[2 * r + a]
            valid = bias > 0.5 * NEG
            s = stage_ref[2 * r + a] + bias
            m = jnp.max(s, axis=0, keepdims=True)
            pr = jnp.where(valid, jnp.exp2(s - m), 0.0)
            l = jnp.sum(pr, axis=0, keepdims=True)
            pc = pr / jnp.where(l > 0.0, l, 1.0)
            pcsum[a] = pcsum[a] + pc
            outs.append(_dot(vct, pc.astype(BF16)))
        ocmp_ref[r * LANES:(r + 1) * LANES, :] = jnp.where(lo, outs[0], outs[1])

    score = _dot_01exact(ov_ref[0], pcsum[0]) + _dot_01exact(ov_ref[1], pcsum[1])
    score = score[:2 * n_blocks]
    shape = (2 * n_blocks, tile)
    row = lax.broadcasted_iota(jnp.int32, shape, 0)
    qpos = i * tile + lax.broadcasted_iota(jnp.int32, shape, 1)
    blk = row % n_blocks
    cur = qpos // SLC_BLOCK
    forced = (blk == 0) | (blk == cur) | (blk == cur - 1)
    future = blk > cur
    score = jnp.where(future, -1.0, jnp.where(forced, FORCED_SCORE, score))
    grp0 = row < n_blocks
    cnt = jnp.zeros(shape, F32)
    for mth in range(n_blocks):
        other = jnp.where(grp0, score[mth:mth + 1, :], score[n_blocks + mth:n_blocks + mth + 1, :])
        ahead = (other > score) | ((other == score) & (blk > mth))
        cnt = cnt + jnp.where(ahead, 1.0, 0.0)
    n_top = min(SLC_TOPK, n_blocks)
    sel_ref[...] = jnp.where(cnt < n_top, 0.0, NEG)


def nsa_cmp_select(main, kc, vct, cmp_bias, overlap2, batch, seq_len, q_col, tile=ATT_TILE):
    nq = seq_len // tile
    n = batch * seq_len
    n_blocks = seq_len // SLC_BLOCK
    qw = NSA_GQA * LANES
    return pl.pallas_call(
        functools.partial(_nsa_cmp_kernel, tile=tile, n_blocks=n_blocks),
        grid=(batch, nq),
        in_specs=[pl.BlockSpec((NSA_GQA, tile, LANES), lambda b, i: (q_col // NSA_GQA, b * nq + i, 0)),
                  pl.BlockSpec((1,) + kc.shape[1:], lambda b, i: (b, 0, 0)),
                  pl.BlockSpec((1,) + vct.shape[1:], lambda b, i: (b, 0, 0)),
                  pl.BlockSpec((NSA_HEADS, cmp_bias.shape[1], tile), lambda b, i: (0, 0, i)),
                  _full(overlap2)],
        out_specs=[pl.BlockSpec((qw, tile), lambda b, i: (0, b * nq + i)),
                   pl.BlockSpec((2 * n_blocks, tile), lambda b, i: (0, b * nq + i))],
        out_shape=[jax.ShapeDtypeStruct((qw, n), F32),
                   jax.ShapeDtypeStruct((2 * n_blocks, n), F32)],
        scratch_shapes=[pltpu.VMEM((NSA_HEADS, kc.shape[1], tile), F32)],
        compiler_params=_params("parallel", "parallel"),
        name="nsa_cmp_select",
    )(main, kc, vct, cmp_bias, overlap2)


def _nsa_main_kernel(far_ref, q_ref, sel_ref, ocmp_ref, gate_ref, ks_ref, vst_ref, kw_ref, vwt_ref, tz_ref,
                     o_ref, stage_ref, *, tile, n_tiles, n_blocks):
    r = pl.program_id(1)
    lo = _row_lo((LANES, tile))
    blocks_per_tile = tile // SLC_BLOCK
    far = tuple(far_ref[2 * r + a] for a in range(2))

    def queries(i):
        return _split_pair(q_ref[i * tile:(i + 1) * tile, :], HEAD_DIM ** -0.5)

    def chosen(a, i, j, offset=None):
        rows = []
        for nb in range(blocks_per_tile):
            blk = a * n_blocks + j * blocks_per_tile + nb
            row = sel_ref[blk:blk + 1, i * tile:(i + 1) * tile]
            if offset is not None:
                row = row + offset
            rows.append(jnp.broadcast_to(row, (SLC_BLOCK, tile)))
        return jnp.concatenate(rows, axis=0)

    tasks = []
    for i in range(n_tiles):
        tasks += [("slc", i, j) for j in range(i + 1)]
        tasks += [("win", i, j) for j in ((i - 1, i) if i else (i,))]
    qcache = {}

    def scores(t):
        kind, i, j = tasks[t]
        if i not in qcache:
            qcache[i] = queries(i)
        k_ref_ = ks_ref if kind == "slc" else kw_ref
        kj = k_ref_[j * tile:(j + 1) * tile, :]
        for a in range(2):
            stage_ref[t % 2, a] = _dot_nt(kj, qcache[i][a])

    def fresh():
        return ([(jnp.full((1, tile), NEG, F32), jnp.zeros((1, tile), F32)) for _ in range(2)],
                [jnp.zeros((LANES, tile), F32) for _ in range(2)])

    scores(0)
    for t, (kind, i, j) in enumerate(tasks):
        if t + 1 < len(tasks):
            scores(t + 1)
        qk = [stage_ref[t % 2, a] for a in range(2)]
        first = j == 0 if kind == "slc" else j == max(i - 1, 0)
        if first:
            stats, acc = fresh()
        vt = (vst_ref if kind == "slc" else vwt_ref)[j]
        alphas, pvs = [], []
        for a in range(2):
            if j == i:
                s = qk[a] + tz_ref[a, 0]
            elif j == i - 1:
                s = qk[a] + tz_ref[a, 1 if kind == "slc" else 2]
            if kind == "slc":
                s = s + chosen(a, i, j) if j >= i - 1 else qk[a] + chosen(a, i, j, far[a])
            m_prev, l_prev = stats[a]
            m_new = jnp.maximum(m_prev, jnp.max(s, axis=0, keepdims=True))
            alpha = jnp.exp2(m_prev - m_new)
            pr = jnp.exp2(s - m_new)
            stats[a] = (m_new, alpha * l_prev + jnp.sum(pr, axis=0, keepdims=True))
            alphas.append(alpha)
            pvs.append(_dot(vt, pr.astype(BF16)))
        for a in range(2):
            acc[a] = alphas[a] * acc[a] + pvs[a]
        if j == i:
            branch = jnp.where(lo, acc[0] / stats[0][1], acc[1] / stats[1][1])
            if kind == "slc":
                o_slc = branch
            else:
                cols = slice(i * tile, (i + 1) * tile)
                gexp = [jnp.where(lo, gate_ref[2 * c:2 * c + 1, cols], gate_ref[2 * c + 1:2 * c + 2, cols])
                        for c in range(3)]
                out = gexp[0] * ocmp_ref[:, cols] + gexp[1] * o_slc + gexp[2] * branch
                o_ref[cols, :] = out.T.astype(o_ref.dtype)


def nsa_main(far_bias, main, sel, ocmp, gates, vt, tz, batch, seq_len, q_col, ks_col, kw_col, vs_row, vw_row,
             tile=ATT_TILE):
    nq = seq_len // tile
    n = batch * seq_len
    n_blocks = seq_len // SLC_BLOCK
    kv = lambda col: pl.BlockSpec((None, seq_len, LANES), lambda b, r: (col, b, 0))
    vts = lambda rowblk: pl.BlockSpec((nq, LANES, tile), lambda b, r: (b, rowblk, 0))
    return pl.pallas_call(
        functools.partial(_nsa_main_kernel, tile=tile, n_tiles=nq, n_blocks=n_blocks),
        grid=(batch, NSA_GQA),
        in_specs=[pl.BlockSpec(memory_space=pltpu.SMEM),
                  pl.BlockSpec((None, seq_len, LANES), lambda b, r: (q_col + r, b, 0)),
                  pl.BlockSpec((2 * n_blocks, seq_len), lambda b, r: (0, b)),
                  pl.BlockSpec((LANES, seq_len), lambda b, r: (r, b)),
                  pl.BlockSpec((GATE_ROWS, seq_len), lambda b, r: (r, b)),
                  kv(ks_col), vts(vs_row), kv(kw_col), vts(vw_row),
                  pl.BlockSpec((2, 3, tile, tile), lambda b, r: (r, 0, 0, 0))],
        out_specs=pl.BlockSpec((None, seq_len, LANES), lambda b, r: (r, b, 0)),
        out_shape=jax.ShapeDtypeStruct((NSA_GQA, n, LANES), BF16),
        scratch_shapes=[pltpu.VMEM((2, 2, tile, tile), F32)],
        compiler_params=_params("parallel", "parallel"),
        name="nsa_main",
    )(far_bias, main, sel, ocmp, gates, main, vt, main, vt, tz)


def _layer_tail_kernel(*refs, n_in, final_norm):
    a_refs = refs[:n_in]
    w_refs = refs[n_in:2 * n_in]
    (x_ref, gx_ref, wq_ref, kv_ref, wo_ref, gm_ref, w1_ref, w2_ref, gf_ref,
     o_ref, hn_ref, acc_ref) = refs[2 * n_in:]
    f = pl.program_id(1)

    @pl.when(f == 0)
    def _():
        x1 = x_ref[...]
        for a_ref, w_ref in zip(a_refs, w_refs):
            a = jnp.concatenate([a_ref[c] for c in range(a_ref.shape[0])], axis=1)
            x1 = x1 + _dot(a, w_ref[...])
        q = _dot(_rmsnorm(x1, gx_ref[...]).astype(BF16), wq_ref[...]).astype(BF16)
        width = XA_HEADS * XA_HEAD_DIM
        outs = []
        for h in range(XA_HEADS):
            hs = slice(h * XA_HEAD_DIM, (h + 1) * XA_HEAD_DIM)
            s = _dot_nt(q[:, hs], kv_ref[:, hs]) * (XA_HEAD_DIM ** -0.5)
            m = jnp.max(s, axis=-1, keepdims=True)
            p = jnp.exp(s - m)
            l = jnp.sum(p, axis=-1, keepdims=True)
            v = kv_ref[:, width + h * XA_HEAD_DIM:width + (h + 1) * XA_HEAD_DIM]
            outs.append((_dot(p.astype(BF16), v) / l).astype(BF16))
        x2 = x1 + _dot(jnp.concatenate(outs, axis=-1), wo_ref[...])
        hn_ref[...] = _rmsnorm(x2, gm_ref[...]).astype(BF16)
        acc_ref[...] = x2

    h = jnp.maximum(_dot(hn_ref[...], w1_ref[...]), 0.0)
    acc_ref[...] += _dot((h * h).astype(BF16), w2_ref[...])

    @pl.when(f == pl.num_programs(1) - 1)
    def _():
        y = acc_ref[...]
        if final_norm:
            y = _rmsnorm(y, gf_ref[...])
        o_ref[...] = y


def layer_tail(a_list, w_list, w_idx, x, layer, gx, wq, kv, wo, gm, w1, w2, gf, final_norm, seq_len, mem_len,
               tm=MLP_ROW_TILE, tf=MLP_FF_TILE):
    n, d = x.shape
    ff = w1.shape[2]
    tm = min(tm, seq_len)
    tf = min(tf, ff)
    tiles_per_seq = seq_len // tm
    once = pl.Buffered(1)
    row = lambda a: pl.BlockSpec((tm, a.shape[1]), lambda i, f: (i, 0))
    pick = lambda a, k: pl.BlockSpec((None,) + a.shape[1:], lambda i, f: (k,) + (0,) * (a.ndim - 1),
                                     pipeline_mode=once)
    return pl.pallas_call(
        functools.partial(_layer_tail_kernel, n_in=len(a_list), final_norm=final_norm),
        grid=(n // tm, ff // tf),
        in_specs=[pl.BlockSpec((a.shape[0], tm, LANES), lambda i, f: (0, i, 0)) for a in a_list]
        + [pick(w, w_idx) for w in w_list]
        + [row(x), pick(gx, layer), pick(wq, layer),
           pl.BlockSpec((mem_len, 2 * XA_HEADS * XA_HEAD_DIM), lambda i, f: (i // tiles_per_seq, layer)),
           pick(wo, layer), pick(gm, layer),
           pl.BlockSpec((None, d, tf), lambda i, f: (layer, 0, f)),
           pl.BlockSpec((None, tf, d), lambda i, f: (layer, f, 0)),
           pl.BlockSpec(gf.shape, lambda i, f: (0, 0), pipeline_mode=once)],
        out_specs=pl.BlockSpec((tm, d), lambda i, f: (i, 0)),
        out_shape=jax.ShapeDtypeStruct((n, d), F32),
        scratch_shapes=[pltpu.VMEM((tm, d), BF16), pltpu.VMEM((tm, d), F32)],
        compiler_params=_params("parallel", "arbitrary"),
        name="layer_tail",
    )(*a_list, *w_list, x, gx, wq, kv, wo, gm, w1, w2, gf)


def _static_tables(seq_len):
    tile = ATT_TILE
    n_chunks = seq_len // CMP_STRIDE
    n_cmp = (seq_len - CMP_BLOCK) // CMP_STRIDE + 1
    n_slc = seq_len // SLC_BLOCK
    assert n_chunks == LANES and 2 * n_slc <= LANES and seq_len % tile == 0 and WINDOW == tile
    c = np.arange(n_chunks)[:, None]
    t = np.arange(seq_len)[None, :]
    cdist = t - (c * CMP_STRIDE + CMP_BLOCK - 1)
    cmp_idx = np.where((cdist >= 0) & (c < n_cmp), _t5_bucket_np(cdist), -1).astype(np.int32)
    sl = np.arange(tile)[:, None]
    tl = np.arange(tile)[None, :]
    prev = _t5_bucket_np(tl - sl + tile)
    tz_idx = np.concatenate([np.where(sl <= tl, _t5_bucket_np(tl - sl), -1), prev,
                             np.where(sl > tl, prev, -1)], axis=0).astype(np.int32)
    far_bucket = int(_t5_bucket_np(np.array([tile + 1]))[0])
    assert far_bucket == int(_t5_bucket_np(np.array([seq_len]))[0])
    cmp_start = np.arange(n_cmp) * CMP_STRIDE
    cmp_stop = cmp_start + CMP_BLOCK - 1
    slc_start = np.arange(n_slc) * SLC_BLOCK
    slc_stop = slc_start + SLC_BLOCK - 1
    ov = ((cmp_start[:, None] <= slc_stop[None, :]) & (cmp_stop[:, None] >= slc_start[None, :]))
    overlap2 = np.zeros((2, LANES, LANES), np.float32)
    for a in range(2):
        overlap2[a, a * n_slc:(a + 1) * n_slc, :n_cmp] = ov.T
    return cmp_idx, tz_idx, far_bucket, overlap2, n_cmp


def _tri_prefix(nn):
    s = np.arange(nn)[:, None]
    j = np.arange(nn)[None, :]
    return (j <= s).astype(np.float32)


def _tri_later(nn):
    s = np.arange(nn)[:, None]
    j = np.arange(nn)[None, :]
    return (j > s).astype(np.float32)


_NSA_HEAD_PERM = [a * NSA_GQA + r for r in range(NSA_GQA) for a in range(NSA_GROUPS)]


def _block_diag2(m):
    z = jnp.zeros_like(m)
    return jnp.concatenate([jnp.concatenate([m, z], axis=1), jnp.concatenate([z, m], axis=1)], axis=0)


def _compress_weights(pe, w1, w2):
    half = CMP_BLOCK // 2
    pe2 = jnp.concatenate([pe, pe], axis=1).reshape(2, half * LANES)
    w1e = jax.vmap(_block_diag2)(w1).reshape(2, half * LANES, LANES)
    return pe2.astype(F32), w1e.astype(BF16), _block_diag2(w2).astype(BF16)


def kernel(x, mem, rel_bias, mem_norm_g, norm_mix_g, norm_xattn_g, norm_mlp_g, final_norm_g, w_in_even, b_forget, cmp_pe_k, cmp_w1_k, cmp_w2_k, cmp_pe_v, cmp_w1_v, cmp_w2_v, w_out_even, w_in_odd, w_out_odd, xa_wq, xa_wkv, xa_wo, mlp_w1, mlp_w2):
    batch, seq_len, d = x.shape
    mem_len = mem.shape[1]
    depth = norm_mix_g.shape[0]
    n = batch * seq_len
    fox_w = FOX_HEADS * HEAD_DIM
    nsa_w = NSA_HEADS * HEAD_DIM
    kv_w = NSA_GROUPS * HEAD_DIM
    splits = np.cumsum([fox_w, fox_w, fox_w, FOX_HEADS, nsa_w, kv_w, kv_w, kv_w, kv_w, kv_w, kv_w])

    cmp_idx, tz_idx, far_bucket, overlap2, n_cmp = _static_tables(seq_len)
    tab = LOG2E * rel_bias.astype(F32)[:, _NSA_HEAD_PERM]
    cmp_bias = bucket_lookup(tab, jnp.asarray(cmp_idx), rows=cmp_idx.shape[0])
    tz = bucket_lookup(tab, jnp.asarray(tz_idx), rows=ATT_TILE)
    tz = tz.reshape(NSA_HEADS, 3, ATT_TILE, ATT_TILE)
    far_bias = tab[far_bucket]
    overlap2 = jnp.asarray(overlap2, BF16)
    tri_prefix = jnp.asarray(_tri_prefix(min(ROW_TILE, seq_len)), BF16)
    tri_later = jnp.asarray(_tri_later(ATT_TILE), BF16)

    xf = x.reshape(n, d)
    memf = mem.reshape(batch * mem_len, d)
    row = lambda v: v.reshape(1, -1).astype(F32)
    rows = lambda v: v.reshape(v.shape[0], 1, -1).astype(F32)
    g_mix_all, g_xa_all, g_mlp_all = rows(norm_mix_g), rows(norm_xattn_g), rows(norm_mlp_g)
    g_mem, g_final = row(mem_norm_g), row(final_norm_g)

    ne = w_in_even.shape[0]
    (w_fq, w_fk, w_fv, w_ff, w_nq, w_kc, w_vc, w_ks, w_vs, w_kw, w_vw, w_ng) = jnp.split(w_in_even, splits, axis=2)
    w_fq = LOG2E * w_fq
    w_nq = LOG2E * w_nq.reshape(ne, d, NSA_HEADS, HEAD_DIM)[:, :, _NSA_HEAD_PERM, :].reshape(ne, d, nsa_w)
    w_main_all = jnp.concatenate([w_fq, w_fk, w_nq, w_ks, w_kw], axis=2).astype(BF16)
    w_t_all = jnp.concatenate([w_fv, w_vs, w_vw], axis=2).transpose(0, 2, 1).astype(BF16)
    w_kvc_all = jnp.concatenate([w_kc, w_vc], axis=2).astype(BF16)
    w_ng_t = w_ng.reshape(ne, d, NSA_GROUPS, NSA_GQA, 3).transpose(0, 3, 4, 2, 1).reshape(ne, NSA_GQA, 6, d)
    w_ng_all = jnp.pad(w_ng_t, ((0, 0), (0, 0), (0, GATE_ROWS - 6), (0, 0))).reshape(
        ne, NSA_GQA * GATE_ROWS, d).astype(BF16)
    pad_f = LANES - F_PIECES * FOX_HEADS
    w_ff_all = jnp.pad(jnp.repeat(w_ff, F_PIECES, axis=2), ((0, 0), (0, 0), (0, pad_f))).astype(BF16)
    b_f_all = jnp.pad(jnp.repeat(b_forget.astype(F32), F_PIECES, axis=1), ((0, 0), (0, pad_f))).reshape(ne, 1, LANES)
    cmp_k = jax.vmap(_compress_weights)(cmp_pe_k, cmp_w1_k, cmp_w2_k)
    cmp_v = jax.vmap(_compress_weights)(cmp_pe_v, cmp_w1_v, cmp_w2_v)
    w_o_fox_all = w_out_even[:, :fox_w].astype(BF16)
    w_o_nsa_all = w_out_even[:, fox_w:].reshape(ne, NSA_HEADS, HEAD_DIM, d)[:, _NSA_HEAD_PERM].reshape(
        ne, nsa_w, d).astype(BF16)
    sb_w = SB_HEADS * HEAD_DIM
    w_odd_qk_all = w_in_odd[:, :, :2 * sb_w].astype(BF16)
    w_odd_vt_all = w_in_odd[:, :, 2 * sb_w:].transpose(0, 2, 1).astype(BF16)
    w_o_odd_all = w_out_odd.astype(BF16)
    xa_wq_all, xa_wo_all = xa_wq.astype(BF16), xa_wo.astype(BF16)
    mlp_w1_all, mlp_w2_all = mlp_w1.astype(BF16), mlp_w2.astype(BF16)
    xa_wkv_cat = xa_wkv.transpose(1, 0, 2).reshape(d, depth * xa_wkv.shape[2]).astype(BF16)
    kv_mem = norm_proj(memf, g_mem[None], 0, xa_wkv_cat[None])

    for layer in range(depth):
        if layer % 2 == 0:
            e = layer // 2
            main, vt, kc_in, vc_in, fs, gates = even_proj(
                xf, g_mix_all, layer, w_main_all, w_t_all, w_kvc_all, w_ff_all, b_f_all, w_ng_all, e,
                tri_prefix, seq_len)
            nblk = fox_w // LANES
            fox = fox_attention(main, fs, vt, batch, seq_len, 0, nblk, 0, FOX_HEADS // 2)

            kc, vct = compress(kc_in, vc_in,
                               cmp_k[0][e], cmp_v[0][e], cmp_k[1][e], cmp_v[1][e], cmp_k[2][e], cmp_v[2][e],
                               batch, n_cmp)
            q_col = 2 * nblk
            ocmp, sel = nsa_cmp_select(main, kc, vct, cmp_bias, overlap2, batch, seq_len, q_col)
            nsa = nsa_main(far_bias, main, sel, ocmp, gates, vt, tz, batch, seq_len, q_col,
                           q_col + NSA_GQA, q_col + NSA_GQA + 1, nblk, nblk + 1)
            a_list, w_list, w_idx = [fox, nsa], [w_o_fox_all, w_o_nsa_all], e
        else:
            o = layer // 2
            main, vt = norm_proj(xf, g_mix_all, layer, w_odd_qk_all, w_odd_vt_all, o, col_blocks=True)
            nblk = sb_w // LANES
            sb = sb_attention(main, vt, batch, seq_len, 0, nblk, SB_HEADS // 2, tri_later)
            a_list, w_list, w_idx = [sb], [w_o_odd_all], o

        xf = layer_tail(a_list, w_list, w_idx, xf, layer, g_xa_all, xa_wq_all, kv_mem, xa_wo_all, g_mlp_all,
                        mlp_w1_all, mlp_w2_all, g_final, layer == depth - 1, seq_len, mem_len)
    return xf.reshape(batch, seq_len, d)
```

```python
import functools
import math

import numpy as np
import jax
import jax.numpy as jnp
from jax import lax
from jax.experimental import pallas as pl
from jax.experimental.pallas import tpu as pltpu

F32 = jnp.float32
BF16 = jnp.bfloat16

LANES = 128
HEAD_DIM = 64
FOX_HEADS = 8
NSA_HEADS = 8
NSA_GROUPS = 2
NSA_GQA = NSA_HEADS // NSA_GROUPS
SB_HEADS = 16
CMP_BLOCK = 32
CMP_STRIDE = 16
SLC_BLOCK = 64
SLC_TOPK = 8
WINDOW = 256
N_BUCKETS = 32
MAX_DISTANCE = 128
XA_HEADS = 4
XA_HEAD_DIM = 128
EPS = 1e-6
NEG = -1e30
FORCED_SCORE = 1e4
EXP_UNDERFLOW = -104.0
LOG2E = math.log2(math.e)
KEY_SPLIT = 2
SOFTPLUS_CLAMP = 80.0

ATT_TILE = 256
ROW_TILE = 512
MLP_ROW_TILE = 1024
MLP_FF_TILE = 1024
GATE_ROWS = 8
F_PIECES = 3


def _dot(a, b):
    return jnp.dot(a, b, preferred_element_type=F32)


def _dot_nt(a, b):
    return lax.dot_general(a, b, (((1,), (1,)), ((), ())), preferred_element_type=F32)


def _split3(x):
    hi = x.astype(BF16)
    r1 = x - hi.astype(F32)
    mid = r1.astype(BF16)
    lo = (r1 - mid.astype(F32)).astype(BF16)
    return hi, mid, lo


def _dot_exact01(x, m01):
    hi, mid, lo = _split3(x)
    return _dot(hi, m01) + _dot(mid, m01) + _dot(lo, m01)


def _dot_01exact(m01, x):
    hi, mid, lo = _split3(x)
    return _dot(m01, hi) + _dot(m01, mid) + _dot(m01, lo)


def _rmsnorm(x, g):
    ms = jnp.mean(x * x, axis=-1, keepdims=True)
    return x * lax.rsqrt(ms + EPS) * g


def _sigmoid(x):
    return 1.0 / (1.0 + jnp.exp(-x))


def _softplus(x):
    return jnp.maximum(x, 0.0) + jnp.log1p(jnp.exp(-jnp.abs(x)))


def _split_pair(q, scale):
    lo = lax.broadcasted_iota(jnp.int32, q.shape, 1) < HEAD_DIM
    qs = q * jnp.asarray(scale, q.dtype)
    zero = jnp.zeros_like(qs)
    return jnp.where(lo, qs, zero), jnp.where(lo, zero, qs)


def _row_lo(shape):
    return lax.broadcasted_iota(jnp.int32, shape, 0) < HEAD_DIM


def _t5_bucket_np(dist):
    dist = np.maximum(dist, 0)
    max_exact = N_BUCKETS // 2
    d_f = np.maximum(dist, 1).astype(np.float64)
    large = max_exact + (np.log(d_f / max_exact) / math.log(MAX_DISTANCE / max_exact)
                         * (N_BUCKETS - max_exact)).astype(np.int32)
    large = np.minimum(large, N_BUCKETS - 1)
    return np.where(dist < max_exact, dist, large).astype(np.int32)


def _params(*sem):
    return pltpu.CompilerParams(dimension_semantics=sem)


def _full(a):
    return pl.BlockSpec(a.shape, lambda *_: (0,) * a.ndim)


def _store_col_blocks(o_ref, val):
    for c in range(o_ref.shape[0]):
        o_ref[c] = val[:, c * LANES:(c + 1) * LANES]


def _pick(a, k):
    return pl.BlockSpec((None,) + a.shape[1:], lambda *_: (k,) + (0,) * (a.ndim - 1))


def _norm_proj_kernel(x_ref, g_ref, w_ref, wt_ref, o_ref, ot_ref, *, tile):
    xn = _rmsnorm(x_ref[...], g_ref[...]).astype(BF16)
    res = _dot(xn, w_ref[...]).astype(o_ref.dtype)
    if len(o_ref.shape) == 3:
        _store_col_blocks(o_ref, res)
    else:
        o_ref[...] = res
    if wt_ref is not None:
        t = _dot_nt(wt_ref[...], xn).astype(ot_ref.dtype)
        for c in range(t.shape[1] // tile):
            ot_ref[c] = t[:, c * tile:(c + 1) * tile]


def norm_proj(x, g, g_layer, w, w_t=None, layer=0, col_blocks=False, tile=ATT_TILE, tm=ROW_TILE):
    n, d = x.shape
    p = w.shape[2]
    tm = min(tm, n)
    in_specs = [pl.BlockSpec((tm, d), lambda i: (i, 0)), _pick(g, g_layer), _pick(w, layer)]
    if col_blocks:
        out_specs = [pl.BlockSpec((p // LANES, tm, LANES), lambda i: (0, i, 0))]
        out_shape = [jax.ShapeDtypeStruct((p // LANES, n, LANES), BF16)]
    else:
        out_specs = [pl.BlockSpec((tm, p), lambda i: (i, 0))]
        out_shape = [jax.ShapeDtypeStruct((n, p), BF16)]
    args = [x, g, w]
    if w_t is None:
        body = lambda x_ref, g_ref, w_ref, o_ref: _norm_proj_kernel(
            x_ref, g_ref, w_ref, None, o_ref, None, tile=tile)
    else:
        pt = w_t.shape[1]
        in_specs.append(_pick(w_t, layer))
        out_specs.append(pl.BlockSpec((tm // tile, pt, tile), lambda i: (i, 0, 0)))
        out_shape.append(jax.ShapeDtypeStruct((n // tile, pt, tile), BF16))
        args.append(w_t)
        body = functools.partial(_norm_proj_kernel, tile=tile)
    out = pl.pallas_call(
        body, grid=(n // tm,), in_specs=in_specs, out_specs=out_specs, out_shape=out_shape,
        compiler_params=_params("parallel"), name="norm_proj",
    )(*args)
    return out if w_t is not None else out[0]


def _even_proj_kernel(x_ref, g_ref, w_ref, wt_ref, wkvc_ref, wff_ref, bf_ref, wng_ref, tri_ref,
                      main_ref, vt_ref, kc_ref, vc_ref, f_ref, gate_ref, carry_ref, chunk_ref,
                      *, tiles_per_seq, tile):
    i = pl.program_id(0)
    xn = _rmsnorm(x_ref[...], g_ref[...]).astype(BF16)
    _store_col_blocks(main_ref, _dot(xn, w_ref[...]).astype(BF16))
    t = _dot_nt(wt_ref[...], xn).astype(BF16)
    for c in range(t.shape[1] // tile):
        vt_ref[c] = t[:, c * tile:(c + 1) * tile]
    chunks = chunk_ref.shape[1] // CMP_STRIDE
    kvc = _dot(xn, wkvc_ref[...])
    for half, out_ref in enumerate((kc_ref, vc_ref)):
        chunk_ref[half] = kvc[:, half * LANES:(half + 1) * LANES]
        for pos in range(CMP_STRIDE):
            out_ref[:, pos * LANES:(pos + 1) * LANES] = chunk_ref[
                half, pl.ds(pos, chunks, stride=CMP_STRIDE), :].astype(BF16)
    gate_ref[...] = _sigmoid(_dot_nt(wng_ref[...], xn))
    ff = _dot(xn, wff_ref[...]) + bf_ref[...]
    logf = jnp.minimum(ff, 0.0) - jnp.log1p(jnp.exp(-jnp.abs(ff)))

    @pl.when(i % tiles_per_seq == 0)
    def _():
        carry_ref[...] = jnp.zeros_like(carry_ref)

    cs = _dot_01exact(tri_ref[...], logf) + carry_ref[0:1, :]
    carry_ref[...] = jnp.broadcast_to(cs[cs.shape[0] - 1:, :], carry_ref.shape)
    hi, mid, lo = _split3(-LOG2E * cs)
    piece = lax.broadcasted_iota(jnp.int32, cs.shape, 1) % F_PIECES
    f_ref[...] = jnp.where(piece == 0, hi, jnp.where(piece == 1, mid, lo))


def even_proj(x, g, layer, w_main, w_t, w_kvc, w_ff, b_f, w_ng_t, e, tri_l, seq_len,
              tile=ATT_TILE, tm=ROW_TILE):
    n, d = x.shape
    tm = min(tm, seq_len)
    pm = w_main.shape[2]
    pt = w_t.shape[1]
    gr = w_ng_t.shape[1]
    row = lambda width: pl.BlockSpec((tm, width), lambda i: (i, 0))
    chunked = pl.BlockSpec((tm // CMP_STRIDE, CMP_STRIDE * LANES), lambda i: (i, 0))
    chunked_shape = jax.ShapeDtypeStruct((n // CMP_STRIDE, CMP_STRIDE * LANES), BF16)
    return pl.pallas_call(
        functools.partial(_even_proj_kernel, tiles_per_seq=seq_len // tm, tile=tile),
        grid=(n // tm,),
        in_specs=[row(d), _pick(g, layer), _pick(w_main, e), _pick(w_t, e), _pick(w_kvc, e),
                  _pick(w_ff, e), _pick(b_f, e), _pick(w_ng_t, e), _full(tri_l)],
        out_specs=[pl.BlockSpec((pm // LANES, tm, LANES), lambda i: (0, i, 0)),
                   pl.BlockSpec((tm // tile, pt, tile), lambda i: (i, 0, 0)),
                   chunked, chunked, row(LANES),
                   pl.BlockSpec((gr, tm), lambda i: (0, i))],
        out_shape=[jax.ShapeDtypeStruct((pm // LANES, n, LANES), BF16),
                   jax.ShapeDtypeStruct((n // tile, pt, tile), BF16),
                   chunked_shape, chunked_shape,
                   jax.ShapeDtypeStruct((n, LANES), BF16),
                   jax.ShapeDtypeStruct((gr, n), F32)],
        scratch_shapes=[pltpu.VMEM((8, LANES), F32), pltpu.VMEM((2, tm, LANES), F32)],
        compiler_params=_params("arbitrary"),
        name="even_proj",
    )(x, g, w_main, w_t, w_kvc, w_ff, b_f, w_ng_t, tri_l)


def _bucket_lookup_kernel(tab_ref, idx_ref, o_ref):
    h = pl.program_id(0)
    idx = idx_ref[...]
    out = jnp.full(idx.shape, NEG, F32)
    for b in range(N_BUCKETS):
        out = jnp.where(idx == b, tab_ref[b, h], out)
    o_ref[0] = out


def bucket_lookup(tab, idx, rows):
    r, c = idx.shape
    h = tab.shape[1]
    return pl.pallas_call(
        _bucket_lookup_kernel,
        grid=(h, r // rows),
        in_specs=[pl.BlockSpec(memory_space=pltpu.SMEM),
                  pl.BlockSpec((rows, c), lambda hh, j: (j, 0))],
        out_specs=pl.BlockSpec((1, rows, c), lambda hh, j: (hh, j, 0)),
        out_shape=jax.ShapeDtypeStruct((h, r, c), F32),
        compiler_params=_params("parallel", "parallel"),
        name="bucket_lookup",
    )(tab, idx)


def _fox_kernel(q_ref, k_ref, fs_ref, vt_ref, o_ref, stage_ref, *, tile, n_tiles):
    p = pl.program_id(1)
    ktile = tile // KEY_SPLIT
    lane = lax.broadcasted_iota(jnp.int32, (tile, LANES), 1)
    key_row = lax.broadcasted_iota(jnp.int32, (ktile, tile), 0)
    qry_col = lax.broadcasted_iota(jnp.int32, (ktile, tile), 1)
    causal = [(d * ktile + key_row) <= qry_col for d in range(KEY_SPLIT)]
    lo = _row_lo((LANES, tile))

    def keys(j):
        sl = slice(j * ktile, (j + 1) * ktile)
        return jnp.concatenate([k_ref[sl, :], fs_ref[sl, :]], axis=1)

    def queries(i):
        heads = _split_pair(q_ref[i * tile:(i + 1) * tile, :], HEAD_DIM ** -0.5)
        return tuple(jnp.concatenate(
            [heads[a], jnp.where(lane // F_PIECES == 2 * p + a, 1.0, 0.0).astype(BF16)], axis=1)
            for a in range(2))

    tasks = [(i, j) for i in range(n_tiles) for j in range(KEY_SPLIT * (i + 1))]
    qcache = {}

    def scores(t):
        i, j = tasks[t]
        if i not in qcache:
            qcache[i] = queries(i)
        kj = keys(j)
        for a in range(2):
            stage_ref[t % 2, a] = _dot_nt(kj, qcache[i][a])

    scores(0)
    for t, (i, j) in enumerate(tasks):
        if t + 1 < len(tasks):
            scores(t + 1)
        qk = [stage_ref[t % 2, a] for a in range(2)]
        if j == 0:
            stats = [(jnp.full((1, tile), NEG, F32), jnp.zeros((1, tile), F32)) for _ in range(2)]
            acc = [jnp.zeros((LANES, tile), F32) for _ in range(2)]
        sub = j % KEY_SPLIT
        vt = vt_ref[j // KEY_SPLIT][:, sub * ktile:(sub + 1) * ktile]
        diag = j - KEY_SPLIT * i
        alphas, pvs = [], []
        for a in range(2):
            m_prev, l_prev = stats[a]
            s = jnp.where(causal[diag], qk[a], NEG) if diag >= 0 else qk[a]
            m_new = jnp.maximum(m_prev, jnp.max(s, axis=0, keepdims=True))
            alpha = jnp.exp2(m_prev - m_new)
            pr = jnp.exp2(s - m_new)
            stats[a] = (m_new, alpha * l_prev + jnp.sum(pr, axis=0, keepdims=True))
            alphas.append(alpha)
            pvs.append(_dot(vt, pr.astype(BF16)))
        for a in range(2):
            acc[a] = alphas[a] * acc[a] + pvs[a]
        if j == KEY_SPLIT * (i + 1) - 1:
            o = jnp.where(lo, acc[0] / stats[0][1], acc[1] / stats[1][1])
            o_ref[i * tile:(i + 1) * tile, :] = o.T.astype(o_ref.dtype)


def fox_attention(main, fs, vt, batch, seq_len, q_col, k_col, v_row, n_pairs, tile=ATT_TILE):
    nq = seq_len // tile
    n = batch * seq_len
    return pl.pallas_call(
        functools.partial(_fox_kernel, tile=tile, n_tiles=nq),
        grid=(batch, n_pairs),
        in_specs=[pl.BlockSpec((None, seq_len, LANES), lambda b, p: (q_col + p, b, 0)),
                  pl.BlockSpec((None, seq_len, LANES), lambda b, p: (k_col + p, b, 0)),
                  pl.BlockSpec((seq_len, LANES), lambda b, p: (b, 0)),
                  pl.BlockSpec((nq, LANES, tile), lambda b, p: (b, v_row + p, 0))],
        out_specs=pl.BlockSpec((None, seq_len, LANES), lambda b, p: (p, b, 0)),
        out_shape=jax.ShapeDtypeStruct((n_pairs, n, LANES), BF16),
        scratch_shapes=[pltpu.VMEM((2, 2, tile // KEY_SPLIT, tile), F32)],
        compiler_params=_params("parallel", "parallel"),
        name="fox_attention",
    )(main, main, fs, vt)


def _sb_kernel(q_ref, k_ref, vt_ref, tri_ref, o_ref, acc_ref, stage_ref, *, tile, n_tiles):
    tri = tri_ref[...]
    strict = (lax.broadcasted_iota(jnp.int32, (tile, tile), 0)
              < lax.broadcasted_iota(jnp.int32, (tile, tile), 1))
    lo = _row_lo((LANES, tile))

    def queries(i):
        return _split_pair(q_ref[i * tile:(i + 1) * tile, :], HEAD_DIM ** -0.5)

    def scores(ks, heads):
        return tuple(_dot_nt(ks, heads[a]) for a in range(2))

    def consume(zz, rs, vt, masked):
        sps, base, later, pvs = [], [], [], []
        for a in range(2):
            z = zz[a]
            sp = jnp.maximum(z, jnp.log(1.0 + jnp.exp(jnp.minimum(z, SOFTPLUS_CLAMP))))
            base.append(z - sp)
            if masked:
                sp = jnp.where(strict, sp, 0.0)
            sps.append(sp)
        for a in range(2):
            later.append(_dot(tri, sps[a].astype(BF16)))
        for a in range(2):
            wgt = jnp.exp(base[a] - later[a] + rs[a])
            if masked:
                wgt = jnp.where(strict, wgt, 0.0)
            pvs.append(_dot(vt, wgt.astype(BF16)))
        return tuple(rs[a] - jnp.sum(sps[a], axis=0, keepdims=True) for a in range(2)), pvs

    def live(rs):
        return (jnp.max(jnp.maximum(rs[0], rs[1])) > EXP_UNDERFLOW).astype(jnp.int32)

    tasks = [(i, j) for i in range(n_tiles) for j in ((i, i - 1) if i else (i,))]
    qcache = {}

    def task_scores(t):
        i, j = tasks[t]
        if i not in qcache:
            qcache[i] = queries(i)
        zz = scores(k_ref[j * tile:(j + 1) * tile, :], qcache[i])
        for a in range(2):
            stage_ref[t % 2, a] = zz[a]

    zero = jnp.zeros((1, tile), F32)
    survival = {}
    task_scores(0)
    for t, (i, j) in enumerate(tasks):
        if t + 1 < len(tasks):
            task_scores(t + 1)
        zz = [stage_ref[t % 2, a] for a in range(2)]
        if j == i:
            rs, acc = (zero, zero), None
        rs, pvs = consume(zz, rs, vt_ref[j], j == i)
        acc = pvs if acc is None else [acc[a] + pvs[a] for a in range(2)]
        if j == max(i - 1, 0):
            for a in range(2):
                acc_ref[i, a] = acc[a]
            survival[i] = rs

    for i in range(2, n_tiles):
        heads = qcache[i]

        def cond(state):
            j, alive, _ = state
            return (j >= 0) & (alive > 0)

        def body(state, i=i, heads=heads):
            j, _, rs = state
            rs, pvs = consume(scores(k_ref[pl.ds(j * tile, tile), :], heads), rs, vt_ref[j], False)
            for a in range(2):
                acc_ref[i, a] = acc_ref[i, a] + pvs[a]
            return j - 1, live(rs), rs

        lax.while_loop(cond, body, (i - 2, live(survival[i]), survival[i]))

    for i in range(n_tiles):
        o = jnp.where(lo, acc_ref[i, 0], acc_ref[i, 1])
        o_ref[i * tile:(i + 1) * tile, :] = o.T.astype(o_ref.dtype)


def sb_attention(main, vt, batch, seq_len, q_col, k_col, n_pairs, tri, tile=ATT_TILE):
    nq = seq_len // tile
    n = batch * seq_len
    return pl.pallas_call(
        functools.partial(_sb_kernel, tile=tile, n_tiles=nq),
        grid=(batch, n_pairs),
        in_specs=[pl.BlockSpec((None, seq_len, LANES), lambda b, p: (q_col + p, b, 0)),
                  pl.BlockSpec((None, seq_len, LANES), lambda b, p: (k_col + p, b, 0)),
                  pl.BlockSpec((nq, LANES, tile), lambda b, p: (b, p, 0)),
                  _full(tri)],
        out_specs=pl.BlockSpec((None, seq_len, LANES), lambda b, p: (p, b, 0)),
        out_shape=jax.ShapeDtypeStruct((n_pairs, n, LANES), BF16),
        scratch_shapes=[pltpu.VMEM((nq, 2, LANES, tile), F32), pltpu.VMEM((2, 2, tile, tile), F32)],
        compiler_params=_params("parallel", "parallel"),
        name="sb_attention",
    )(main, main, vt, tri)


def _gelu_tanh(x):
    return 0.5 * x * (1.0 + jnp.tanh(math.sqrt(2.0 / math.pi) * (x + 0.044715 * (x * x * x))))


def _compress_kernel(xk_ref, xv_ref, pek_ref, pev_ref, w1k_ref, w1v_ref, w2k_ref, w2v_ref,
                     ok_ref, ov_ref, *, n_cmp):
    def one(x_ref, pe_ref, w1_ref, w2_ref):
        x = x_ref[...].astype(F32)
        xa = (x + pe_ref[0:1, :]).astype(BF16)
        xb = (x + pe_ref[1:2, :]).astype(BF16)
        ha = _dot(xa, w1_ref[0])
        hb = _dot(xb, w1_ref[1])
        rows = ha.shape[0]
        hb = pltpu.roll(hb, rows - 1, 0)
        h = _gelu_tanh(ha + hb)
        out = _dot(h.astype(BF16), w2_ref[...])
        ridx = lax.broadcasted_iota(jnp.int32, out.shape, 0)
        return jnp.where(ridx < n_cmp, out, 0.0)

    ok_ref[0] = one(xk_ref, pek_ref, w1k_ref, w2k_ref).astype(ok_ref.dtype)
    ov_ref[0] = one(xv_ref, pev_ref, w1v_ref, w2v_ref).T.astype(ov_ref.dtype)


def compress(xk, xv, pek, pev, w1k, w1v, w2k, w2v, batch, n_cmp):
    chunks = xk.shape[0] // batch
    row = pl.BlockSpec((chunks, xk.shape[1]), lambda b: (b, 0))
    return pl.pallas_call(
        functools.partial(_compress_kernel, n_cmp=n_cmp),
        grid=(batch,),
        in_specs=[row, row, _full(pek), _full(pev), _full(w1k), _full(w1v), _full(w2k), _full(w2v)],
        out_specs=[pl.BlockSpec((1, chunks, LANES), lambda b: (b, 0, 0)),
                   pl.BlockSpec((1, LANES, chunks), lambda b: (b, 0, 0))],
        out_shape=[jax.ShapeDtypeStruct((batch, chunks, LANES), BF16),
                   jax.ShapeDtypeStruct((batch, LANES, chunks), BF16)],
        compiler_params=_params("parallel"),
        name="nsa_compress",
    )(xk, xv, pek, pev, w1k, w1v, w2k, w2v)


def _nsa_cmp_kernel(q_ref, kc_ref, vct_ref, cb_ref, ov_ref, ocmp_ref, sel_ref, stage_ref, *, tile, n_blocks):
    i = pl.program_id(1)
    kc = kc_ref[0]
    vct = vct_ref[0]
    pcsum = [jnp.zeros((LANES, tile), F32), jnp.zeros((LANES, tile), F32)]
    lo = _row_lo((LANES, tile))
    for r in range(NSA_GQA):
        heads = _split_pair(q_ref[r], HEAD_DIM ** -0.5)
        for a in range(2):
            stage_ref[2 * r + a] = _dot_nt(kc, heads[a])
    for r in range(NSA_GQA):
        outs = []
        for a in range(2):
            bias = cb_ref[2 * r + a]
            valid = bias > 0.5 * NEG
            s = stage_ref[2 * r + a] + bias
            m = jnp.max(s, axis=0, keepdims=True)
            pr = jnp.where(valid, jnp.exp2(s - m), 0.0)
            l = jnp.sum(pr, axis=0, keepdims=True)
            pc = pr / jnp.where(l > 0.0, l, 1.0)
            pcsum[a] = pcsum[a] + pc
            outs.append(_dot(vct, pc.astype(BF16)))
        ocmp_ref[r * LANES:(r + 1) * LANES, :] = jnp.where(lo, outs[0], outs[1])

    score = _dot_01exact(ov_ref[0], pcsum[0]) + _dot_01exact(ov_ref[1], pcsum[1])
    score = score[:2 * n_blocks]
    shape = (2 * n_blocks, tile)
    row = lax.broadcasted_iota(jnp.int32, shape, 0)
    qpos = i * tile + lax.broadcasted_iota(jnp.int32, shape, 1)
    blk = row % n_blocks
    cur = qpos // SLC_BLOCK
    forced = (blk == 0) | (blk == cur) | (blk == cur - 1)
    future = blk > cur
    score = jnp.where(future, -1.0, jnp.where(forced, FORCED_SCORE, score))
    grp0 = row < n_blocks
    cnt = jnp.zeros(shape, F32)
    for mth in range(n_blocks):
        other = jnp.where(grp0, score[mth:mth + 1, :], score[n_blocks + mth:n_blocks + mth + 1, :])
        ahead = (other > score) | ((other == score) & (blk > mth))
        cnt = cnt + jnp.where(ahead, 1.0, 0.0)
    n_top = min(SLC_TOPK, n_blocks)
    sel_ref[...] = jnp.where(cnt < n_top, 0.0, NEG)


def nsa_cmp_select(main, kc, vct, cmp_bias, overlap2, batch, seq_len, q_col, tile=ATT_TILE):
    nq = seq_len // tile
    n = batch * seq_len
    n_blocks = seq_len // SLC_BLOCK
    qw = NSA_GQA * LANES
    return pl.pallas_call(
        functools.partial(_nsa_cmp_kernel, tile=tile, n_blocks=n_blocks),
        grid=(batch, nq),
        in_specs=[pl.BlockSpec((NSA_GQA, tile, LANES), lambda b, i: (q_col // NSA_GQA, b * nq + i, 0)),
                  pl.BlockSpec((1,) + kc.shape[1:], lambda b, i: (b, 0, 0)),
                  pl.BlockSpec((1,) + vct.shape[1:], lambda b, i: (b, 0, 0)),
                  pl.BlockSpec((NSA_HEADS, cmp_bias.shape[1], tile), lambda b, i: (0, 0, i)),
                  _full(overlap2)],
        out_specs=[pl.BlockSpec((qw, tile), lambda b, i: (0, b * nq + i)),
                   pl.BlockSpec((2 * n_blocks, tile), lambda b, i: (0, b * nq + i))],
        out_shape=[jax.ShapeDtypeStruct((qw, n), F32),
                   jax.ShapeDtypeStruct((2 * n_blocks, n), F32)],
        scratch_shapes=[pltpu.VMEM((NSA_HEADS, kc.shape[1], tile), F32)],
        compiler_params=_params("parallel", "parallel"),
        name="nsa_cmp_select",
    )(main, kc, vct, cmp_bias, overlap2)


def _nsa_main_kernel(far_ref, q_ref, sel_ref, ocmp_ref, gate_ref, ks_ref, vst_ref, kw_ref, vwt_ref, tz_ref,
                     o_ref, stage_ref, *, tile, n_tiles, n_blocks):
    r = pl.program_id(1)
    lo = _row_lo((LANES, tile))
    blocks_per_tile = tile // SLC_BLOCK
    far = tuple(far_ref[2 * r + a] for a in range(2))

    def queries(i):
        return _split_pair(q_ref[i * tile:(i + 1) * tile, :], HEAD_DIM ** -0.5)

    def chosen(a, i, j, offset=None):
        rows = []
        for nb in range(blocks_per_tile):
            blk = a * n_blocks + j * blocks_per_tile + nb
            row = sel_ref[blk:blk + 1, i * tile:(i + 1) * tile]
            if offset is not None:
                row = row + offset
            rows.append(jnp.broadcast_to(row, (SLC_BLOCK, tile)))
        return jnp.concatenate(rows, axis=0)

    tasks = []
    for i in range(n_tiles):
        tasks += [("slc", i, j) for j in range(i + 1)]
        tasks += [("win", i, j) for j in ((i - 1, i) if i else (i,))]
    qcache = {}

    def scores(t):
        kind, i, j = tasks[t]
        if i not in qcache:
            qcache[i] = queries(i)
        k_ref_ = ks_ref if kind == "slc" else kw_ref
        kj = k_ref_[j * tile:(j + 1) * tile, :]
        for a in range(2):
            stage_ref[t % 2, a] = _dot_nt(kj, qcache[i][a])

    def fresh():
        return ([(jnp.full((1, tile), NEG, F32), jnp.zeros((1, tile), F32)) for _ in range(2)],
                [jnp.zeros((LANES, tile), F32) for _ in range(2)])

    scores(0)
    for t, (kind, i, j) in enumerate(tasks):
        if t + 1 < len(tasks):
            scores(t + 1)
        qk = [stage_ref[t % 2, a] for a in range(2)]
        first = j == 0 if kind == "slc" else j == max(i - 1, 0)
        if first:
            stats, acc = fresh()
        vt = (vst_ref if kind == "slc" else vwt_ref)[j]
        alphas, pvs = [], []
        for a in range(2):
            if j == i:
                s = qk[a] + tz_ref[a, 0]
            elif j == i - 1:
                s = qk[a] + tz_ref[a, 1 if kind == "slc" else 2]
            if kind == "slc":
                s = s + chosen(a, i, j) if j >= i - 1 else qk[a] + chosen(a, i, j, far[a])
            m_prev, l_prev = stats[a]
            m_new = jnp.maximum(m_prev, jnp.max(s, axis=0, keepdims=True))
            alpha = jnp.exp2(m_prev - m_new)
            pr = jnp.exp2(s - m_new)
            stats[a] = (m_new, alpha * l_prev + jnp.sum(pr, axis=0, keepdims=True))
            alphas.append(alpha)
            pvs.append(_dot(vt, pr.astype(BF16)))
        for a in range(2):
            acc[a] = alphas[a] * acc[a] + pvs[a]
        if j == i:
            branch = jnp.where(lo, acc[0] / stats[0][1], acc[1] / stats[1][1])
            if kind == "slc":
                o_slc = branch
            else:
                cols = slice(i * tile, (i + 1) * tile)
                gexp = [jnp.where(lo, gate_ref[2 * c:2 * c + 1, cols], gate_ref[2 * c + 1:2 * c + 2, cols])
                        for c in range(3)]
                out = gexp[0] * ocmp_ref[:, cols] + gexp[1] * o_slc + gexp[2] * branch
                o_ref[cols, :] = out.T.astype(o_ref.dtype)


def nsa_main(far_bias, main, sel, ocmp, gates, vt, tz, batch, seq_len, q_col, ks_col, kw_col, vs_row, vw_row,
             tile=ATT_TILE):
    nq = seq_len // tile
    n = batch * seq_len
    n_blocks = seq_len // SLC_BLOCK
    kv = lambda col: pl.BlockSpec((None, seq_len, LANES), lambda b, r: (col, b, 0))
    vts = lambda rowblk: pl.BlockSpec((nq, LANES, tile), lambda b, r: (b, rowblk, 0))
    return pl.pallas_call(
        functools.partial(_nsa_main_kernel, tile=tile, n_tiles=nq, n_blocks=n_blocks),
        grid=(batch, NSA_GQA),
        in_specs=[pl.BlockSpec(memory_space=pltpu.SMEM),
                  pl.BlockSpec((None, seq_len, LANES), lambda b, r: (q_col + r, b, 0)),
                  pl.BlockSpec((2 * n_blocks, seq_len), lambda b, r: (0, b)),
                  pl.BlockSpec((LANES, seq_len), lambda b, r: (r, b)),
                  pl.BlockSpec((GATE_ROWS, seq_len), lambda b, r: (r, b)),
                  kv(ks_col), vts(vs_row), kv(kw_col), vts(vw_row),
                  pl.BlockSpec((2, 3, tile, tile), lambda b, r: (r, 0, 0, 0))],
        out_specs=pl.BlockSpec((None, seq_len, LANES), lambda b, r: (r, b, 0)),
        out_shape=jax.ShapeDtypeStruct((NSA_GQA, n, LANES), BF16),
        scratch_shapes=[pltpu.VMEM((2, 2, tile, tile), F32)],
        compiler_params=_params("parallel", "parallel"),
        name="nsa_main",
    )(far_bias, main, sel, ocmp, gates, main, vt, main, vt, tz)


def _layer_tail_kernel(*refs, n_in, final_norm):
    a_refs = refs[:n_in]
    w_refs = refs[n_in:2 * n_in]
    (x_ref, gx_ref, wq_ref, kv_ref, wo_ref, gm_ref, w1_ref, w2_ref, gf_ref,
     o_ref, hn_ref, acc_ref) = refs[2 * n_in:]
    f = pl.program_id(1)

    @pl.when(f == 0)
    def _():
        x1 = x_ref[...]
        for a_ref, w_ref in zip(a_refs, w_refs):
            a = jnp.concatenate([a_ref[c] for c in range(a_ref.shape[0])], axis=1)
            x1 = x1 + _dot(a, w_ref[...])
        q = _dot(_rmsnorm(x1, gx_ref[...]).astype(BF16), wq_ref[...]).astype(BF16)
        width = XA_HEADS * XA_HEAD_DIM
        outs = []
        for h in range(XA_HEADS):
            hs = slice(h * XA_HEAD_DIM, (h + 1) * XA_HEAD_DIM)
            s = _dot_nt(q[:, hs], kv_ref[:, hs]) * (XA_HEAD_DIM ** -0.5)
            m = jnp.max(s, axis=-1, keepdims=True)
            p = jnp.exp(s - m)
            l = jnp.sum(p, axis=-1, keepdims=True)
            v = kv_ref[:, width + h * XA_HEAD_DIM:width + (h + 1) * XA_HEAD_DIM]
            outs.append((_dot(p.astype(BF16), v) / l).astype(BF16))
        x2 = x1 + _dot(jnp.concatenate(outs, axis=-1), wo_ref[...])
        hn_ref[...] = _rmsnorm(x2, gm_ref[...]).astype(BF16)
        acc_ref[...] = x2

    h = jnp.maximum(_dot(hn_ref[...], w1_ref[...]), 0.0)
    acc_ref[...] += _dot((h * h).astype(BF16), w2_ref[...])

    @pl.when(f == pl.num_programs(1) - 1)
    def _():
        y = acc_ref[...]
        if final_norm:
            y = _rmsnorm(y, gf_ref[...])
        o_ref[...] = y


def layer_tail(a_list, w_list, w_idx, x, layer, gx, wq, kv, wo, gm, w1, w2, gf, final_norm, seq_len, mem_len,
               tm=MLP_ROW_TILE, tf=MLP_FF_TILE):
    n, d = x.shape
    ff = w1.shape[2]
    tm = min(tm, seq_len)
    tf = min(tf, ff)
    tiles_per_seq = seq_len // tm
    once = pl.Buffered(1)
    row = lambda a: pl.BlockSpec((tm, a.shape[1]), lambda i, f: (i, 0))
    pick = lambda a, k: pl.BlockSpec((None,) + a.shape[1:], lambda i, f: (k,) + (0,) * (a.ndim - 1),
                                     pipeline_mode=once)
    return pl.pallas_call(
        functools.partial(_layer_tail_kernel, n_in=len(a_list), final_norm=final_norm),
        grid=(n // tm, ff // tf),
        in_specs=[pl.BlockSpec((a.shape[0], tm, LANES), lambda i, f: (0, i, 0)) for a in a_list]
        + [pick(w, w_idx) for w in w_list]
        + [row(x), pick(gx, layer), pick(wq, layer),
           pl.BlockSpec((mem_len, 2 * XA_HEADS * XA_HEAD_DIM), lambda i, f: (i // tiles_per_seq, layer)),
           pick(wo, layer), pick(gm, layer),
           pl.BlockSpec((None, d, tf), lambda i, f: (layer, 0, f)),
           pl.BlockSpec((None, tf, d), lambda i, f: (layer, f, 0)),
           pl.BlockSpec(gf.shape, lambda i, f: (0, 0), pipeline_mode=once)],
        out_specs=pl.BlockSpec((tm, d), lambda i, f: (i, 0)),
        out_shape=jax.ShapeDtypeStruct((n, d), F32),
        scratch_shapes=[pltpu.VMEM((tm, d), BF16), pltpu.VMEM((tm, d), F32)],
        compiler_params=_params("parallel", "arbitrary"),
        name="layer_tail",
    )(*a_list, *w_list, x, gx, wq, kv, wo, gm, w1, w2, gf)


def _static_tables(seq_len):
    tile = ATT_TILE
    n_chunks = seq_len // CMP_STRIDE
    n_cmp = (seq_len - CMP_BLOCK) // CMP_STRIDE + 1
    n_slc = seq_len // SLC_BLOCK
    assert n_chunks == LANES and 2 * n_slc <= LANES and seq_len % tile == 0 and WINDOW == tile
    c = np.arange(n_chunks)[:, None]
    t = np.arange(seq_len)[None, :]
    cdist = t - (c * CMP_STRIDE + CMP_BLOCK - 1)
    cmp_idx = np.where((cdist >= 0) & (c < n_cmp), _t5_bucket_np(cdist), -1).astype(np.int32)
    sl = np.arange(tile)[:, None]
    tl = np.arange(tile)[None, :]
    prev = _t5_bucket_np(tl - sl + tile)
    tz_idx = np.concatenate([np.where(sl <= tl, _t5_bucket_np(tl - sl), -1), prev,
                             np.where(sl > tl, prev, -1)], axis=0).astype(np.int32)
    far_bucket = int(_t5_bucket_np(np.array([tile + 1]))[0])
    assert far_bucket == int(_t5_bucket_np(np.array([seq_len]))[0])
    cmp_start = np.arange(n_cmp) * CMP_STRIDE
    cmp_stop = cmp_start + CMP_BLOCK - 1
    slc_start = np.arange(n_slc) * SLC_BLOCK
    slc_stop = slc_start + SLC_BLOCK - 1
    ov = ((cmp_start[:, None] <= slc_stop[None, :]) & (cmp_stop[:, None] >= slc_start[None, :]))
    overlap2 = np.zeros((2, LANES, LANES), np.float32)
    for a in range(2):
        overlap2[a, a * n_slc:(a + 1) * n_slc, :n_cmp] = ov.T
    return cmp_idx, tz_idx, far_bucket, overlap2, n_cmp


def _tri_prefix(nn):
    s = np.arange(nn)[:, None]
    j = np.arange(nn)[None, :]
    return (j <= s).astype(np.float32)


def _tri_later(nn):
    s = np.arange(nn)[:, None]
    j = np.arange(nn)[None, :]
    return (j > s).astype(np.float32)


_NSA_HEAD_PERM = [a * NSA_GQA + r for r in range(NSA_GQA) for a in range(NSA_GROUPS)]


def _block_diag2(m):
    z = jnp.zeros_like(m)
    return jnp.concatenate([jnp.concatenate([m, z], axis=1), jnp.concatenate([z, m], axis=1)], axis=0)


def _compress_weights(pe, w1, w2):
    half = CMP_BLOCK // 2
    pe2 = jnp.concatenate([pe, pe], axis=1).reshape(2, half * LANES)
    w1e = jax.vmap(_block_diag2)(w1).reshape(2, half * LANES, LANES)
    return pe2.astype(F32), w1e.astype(BF16), _block_diag2(w2).astype(BF16)


def kernel(x, mem, rel_bias, mem_norm_g, norm_mix_g, norm_xattn_g, norm_mlp_g, final_norm_g, w_in_even, b_forget, cmp_pe_k, cmp_w1_k, cmp_w2_k, cmp_pe_v, cmp_w1_v, cmp_w2_v, w_out_even, w_in_odd, w_out_odd, xa_wq, xa_wkv, xa_wo, mlp_w1, mlp_w2):
    batch, seq_len, d = x.shape
    mem_len = mem.shape[1]
    depth = norm_mix_g.shape[0]
    n = batch * seq_len
    fox_w = FOX_HEADS * HEAD_DIM
    nsa_w = NSA_HEADS * HEAD_DIM
    kv_w = NSA_GROUPS * HEAD_DIM
    splits = np.cumsum([fox_w, fox_w, fox_w, FOX_HEADS, nsa_w, kv_w, kv_w, kv_w, kv_w, kv_w, kv_w])

    cmp_idx, tz_idx, far_bucket, overlap2, n_cmp = _static_tables(seq_len)
    tab = LOG2E * rel_bias.astype(F32)[:, _NSA_HEAD_PERM]
    cmp_bias = bucket_lookup(tab, jnp.asarray(cmp_idx), rows=cmp_idx.shape[0])
    tz = bucket_lookup(tab, jnp.asarray(tz_idx), rows=ATT_TILE)
    tz = tz.reshape(NSA_HEADS, 3, ATT_TILE, ATT_TILE)
    far_bias = tab[far_bucket]
    overlap2 = jnp.asarray(overlap2, BF16)
    tri_prefix = jnp.asarray(_tri_prefix(min(ROW_TILE, seq_len)), BF16)
    tri_later = jnp.asarray(_tri_later(ATT_TILE), BF16)

    xf = x.reshape(n, d)
    memf = mem.reshape(batch * mem_len, d)
    row = lambda v: v.reshape(1, -1).astype(F32)
    rows = lambda v: v.reshape(v.shape[0], 1, -1).astype(F32)
    g_mix_all, g_xa_all, g_mlp_all = rows(norm_mix_g), rows(norm_xattn_g), rows(norm_mlp_g)
    g_mem, g_final = row(mem_norm_g), row(final_norm_g)

    ne = w_in_even.shape[0]
    (w_fq, w_fk, w_fv, w_ff, w_nq, w_kc, w_vc, w_ks, w_vs, w_kw, w_vw, w_ng) = jnp.split(w_in_even, splits, axis=2)
    w_fq = LOG2E * w_fq
    w_nq = LOG2E * w_nq.reshape(ne, d, NSA_HEADS, HEAD_DIM)[:, :, _NSA_HEAD_PERM, :].reshape(ne, d, nsa_w)
    w_main_all = jnp.concatenate([w_fq, w_fk, w_nq, w_ks, w_kw], axis=2).astype(BF16)
    w_t_all = jnp.concatenate([w_fv, w_vs, w_vw], axis=2).transpose(0, 2, 1).astype(BF16)
    w_kvc_all = jnp.concatenate([w_kc, w_vc], axis=2).astype(BF16)
    w_ng_t = w_ng.reshape(ne, d, NSA_GROUPS, NSA_GQA, 3).transpose(0, 3, 4, 2, 1).reshape(ne, NSA_GQA, 6, d)
    w_ng_all = jnp.pad(w_ng_t, ((0, 0), (0, 0), (0, GATE_ROWS - 6), (0, 0))).reshape(
        ne, NSA_GQA * GATE_ROWS, d).astype(BF16)
    pad_f = LANES - F_PIECES * FOX_HEADS
    w_ff_all = jnp.pad(jnp.repeat(w_ff, F_PIECES, axis=2), ((0, 0), (0, 0), (0, pad_f))).astype(BF16)
    b_f_all = jnp.pad(jnp.repeat(b_forget.astype(F32), F_PIECES, axis=1), ((0, 0), (0, pad_f))).reshape(ne, 1, LANES)
    cmp_k = jax.vmap(_compress_weights)(cmp_pe_k, cmp_w1_k, cmp_w2_k)
    cmp_v = jax.vmap(_compress_weights)(cmp_pe_v, cmp_w1_v, cmp_w2_v)
    w_o_fox_all = w_out_even[:, :fox_w].astype(BF16)
    w_o_nsa_all = w_out_even[:, fox_w:].reshape(ne, NSA_HEADS, HEAD_DIM, d)[:, _NSA_HEAD_PERM].reshape(
        ne, nsa_w, d).astype(BF16)
    sb_w = SB_HEADS * HEAD_DIM
    w_odd_qk_all = w_in_odd[:, :, :2 * sb_w].astype(BF16)
    w_odd_vt_all = w_in_odd[:, :, 2 * sb_w:].transpose(0, 2, 1).astype(BF16)
    w_o_odd_all = w_out_odd.astype(BF16)
    xa_wq_all, xa_wo_all = xa_wq.astype(BF16), xa_wo.astype(BF16)
    mlp_w1_all, mlp_w2_all = mlp_w1.astype(BF16), mlp_w2.astype(BF16)
    xa_wkv_cat = xa_wkv.transpose(1, 0, 2).reshape(d, depth * xa_wkv.shape[2]).astype(BF16)
    kv_mem = norm_proj(memf, g_mem[None], 0, xa_wkv_cat[None])

    for layer in range(depth):
        if layer % 2 == 0:
            e = layer // 2
            main, vt, kc_in, vc_in, fs, gates = even_proj(
                xf, g_mix_all, layer, w_main_all, w_t_all, w_kvc_all, w_ff_all, b_f_all, w_ng_all, e,
                tri_prefix, seq_len)
            nblk = fox_w // LANES
            fox = fox_attention(main, fs, vt, batch, seq_len, 0, nblk, 0, FOX_HEADS // 2)

            kc, vct = compress(kc_in, vc_in,
                               cmp_k[0][e], cmp_v[0][e], cmp_k[1][e], cmp_v[1][e], cmp_k[2][e], cmp_v[2][e],
                               batch, n_cmp)
            q_col = 2 * nblk
            ocmp, sel = nsa_cmp_select(main, kc, vct, cmp_bias, overlap2, batch, seq_len, q_col)
            nsa = nsa_main(far_bias, main, sel, ocmp, gates, vt, tz, batch, seq_len, q_col,
                           q_col + NSA_GQA, q_col + NSA_GQA + 1, nblk, nblk + 1)
            a_list, w_list, w_idx = [fox, nsa], [w_o_fox_all, w_o_nsa_all], e
        else:
            o = layer // 2
            main, vt = norm_proj(xf, g_mix_all, layer, w_odd_qk_all, w_odd_vt_all, o, col_blocks=True)
            nblk = sb_w // LANES
            sb = sb_attention(main, vt, batch, seq_len, 0, nblk, SB_HEADS // 2, tri_later)
            a_list, w_list, w_idx = [sb], [w_o_odd_all], o

        xf = layer_tail(a_list, w_list, w_idx, xf, layer, g_xa_all, xa_wq_all, kv_mem, xa_wo_all, g_mlp_all,
                        mlp_w1_all, mlp_w2_all, g_final, layer == depth - 1, seq_len, mem_len)
    return xf.reshape(batch, seq_len, d)
```

```python
import functools
import math

import numpy as np
import jax
import jax.numpy as jnp
from jax import lax
from jax.experimental import pallas as pl
from jax.experimental.pallas import tpu as pltpu

F32 = jnp.float32
BF16 = jnp.bfloat16

LANES = 128
HEAD_DIM = 64
FOX_HEADS = 8
NSA_HEADS = 8
NSA_GROUPS = 2
NSA_GQA = NSA_HEADS // NSA_GROUPS
SB_HEADS = 16
CMP_BLOCK = 32
CMP_STRIDE = 16
SLC_BLOCK = 64
SLC_TOPK = 8
WINDOW = 256
N_BUCKETS = 32
MAX_DISTANCE = 128
XA_HEADS = 4
XA_HEAD_DIM = 128
EPS = 1e-6
NEG = -1e30
FORCED_SCORE = 1e4
EXP_UNDERFLOW = -104.0
LOG2E = math.log2(math.e)
SOFTPLUS_CLAMP = 80.0

ATT_TILE = 256
ROW_TILE = 512
MLP_ROW_TILE = 1024
MLP_FF_TILE = 1024
GATE_ROWS = 8
F_PIECES = 3


def _dot(a, b):
    return jnp.dot(a, b, preferred_element_type=F32)


def _dot_nt(a, b):
    return lax.dot_general(a, b, (((1,), (1,)), ((), ())), preferred_element_type=F32)


def _split3(x):
    hi = x.astype(BF16)
    r1 = x - hi.astype(F32)
    mid = r1.astype(BF16)
    lo = (r1 - mid.astype(F32)).astype(BF16)
    return hi, mid, lo


def _dot_exact01(x, m01):
    hi, mid, lo = _split3(x)
    return _dot(hi, m01) + _dot(mid, m01) + _dot(lo, m01)


def _dot_01exact(m01, x):
    hi, mid, lo = _split3(x)
    return _dot(m01, hi) + _dot(m01, mid) + _dot(m01, lo)


def _rmsnorm(x, g):
    ms = jnp.mean(x * x, axis=-1, keepdims=True)
    return x * lax.rsqrt(ms + EPS) * g


def _sigmoid(x):
    return 1.0 / (1.0 + jnp.exp(-x))


def _softplus(x):
    return jnp.maximum(x, 0.0) + jnp.log1p(jnp.exp(-jnp.abs(x)))


def _split_pair(q, scale):
    lo = lax.broadcasted_iota(jnp.int32, q.shape, 1) < HEAD_DIM
    qs = q * jnp.asarray(scale, q.dtype)
    zero = jnp.zeros_like(qs)
    return jnp.where(lo, qs, zero), jnp.where(lo, zero, qs)


def _row_lo(shape):
    return lax.broadcasted_iota(jnp.int32, shape, 0) < HEAD_DIM


def _t5_bucket_np(dist):
    dist = np.maximum(dist, 0)
    max_exact = N_BUCKETS // 2
    d_f = np.maximum(dist, 1).astype(np.float64)
    large = max_exact + (np.log(d_f / max_exact) / math.log(MAX_DISTANCE / max_exact)
                         * (N_BUCKETS - max_exact)).astype(np.int32)
    large = np.minimum(large, N_BUCKETS - 1)
    return np.where(dist < max_exact, dist, large).astype(np.int32)


def _params(*sem):
    return pltpu.CompilerParams(dimension_semantics=sem)


def _full(a):
    return pl.BlockSpec(a.shape, lambda *_: (0,) * a.ndim)


def _store_col_blocks(o_ref, val):
    for c in range(o_ref.shape[0]):
        o_ref[c] = val[:, c * LANES:(c + 1) * LANES]


def _pick(a, k):
    return pl.BlockSpec((None,) + a.shape[1:], lambda *_: (k,) + (0,) * (a.ndim - 1))


def _norm_proj_kernel(x_ref, g_ref, w_ref, wt_ref, o_ref, ot_ref, *, tile):
    xn = _rmsnorm(x_ref[...], g_ref[...]).astype(BF16)
    res = _dot(xn, w_ref[...]).astype(o_ref.dtype)
    if len(o_ref.shape) == 3:
        _store_col_blocks(o_ref, res)
    else:
        o_ref[...] = res
    if wt_ref is not None:
        t = _dot_nt(wt_ref[...], xn).astype(ot_ref.dtype)
        for c in range(t.shape[1] // tile):
            ot_ref[c] = t[:, c * tile:(c + 1) * tile]


def norm_proj(x, g, g_layer, w, w_t=None, layer=0, col_blocks=False, tile=ATT_TILE, tm=ROW_TILE):
    n, d = x.shape
    p = w.shape[2]
    tm = min(tm, n)
    in_specs = [pl.BlockSpec((tm, d), lambda i: (i, 0)), _pick(g, g_layer), _pick(w, layer)]
    if col_blocks:
        out_specs = [pl.BlockSpec((p // LANES, tm, LANES), lambda i: (0, i, 0))]
        out_shape = [jax.ShapeDtypeStruct((p // LANES, n, LANES), BF16)]
    else:
        out_specs = [pl.BlockSpec((tm, p), lambda i: (i, 0))]
        out_shape = [jax.ShapeDtypeStruct((n, p), BF16)]
    args = [x, g, w]
    if w_t is None:
        body = lambda x_ref, g_ref, w_ref, o_ref: _norm_proj_kernel(
            x_ref, g_ref, w_ref, None, o_ref, None, tile=tile)
    else:
        pt = w_t.shape[1]
        in_specs.append(_pick(w_t, layer))
        out_specs.append(pl.BlockSpec((tm // tile, pt, tile), lambda i: (i, 0, 0)))
        out_shape.append(jax.ShapeDtypeStruct((n // tile, pt, tile), BF16))
        args.append(w_t)
        body = functools.partial(_norm_proj_kernel, tile=tile)
    out = pl.pallas_call(
        body, grid=(n // tm,), in_specs=in_specs, out_specs=out_specs, out_shape=out_shape,
        compiler_params=_params("parallel"), name="norm_proj",
    )(*args)
    return out if w_t is not None else out[0]


def _even_proj_kernel(x_ref, g_ref, w_ref, wt_ref, wkvc_ref, wff_ref, bf_ref, wng_ref, tri_ref,
                      main_ref, vt_ref, kc_ref, vc_ref, f_ref, gate_ref, carry_ref, chunk_ref,
                      *, tiles_per_seq, tile):
    i = pl.program_id(0)
    xn = _rmsnorm(x_ref[...], g_ref[...]).astype(BF16)
    _store_col_blocks(main_ref, _dot(xn, w_ref[...]).astype(BF16))
    t = _dot_nt(wt_ref[...], xn).astype(BF16)
    for c in range(t.shape[1] // tile):
        vt_ref[c] = t[:, c * tile:(c + 1) * tile]
    chunks = chunk_ref.shape[1] // CMP_STRIDE
    kvc = _dot(xn, wkvc_ref[...])
    for half, out_ref in enumerate((kc_ref, vc_ref)):
        chunk_ref[half] = kvc[:, half * LANES:(half + 1) * LANES]
        for pos in range(CMP_STRIDE):
            out_ref[:, pos * LANES:(pos + 1) * LANES] = chunk_ref[
                half, pl.ds(pos, chunks, stride=CMP_STRIDE), :].astype(BF16)
    gate_ref[...] = _sigmoid(_dot_nt(wng_ref[...], xn))
    ff = _dot(xn, wff_ref[...]) + bf_ref[...]
    logf = jnp.minimum(ff, 0.0) - jnp.log1p(jnp.exp(-jnp.abs(ff)))

    @pl.when(i % tiles_per_seq == 0)
    def _():
        carry_ref[...] = jnp.zeros_like(carry_ref)

    cs = _dot_01exact(tri_ref[...], logf) + carry_ref[0:1, :]
    carry_ref[...] = jnp.broadcast_to(cs[cs.shape[0] - 1:, :], carry_ref.shape)
    hi, mid, lo = _split3(-LOG2E * cs)
    piece = lax.broadcasted_iota(jnp.int32, cs.shape, 1) % F_PIECES
    f_ref[...] = jnp.where(piece == 0, hi, jnp.where(piece == 1, mid, lo))


def even_proj(x, g, layer, w_main, w_t, w_kvc, w_ff, b_f, w_ng_t, e, tri_l, seq_len,
              tile=ATT_TILE, tm=ROW_TILE):
    n, d = x.shape
    tm = min(tm, seq_len)
    pm = w_main.shape[2]
    pt = w_t.shape[1]
    gr = w_ng_t.shape[1]
    row = lambda width: pl.BlockSpec((tm, width), lambda i: (i, 0))
    chunked = pl.BlockSpec((tm // CMP_STRIDE, CMP_STRIDE * LANES), lambda i: (i, 0))
    chunked_shape = jax.ShapeDtypeStruct((n // CMP_STRIDE, CMP_STRIDE * LANES), BF16)
    return pl.pallas_call(
        functools.partial(_even_proj_kernel, tiles_per_seq=seq_len // tm, tile=tile),
        grid=(n // tm,),
        in_specs=[row(d), _pick(g, layer), _pick(w_main, e), _pick(w_t, e), _pick(w_kvc, e),
                  _pick(w_ff, e), _pick(b_f, e), _pick(w_ng_t, e), _full(tri_l)],
        out_specs=[pl.BlockSpec((pm // LANES, tm, LANES), lambda i: (0, i, 0)),
                   pl.BlockSpec((tm // tile, pt, tile), lambda i: (i, 0, 0)),
                   chunked, chunked, row(LANES),
                   pl.BlockSpec((gr, tm), lambda i: (0, i))],
        out_shape=[jax.ShapeDtypeStruct((pm // LANES, n, LANES), BF16),
                   jax.ShapeDtypeStruct((n // tile, pt, tile), BF16),
                   chunked_shape, chunked_shape,
                   jax.ShapeDtypeStruct((n, LANES), BF16),
                   jax.ShapeDtypeStruct((gr, n), F32)],
        scratch_shapes=[pltpu.VMEM((8, LANES), F32), pltpu.VMEM((2, tm, LANES), F32)],
        compiler_params=_params("arbitrary"),
        name="even_proj",
    )(x, g, w_main, w_t, w_kvc, w_ff, b_f, w_ng_t, tri_l)


def _bucket_lookup_kernel(tab_ref, idx_ref, o_ref):
    h = pl.program_id(0)
    idx = idx_ref[...]
    out = jnp.full(idx.shape, NEG, F32)
    for b in range(N_BUCKETS):
        out = jnp.where(idx == b, tab_ref[b, h], out)
    o_ref[0] = out


def bucket_lookup(tab, idx, rows):
    r, c = idx.shape
    h = tab.shape[1]
    return pl.pallas_call(
        _bucket_lookup_kernel,
        grid=(h, r // rows),
        in_specs=[pl.BlockSpec(memory_space=pltpu.SMEM),
                  pl.BlockSpec((rows, c), lambda hh, j: (j, 0))],
        out_specs=pl.BlockSpec((1, rows, c), lambda hh, j: (hh, j, 0)),
        out_shape=jax.ShapeDtypeStruct((h, r, c), F32),
        compiler_params=_params("parallel", "parallel"),
        name="bucket_lookup",
    )(tab, idx)


def _fox_kernel(q_ref, k_ref, fs_ref, vt_ref, o_ref, stage_ref, *, tile, n_tiles):
    p = pl.program_id(1)
    lane = lax.broadcasted_iota(jnp.int32, (tile, LANES), 1)
    causal = (lax.broadcasted_iota(jnp.int32, (tile, tile), 0)
              <= lax.broadcasted_iota(jnp.int32, (tile, tile), 1))
    lo = _row_lo((LANES, tile))

    def keys(j):
        sl = slice(j * tile, (j + 1) * tile)
        return jnp.concatenate([k_ref[sl, :], fs_ref[sl, :]], axis=1)

    def queries(i):
        heads = _split_pair(q_ref[i * tile:(i + 1) * tile, :], HEAD_DIM ** -0.5)
        return tuple(jnp.concatenate(
            [heads[a], jnp.where(lane // F_PIECES == 2 * p + a, 1.0, 0.0).astype(BF16)], axis=1)
            for a in range(2))

    tasks = [(i, j) for i in range(n_tiles) for j in range(i + 1)]
    qcache = {}

    def scores(t):
        i, j = tasks[t]
        if i not in qcache:
            qcache[i] = queries(i)
        kj = keys(j)
        for a in range(2):
            stage_ref[t % 3, a] = _dot_nt(kj, qcache[i][a])

    scores(0)
    scores(1)
    for t, (i, j) in enumerate(tasks):
        if t + 2 < len(tasks):
            scores(t + 2)
        qk = [stage_ref[t % 3, a] for a in range(2)]
        if j == 0:
            stats = [(jnp.full((1, tile), NEG, F32), jnp.zeros((1, tile), F32)) for _ in range(2)]
            acc = [jnp.zeros((LANES, tile), F32) for _ in range(2)]
        vt = vt_ref[j]
        alphas, pvs = [], []
        for a in range(2):
            m_prev, l_prev = stats[a]
            s = jnp.where(causal, qk[a], NEG) if j == i else qk[a]
            m_new = jnp.maximum(m_prev, jnp.max(s, axis=0, keepdims=True))
            alpha = jnp.exp2(m_prev - m_new)
            pr = jnp.exp2(s - m_new)
            stats[a] = (m_new, alpha * l_prev + jnp.sum(pr, axis=0, keepdims=True))
            alphas.append(alpha)
            pvs.append(_dot(vt, pr.astype(BF16)))
        for a in range(2):
            acc[a] = alphas[a] * acc[a] + pvs[a]
        if j == i:
            o = jnp.where(lo, acc[0] / stats[0][1], acc[1] / stats[1][1])
            o_ref[i * tile:(i + 1) * tile, :] = o.T.astype(o_ref.dtype)


def fox_attention(main, fs, vt, batch, seq_len, q_col, k_col, v_row, n_pairs, tile=ATT_TILE):
    nq = seq_len // tile
    n = batch * seq_len
    return pl.pallas_call(
        functools.partial(_fox_kernel, tile=tile, n_tiles=nq),
        grid=(batch, n_pairs),
        in_specs=[pl.BlockSpec((None, seq_len, LANES), lambda b, p: (q_col + p, b, 0)),
                  pl.BlockSpec((None, seq_len, LANES), lambda b, p: (k_col + p, b, 0)),
                  pl.BlockSpec((seq_len, LANES), lambda b, p: (b, 0)),
                  pl.BlockSpec((nq, LANES, tile), lambda b, p: (b, v_row + p, 0))],
        out_specs=pl.BlockSpec((None, seq_len, LANES), lambda b, p: (p, b, 0)),
        out_shape=jax.ShapeDtypeStruct((n_pairs, n, LANES), BF16),
        scratch_shapes=[pltpu.VMEM((3, 2, tile, tile), F32)],
        compiler_params=_params("parallel", "parallel"),
        name="fox_attention",
    )(main, main, fs, vt)


def _sb_kernel(q_ref, k_ref, vt_ref, tri_ref, o_ref, acc_ref, stage_ref, *, tile, n_tiles):
    tri = tri_ref[...]
    strict = (lax.broadcasted_iota(jnp.int32, (tile, tile), 0)
              < lax.broadcasted_iota(jnp.int32, (tile, tile), 1))
    lo = _row_lo((LANES, tile))

    def queries(i):
        return _split_pair(q_ref[i * tile:(i + 1) * tile, :], HEAD_DIM ** -0.5)

    def scores(ks, heads):
        return tuple(_dot_nt(ks, heads[a]) for a in range(2))

    def consume(zz, rs, vt, masked):
        sps, base, later, pvs = [], [], [], []
        for a in range(2):
            z = zz[a]
            sp = jnp.maximum(z, jnp.log(1.0 + jnp.exp(jnp.minimum(z, SOFTPLUS_CLAMP))))
            base.append(z - sp)
            if masked:
                sp = jnp.where(strict, sp, 0.0)
            sps.append(sp)
        for a in range(2):
            later.append(_dot(tri, sps[a].astype(BF16)))
        for a in range(2):
            wgt = jnp.exp(base[a] - later[a] + rs[a])
            if masked:
                wgt = jnp.where(strict, wgt, 0.0)
            pvs.append(_dot(vt, wgt.astype(BF16)))
        return tuple(rs[a] - jnp.sum(sps[a], axis=0, keepdims=True) for a in range(2)), pvs

    def live(rs):
        return (jnp.max(jnp.maximum(rs[0], rs[1])) > EXP_UNDERFLOW).astype(jnp.int32)

    tasks = [(i, j) for i in range(n_tiles) for j in ((i, i - 1) if i else (i,))]
    qcache = {}

    def task_scores(t):
        i, j = tasks[t]
        if i not in qcache:
            qcache[i] = queries(i)
        zz = scores(k_ref[j * tile:(j + 1) * tile, :], qcache[i])
        for a in range(2):
            stage_ref[t % 2, a] = zz[a]

    zero = jnp.zeros((1, tile), F32)
    survival = {}
    task_scores(0)
    for t, (i, j) in enumerate(tasks):
        if t + 1 < len(tasks):
            task_scores(t + 1)
        zz = [stage_ref[t % 2, a] for a in range(2)]
        if j == i:
            rs, acc = (zero, zero), None
        rs, pvs = consume(zz, rs, vt_ref[j], j == i)
        acc = pvs if acc is None else [acc[a] + pvs[a] for a in range(2)]
        if j == max(i - 1, 0):
            for a in range(2):
                acc_ref[i, a] = acc[a]
            survival[i] = rs

    for i in range(2, n_tiles):
        heads = qcache[i]

        def cond(state):
            j, alive, _ = state
            return (j >= 0) & (alive > 0)

        def body(state, i=i, heads=heads):
            j, _, rs = state
            rs, pvs = consume(scores(k_ref[pl.ds(j * tile, tile), :], heads), rs, vt_ref[j], False)
            for a in range(2):
                acc_ref[i, a] = acc_ref[i, a] + pvs[a]
            return j - 1, live(rs), rs

        lax.while_loop(cond, body, (i - 2, live(survival[i]), survival[i]))

    for i in range(n_tiles):
        o = jnp.where(lo, acc_ref[i, 0], acc_ref[i, 1])
        o_ref[i * tile:(i + 1) * tile, :] = o.T.astype(o_ref.dtype)


def sb_attention(main, vt, batch, seq_len, q_col, k_col, n_pairs, tri, tile=ATT_TILE):
    nq = seq_len // tile
    n = batch * seq_len
    return pl.pallas_call(
        functools.partial(_sb_kernel, tile=tile, n_tiles=nq),
        grid=(batch, n_pairs),
        in_specs=[pl.BlockSpec((None, seq_len, LANES), lambda b, p: (q_col + p, b, 0)),
                  pl.BlockSpec((None, seq_len, LANES), lambda b, p: (k_col + p, b, 0)),
                  pl.BlockSpec((nq, LANES, tile), lambda b, p: (b, p, 0)),
                  _full(tri)],
        out_specs=pl.BlockSpec((None, seq_len, LANES), lambda b, p: (p, b, 0)),
        out_shape=jax.ShapeDtypeStruct((n_pairs, n, LANES), BF16),
        scratch_shapes=[pltpu.VMEM((nq, 2, LANES, tile), F32), pltpu.VMEM((2, 2, tile, tile), F32)],
        compiler_params=_params("parallel", "parallel"),
        name="sb_attention",
    )(main, main, vt, tri)


def _gelu_tanh(x):
    return 0.5 * x * (1.0 + jnp.tanh(math.sqrt(2.0 / math.pi) * (x + 0.044715 * (x * x * x))))


def _compress_kernel(xk_ref, xv_ref, pek_ref, pev_ref, w1k_ref, w1v_ref, w2k_ref, w2v_ref,
                     ok_ref, ov_ref, *, n_cmp):
    def one(x_ref, pe_ref, w1_ref, w2_ref):
        x = x_ref[...].astype(F32)
        xa = (x + pe_ref[0:1, :]).astype(BF16)
        xb = (x + pe_ref[1:2, :]).astype(BF16)
        ha = _dot(xa, w1_ref[0])
        hb = _dot(xb, w1_ref[1])
        rows = ha.shape[0]
        hb = pltpu.roll(hb, rows - 1, 0)
        h = _gelu_tanh(ha + hb)
        out = _dot(h.astype(BF16), w2_ref[...])
        ridx = lax.broadcasted_iota(jnp.int32, out.shape, 0)
        return jnp.where(ridx < n_cmp, out, 0.0)

    ok_ref[0] = one(xk_ref, pek_ref, w1k_ref, w2k_ref).astype(ok_ref.dtype)
    ov_ref[0] = one(xv_ref, pev_ref, w1v_ref, w2v_ref).T.astype(ov_ref.dtype)


def compress(xk, xv, pek, pev, w1k, w1v, w2k, w2v, batch, n_cmp):
    chunks = xk.shape[0] // batch
    row = pl.BlockSpec((chunks, xk.shape[1]), lambda b: (b, 0))
    return pl.pallas_call(
        functools.partial(_compress_kernel, n_cmp=n_cmp),
        grid=(batch,),
        in_specs=[row, row, _full(pek), _full(pev), _full(w1k), _full(w1v), _full(w2k), _full(w2v)],
        out_specs=[pl.BlockSpec((1, chunks, LANES), lambda b: (b, 0, 0)),
                   pl.BlockSpec((1, LANES, chunks), lambda b: (b, 0, 0))],
        out_shape=[jax.ShapeDtypeStruct((batch, chunks, LANES), BF16),
                   jax.ShapeDtypeStruct((batch, LANES, chunks), BF16)],
        compiler_params=_params("parallel"),
        name="nsa_compress",
    )(xk, xv, pek, pev, w1k, w1v, w2k, w2v)


def _nsa_cmp_kernel(q_ref, kc_ref, vct_ref, cb_ref, ov_ref, ocmp_ref, sel_ref, stage_ref, *, tile, n_blocks):
    i = pl.program_id(1)
    kc = kc_ref[0]
    vct = vct_ref[0]
    pcsum = [jnp.zeros((LANES, tile), F32), jnp.zeros((LANES, tile), F32)]
    lo = _row_lo((LANES, tile))
    for r in range(NSA_GQA):
        heads = _split_pair(q_ref[r], HEAD_DIM ** -0.5)
        for a in range(2):
            stage_ref[2 * r + a] = _dot_nt(kc, heads[a])
    for r in range(NSA_GQA):
        outs = []
        for a in range(2):
            bias = cb_ref[2 * r + a]
            valid = bias > 0.5 * NEG
            s = stage_ref[2 * r + a] + bias
            m = jnp.max(s, axis=0, keepdims=True)
            pr = jnp.where(valid, jnp.exp2(s - m), 0.0)
            l = jnp.sum(pr, axis=0, keepdims=True)
            pc = pr / jnp.where(l > 0.0, l, 1.0)
            pcsum[a] = pcsum[a] + pc
            outs.append(_dot(vct, pc.astype(BF16)))
        ocmp_ref[r * LANES:(r + 1) * LANES, :] = jnp.where(lo, outs[0], outs[1])

    score = _dot_01exact(ov_ref[0], pcsum[0]) + _dot_01exact(ov_ref[1], pcsum[1])
    score = score[:2 * n_blocks]
    shape = (2 * n_blocks, tile)
    row = lax.broadcasted_iota(jnp.int32, shape, 0)
    qpos = i * tile + lax.broadcasted_iota(jnp.int32, shape, 1)
    blk = row % n_blocks
    cur = qpos // SLC_BLOCK
    forced = (blk == 0) | (blk == cur) | (blk == cur - 1)
    future = blk > cur
    score = jnp.where(future, -1.0, jnp.where(forced, FORCED_SCORE, score))
    grp0 = row < n_blocks
    cnt = jnp.zeros(shape, F32)
    for mth in range(n_blocks):
        other = jnp.where(grp0, score[mth:mth + 1, :], score[n_blocks + mth:n_blocks + mth + 1, :])
        ahead = (other > score) | ((other == score) & (blk > mth))
        cnt = cnt + jnp.where(ahead, 1.0, 0.0)
    n_top = min(SLC_TOPK, n_blocks)
    sel_ref[...] = jnp.where(cnt < n_top, 0.0, NEG)


def nsa_cmp_select(main, kc, vct, cmp_bias, overlap2, batch, seq_len, q_col, tile=ATT_TILE):
    nq = seq_len // tile
    n = batch * seq_len
    n_blocks = seq_len // SLC_BLOCK
    qw = NSA_GQA * LANES
    return pl.pallas_call(
        functools.partial(_nsa_cmp_kernel, tile=tile, n_blocks=n_blocks),
        grid=(batch, nq),
        in_specs=[pl.BlockSpec((NSA_GQA, tile, LANES), lambda b, i: (q_col // NSA_GQA, b * nq + i, 0)),
                  pl.BlockSpec((1,) + kc.shape[1:], lambda b, i: (b, 0, 0)),
                  pl.BlockSpec((1,) + vct.shape[1:], lambda b, i: (b, 0, 0)),
                  pl.BlockSpec((NSA_HEADS, cmp_bias.shape[1], tile), lambda b, i: (0, 0, i)),
                  _full(overlap2)],
        out_specs=[pl.BlockSpec((qw, tile), lambda b, i: (0, b * nq + i)),
                   pl.BlockSpec((2 * n_blocks, tile), lambda b, i: (0, b * nq + i))],
        out_shape=[jax.ShapeDtypeStruct((qw, n), F32),
                   jax.ShapeDtypeStruct((2 * n_blocks, n), F32)],
        scratch_shapes=[pltpu.VMEM((NSA_HEADS, kc.shape[1], tile), F32)],
        compiler_params=_params("parallel", "parallel"),
        name="nsa_cmp_select",
    )(main, kc, vct, cmp_bias, overlap2)


def _nsa_main_kernel(far_ref, q_ref, sel_ref, ocmp_ref, gate_ref, ks_ref, vst_ref, kw_ref, vwt_ref, tz_ref,
                     o_ref, stage_ref, *, tile, n_tiles, n_blocks):
    r = pl.program_id(1)
    lo = _row_lo((LANES, tile))
    blocks_per_tile = tile // SLC_BLOCK
    far = tuple(far_ref[2 * r + a] for a in range(2))

    def queries(i):
        return _split_pair(q_ref[i * tile:(i + 1) * tile, :], HEAD_DIM ** -0.5)

    def chosen(a, i, j, offset=None):
        rows = []
        for nb in range(blocks_per_tile):
            blk = a * n_blocks + j * blocks_per_tile + nb
            row = sel_ref[blk:blk + 1, i * tile:(i + 1) * tile]
            if offset is not None:
                row = row + offset
            rows.append(jnp.broadcast_to(row, (SLC_BLOCK, tile)))
        return jnp.concatenate(rows, axis=0)

    tasks = []
    for i in range(n_tiles):
        tasks += [("slc", i, j) for j in range(i + 1)]
        tasks += [("win", i, j) for j in ((i - 1, i) if i else (i,))]
    qcache = {}

    def scores(t):
        kind, i, j = tasks[t]
        if i not in qcache:
            qcache[i] = queries(i)
        k_ref_ = ks_ref if kind == "slc" else kw_ref
        kj = k_ref_[j * tile:(j + 1) * tile, :]
        for a in range(2):
            stage_ref[t % 2, a] = _dot_nt(kj, qcache[i][a])

    def fresh():
        return ([(jnp.full((1, tile), NEG, F32), jnp.zeros((1, tile), F32)) for _ in range(2)],
                [jnp.zeros((LANES, tile), F32) for _ in range(2)])

    scores(0)
    for t, (kind, i, j) in enumerate(tasks):
        if t + 1 < len(tasks):
            scores(t + 1)
        qk = [stage_ref[t % 2, a] for a in range(2)]
        first = j == 0 if kind == "slc" else j == max(i - 1, 0)
        if first:
            stats, acc = fresh()
        vt = (vst_ref if kind == "slc" else vwt_ref)[j]
        alphas, pvs = [], []
        for a in range(2):
            if j == i:
                s = qk[a] + tz_ref[a, 0]
            elif j == i - 1:
                s = qk[a] + tz_ref[a, 1 if kind == "slc" else 2]
            if kind == "slc":
                s = s + chosen(a, i, j) if j >= i - 1 else qk[a] + chosen(a, i, j, far[a])
            m_prev, l_prev = stats[a]
            m_new = jnp.maximum(m_prev, jnp.max(s, axis=0, keepdims=True))
            alpha = jnp.exp2(m_prev - m_new)
            pr = jnp.exp2(s - m_new)
            stats[a] = (m_new, alpha * l_prev + jnp.sum(pr, axis=0, keepdims=True))
            alphas.append(alpha)
            pvs.append(_dot(vt, pr.astype(BF16)))
        for a in range(2):
            acc[a] = alphas[a] * acc[a] + pvs[a]
        if j == i:
            branch = jnp.where(lo, acc[0] / stats[0][1], acc[1] / stats[1][1])
            if kind == "slc":
                o_slc = branch
            else:
                cols = slice(i * tile, (i + 1) * tile)
                gexp = [jnp.where(lo, gate_ref[2 * c:2 * c + 1, cols], gate_ref[2 * c + 1:2 * c + 2, cols])
                        for c in range(3)]
                out = gexp[0] * ocmp_ref[:, cols] + gexp[1] * o_slc + gexp[2] * branch
                o_ref[cols, :] = out.T.astype(o_ref.dtype)


def nsa_main(far_bias, main, sel, ocmp, gates, vt, tz, batch, seq_len, q_col, ks_col, kw_col, vs_row, vw_row,
             tile=ATT_TILE):
    nq = seq_len // tile
    n = batch * seq_len
    n_blocks = seq_len // SLC_BLOCK
    kv = lambda col: pl.BlockSpec((None, seq_len, LANES), lambda b, r: (col, b, 0))
    vts = lambda rowblk: pl.BlockSpec((nq, LANES, tile), lambda b, r: (b, rowblk, 0))
    return pl.pallas_call(
        functools.partial(_nsa_main_kernel, tile=tile, n_tiles=nq, n_blocks=n_blocks),
        grid=(batch, NSA_GQA),
        in_specs=[pl.BlockSpec(memory_space=pltpu.SMEM),
                  pl.BlockSpec((None, seq_len, LANES), lambda b, r: (q_col + r, b, 0)),
                  pl.BlockSpec((2 * n_blocks, seq_len), lambda b, r: (0, b)),
                  pl.BlockSpec((LANES, seq_len), lambda b, r: (r, b)),
                  pl.BlockSpec((GATE_ROWS, seq_len), lambda b, r: (r, b)),
                  kv(ks_col), vts(vs_row), kv(kw_col), vts(vw_row),
                  pl.BlockSpec((2, 3, tile, tile), lambda b, r: (r, 0, 0, 0))],
        out_specs=pl.BlockSpec((None, seq_len, LANES), lambda b, r: (r, b, 0)),
        out_shape=jax.ShapeDtypeStruct((NSA_GQA, n, LANES), BF16),
        scratch_shapes=[pltpu.VMEM((2, 2, tile, tile), F32)],
        compiler_params=_params("parallel", "parallel"),
        name="nsa_main",
    )(far_bias, main, sel, ocmp, gates, main, vt, main, vt, tz)


def _layer_tail_kernel(*refs, n_in, final_norm):
    a_refs = refs[:n_in]
    w_refs = refs[n_in:2 * n_in]
    (x_ref, gx_ref, wq_ref, kv_ref, wo_ref, gm_ref, w1_ref, w2_ref, gf_ref,
     o_ref, hn_ref, acc_ref) = refs[2 * n_in:]
    f = pl.program_id(1)

    @pl.when(f == 0)
    def _():
        x1 = x_ref[...]
        for a_ref, w_ref in zip(a_refs, w_refs):
            a = jnp.concatenate([a_ref[c] for c in range(a_ref.shape[0])], axis=1)
            x1 = x1 + _dot(a, w_ref[...])
        q = _dot(_rmsnorm(x1, gx_ref[...]).astype(BF16), wq_ref[...]).astype(BF16)
        width = XA_HEADS * XA_HEAD_DIM
        outs = []
        for h in range(XA_HEADS):
            hs = slice(h * XA_HEAD_DIM, (h + 1) * XA_HEAD_DIM)
            s = _dot_nt(q[:, hs], kv_ref[:, hs]) * (XA_HEAD_DIM ** -0.5)
            m = jnp.max(s, axis=-1, keepdims=True)
            p = jnp.exp(s - m)
            l = jnp.sum(p, axis=-1, keepdims=True)
            v = kv_ref[:, width + h * XA_HEAD_DIM:width + (h + 1) * XA_HEAD_DIM]
            outs.append((_dot(p.astype(BF16), v) / l).astype(BF16))
        x2 = x1 + _dot(jnp.concatenate(outs, axis=-1), wo_ref[...])
        hn_ref[...] = _rmsnorm(x2, gm_ref[...]).astype(BF16)
        acc_ref[...] = x2

    h = jnp.maximum(_dot(hn_ref[...], w1_ref[...]), 0.0)
    acc_ref[...] += _dot((h * h).astype(BF16), w2_ref[...])

    @pl.when(f == pl.num_programs(1) - 1)
    def _():
        y = acc_ref[...]
        if final_norm:
            y = _rmsnorm(y, gf_ref[...])
        o_ref[...] = y


def layer_tail(a_list, w_list, w_idx, x, layer, gx, wq, kv, wo, gm, w1, w2, gf, final_norm, seq_len, mem_len,
               tm=MLP_ROW_TILE, tf=MLP_FF_TILE):
    n, d = x.shape
    ff = w1.shape[2]
    tm = min(tm, seq_len)
    tf = min(tf, ff)
    tiles_per_seq = seq_len // tm
    once = pl.Buffered(1)
    row = lambda a: pl.BlockSpec((tm, a.shape[1]), lambda i, f: (i, 0))
    pick = lambda a, k: pl.BlockSpec((None,) + a.shape[1:], lambda i, f: (k,) + (0,) * (a.ndim - 1),
                                     pipeline_mode=once)
    return pl.pallas_call(
        functools.partial(_layer_tail_kernel, n_in=len(a_list), final_norm=final_norm),
        grid=(n // tm, ff // tf),
        in_specs=[pl.BlockSpec((a.shape[0], tm, LANES), lambda i, f: (0, i, 0)) for a in a_list]
        + [pick(w, w_idx) for w in w_list]
        + [row(x), pick(gx, layer), pick(wq, layer),
           pl.BlockSpec((mem_len, 2 * XA_HEADS * XA_HEAD_DIM), lambda i, f: (i // tiles_per_seq, layer)),
           pick(wo, layer), pick(gm, layer),
           pl.BlockSpec((None, d, tf), lambda i, f: (layer, 0, f)),
           pl.BlockSpec((None, tf, d), lambda i, f: (layer, f, 0)),
           pl.BlockSpec(gf.shape, lambda i, f: (0, 0), pipeline_mode=once)],
        out_specs=pl.BlockSpec((tm, d), lambda i, f: (i, 0)),
        out_shape=jax.ShapeDtypeStruct((n, d), F32),
        scratch_shapes=[pltpu.VMEM((tm, d), BF16), pltpu.VMEM((tm, d), F32)],
        compiler_params=_params("parallel", "arbitrary"),
        name="layer_tail",
    )(*a_list, *w_list, x, gx, wq, kv, wo, gm, w1, w2, gf)


def _static_tables(seq_len):
    tile = ATT_TILE
    n_chunks = seq_len // CMP_STRIDE
    n_cmp = (seq_len - CMP_BLOCK) // CMP_STRIDE + 1
    n_slc = seq_len // SLC_BLOCK
    assert n_chunks == LANES and 2 * n_slc <= LANES and seq_len % tile == 0 and WINDOW == tile
    c = np.arange(n_chunks)[:, None]
    t = np.arange(seq_len)[None, :]
    cdist = t - (c * CMP_STRIDE + CMP_BLOCK - 1)
    cmp_idx = np.where((cdist >= 0) & (c < n_cmp), _t5_bucket_np(cdist), -1).astype(np.int32)
    sl = np.arange(tile)[:, None]
    tl = np.arange(tile)[None, :]
    prev = _t5_bucket_np(tl - sl + tile)
    tz_idx = np.concatenate([np.where(sl <= tl, _t5_bucket_np(tl - sl), -1), prev,
                             np.where(sl > tl, prev, -1)], axis=0).astype(np.int32)
    far_bucket = int(_t5_bucket_np(np.array([tile + 1]))[0])
    assert far_bucket == int(_t5_bucket_np(np.array([seq_len]))[0])
    cmp_start = np.arange(n_cmp) * CMP_STRIDE
    cmp_stop = cmp_start + CMP_BLOCK - 1
    slc_start = np.arange(n_slc) * SLC_BLOCK
    slc_stop = slc_start + SLC_BLOCK - 1
    ov = ((cmp_start[:, None] <= slc_stop[None, :]) & (cmp_stop[:, None] >= slc_start[None, :]))
    overlap2 = np.zeros((2, LANES, LANES), np.float32)
    for a in range(2):
        overlap2[a, a * n_slc:(a + 1) * n_slc, :n_cmp] = ov.T
    return cmp_idx, tz_idx, far_bucket, overlap2, n_cmp


def _tri_prefix(nn):
    s = np.arange(nn)[:, None]
    j = np.arange(nn)[None, :]
    return (j <= s).astype(np.float32)


def _tri_later(nn):
    s = np.arange(nn)[:, None]
    j = np.arange(nn)[None, :]
    return (j > s).astype(np.float32)


_NSA_HEAD_PERM = [a * NSA_GQA + r for r in range(NSA_GQA) for a in range(NSA_GROUPS)]


def _block_diag2(m):
    z = jnp.zeros_like(m)
    return jnp.concatenate([jnp.concatenate([m, z], axis=1), jnp.concatenate([z, m], axis=1)], axis=0)


def _compress_weights(pe, w1, w2):
    half = CMP_BLOCK // 2
    pe2 = jnp.concatenate([pe, pe], axis=1).reshape(2, half * LANES)
    w1e = jax.vmap(_block_diag2)(w1).reshape(2, half * LANES, LANES)
    return pe2.astype(F32), w1e.astype(BF16), _block_diag2(w2).astype(BF16)


def kernel(x, mem, rel_bias, mem_norm_g, norm_mix_g, norm_xattn_g, norm_mlp_g, final_norm_g, w_in_even, b_forget, cmp_pe_k, cmp_w1_k, cmp_w2_k, cmp_pe_v, cmp_w1_v, cmp_w2_v, w_out_even, w_in_odd, w_out_odd, xa_wq, xa_wkv, xa_wo, mlp_w1, mlp_w2):
    batch, seq_len, d = x.shape
    mem_len = mem.shape[1]
    depth = norm_mix_g.shape[0]
    n = batch * seq_len
    fox_w = FOX_HEADS * HEAD_DIM
    nsa_w = NSA_HEADS * HEAD_DIM
    kv_w = NSA_GROUPS * HEAD_DIM
    splits = np.cumsum([fox_w, fox_w, fox_w, FOX_HEADS, nsa_w, kv_w, kv_w, kv_w, kv_w, kv_w, kv_w])

    cmp_idx, tz_idx, far_bucket, overlap2, n_cmp = _static_tables(seq_len)
    tab = LOG2E * rel_bias.astype(F32)[:, _NSA_HEAD_PERM]
    cmp_bias = bucket_lookup(tab, jnp.asarray(cmp_idx), rows=cmp_idx.shape[0])
    tz = bucket_lookup(tab, jnp.asarray(tz_idx), rows=ATT_TILE)
    tz = tz.reshape(NSA_HEADS, 3, ATT_TILE, ATT_TILE)
    far_bias = tab[far_bucket]
    overlap2 = jnp.asarray(overlap2, BF16)
    tri_prefix = jnp.asarray(_tri_prefix(min(ROW_TILE, seq_len)), BF16)
    tri_later = jnp.asarray(_tri_later(ATT_TILE), BF16)

    xf = x.reshape(n, d)
    memf = mem.reshape(batch * mem_len, d)
    row = lambda v: v.reshape(1, -1).astype(F32)
    rows = lambda v: v.reshape(v.shape[0], 1, -1).astype(F32)
    g_mix_all, g_xa_all, g_mlp_all = rows(norm_mix_g), rows(norm_xattn_g), rows(norm_mlp_g)
    g_mem, g_final = row(mem_norm_g), row(final_norm_g)

    ne = w_in_even.shape[0]
    (w_fq, w_fk, w_fv, w_ff, w_nq, w_kc, w_vc, w_ks, w_vs, w_kw, w_vw, w_ng) = jnp.split(w_in_even, splits, axis=2)
    w_fq = LOG2E * w_fq
    w_nq = LOG2E * w_nq.reshape(ne, d, NSA_HEADS, HEAD_DIM)[:, :, _NSA_HEAD_PERM, :].reshape(ne, d, nsa_w)
    w_main_all = jnp.concatenate([w_fq, w_fk, w_nq, w_ks, w_kw], axis=2).astype(BF16)
    w_t_all = jnp.concatenate([w_fv, w_vs, w_vw], axis=2).transpose(0, 2, 1).astype(BF16)
    w_kvc_all = jnp.concatenate([w_kc, w_vc], axis=2).astype(BF16)
    w_ng_t = w_ng.reshape(ne, d, NSA_GROUPS, NSA_GQA, 3).transpose(0, 3, 4, 2, 1).reshape(ne, NSA_GQA, 6, d)
    w_ng_all = jnp.pad(w_ng_t, ((0, 0), (0, 0), (0, GATE_ROWS - 6), (0, 0))).reshape(
        ne, NSA_GQA * GATE_ROWS, d).astype(BF16)
    pad_f = LANES - F_PIECES * FOX_HEADS
    w_ff_all = jnp.pad(jnp.repeat(w_ff, F_PIECES, axis=2), ((0, 0), (0, 0), (0, pad_f))).astype(BF16)
    b_f_all = jnp.pad(jnp.repeat(b_forget.astype(F32), F_PIECES, axis=1), ((0, 0), (0, pad_f))).reshape(ne, 1, LANES)
    cmp_k = jax.vmap(_compress_weights)(cmp_pe_k, cmp_w1_k, cmp_w2_k)
    cmp_v = jax.vmap(_compress_weights)(cmp_pe_v, cmp_w1_v, cmp_w2_v)
    w_o_fox_all = w_out_even[:, :fox_w].astype(BF16)
    w_o_nsa_all = w_out_even[:, fox_w:].reshape(ne, NSA_HEADS, HEAD_DIM, d)[:, _NSA_HEAD_PERM].reshape(
        ne, nsa_w, d).astype(BF16)
    sb_w = SB_HEADS * HEAD_DIM
    w_odd_qk_all = w_in_odd[:, :, :2 * sb_w].astype(BF16)
    w_odd_vt_all = w_in_odd[:, :, 2 * sb_w:].transpose(0, 2, 1).astype(BF16)
    w_o_odd_all = w_out_odd.astype(BF16)
    xa_wq_all, xa_wo_all = xa_wq.astype(BF16), xa_wo.astype(BF16)
    mlp_w1_all, mlp_w2_all = mlp_w1.astype(BF16), mlp_w2.astype(BF16)
    xa_wkv_cat = xa_wkv.transpose(1, 0, 2).reshape(d, depth * xa_wkv.shape[2]).astype(BF16)
    kv_mem = norm_proj(memf, g_mem[None], 0, xa_wkv_cat[None])

    for layer in range(depth):
        if layer % 2 == 0:
            e = layer // 2
            main, vt, kc_in, vc_in, fs, gates = even_proj(
                xf, g_mix_all, layer, w_main_all, w_t_all, w_kvc_all, w_ff_all, b_f_all, w_ng_all, e,
                tri_prefix, seq_len)
            nblk = fox_w // LANES
            fox = fox_attention(main, fs, vt, batch, seq_len, 0, nblk, 0, FOX_HEADS // 2)

            kc, vct = compress(kc_in, vc_in,
                               cmp_k[0][e], cmp_v[0][e], cmp_k[1][e], cmp_v[1][e], cmp_k[2][e], cmp_v[2][e],
                               batch, n_cmp)
            q_col = 2 * nblk
            ocmp, sel = nsa_cmp_select(main, kc, vct, cmp_bias, overlap2, batch, seq_len, q_col)
            nsa = nsa_main(far_bias, main, sel, ocmp, gates, vt, tz, batch, seq_len, q_col,
                           q_col + NSA_GQA, q_col + NSA_GQA + 1, nblk, nblk + 1)
            a_list, w_list, w_idx = [fox, nsa], [w_o_fox_all, w_o_nsa_all], e
        else:
            o = layer // 2
            main, vt = norm_proj(xf, g_mix_all, layer, w_odd_qk_all, w_odd_vt_all, o, col_blocks=True)
            nblk = sb_w // LANES
            sb = sb_attention(main, vt, batch, seq_len, 0, nblk, SB_HEADS // 2, tri_later)
            a_list, w_list, w_idx = [sb], [w_o_odd_all], o

        xf = layer_tail(a_list, w_list, w_idx, xf, layer, g_xa_all, xa_wq_all, kv_mem, xa_wo_all, g_mlp_all,
                        mlp_w1_all, mlp_w2_all, g_final, layer == depth - 1, seq_len, mem_len)
    return xf.reshape(batch, seq_len, d)
```
